```python
import math
import jax, jax.numpy as jnp
from jax import lax
import numpy as np

D_MODEL = 1024
BATCH = 8
SEQ = 4096
DEPTH = 4
DEC_BATCH = 8
DEC_SEQ = 8192
PAST_LEN = 128

GRID_W = 64
D_MIX = D_MODEL
ATT_WIDTH = D_MIX // 2
HEAD_DIM = 64
N_HEADS = ATT_WIDTH // HEAD_DIM
N_KV_HEADS = 2
KV_GROUP = N_HEADS // N_KV_HEADS
ROPE_THETA = 10000.0
ROPE_FREQS = HEAD_DIM // 4
Q_BLOCK = 128
QK_EPS = 1e-6
LRU_WIDTH = D_MIX // 4
LRU_HEADS = 4
LRU_HEAD_DIM = LRU_WIDTH // LRU_HEADS
LRU_CONV = 4
LRU_PAD_L = 2
LRU_PAD_R = 1
LRU_C = 8.0
HY_WIDTH = D_MIX - ATT_WIDTH - LRU_WIDTH
HY_ORDER = 2
HY_SHORT = 3
HY_EMB = 33
HY_BANDS = (HY_EMB - 1) // 2
HY_FFN = 64
HY_MIN_DECAY = abs(math.log(1e-2)) / 1.5
HY_MAX_DECAY = abs(math.log(1e-2)) / 0.3
D_FF = 2816
FFN_CONV = 3
ALPHA = (2 * DEPTH) ** 0.25
BETA = (8 * DEPTH) ** -0.25
LN_EPS = 1e-5
IN_Q = N_HEADS * HEAD_DIM
IN_KV = N_KV_HEADS * HEAD_DIM
IN_TOTAL = IN_Q + 2 * IN_KV + 2 * LRU_WIDTH + (HY_ORDER + 1) * HY_WIDTH
IN_SPLITS = (IN_Q, IN_Q + IN_KV, IN_Q + 2 * IN_KV, IN_Q + 2 * IN_KV + LRU_WIDTH, IN_Q + 2 * IN_KV + 2 * LRU_WIDTH)

kernel_name = 'hybrid_attn_rglru_hyena_encoder'


def layer_norm(x, g, b):
    xf = x.astype(jnp.float32)
    mu = jnp.mean(xf, axis=-1, keepdims=True)
    var = jnp.mean(jnp.square(xf - mu), axis=-1, keepdims=True)
    return ((xf - mu) * lax.rsqrt(var + LN_EPS) * g + b).astype(x.dtype)


def rms_norm(x, g):
    xf = x.astype(jnp.float32)
    return (xf * lax.rsqrt(jnp.mean(jnp.square(xf), axis=-1, keepdims=True) + QK_EPS) * g).astype(x.dtype)


def dw_conv(x, w, b, left, right):
    L = x.shape[1]
    xp = jnp.pad(x, ((0, 0), (left, right), (0, 0)))
    y = b
    for k in range(w.shape[0]):
        y = y + xp[:, k:k + L] * w[k]
    return y.astype(x.dtype)


def axial_rope(L):
    rows = L // GRID_W
    row = jnp.repeat(jnp.arange(rows, dtype=jnp.float32), GRID_W)
    col = jnp.tile(jnp.arange(GRID_W, dtype=jnp.float32), rows)
    inv = ROPE_THETA ** (-jnp.arange(ROPE_FREQS, dtype=jnp.float32) / ROPE_FREQS)
    ang = jnp.stack([row[:, None] * inv, col[:, None] * inv], axis=1)
    return jnp.cos(ang), jnp.sin(ang)


def apply_rope(x, cos, sin):
    B, L, H, _ = x.shape
    xs = x.astype(jnp.float32).reshape(B, L, H, 2, 2 * ROPE_FREQS)
    x1, x2 = xs[..., :ROPE_FREQS], xs[..., ROPE_FREQS:]
    c = cos[None, :, None]
    s = sin[None, :, None]
    out = jnp.concatenate([x1 * c - x2 * s, x2 * c + x1 * s], axis=-1)
    return out.reshape(B, L, H, HEAD_DIM).astype(x.dtype)


def block_attention(q, k, v):
    B, L = q.shape[:2]
    nblk = L // Q_BLOCK
    qb = q.reshape(B, nblk, Q_BLOCK, N_KV_HEADS, KV_GROUP, HEAD_DIM).transpose(1, 0, 2, 3, 4, 5)
    scale = HEAD_DIM ** -0.5

    def one_block(qi):
        s = jnp.einsum('bqkgd,bskd->bkgqs', qi, k).astype(jnp.float32) * scale
        p = jax.nn.softmax(s, axis=-1).astype(v.dtype)
        return jnp.einsum('bkgqs,bskd->bqkgd', p, v)

    o = lax.map(one_block, qb)
    return o.transpose(1, 0, 2, 3, 4, 5).reshape(B, L, N_HEADS * HEAD_DIM)


def rglru_scan(xc, wa, ba, wx, bx, lam, reverse):
    B, L, C = xc.shape
    xh = xc.reshape(B, L, LRU_HEADS, LRU_HEAD_DIM)
    r = jax.nn.sigmoid(jnp.einsum('blhi,hij->blhj', xh, wa).reshape(B, L, C) + ba)
    i = jax.nn.sigmoid(jnp.einsum('blhi,hij->blhj', xh, wx).reshape(B, L, C) + bx)
    log_a = (-LRU_C * r * jax.nn.softplus(-lam)).astype(jnp.float32)
    a = jnp.exp(log_a)
    b = jnp.sqrt(-jnp.expm1(2.0 * log_a)) * (i * xc).astype(jnp.float32)

    def combine(e1, e2):
        a1, b1 = e1
        a2, b2 = e2
        return a1 * a2, a2 * b1 + b2

    _, h = lax.associative_scan(combine, (a, b), reverse=reverse, axis=1)
    return h


def hyena_filters(L, w1, b1, w2, b2, w3, b3, freq, wout):
    t = jnp.linspace(0.0, 1.0, L, dtype=jnp.float32)[:, None]
    w = 2.0 * math.pi * jnp.arange(L, dtype=jnp.float32)[:, None] / L
    f = jnp.linspace(1e-4, HY_BANDS - 1, HY_BANDS, dtype=jnp.float32)[None, :]
    z = jnp.concatenate([t, jnp.cos(f * w), -jnp.sin(f * w)], axis=-1)
    h = jnp.sin(freq * (z @ w1 + b1))
    h = jnp.sin(freq * (h @ w2 + b2))
    h = jnp.sin(freq * (h @ w3 + b3))
    k = (h @ wout).astype(jnp.float32).reshape(L, HY_ORDER, 2, HY_WIDTH)
    deltas = jnp.linspace(HY_MIN_DECAY, HY_MAX_DECAY, HY_WIDTH, dtype=jnp.float32)
    k = k * jnp.exp(-t * deltas)[:, None, None, :]
    kf, kb = k[:, :, 0], k[:, :, 1]
    full = jnp.concatenate([kf.at[0].add(kb[0]), jnp.zeros((1, HY_ORDER, HY_WIDTH), jnp.float32), kb[:0:-1]], axis=0)
    full = full / jnp.sum(jnp.abs(full), axis=0, keepdims=True)
    return jnp.fft.rfft(full, axis=0)


def long_conv(z, kf, d):
    L = z.shape[1]
    zf = z.astype(jnp.float32)
    y = jnp.fft.irfft(jnp.fft.rfft(zf, n=2 * L, axis=1) * kf, n=2 * L, axis=1)[:, :L]
    return (y + zf * d).astype(z.dtype)


def mixer(u, w_in, q_gain, k_gain, lru_conv_w, lru_conv_b, lru_wa, lru_ba, lru_wx, lru_bx, lru_lambda,
          hy_conv_w, hy_conv_b, hy_w1, hy_b1, hy_w2, hy_b2, hy_w3, hy_b3, hy_freq, hy_wout, hy_bias, w_out):
    B, L, _ = u.shape
    proj = u @ w_in
    q, k, v, xr, gr, hy = jnp.split(proj, IN_SPLITS, axis=-1)
    q = rms_norm(q.reshape(B, L, N_HEADS, HEAD_DIM), q_gain)
    k = rms_norm(k.reshape(B, L, N_KV_HEADS, HEAD_DIM), k_gain)
    v = v.reshape(B, L, N_KV_HEADS, HEAD_DIM)
    cos, sin = axial_rope(L)
    attn = block_attention(apply_rope(q, cos, sin), apply_rope(k, cos, sin), v)
    xc = dw_conv(xr, lru_conv_w, lru_conv_b, LRU_PAD_L, LRU_PAD_R)
    h = (rglru_scan(xc, lru_wa[0], lru_ba[0], lru_wx[0], lru_bx[0], lru_lambda[0], False)
         + rglru_scan(xc, lru_wa[1], lru_ba[1], lru_wx[1], lru_bx[1], lru_lambda[1], True))
    lru = h.astype(u.dtype) * jax.nn.gelu(gr)
    hc = dw_conv(hy, hy_conv_w, hy_conv_b, 1, 1)
    z, g1, g2 = jnp.split(hc, 3, axis=-1)
    filt = hyena_filters(L, hy_w1, hy_b1, hy_w2, hy_b2, hy_w3, hy_b3, hy_freq, hy_wout)
    z = g1 * long_conv(z, filt[:, 0], hy_bias[0])
    z = g2 * long_conv(z, filt[:, 1], hy_bias[1])
    return jnp.concatenate([attn, lru, z], axis=-1) @ w_out


def conv_ffn(u, w_up, conv_w, conv_b, w_down):
    hcat = dw_conv(u @ w_up, conv_w, conv_b, 1, 1)
    gate, val = jnp.split(hcat, 2, axis=-1)
    return (jax.nn.gelu(gate) * val) @ w_down


def layer(x, c, ada_w, ada_b, w_in, q_gain, k_gain, lru_conv_w, lru_conv_b, lru_wa, lru_ba, lru_wx, lru_bx,
          lru_lambda, hy_conv_w, hy_conv_b, hy_w1, hy_b1, hy_w2, hy_b2, hy_w3, hy_b3, hy_freq, hy_wout, hy_bias,
          w_out, ln1_g, ln1_b, ffn_w_up, ffn_conv_w, ffn_conv_b, ffn_w_down, ln2_g, ln2_b):
    mod = (jax.nn.silu(c) @ ada_w + ada_b)[:, None, :]
    sh1, sc1, gt1, sh2, sc2, gt2 = jnp.split(mod, 6, axis=-1)
    u = x * (1.0 + sc1) + sh1
    m = mixer(u, w_in, q_gain, k_gain, lru_conv_w, lru_conv_b, lru_wa, lru_ba, lru_wx, lru_bx, lru_lambda,
              hy_conv_w, hy_conv_b, hy_w1, hy_b1, hy_w2, hy_b2, hy_w3, hy_b3, hy_freq, hy_wout, hy_bias, w_out)
    x = layer_norm(ALPHA * x + gt1 * m, ln1_g, ln1_b)
    u = x * (1.0 + sc2) + sh2
    f = conv_ffn(u, ffn_w_up, ffn_conv_w, ffn_conv_b, ffn_w_down)
    return layer_norm(ALPHA * x + gt2 * f, ln2_g, ln2_b)


def trunk(x, c, params):
    for l in range(DEPTH):
        x = layer(x, c, *[p[l] for p in params])
    return x


def setup_inputs(seed: int = 0) -> dict:
    key = jax.random.key(seed)
    ks = jax.random.split(key, 40)
    f32 = jnp.float32

    def nrm(k, shape, scale):
        return jax.random.normal(k, shape, f32) * scale

    u = jax.random.uniform(ks[15], (DEPTH, 2, LRU_WIDTH), f32, minval=0.9, maxval=0.999)
    a0 = u ** (1.0 / LRU_C)
    lru_lambda = jnp.log(a0) - jnp.log1p(-a0)
    return {
        'x_prompt': nrm(ks[0], (BATCH, SEQ, D_MODEL), 1.0),
        'x_sample': nrm(ks[1], (DEC_BATCH, DEC_SEQ, D_MODEL), 1.0),
        'c_prompt': nrm(ks[2], (BATCH, D_MODEL), 1.0),
        'c_sample': nrm(ks[3], (DEC_BATCH, D_MODEL), 1.0),
        'ada_w': nrm(ks[4], (DEPTH, D_MODEL, 6 * D_MODEL), D_MODEL ** -0.5),
        'ada_b': nrm(ks[5], (DEPTH, 6 * D_MODEL), 0.02),
        'w_in': nrm(ks[6], (DEPTH, D_MODEL, IN_TOTAL), D_MODEL ** -0.5),
        'q_gain': 1.0 + nrm(ks[7], (DEPTH, HEAD_DIM), 0.02),
        'k_gain': 1.0 + nrm(ks[8], (DEPTH, HEAD_DIM), 0.02),
        'lru_conv_w': nrm(ks[9], (DEPTH, LRU_CONV, LRU_WIDTH), LRU_CONV ** -0.5),
        'lru_conv_b': nrm(ks[10], (DEPTH, LRU_WIDTH), 0.02),
        'lru_wa': nrm(ks[11], (DEPTH, 2, LRU_HEADS, LRU_HEAD_DIM, LRU_HEAD_DIM), LRU_HEAD_DIM ** -0.5),
        'lru_ba': nrm(ks[12], (DEPTH, 2, LRU_WIDTH), 0.02),
        'lru_wx': nrm(ks[13], (DEPTH, 2, LRU_HEADS, LRU_HEAD_DIM, LRU_HEAD_DIM), LRU_HEAD_DIM ** -0.5),
        'lru_bx': nrm(ks[14], (DEPTH, 2, LRU_WIDTH), 0.02),
        'lru_lambda': lru_lambda,
        'hy_conv_w': nrm(ks[16], (DEPTH, HY_SHORT, (HY_ORDER + 1) * HY_WIDTH), HY_SHORT ** -0.5),
        'hy_conv_b': nrm(ks[17], (DEPTH, (HY_ORDER + 1) * HY_WIDTH), 0.02),
        'hy_w1': nrm(ks[18], (DEPTH, HY_EMB, HY_FFN), HY_EMB ** -0.5),
        'hy_b1': nrm(ks[19], (DEPTH, HY_FFN), 0.02),
        'hy_w2': nrm(ks[20], (DEPTH, HY_FFN, HY_FFN), HY_FFN ** -0.5),
        'hy_b2': nrm(ks[21], (DEPTH, HY_FFN), 0.02),
        'hy_w3': nrm(ks[22], (DEPTH, HY_FFN, HY_FFN), HY_FFN ** -0.5),
        'hy_b3': nrm(ks[23], (DEPTH, HY_FFN), 0.02),
        'hy_freq': 1.0 + nrm(ks[24], (DEPTH, HY_FFN), 0.02),
        'hy_wout': nrm(ks[25], (DEPTH, HY_FFN, HY_ORDER * 2 * HY_WIDTH), HY_FFN ** -0.5),
        'hy_bias': nrm(ks[26], (DEPTH, HY_ORDER, HY_WIDTH), 0.1),
        'w_out': nrm(ks[27], (DEPTH, D_MIX, D_MODEL), D_MIX ** -0.5 * BETA),
        'ln1_g': 1.0 + nrm(ks[28], (DEPTH, D_MODEL), 0.02),
        'ln1_b': nrm(ks[29], (DEPTH, D_MODEL), 0.02),
        'ffn_w_up': nrm(ks[30], (DEPTH, D_MODEL, 2 * D_FF), D_MODEL ** -0.5),
        'ffn_conv_w': nrm(ks[31], (DEPTH, FFN_CONV, 2 * D_FF), FFN_CONV ** -0.5),
        'ffn_conv_b': nrm(ks[32], (DEPTH, 2 * D_FF), 0.02),
        'ffn_w_down': nrm(ks[33], (DEPTH, D_FF, D_MODEL), D_FF ** -0.5 * BETA),
        'ln2_g': 1.0 + nrm(ks[34], (DEPTH, D_MODEL), 0.02),
        'ln2_b': nrm(ks[35], (DEPTH, D_MODEL), 0.02),
    }


def reference(x_prompt, x_sample, c_prompt, c_sample, ada_w, ada_b, w_in, q_gain, k_gain, lru_conv_w, lru_conv_b,
              lru_wa, lru_ba, lru_wx, lru_bx, lru_lambda, hy_conv_w, hy_conv_b, hy_w1, hy_b1, hy_w2, hy_b2, hy_w3,
              hy_b3, hy_freq, hy_wout, hy_bias, w_out, ln1_g, ln1_b, ffn_w_up, ffn_conv_w, ffn_conv_b, ffn_w_down,
              ln2_g, ln2_b):
    params = (ada_w, ada_b, w_in, q_gain, k_gain, lru_conv_w, lru_conv_b, lru_wa, lru_ba, lru_wx, lru_bx,
              lru_lambda, hy_conv_w, hy_conv_b, hy_w1, hy_b1, hy_w2, hy_b2, hy_w3, hy_b3, hy_freq, hy_wout,
              hy_bias, w_out, ln1_g, ln1_b, ffn_w_up, ffn_conv_w, ffn_conv_b, ffn_w_down, ln2_g, ln2_b)
    y_prompt = trunk(x_prompt, c_prompt, params)
    y_sample = trunk(x_sample, c_sample, params)
    return (y_prompt, y_sample)
```

```python
import functools
import math

import numpy as np
import jax
import jax.numpy as jnp
from jax import lax
from jax.experimental import pallas as pl
from jax.experimental.pallas import tpu as pltpu

F32 = jnp.float32
BF16 = jnp.bfloat16
HIGHEST = lax.Precision.HIGHEST

GRID_W = 64
HEAD_DIM = 64
N_HEADS = 8
N_KV_HEADS = 2
KV_GROUP = N_HEADS // N_KV_HEADS
ROPE_THETA = 10000.0
ROPE_FREQS = HEAD_DIM // 4
QK_EPS = 1e-6
LRU_WIDTH = 256
LRU_HEADS = 4
LRU_C = 8.0
HY_WIDTH = 256
HY_ORDER = 2
HY_BANDS = 16
HY_MIN_DECAY = abs(math.log(1e-2)) / 1.5
HY_MAX_DECAY = abs(math.log(1e-2)) / 0.3
LN_EPS = 1e-5
IN_Q = N_HEADS * HEAD_DIM
IN_KV = N_KV_HEADS * HEAD_DIM

LANES = 128
SUBLANES = 8
DFT_N2 = 128
VMEM_LIMIT = 48 * 1024 * 1024


def _cparams(sem):
    return pltpu.CompilerParams(dimension_semantics=sem, vmem_limit_bytes=VMEM_LIMIT)


def _dot(a, b):
    return jnp.dot(a, b, preferred_element_type=F32)


def _layer_norm(y, g, b):
    mu = jnp.mean(y, axis=-1, keepdims=True)
    yc = y - mu
    var = jnp.mean(yc * yc, axis=-1, keepdims=True)
    return yc * lax.rsqrt(var + LN_EPS) * g + b


def _ada_kernel(c_ref, w_ref, b_ref, o_ref):
    c = c_ref[...]
    s = c * jax.nn.sigmoid(c)
    o_ref[0] = jnp.dot(s, w_ref[0], precision=HIGHEST, preferred_element_type=F32) + b_ref[0]


def _ada_mod(c_all, ada_w, ada_b):
    depth, d, n = ada_w.shape
    rows = c_all.shape[0]
    tn = 768
    return pl.pallas_call(
        _ada_kernel,
        out_shape=jax.ShapeDtypeStruct((depth, rows, n), F32),
        grid=(depth, n // tn),
        in_specs=[
            pl.BlockSpec((rows, d), lambda l, j: (0, 0)),
            pl.BlockSpec((1, d, tn), lambda l, j: (l, 0, j)),
            pl.BlockSpec((1, 1, tn), lambda l, j: (l, 0, j)),
        ],
        out_specs=pl.BlockSpec((1, rows, tn), lambda l, j: (l, 0, j)),
        compiler_params=_cparams(("parallel", "parallel")),
        name="ada_mod",
    )(c_all, ada_w, ada_b.reshape(depth, 1, n))


def _rope_tables(L):
    rows = L // GRID_W
    row = np.repeat(np.arange(rows, dtype=np.float64), GRID_W)
    col = np.tile(np.arange(GRID_W, dtype=np.float64), rows)
    inv = ROPE_THETA ** (-np.arange(ROPE_FREQS, dtype=np.float64) / ROPE_FREQS)
    ar = row[:, None] * inv
    ac = col[:, None] * inv
    zeros = np.zeros_like(ar)
    cos = np.concatenate([np.cos(ar), np.cos(ar), np.cos(ac), np.cos(ac)], axis=1)
    sin_up = np.concatenate([-np.sin(ar), zeros, -np.sin(ac), zeros], axis=1)
    sin_dn = np.concatenate([zeros, np.sin(ar), zeros, np.sin(ac)], axis=1)
    two = lambda t: jnp.asarray(np.concatenate([t, t], axis=1), F32)
    return two(cos), two(sin_up), two(sin_dn)


def _inproj_kernel(x_ref, sc_ref, sh_ref, w_ref, qg_ref, kg_ref, cos_ref, sup_ref, sdn_ref, bd_ref,
                   q_ref, k_ref, v_ref, xg_ref, hy_ref):
    u = (x_ref[0] * (1.0 + sc_ref[0]) + sh_ref[0]).astype(BF16)
    proj = _dot(u, w_ref[...])
    cos = cos_ref[...]
    sup = sup_ref[...]
    sdn = sdn_ref[...]
    bd = bd_ref[...]
    half = ROPE_FREQS

    def norm_rope(t, gain):
        sq = t * t
        hi = sq.astype(BF16)
        lo = (sq - hi.astype(F32)).astype(BF16)
        ms = _dot(hi, bd) + _dot(lo, bd)
        tn = t * lax.rsqrt(ms + QK_EPS) * gain
        return (tn * cos + pltpu.roll(tn, LANES - half, 1) * sup + pltpu.roll(tn, half, 1) * sdn)

    qg = qg_ref[...]
    for j in range(IN_Q // LANES):
        sl = slice(j * LANES, (j + 1) * LANES)
        q_ref[0, :, sl] = (norm_rope(proj[:, sl], qg) * (HEAD_DIM ** -0.5)).astype(BF16)
    k_ref[0] = norm_rope(proj[:, IN_Q:IN_Q + IN_KV], kg_ref[...]).astype(BF16)
    v_ref[0] = proj[:, IN_Q + IN_KV:IN_Q + 2 * IN_KV].astype(BF16)
    o = IN_Q + 2 * IN_KV
    xg_ref[0] = proj[:, o:o + 2 * LRU_WIDTH]
    hy_ref[0] = proj[:, o + 2 * LRU_WIDTH:]


def _in_proj(x, sc, sh, w_in, q_gain, k_gain, rope):
    B, L, D = x.shape
    n_in = w_in.shape[1]
    T = min(512, L)
    cos, sup, sdn = rope
    bd = jnp.asarray(np.kron(np.eye(2), np.full((HEAD_DIM, HEAD_DIM), 1.0 / HEAD_DIM)), BF16)
    qg = jnp.tile(q_gain, 2).reshape(1, LANES)
    kg = jnp.tile(k_gain, 2).reshape(1, LANES)
    n_hy = n_in - IN_Q - 2 * IN_KV - 2 * LRU_WIDTH
    row = lambda b, i: (b, i, 0)
    vec = lambda b, i: (b, 0, 0)
    tab = lambda b, i: (i, 0)
    const = lambda b, i: (0, 0)
    return pl.pallas_call(
        _inproj_kernel,
        out_shape=(
            jax.ShapeDtypeStruct((B, L, IN_Q), BF16),
            jax.ShapeDtypeStruct((B, L, IN_KV), BF16),
            jax.ShapeDtypeStruct((B, L, IN_KV), BF16),
            jax.ShapeDtypeStruct((B, L, 2 * LRU_WIDTH), F32),
            jax.ShapeDtypeStruct((B, L, n_hy), F32),
        ),
        grid=(B, L // T),
        in_specs=[
            pl.BlockSpec((1, T, D), row),
            pl.BlockSpec((1, 1, D), vec),
            pl.BlockSpec((1, 1, D), vec),
            pl.BlockSpec((D, n_in), const),
            pl.BlockSpec((1, LANES), const),
            pl.BlockSpec((1, LANES), const),
            pl.BlockSpec((T, LANES), tab),
            pl.BlockSpec((T, LANES), tab),
            pl.BlockSpec((T, LANES), tab),
            pl.BlockSpec((LANES, LANES), const),
        ],
        out_specs=(
            pl.BlockSpec((1, T, IN_Q), row),
            pl.BlockSpec((1, T, IN_KV), row),
            pl.BlockSpec((1, T, IN_KV), row),
            pl.BlockSpec((1, T, 2 * LRU_WIDTH), row),
            pl.BlockSpec((1, T, n_hy), row),
        ),
        compiler_params=_cparams(("parallel", "parallel")),
        name="in_proj",
    )(x, sc, sh, w_in, qg, kg, cos, sup, sdn, bd)


def _attn_kernel(q_ref, kt_ref, v_ref, o_ref, qs_sc, m_sc, acc_sc, *, tq):
    j = pl.program_id(3)

    @pl.when(j == 0)
    def _():
        m_sc[...] = jnp.full(m_sc.shape, -jnp.inf, F32)
        acc_sc[...] = jnp.zeros(acc_sc.shape, F32)
        for h in range(KV_GROUP):
            qs_sc[h * tq:(h + 1) * tq, :] = q_ref[0, :, h * HEAD_DIM:(h + 1) * HEAD_DIM]

    s = _dot(qs_sc[...], kt_ref[0, 0])
    m_prev = m_sc[...]
    m_new = jnp.maximum(m_prev, jnp.max(s, axis=1, keepdims=True))
    p = jnp.exp(s - m_new)
    alpha = jnp.exp(m_prev - m_new)
    acc_sc[...] = alpha * acc_sc[...] + _dot(p.astype(BF16), v_ref[0, 0])
    m_sc[...] = m_new

    @pl.when(j == pl.num_programs(3) - 1)
    def _():
        acc = acc_sc[...]
        out = acc[:, :HEAD_DIM] / acc[:, HEAD_DIM:HEAD_DIM + 1]
        for h in range(KV_GROUP):
            o_ref[0, :, h * HEAD_DIM:(h + 1) * HEAD_DIM] = out[h * tq:(h + 1) * tq].astype(o_ref.dtype)


def _attention(q, k, v):
    B, L, _ = q.shape
    tq = min(256, L)
    tk = min(512, L)
    kt = k.reshape(B, L, N_KV_HEADS, HEAD_DIM).transpose(0, 2, 3, 1)
    vh = v.reshape(B, L, N_KV_HEADS, HEAD_DIM).transpose(0, 2, 1, 3)
    ones = jnp.ones((B, N_KV_HEADS, L, 1), BF16)
    pad = jnp.zeros((B, N_KV_HEADS, L, LANES - HEAD_DIM - 1), BF16)
    v_ext = jnp.concatenate([vh, ones, pad], axis=-1)
    gw = KV_GROUP * HEAD_DIM
    return pl.pallas_call(
        functools.partial(_attn_kernel, tq=tq),
        out_shape=jax.ShapeDtypeStruct((B, L, IN_Q), BF16),
        grid=(B, N_KV_HEADS, L // tq, L // tk),
        in_specs=[
            pl.BlockSpec((1, tq, gw), lambda b, g, i, j: (b, i, g)),
            pl.BlockSpec((1, 1, HEAD_DIM, tk), lambda b, g, i, j: (b, g, 0, j)),
            pl.BlockSpec((1, 1, tk, LANES), lambda b, g, i, j: (b, g, j, 0)),
        ],
        out_specs=pl.BlockSpec((1, tq, gw), lambda b, g, i, j: (b, i, g)),
        scratch_shapes=[
            pltpu.VMEM((KV_GROUP * tq, HEAD_DIM), BF16),
            pltpu.VMEM((KV_GROUP * tq, 1), F32),
            pltpu.VMEM((KV_GROUP * tq, LANES), F32),
        ],
        compiler_params=_cparams(("parallel", "parallel", "parallel", "arbitrary")),
        name="attention",
    )(q, kt, v_ext)


def _lru_kernel(xp_ref, x_ref, xn_ref, cw_ref, cb_ref, wa_ref, ba_ref, wx_ref, bx_ref, lam_ref,
                o_ref, xe_sc, a_sc, b_sc, h_sc, *, T, nt):
    d = pl.program_id(0)
    i = pl.program_id(1)
    tile = jnp.where(d == 0, i, nt - 1 - i)
    nb, _, C = x_ref.shape
    H = SUBLANES
    xe_sc[:, H:H + T, :] = x_ref[...]
    xe_sc[:, 0:H, :] = jnp.where(tile > 0, xp_ref[...], 0.0)
    xe_sc[:, H + T:2 * H + T, :] = jnp.where(tile < nt - 1, xn_ref[...], 0.0)
    cw = cw_ref[...]
    xc = cb_ref[...].reshape(1, 1, C)
    for k in range(cw.shape[0]):
        xc = xc + xe_sc[:, H - 2 + k:H - 2 + k + T, :] * cw[k].reshape(1, 1, C)
    xc2 = xc.reshape(nb * T, C)
    xb = xc2.astype(BF16)
    r = jax.nn.sigmoid(_dot(xb, wa_ref[0]) + ba_ref[0])
    ig = jax.nn.sigmoid(_dot(xb, wx_ref[0]) + bx_ref[0])
    lam = lam_ref[0]
    softplus_neg = jnp.maximum(-lam, 0.0) + jnp.log1p(jnp.exp(-jnp.abs(lam)))
    log_a = -LRU_C * r * softplus_neg
    a = jnp.exp(log_a)
    bb = jnp.sqrt(-jnp.tanh(log_a) * (1.0 + a * a)) * (ig * xc2)
    a_sc[...] = a.reshape(nb, T, C)
    b_sc[...] = bb.reshape(nb, T, C)

    @pl.when(i == 0)
    def _():
        h_sc[...] = jnp.zeros(h_sc.shape, F32)

    def body(s, h):
        t = jnp.where(d == 0, s, T - 1 - s)
        h = a_sc[:, t, :] * h + b_sc[:, t, :]
        o_ref[0, :, t, :] = h
        return h

    h_sc[...] = lax.fori_loop(0, T, body, h_sc[...])


def _rglru(xg, conv_w, conv_b, wa, ba, wx, bx, lam):
    B, L, _ = xg.shape
    C = LRU_WIDTH
    assert B == SUBLANES
    T = min(256, L)
    nt = L // T
    hb = T // SUBLANES
    n_halo = L // SUBLANES

    def tile_of(d, i):
        return jnp.where(d == 0, i, nt - 1 - i)

    def blockdiag(w):
        eye = jnp.eye(LRU_HEADS, dtype=w.dtype)
        return jnp.einsum('dhij,hg->dhigj', w, eye).reshape(2, C, C).astype(BF16)

    kern = functools.partial(_lru_kernel, T=T, nt=nt)
    return pl.pallas_call(
        kern,
        out_shape=jax.ShapeDtypeStruct((2, B, L, C), F32),
        grid=(2, nt),
        in_specs=[
            pl.BlockSpec((B, SUBLANES, C), lambda d, i: (0, jnp.maximum(tile_of(d, i) * hb - 1, 0), 0)),
            pl.BlockSpec((B, T, C), lambda d, i: (0, tile_of(d, i), 0)),
            pl.BlockSpec((B, SUBLANES, C), lambda d, i: (0, jnp.minimum((tile_of(d, i) + 1) * hb, n_halo - 1), 0)),
            pl.BlockSpec(conv_w.shape, lambda d, i: (0, 0)),
            pl.BlockSpec((1, C), lambda d, i: (0, 0)),
            pl.BlockSpec((1, C, C), lambda d, i: (d, 0, 0)),
            pl.BlockSpec((1, 1, C), lambda d, i: (d, 0, 0)),
            pl.BlockSpec((1, C, C), lambda d, i: (d, 0, 0)),
            pl.BlockSpec((1, 1, C), lambda d, i: (d, 0, 0)),
            pl.BlockSpec((1, 1, C), lambda d, i: (d, 0, 0)),
        ],
        out_specs=pl.BlockSpec((1, B, T, C), lambda d, i: (d, 0, tile_of(d, i), 0)),
        scratch_shapes=[
            pltpu.VMEM((B, T + 2 * SUBLANES, C), F32),
            pltpu.VMEM((B, T, C), F32),
            pltpu.VMEM((B, T, C), F32),
            pltpu.VMEM((B, C), F32),
        ],
        compiler_params=_cparams(("arbitrary", "arbitrary")),
        name="rglru",
    )(xg, xg, xg, conv_w, conv_b.reshape(1, C), blockdiag(wa), ba.reshape(2, 1, C),
      blockdiag(wx), bx.reshape(2, 1, C), lam.reshape(2, 1, C))


def _shortconv_kernel(xp_ref, x_ref, xn_ref, w_ref, b_ref, o_ref, xe_sc, *, T, nt):
    i = pl.program_id(1)
    H = SUBLANES
    xe_sc[H:H + T, :] = x_ref[0]
    xe_sc[0:H, :] = jnp.where(i > 0, xp_ref[0], 0.0)
    xe_sc[H + T:2 * H + T, :] = jnp.where(i < nt - 1, xn_ref[0], 0.0)
    w = w_ref[...]
    y = b_ref[...]
    for k in range(w.shape[0]):
        y = y + xe_sc[H - 1 + k:H - 1 + k + T, :] * w[k:k + 1]
    o_ref[0] = y


def _short_conv(x, w, b):
    B, L, C = x.shape
    T = min(512, L)
    nt = L // T
    hb = T // SUBLANES
    n_halo = L // SUBLANES
    return pl.pallas_call(
        functools.partial(_shortconv_kernel, T=T, nt=nt),
        out_shape=jax.ShapeDtypeStruct((B, L, C), F32),
        grid=(B, nt),
        in_specs=[
            pl.BlockSpec((1, SUBLANES, C), lambda b_, i: (b_, jnp.maximum(i * hb - 1, 0), 0)),
            pl.BlockSpec((1, T, C), lambda b_, i: (b_, i, 0)),
            pl.BlockSpec((1, SUBLANES, C), lambda b_, i: (b_, jnp.minimum((i + 1) * hb, n_halo - 1), 0)),
            pl.BlockSpec(w.shape, lambda b_, i: (0, 0)),
            pl.BlockSpec((1, C), lambda b_, i: (0, 0)),
        ],
        out_specs=pl.BlockSpec((1, T, C), lambda b_, i: (b_, i, 0)),
        scratch_shapes=[pltpu.VMEM((T + 2 * SUBLANES, C), F32)],
        compiler_params=_cparams(("parallel", "parallel")),
        name="hy_short_conv",
    )(x, x, x, w, b.reshape(1, C))


def _filter_positions(L):
    t = np.linspace(0.0, 1.0, L)[:, None]
    w = 2.0 * math.pi * np.arange(L, dtype=np.float64)[:, None] / L
    f = np.linspace(1e-4, HY_BANDS - 1, HY_BANDS)[None, :]
    z = np.concatenate([t, np.cos(f * w), -np.sin(f * w)], axis=-1)
    zrev = np.concatenate([z[:1], z[:0:-1]], axis=0)
    return jnp.asarray(z, F32), jnp.asarray(zrev, F32)


def _filter_kernel(z_ref, zr_ref, w1_ref, b1_ref, w2_ref, b2_ref, w3_ref, b3_ref, fr_ref, wo_ref, dl_ref,
                   fs_ref, asum_ref, *, T):
    i = pl.program_id(0)
    fr = fr_ref[...]
    hd = lambda a, b: jnp.dot(a, b, precision=HIGHEST, preferred_element_type=F32)

    def mlp(z):
        h = jnp.sin(fr * (hd(z, w1_ref[...]) + b1_ref[...]))
        h = jnp.sin(fr * (hd(h, w2_ref[...]) + b2_ref[...]))
        h = jnp.sin(fr * (hd(h, w3_ref[...]) + b3_ref[...]))
        return hd(h, wo_ref[...])

    z = z_ref[...]
    zr = zr_ref[...]
    kz = mlp(z)
    kr = mlp(zr)
    dl = dl_ref[...]
    dec_f = jnp.exp(-z[:, 0:1] * dl)
    dec_r = jnp.exp(-zr[:, 0:1] * dl)
    C = HY_WIDTH
    first_row = (i * T + lax.broadcasted_iota(jnp.int32, (T, 1), 0)) == 0

    @pl.when(i == 0)
    def _():
        asum_ref[...] = jnp.zeros(asum_ref.shape, F32)

    for o in range(HY_ORDER):
        base = o * 2 * C
        kf = kz[:, base:base + C] * dec_f
        kb_here = kz[:, base + C:base + 2 * C] * dec_f
        kb_rev = kr[:, base + C:base + 2 * C] * dec_r
        first = kf + jnp.where(first_row, kb_here, 0.0)
        second = jnp.where(first_row, 0.0, kb_rev)
        fs_ref[o] = first
        fs_ref[HY_ORDER + o] = second
        asum_ref[o:o + 1, :] += jnp.sum(jnp.abs(first) + jnp.abs(second), axis=0, keepdims=True)


def _hyena_filter_time(L, w1, b1, w2, b2, w3, b3, freq, wout):
    z, zrev = _filter_positions(L)
    T = min(512, L)
    C = HY_WIDTH
    nemb = z.shape[1]
    nf = w2.shape[0]
    deltas = jnp.asarray(np.linspace(HY_MIN_DECAY, HY_MAX_DECAY, C)[None, :], F32)
    const = lambda i: (0, 0)
    return pl.pallas_call(
        functools.partial(_filter_kernel, T=T),
        out_shape=(jax.ShapeDtypeStruct((2 * HY_ORDER, L, C), F32), jax.ShapeDtypeStruct((HY_ORDER, C), F32)),
        grid=(L // T,),
        in_specs=[
            pl.BlockSpec((T, nemb), lambda i: (i, 0)),
            pl.BlockSpec((T, nemb), lambda i: (i, 0)),
            pl.BlockSpec((nemb, nf), const), pl.BlockSpec((1, nf), const),
            pl.BlockSpec((nf, nf), const), pl.BlockSpec((1, nf), const),
            pl.BlockSpec((nf, nf), const), pl.BlockSpec((1, nf), const),
            pl.BlockSpec((1, nf), const),
            pl.BlockSpec(wout.shape, const),
            pl.BlockSpec((1, C), const),
        ],
        out_specs=(pl.BlockSpec((2 * HY_ORDER, T, C), lambda i: (0, i, 0)),
                   pl.BlockSpec((HY_ORDER, C), const)),
        compiler_params=_cparams(("arbitrary",)),
        name="hy_filter_mlp",
    )(z, zrev, w1, b1.reshape(1, nf), w2, b2.reshape(1, nf), w3, b3.reshape(1, nf), freq.reshape(1, nf), wout, deltas)


def _dft_tables(L):
    N = 2 * L
    N2 = DFT_N2
    N1 = N // N2
    K1 = N1 // 2
    n1 = np.arange(K1)[None, None, :]
    k1 = np.arange(N1)[None, :, None]
    n2 = np.arange(N2)[:, None, None]
    ang = 2.0 * np.pi * (((n1 * N2 + n2) * k1) % N) / N
    g_fwd = np.concatenate([np.cos(ang), -np.sin(ang)], axis=1)
    angt = np.transpose(ang, (0, 2, 1))
    g_inv = np.concatenate([np.cos(angt), -np.sin(angt)], axis=2) / N
    a2 = 2.0 * np.pi * ((np.arange(N2)[:, None] * np.arange(N2)[None, :]) % N2) / N2
    cr, ci = np.cos(a2), -np.sin(a2)
    m2 = np.block([[cr, -ci], [ci, cr]])
    m2i = np.block([[cr, ci], [-ci, cr]])
    as_bf = lambda a: jnp.asarray(a, BF16)
    return as_bf(g_fwd), as_bf(g_inv), as_bf(m2), as_bf(m2i)


def _dft_a_kernel(x_ref, g_ref, o_ref):
    n1 = o_ref.shape[1]
    for j in range(SUBLANES):
        xj = x_ref[0, :, j, :].astype(BF16)
        r = _dot(g_ref[j], xj)
        o_ref[0, :, 0, j, :] = r[:n1]
        o_ref[0, :, 1, j, :] = r[n1:]


def _dft_stage_a(x4, lane_block, g_fwd, C):
    S, K1, N2, _ = x4.shape
    N1 = 2 * K1
    return pl.pallas_call(
        _dft_a_kernel,
        out_shape=jax.ShapeDtypeStruct((S, N1, 2, N2, C), F32),
        grid=(S, N2 // SUBLANES),
        in_specs=[
            pl.BlockSpec((1, K1, SUBLANES, C), lambda s, j: (s, 0, j, lane_block)),
            pl.BlockSpec((SUBLANES, 2 * N1, K1), lambda s, j: (j, 0, 0)),
        ],
        out_specs=pl.BlockSpec((1, N1, 2, SUBLANES, C), lambda s, j: (s, 0, 0, j, 0)),
        compiler_params=_cparams(("parallel", "parallel")),
        name="hy_dft_a",
    )(x4, g_fwd)


def _dft_b_filter_kernel(s1_ref, s2_ref, m2_ref, asum_ref, o_ref, *, tk1):
    n2 = s1_ref.shape[3]
    C = s1_ref.shape[4]
    scale = 1.0 / asum_ref[0]
    for j in range(tk1):
        sign = 1.0 if j % 2 == 0 else -1.0
        s = s1_ref[0, j] + sign * s2_ref[0, j]
        xk = _dot(m2_ref[...], s.reshape(2 * n2, C).astype(BF16)) * scale
        o_ref[0, j] = xk.reshape(2, n2, C)


def _dft_stage_b_filter(sa, m2, asum):
    _, N1, _, N2, C = sa.shape
    tk1 = min(8, N1)
    blk = (1, tk1, 2, N2, C)
    return pl.pallas_call(
        functools.partial(_dft_b_filter_kernel, tk1=tk1),
        out_shape=jax.ShapeDtypeStruct((HY_ORDER, N1, 2, N2, C), F32),
        grid=(HY_ORDER, N1 // tk1),
        in_specs=[
            pl.BlockSpec(blk, lambda o, i: (o, i, 0, 0, 0)),
            pl.BlockSpec(blk, lambda o, i: (HY_ORDER + o, i, 0, 0, 0)),
            pl.BlockSpec((2 * N2, 2 * N2), lambda o, i: (0, 0)),
            pl.BlockSpec((1, 1, C), lambda o, i: (o, 0, 0)),
        ],
        out_specs=pl.BlockSpec(blk, lambda o, i: (o, i, 0, 0, 0)),
        compiler_params=_cparams(("parallel", "parallel")),
        name="hy_dft_b_filter",
    )(sa, sa, m2, asum.reshape(HY_ORDER, 1, C))


def _dft_b_conv_kernel(s_ref, kf_ref, m2_ref, m2i_ref, o_ref, *, tk1):
    n2 = s_ref.shape[3]
    C = s_ref.shape[4]
    for j in range(tk1):
        x = _dot(m2_ref[...], s_ref[0, j].reshape(2 * n2, C).astype(BF16))
        xr, xi = x[:n2], x[n2:]
        kr = kf_ref[0, j, 0]
        ki = kf_ref[0, j, 1]
        y = jnp.concatenate([xr * kr - xi * ki, xr * ki + xi * kr], axis=0).astype(BF16)
        o_ref[0, j] = _dot(m2i_ref[...], y).reshape(2, n2, C)


def _dft_stage_b_conv(sa, kspec, order, m2, m2i):
    S, N1, _, N2, C = sa.shape
    tk1 = min(8, N1)
    blk = (1, tk1, 2, N2, C)
    return pl.pallas_call(
        functools.partial(_dft_b_conv_kernel, tk1=tk1),
        out_shape=jax.ShapeDtypeStruct(sa.shape, F32),
        grid=(S, N1 // tk1),
        in_specs=[
            pl.BlockSpec(blk, lambda s, i: (s, i, 0, 0, 0)),
            pl.BlockSpec(blk, lambda s, i: (order, i, 0, 0, 0)),
            pl.BlockSpec((2 * N2, 2 * N2), lambda s, i: (0, 0)),
            pl.BlockSpec((2 * N2, 2 * N2), lambda s, i: (0, 0)),
        ],
        out_specs=pl.BlockSpec(blk, lambda s, i: (s, i, 0, 0, 0)),
        compiler_params=_cparams(("parallel", "parallel")),
        name="hy_dft_b_conv",
    )(sa, kspec, m2, m2i)


def _dft_c_kernel(c_ref, gi_ref, z_ref, g_ref, d_ref, o_ref):
    for j in range(SUBLANES):
        cat = jnp.concatenate([c_ref[0, :, 0, j, :], c_ref[0, :, 1, j, :]], axis=0).astype(BF16)
        y = _dot(gi_ref[j], cat)
        o_ref[0, :, j, :] = g_ref[0, :, j, :] * (y + z_ref[0, :, j, :] * d_ref[0])


def _dft_stage_c(sc, g_inv, z4, z_block, gate4, gate_block, d):
    S, N1, _, N2, C = sc.shape
    K1 = N1 // 2
    seq = lambda blk: pl.BlockSpec((1, K1, SUBLANES, C), lambda s, j: (s, 0, j, blk))
    return pl.pallas_call(
        _dft_c_kernel,
        out_shape=jax.ShapeDtypeStruct((S, K1, N2, C), F32),
        grid=(S, N2 // SUBLANES),
        in_specs=[
            pl.BlockSpec((1, N1, 2, SUBLANES, C), lambda s, j: (s, 0, 0, j, 0)),
            pl.BlockSpec((SUBLANES, K1, 2 * N1), lambda s, j: (j, 0, 0)),
            seq(z_block),
            seq(gate_block),
            pl.BlockSpec((1, 1, C), lambda s, j: (0, 0, 0)),
        ],
        out_specs=seq(0),
        compiler_params=_cparams(("parallel", "parallel")),
        name="hy_dft_c",
    )(sc, g_inv, z4, gate4, d.reshape(1, 1, C))


def _hyena(hy, conv_w, conv_b, w1, b1, w2, b2, w3, b3, freq, wout, bias, tables):
    B, L, _ = hy.shape
    C = HY_WIDTH
    g_fwd, g_inv, m2, m2i = tables
    N2 = DFT_N2
    K1 = L // N2
    fs, asum = _hyena_filter_time(L, w1, b1, w2, b2, w3, b3, freq, wout)
    fa = _dft_stage_a(fs.reshape(2 * HY_ORDER, K1, N2, C), 0, g_fwd, C)
    kspec = _dft_stage_b_filter(fa, m2, asum)
    hc = _short_conv(hy, conv_w, conv_b)
    hc4 = hc.reshape(B, K1, N2, (HY_ORDER + 1) * C)
    z4, z_block = hc4, 0
    for o in range(HY_ORDER):
        sa = _dft_stage_a(z4, z_block, g_fwd, C)
        sc = _dft_stage_b_conv(sa, kspec, o, m2, m2i)
        z4 = _dft_stage_c(sc, g_inv, z4, z_block, hc4, 1 + o, bias[o])
        z_block = 0
    return z4.reshape(B, L, C)


def _outproj_kernel(x_ref, at_ref, h_ref, gr_ref, zy_ref, w_ref, gt_ref, g_ref, b_ref, o_ref, *, alpha):
    lru = (h_ref[0, 0] + h_ref[1, 0]) * jax.nn.gelu(gr_ref[0])
    na = at_ref.shape[2]
    nl = lru.shape[1]
    m = (_dot(at_ref[0], w_ref[0:na]) + _dot(lru.astype(BF16), w_ref[na:na + nl])
         + _dot(zy_ref[0].astype(BF16), w_ref[na + nl:]))
    o_ref[0] = _layer_norm(alpha * x_ref[0] + gt_ref[0] * m, g_ref[...], b_ref[...])


def _out_proj(x, attn, h, xg, zy, w_out, gt, ln_g, ln_b, alpha):
    B, L, D = x.shape
    T = min(512, L)
    C = LRU_WIDTH
    row = lambda b, i: (b, i, 0)
    vec = lambda b, i: (b, 0, 0)
    const = lambda b, i: (0, 0)
    return pl.pallas_call(
        functools.partial(_outproj_kernel, alpha=alpha),
        out_shape=jax.ShapeDtypeStruct((B, L, D), F32),
        grid=(B, L // T),
        in_specs=[
            pl.BlockSpec((1, T, D), row),
            pl.BlockSpec((1, T, IN_Q), row),
            pl.BlockSpec((2, 1, T, C), lambda b, i: (0, b, i, 0)),
            pl.BlockSpec((1, T, C), lambda b, i: (b, i, 1)),
            pl.BlockSpec((1, T, HY_WIDTH), row),
            pl.BlockSpec(w_out.shape, const),
            pl.BlockSpec((1, 1, D), vec),
            pl.BlockSpec((1, D), const),
            pl.BlockSpec((1, D), const),
        ],
        out_specs=pl.BlockSpec((1, T, D), row),
        compiler_params=_cparams(("parallel", "parallel")),
        name="out_proj",
    )(x, attn, h, xg, zy, w_out, gt, ln_g.reshape(1, D), ln_b.reshape(1, D))


def _ffn_kernel(xp_ref, x_ref, xn_ref, sc_ref, sh_ref, gt_ref, wg_ref, wv_ref, cwg_ref, cwv_ref, cbg_ref, cbv_ref,
                wd_ref, g_ref, b_ref, o_ref, u_sc, acc_sc, *, T, nt, alpha):
    i = pl.program_id(1)
    f = pl.program_id(2)
    H = SUBLANES

    @pl.when(f == 0)
    def _():
        sc = 1.0 + sc_ref[0]
        sh = sh_ref[0]
        u_sc[H:H + T, :] = (x_ref[0] * sc + sh).astype(BF16)
        u_sc[0:H, :] = jnp.where(i > 0, xp_ref[0] * sc + sh, 0.0).astype(BF16)
        u_sc[H + T:2 * H + T, :] = jnp.where(i < nt - 1, xn_ref[0] * sc + sh, 0.0).astype(BF16)
        acc_sc[...] = jnp.zeros(acc_sc.shape, F32)

    u = u_sc[...]
    R = T + 2 * H

    def conv(h, cw, cb):
        prev = pltpu.roll(h, 1, 0)
        nxt = pltpu.roll(h, R - 1, 0)
        y = cb + prev * cw[0:1] + h * cw[1:2] + nxt * cw[2:3]
        return y[H:H + T]

    hg = conv(_dot(u, wg_ref[...]), cwg_ref[...], cbg_ref[...])
    hv = conv(_dot(u, wv_ref[...]), cwv_ref[...], cbv_ref[...])
    act = (jax.nn.gelu(hg) * hv).astype(BF16)
    acc_sc[...] += _dot(act, wd_ref[...])

    @pl.when(f == pl.num_programs(2) - 1)
    def _():
        o_ref[0] = _layer_norm(alpha * x_ref[0] + gt_ref[0] * acc_sc[...], g_ref[...], b_ref[...])


def _conv_ffn(x, sc, sh, gt, w_up, conv_w, conv_b, w_down, ln_g, ln_b, alpha):
    B, L, D = x.shape
    d_ff = w_down.shape[0]
    T = min(512, L)
    F = 256
    nt = L // T
    nf = d_ff // F
    hb = T // SUBLANES
    n_halo = L // SUBLANES
    cb = conv_b.reshape(1, 2 * d_ff)
    row = lambda b, i, f: (b, i, 0)
    vec = lambda b, i, f: (b, 0, 0)
    const = lambda b, i, f: (0, 0)
    gate_col = lambda b, i, f: (0, f)
    val_col = lambda b, i, f: (0, nf + f)
    return pl.pallas_call(
        functools.partial(_ffn_kernel, T=T, nt=nt, alpha=alpha),
        out_shape=jax.ShapeDtypeStruct((B, L, D), F32),
        grid=(B, nt, nf),
        in_specs=[
            pl.BlockSpec((1, SUBLANES, D), lambda b, i, f: (b, jnp.maximum(i * hb - 1, 0), 0)),
            pl.BlockSpec((1, T, D), row),
            pl.BlockSpec((1, SUBLANES, D), lambda b, i, f: (b, jnp.minimum((i + 1) * hb, n_halo - 1), 0)),
            pl.BlockSpec((1, 1, D), vec),
            pl.BlockSpec((1, 1, D), vec),
            pl.BlockSpec((1, 1, D), vec),
            pl.BlockSpec((D, F), gate_col),
            pl.BlockSpec((D, F), val_col),
            pl.BlockSpec((conv_w.shape[0], F), gate_col),
            pl.BlockSpec((conv_w.shape[0], F), val_col),
            pl.BlockSpec((1, F), gate_col),
            pl.BlockSpec((1, F), val_col),
            pl.BlockSpec((F, D), lambda b, i, f: (f, 0)),
            pl.BlockSpec((1, D), const),
            pl.BlockSpec((1, D), const),
        ],
        out_specs=pl.BlockSpec((1, T, D), row),
        scratch_shapes=[
            pltpu.VMEM((T + 2 * SUBLANES, D), BF16),
            pltpu.VMEM((T, D), F32),
        ],
        compiler_params=_cparams(("parallel", "parallel", "arbitrary")),
        name="conv_ffn",
    )(x, x, x, sc, sh, gt, w_up, w_up, conv_w, conv_w, cb, cb, w_down, ln_g.reshape(1, D), ln_b.reshape(1, D))


def _trunk(x, mod, p):
    B, L, D = x.shape
    depth = mod.shape[0]
    alpha = (2 * depth) ** 0.25
    rope = _rope_tables(L)
    tables = _dft_tables(L)
    for l in range(depth):
        m6 = mod[l].reshape(B, 6, 1, D)
        sh1, sc1, gt1, sh2, sc2, gt2 = (m6[:, j] for j in range(6))
        q, k, v, xg, hy = _in_proj(x, sc1, sh1, p['w_in'][l], p['q_gain'][l], p['k_gain'][l], rope)
        attn = _attention(q, k, v)
        h = _rglru(xg, p['lru_conv_w'][l], p['lru_conv_b'][l], p['lru_wa'][l], p['lru_ba'][l],
                   p['lru_wx'][l], p['lru_bx'][l], p['lru_lambda'][l])
        zy = _hyena(hy, p['hy_conv_w'][l], p['hy_conv_b'][l], p['hy_w1'][l], p['hy_b1'][l], p['hy_w2'][l],
                    p['hy_b2'][l], p['hy_w3'][l], p['hy_b3'][l], p['hy_freq'][l], p['hy_wout'][l],
                    p['hy_bias'][l], tables)
        x = _out_proj(x, attn, h, xg, zy, p['w_out'][l], gt1, p['ln1_g'][l], p['ln1_b'][l], alpha)
        x = _conv_ffn(x, sc2, sh2, gt2, p['ffn_w_up'][l], p['ffn_conv_w'][l], p['ffn_conv_b'][l],
                      p['ffn_w_down'][l], p['ln2_g'][l], p['ln2_b'][l], alpha)
    return x


def kernel(x_prompt, x_sample, c_prompt, c_sample, ada_w, ada_b, w_in, q_gain, k_gain, lru_conv_w, lru_conv_b, lru_wa, lru_ba, lru_wx, lru_bx, lru_lambda, hy_conv_w, hy_conv_b, hy_w1, hy_b1, hy_w2, hy_b2, hy_w3, hy_b3, hy_freq, hy_wout, hy_bias, w_out, ln1_g, ln1_b, ffn_w_up, ffn_conv_w, ffn_conv_b, ffn_w_down, ln2_g, ln2_b):
    p = dict(
        w_in=w_in.astype(BF16), q_gain=q_gain, k_gain=k_gain, lru_conv_w=lru_conv_w, lru_conv_b=lru_conv_b,
        lru_wa=lru_wa, lru_ba=lru_ba, lru_wx=lru_wx, lru_bx=lru_bx, lru_lambda=lru_lambda,
        hy_conv_w=hy_conv_w, hy_conv_b=hy_conv_b, hy_w1=hy_w1, hy_b1=hy_b1, hy_w2=hy_w2, hy_b2=hy_b2,
        hy_w3=hy_w3, hy_b3=hy_b3, hy_freq=hy_freq, hy_wout=hy_wout, hy_bias=hy_bias,
        w_out=w_out.astype(BF16), ln1_g=ln1_g, ln1_b=ln1_b, ffn_w_up=ffn_w_up.astype(BF16),
        ffn_conv_w=ffn_conv_w, ffn_conv_b=ffn_conv_b, ffn_w_down=ffn_w_down.astype(BF16), ln2_g=ln2_g, ln2_b=ln2_b,
    )
    nb = x_prompt.shape[0]
    mod = _ada_mod(jnp.concatenate([c_prompt, c_sample], axis=0), ada_w, ada_b)
    y_prompt = _trunk(x_prompt, mod[:, :nb], p)
    y_sample = _trunk(x_sample, mod[:, nb:], p)
    return (y_prompt, y_sample)
```

```python
import functools
import math

import numpy as np
import jax
import jax.numpy as jnp
from jax import lax
from jax.experimental import pallas as pl
from jax.experimental.pallas import tpu as pltpu

F32 = jnp.float32
BF16 = jnp.bfloat16
HIGHEST = lax.Precision.HIGHEST

GRID_W = 64
HEAD_DIM = 64
N_HEADS = 8
N_KV_HEADS = 2
KV_GROUP = N_HEADS // N_KV_HEADS
ROPE_THETA = 10000.0
ROPE_FREQS = HEAD_DIM // 4
QK_EPS = 1e-6
LRU_WIDTH = 256
LRU_HEADS = 4
LRU_C = 8.0
HY_WIDTH = 256
HY_ORDER = 2
HY_BANDS = 16
HY_MIN_DECAY = abs(math.log(1e-2)) / 1.5
HY_MAX_DECAY = abs(math.log(1e-2)) / 0.3
LN_EPS = 1e-5
IN_Q = N_HEADS * HEAD_DIM
IN_KV = N_KV_HEADS * HEAD_DIM

LANES = 128
SUBLANES = 8
DFT_N2 = 128
VMEM_LIMIT = 48 * 1024 * 1024


def _cparams(sem):
    return pltpu.CompilerParams(dimension_semantics=sem, vmem_limit_bytes=VMEM_LIMIT)


def _dot(a, b):
    return jnp.dot(a, b, preferred_element_type=F32)


def _layer_norm(y, g, b):
    mu = jnp.mean(y, axis=-1, keepdims=True)
    yc = y - mu
    var = jnp.mean(yc * yc, axis=-1, keepdims=True)
    return yc * lax.rsqrt(var + LN_EPS) * g + b


def _ada_kernel(c_ref, w_ref, b_ref, o_ref):
    c = c_ref[...]
    s = c * jax.nn.sigmoid(c)
    o_ref[0] = jnp.dot(s, w_ref[0], precision=HIGHEST, preferred_element_type=F32) + b_ref[0]


def _ada_mod(c_all, ada_w, ada_b):
    depth, d, n = ada_w.shape
    rows = c_all.shape[0]
    tn = 768
    return pl.pallas_call(
        _ada_kernel,
        out_shape=jax.ShapeDtypeStruct((depth, rows, n), F32),
        grid=(depth, n // tn),
        in_specs=[
            pl.BlockSpec((rows, d), lambda l, j: (0, 0)),
            pl.BlockSpec((1, d, tn), lambda l, j: (l, 0, j)),
            pl.BlockSpec((1, 1, tn), lambda l, j: (l, 0, j)),
        ],
        out_specs=pl.BlockSpec((1, rows, tn), lambda l, j: (l, 0, j)),
        compiler_params=_cparams(("parallel", "parallel")),
        name="ada_mod",
    )(c_all, ada_w, ada_b.reshape(depth, 1, n))


def _rope_tables(L):
    rows = L // GRID_W
    row = np.repeat(np.arange(rows, dtype=np.float64), GRID_W)
    col = np.tile(np.arange(GRID_W, dtype=np.float64), rows)
    inv = ROPE_THETA ** (-np.arange(ROPE_FREQS, dtype=np.float64) / ROPE_FREQS)
    ar = row[:, None] * inv
    ac = col[:, None] * inv
    zeros = np.zeros_like(ar)
    cos = np.concatenate([np.cos(ar), np.cos(ar), np.cos(ac), np.cos(ac)], axis=1)
    sin_up = np.concatenate([-np.sin(ar), zeros, -np.sin(ac), zeros], axis=1)
    sin_dn = np.concatenate([zeros, np.sin(ar), zeros, np.sin(ac)], axis=1)
    two = lambda t: jnp.asarray(np.concatenate([t, t], axis=1), F32)
    return two(cos), two(sin_up), two(sin_dn)


def _inproj_kernel(x_ref, sc_ref, sh_ref, w_ref, qg_ref, kg_ref, cos_ref, sup_ref, sdn_ref, bd_ref,
                   q_ref, k_ref, v_ref, xg_ref, hy_ref):
    u = (x_ref[0] * (1.0 + sc_ref[0]) + sh_ref[0]).astype(BF16)
    proj = _dot(u, w_ref[...])
    cos = cos_ref[...]
    sup = sup_ref[...]
    sdn = sdn_ref[...]
    bd = bd_ref[...]
    half = ROPE_FREQS

    def norm_rope(t, gain):
        sq = t * t
        hi = sq.astype(BF16)
        lo = (sq - hi.astype(F32)).astype(BF16)
        ms = _dot(hi, bd) + _dot(lo, bd)
        tn = t * lax.rsqrt(ms + QK_EPS) * gain
        return (tn * cos + pltpu.roll(tn, LANES - half, 1) * sup + pltpu.roll(tn, half, 1) * sdn)

    qg = qg_ref[...]
    for j in range(IN_Q // LANES):
        sl = slice(j * LANES, (j + 1) * LANES)
        q_ref[0, :, sl] = (norm_rope(proj[:, sl], qg) * (HEAD_DIM ** -0.5 * math.log2(math.e))).astype(BF16)
    k_ref[0] = norm_rope(proj[:, IN_Q:IN_Q + IN_KV], kg_ref[...]).astype(BF16)
    v_ref[0] = proj[:, IN_Q + IN_KV:IN_Q + 2 * IN_KV].astype(BF16)
    o = IN_Q + 2 * IN_KV
    xg_ref[0] = proj[:, o:o + 2 * LRU_WIDTH]
    hy_ref[0] = proj[:, o + 2 * LRU_WIDTH:]


def _in_proj(x, sc, sh, w_in, q_gain, k_gain, rope):
    B, L, D = x.shape
    n_in = w_in.shape[1]
    T = min(512, L)
    cos, sup, sdn = rope
    bd = jnp.asarray(np.kron(np.eye(2), np.full((HEAD_DIM, HEAD_DIM), 1.0 / HEAD_DIM)), BF16)
    qg = jnp.tile(q_gain, 2).reshape(1, LANES)
    kg = jnp.tile(k_gain, 2).reshape(1, LANES)
    n_hy = n_in - IN_Q - 2 * IN_KV - 2 * LRU_WIDTH
    row = lambda b, i: (b, i, 0)
    vec = lambda b, i: (b, 0, 0)
    tab = lambda b, i: (i, 0)
    const = lambda b, i: (0, 0)
    return pl.pallas_call(
        _inproj_kernel,
        out_shape=(
            jax.ShapeDtypeStruct((B, L, IN_Q), BF16),
            jax.ShapeDtypeStruct((B, L, IN_KV), BF16),
            jax.ShapeDtypeStruct((B, L, IN_KV), BF16),
            jax.ShapeDtypeStruct((B, L, 2 * LRU_WIDTH), F32),
            jax.ShapeDtypeStruct((B, L, n_hy), F32),
        ),
        grid=(B, L // T),
        in_specs=[
            pl.BlockSpec((1, T, D), row),
            pl.BlockSpec((1, 1, D), vec),
            pl.BlockSpec((1, 1, D), vec),
            pl.BlockSpec((D, n_in), const),
            pl.BlockSpec((1, LANES), const),
            pl.BlockSpec((1, LANES), const),
            pl.BlockSpec((T, LANES), tab),
            pl.BlockSpec((T, LANES), tab),
            pl.BlockSpec((T, LANES), tab),
            pl.BlockSpec((LANES, LANES), const),
        ],
        out_specs=(
            pl.BlockSpec((1, T, IN_Q), row),
            pl.BlockSpec((1, T, IN_KV), row),
            pl.BlockSpec((1, T, IN_KV), row),
            pl.BlockSpec((1, T, 2 * LRU_WIDTH), row),
            pl.BlockSpec((1, T, n_hy), row),
        ),
        compiler_params=_cparams(("parallel", "parallel")),
        name="in_proj",
    )(x, sc, sh, w_in, qg, kg, cos, sup, sdn, bd)


V_ROWS = HEAD_DIM + 16


def _attn_kernel(qt_ref, k_ref, vt_ref, o_ref, acc_sc, s_sc, *, tk, nk):
    tq = qt_ref.shape[-1]
    acc_sc[...] = jnp.zeros(acc_sc.shape, F32)

    def scores(c, slot):
        kc = k_ref[0, 0, pl.ds(pl.multiple_of(c * tk, tk), tk), :]
        for h in range(KV_GROUP):
            s_sc[slot, h] = _dot(kc, qt_ref[0, 0, h])

    def consume(c, slot, ms):
        vc = vt_ref[0, 0, c]
        new = []
        for h in range(KV_GROUP):
            s = s_sc[slot, h]
            m_new = jnp.maximum(ms[h], jnp.max(s, axis=0, keepdims=True))
            p = jnp.exp2(s - m_new).astype(BF16)
            alpha = jnp.exp2(ms[h] - m_new)
            acc_sc[h] = alpha * acc_sc[h] + _dot(vc, p)
            new.append(m_new)
        return tuple(new)

    def pair(c2, ms):
        c = 2 * c2
        scores(c + 1, 1)
        ms = consume(c, 0, ms)
        scores(jnp.minimum(c + 2, nk - 1), 0)
        return consume(c + 1, 1, ms)

    scores(0, 0)
    m0 = tuple(jnp.full((1, tq), -jnp.inf, F32) for _ in range(KV_GROUP))
    lax.fori_loop(0, nk // 2, pair, m0)
    for h in range(KV_GROUP):
        acc = acc_sc[h]
        o_ref[0, 0, h] = (acc[:HEAD_DIM] / acc[HEAD_DIM:HEAD_DIM + 1]).astype(o_ref.dtype)


def _attention(q, k, v):
    B, L, _ = q.shape
    tq = min(256, L)
    tk = min(512, L // 2)
    nk = L // tk
    assert nk % 2 == 0
    qt = q.reshape(B, L, N_KV_HEADS, KV_GROUP, HEAD_DIM).transpose(0, 2, 3, 4, 1)
    kh = k.reshape(B, L, N_KV_HEADS, HEAD_DIM).transpose(0, 2, 1, 3)
    vt = v.reshape(B, nk, tk, N_KV_HEADS, HEAD_DIM).transpose(0, 3, 1, 4, 2)
    ones = jnp.ones((B, N_KV_HEADS, nk, 1, tk), BF16)
    pad = jnp.zeros((B, N_KV_HEADS, nk, V_ROWS - HEAD_DIM - 1, tk), BF16)
    vt = jnp.concatenate([vt, ones, pad], axis=3)
    ot = pl.pallas_call(
        functools.partial(_attn_kernel, tk=tk, nk=nk),
        out_shape=jax.ShapeDtypeStruct((B, N_KV_HEADS, KV_GROUP, HEAD_DIM, L), BF16),
        grid=(B, N_KV_HEADS, L // tq),
        in_specs=[
            pl.BlockSpec((1, 1, KV_GROUP, HEAD_DIM, tq), lambda b, g, i: (b, g, 0, 0, i)),
            pl.BlockSpec((1, 1, L, HEAD_DIM), lambda b, g, i: (b, g, 0, 0)),
            pl.BlockSpec((1, 1, nk, V_ROWS, tk), lambda b, g, i: (b, g, 0, 0, 0)),
        ],
        out_specs=pl.BlockSpec((1, 1, KV_GROUP, HEAD_DIM, tq), lambda b, g, i: (b, g, 0, 0, i)),
        scratch_shapes=[pltpu.VMEM((KV_GROUP, V_ROWS, tq), F32), pltpu.VMEM((2, KV_GROUP, tk, tq), F32)],
        compiler_params=_cparams(("parallel", "parallel", "parallel")),
        name="attention",
    )(qt, kh, vt)
    return ot.transpose(0, 4, 1, 2, 3).reshape(B, L, IN_Q)


def _lru_kernel(xp_ref, x_ref, xn_ref, cw_ref, cb_ref, wa_ref, ba_ref, wx_ref, bx_ref, lam_ref,
                o_ref, xe_sc, a_sc, b_sc, h_sc, *, T, nt):
    d = pl.program_id(0)
    i = pl.program_id(1)
    tile = jnp.where(d == 0, i, nt - 1 - i)
    nb, _, C = x_ref.shape
    H = SUBLANES
    xe_sc[:, H:H + T, :] = x_ref[...]
    xe_sc[:, 0:H, :] = jnp.where(tile > 0, xp_ref[...], 0.0)
    xe_sc[:, H + T:2 * H + T, :] = jnp.where(tile < nt - 1, xn_ref[...], 0.0)
    cw = cw_ref[...]
    xc = cb_ref[...].reshape(1, 1, C)
    for k in range(cw.shape[0]):
        xc = xc + xe_sc[:, H - 2 + k:H - 2 + k + T, :] * cw[k].reshape(1, 1, C)
    xc2 = xc.reshape(nb * T, C)
    xb = xc2.astype(BF16)
    r = jax.nn.sigmoid(_dot(xb, wa_ref[0]) + ba_ref[0])
    ig = jax.nn.sigmoid(_dot(xb, wx_ref[0]) + bx_ref[0])
    lam = lam_ref[0]
    softplus_neg = jnp.maximum(-lam, 0.0) + jnp.log1p(jnp.exp(-jnp.abs(lam)))
    log_a = -LRU_C * r * softplus_neg
    a = jnp.exp(log_a)
    bb = jnp.sqrt(-jnp.tanh(log_a) * (1.0 + a * a)) * (ig * xc2)
    a_sc[...] = a.reshape(nb, T, C)
    b_sc[...] = bb.reshape(nb, T, C)

    @pl.when(i == 0)
    def _():
        h_sc[...] = jnp.zeros(h_sc.shape, F32)

    def body(s, h):
        t = jnp.where(d == 0, s, T - 1 - s)
        h = a_sc[:, t, :] * h + b_sc[:, t, :]
        o_ref[0, :, t, :] = h
        return h

    h_sc[...] = lax.fori_loop(0, T, body, h_sc[...])


def _rglru(xg, conv_w, conv_b, wa, ba, wx, bx, lam):
    B, L, _ = xg.shape
    C = LRU_WIDTH
    assert B == SUBLANES
    T = min(256, L)
    nt = L // T
    hb = T // SUBLANES
    n_halo = L // SUBLANES

    def tile_of(d, i):
        return jnp.where(d == 0, i, nt - 1 - i)

    def blockdiag(w):
        eye = jnp.eye(LRU_HEADS, dtype=w.dtype)
        return jnp.einsum('dhij,hg->dhigj', w, eye).reshape(2, C, C).astype(BF16)

    kern = functools.partial(_lru_kernel, T=T, nt=nt)
    return pl.pallas_call(
        kern,
        out_shape=jax.ShapeDtypeStruct((2, B, L, C), F32),
        grid=(2, nt),
        in_specs=[
            pl.BlockSpec((B, SUBLANES, C), lambda d, i: (0, jnp.maximum(tile_of(d, i) * hb - 1, 0), 0)),
            pl.BlockSpec((B, T, C), lambda d, i: (0, tile_of(d, i), 0)),
            pl.BlockSpec((B, SUBLANES, C), lambda d, i: (0, jnp.minimum((tile_of(d, i) + 1) * hb, n_halo - 1), 0)),
            pl.BlockSpec(conv_w.shape, lambda d, i: (0, 0)),
            pl.BlockSpec((1, C), lambda d, i: (0, 0)),
            pl.BlockSpec((1, C, C), lambda d, i: (d, 0, 0)),
            pl.BlockSpec((1, 1, C), lambda d, i: (d, 0, 0)),
            pl.BlockSpec((1, C, C), lambda d, i: (d, 0, 0)),
            pl.BlockSpec((1, 1, C), lambda d, i: (d, 0, 0)),
            pl.BlockSpec((1, 1, C), lambda d, i: (d, 0, 0)),
        ],
        out_specs=pl.BlockSpec((1, B, T, C), lambda d, i: (d, 0, tile_of(d, i), 0)),
        scratch_shapes=[
            pltpu.VMEM((B, T + 2 * SUBLANES, C), F32),
            pltpu.VMEM((B, T, C), F32),
            pltpu.VMEM((B, T, C), F32),
            pltpu.VMEM((B, C), F32),
        ],
        compiler_params=_cparams(("arbitrary", "arbitrary")),
        name="rglru",
    )(xg, xg, xg, conv_w, conv_b.reshape(1, C), blockdiag(wa), ba.reshape(2, 1, C),
      blockdiag(wx), bx.reshape(2, 1, C), lam.reshape(2, 1, C))


def _shortconv_kernel(xp_ref, x_ref, xn_ref, w_ref, b_ref, o_ref, xe_sc, *, T, nt):
    i = pl.program_id(1)
    H = SUBLANES
    xe_sc[H:H + T, :] = x_ref[0]
    xe_sc[0:H, :] = jnp.where(i > 0, xp_ref[0], 0.0)
    xe_sc[H + T:2 * H + T, :] = jnp.where(i < nt - 1, xn_ref[0], 0.0)
    w = w_ref[...]
    y = b_ref[...]
    for k in range(w.shape[0]):
        y = y + xe_sc[H - 1 + k:H - 1 + k + T, :] * w[k:k + 1]
    o_ref[0] = y


def _short_conv(x, w, b):
    B, L, C = x.shape
    T = min(512, L)
    nt = L // T
    hb = T // SUBLANES
    n_halo = L // SUBLANES
    return pl.pallas_call(
        functools.partial(_shortconv_kernel, T=T, nt=nt),
        out_shape=jax.ShapeDtypeStruct((B, L, C), F32),
        grid=(B, nt),
        in_specs=[
            pl.BlockSpec((1, SUBLANES, C), lambda b_, i: (b_, jnp.maximum(i * hb - 1, 0), 0)),
            pl.BlockSpec((1, T, C), lambda b_, i: (b_, i, 0)),
            pl.BlockSpec((1, SUBLANES, C), lambda b_, i: (b_, jnp.minimum((i + 1) * hb, n_halo - 1), 0)),
            pl.BlockSpec(w.shape, lambda b_, i: (0, 0)),
            pl.BlockSpec((1, C), lambda b_, i: (0, 0)),
        ],
        out_specs=pl.BlockSpec((1, T, C), lambda b_, i: (b_, i, 0)),
        scratch_shapes=[pltpu.VMEM((T + 2 * SUBLANES, C), F32)],
        compiler_params=_cparams(("parallel", "parallel")),
        name="hy_short_conv",
    )(x, x, x, w, b.reshape(1, C))


def _filter_positions(L):
    t = np.linspace(0.0, 1.0, L)[:, None]
    w = 2.0 * math.pi * np.arange(L, dtype=np.float64)[:, None] / L
    f = np.linspace(1e-4, HY_BANDS - 1, HY_BANDS)[None, :]
    z = np.concatenate([t, np.cos(f * w), -np.sin(f * w)], axis=-1)
    zrev = np.concatenate([z[:1], z[:0:-1]], axis=0)
    return jnp.asarray(z, F32), jnp.asarray(zrev, F32)


def _filter_kernel(z_ref, zr_ref, w1_ref, b1_ref, w2_ref, b2_ref, w3_ref, b3_ref, fr_ref, wo_ref, dl_ref,
                   fs_ref, asum_ref, *, T):
    i = pl.program_id(0)
    fr = fr_ref[...]
    hd = lambda a, b: jnp.dot(a, b, precision=HIGHEST, preferred_element_type=F32)

    def mlp(z):
        h = jnp.sin(fr * (hd(z, w1_ref[...]) + b1_ref[...]))
        h = jnp.sin(fr * (hd(h, w2_ref[...]) + b2_ref[...]))
        h = jnp.sin(fr * (hd(h, w3_ref[...]) + b3_ref[...]))
        return hd(h, wo_ref[...])

    z = z_ref[...]
    zr = zr_ref[...]
    kz = mlp(z)
    kr = mlp(zr)
    dl = dl_ref[...]
    dec_f = jnp.exp(-z[:, 0:1] * dl)
    dec_r = jnp.exp(-zr[:, 0:1] * dl)
    C = HY_WIDTH
    first_row = (i * T + lax.broadcasted_iota(jnp.int32, (T, 1), 0)) == 0

    @pl.when(i == 0)
    def _():
        asum_ref[...] = jnp.zeros(asum_ref.shape, F32)

    for o in range(HY_ORDER):
        base = o * 2 * C
        kf = kz[:, base:base + C] * dec_f
        kb_here = kz[:, base + C:base + 2 * C] * dec_f
        kb_rev = kr[:, base + C:base + 2 * C] * dec_r
        first = kf + jnp.where(first_row, kb_here, 0.0)
        second = jnp.where(first_row, 0.0, kb_rev)
        fs_ref[o] = first
        fs_ref[HY_ORDER + o] = second
        asum_ref[o:o + 1, :] += jnp.sum(jnp.abs(first) + jnp.abs(second), axis=0, keepdims=True)


def _hyena_filter_time(L, w1, b1, w2, b2, w3, b3, freq, wout):
    z, zrev = _filter_positions(L)
    T = min(512, L)
    C = HY_WIDTH
    nemb = z.shape[1]
    nf = w2.shape[0]
    deltas = jnp.asarray(np.linspace(HY_MIN_DECAY, HY_MAX_DECAY, C)[None, :], F32)
    const = lambda i: (0, 0)
    return pl.pallas_call(
        functools.partial(_filter_kernel, T=T),
        out_shape=(jax.ShapeDtypeStruct((2 * HY_ORDER, L, C), F32), jax.ShapeDtypeStruct((HY_ORDER, C), F32)),
        grid=(L // T,),
        in_specs=[
            pl.BlockSpec((T, nemb), lambda i: (i, 0)),
            pl.BlockSpec((T, nemb), lambda i: (i, 0)),
            pl.BlockSpec((nemb, nf), const), pl.BlockSpec((1, nf), const),
            pl.BlockSpec((nf, nf), const), pl.BlockSpec((1, nf), const),
            pl.BlockSpec((nf, nf), const), pl.BlockSpec((1, nf), const),
            pl.BlockSpec((1, nf), const),
            pl.BlockSpec(wout.shape, const),
            pl.BlockSpec((1, C), const),
        ],
        out_specs=(pl.BlockSpec((2 * HY_ORDER, T, C), lambda i: (0, i, 0)),
                   pl.BlockSpec((HY_ORDER, C), const)),
        compiler_params=_cparams(("arbitrary",)),
        name="hy_filter_mlp",
    )(z, zrev, w1, b1.reshape(1, nf), w2, b2.reshape(1, nf), w3, b3.reshape(1, nf), freq.reshape(1, nf), wout, deltas)


def _dft_tables(L):
    N = 2 * L
    N2 = DFT_N2
    N1 = N // N2
    K1 = N1 // 2
    n1 = np.arange(K1)[None, None, :]
    k1 = np.arange(N1)[None, :, None]
    n2 = np.arange(N2)[:, None, None]
    ang = 2.0 * np.pi * (((n1 * N2 + n2) * k1) % N) / N
    g_fwd = np.concatenate([np.cos(ang), -np.sin(ang)], axis=1)
    angt = np.transpose(ang, (0, 2, 1))
    g_inv = np.concatenate([np.cos(angt), -np.sin(angt)], axis=2) / N
    a2 = 2.0 * np.pi * ((np.arange(N2)[:, None] * np.arange(N2)[None, :]) % N2) / N2
    cr, ci = np.cos(a2), -np.sin(a2)
    m2 = np.block([[cr, -ci], [ci, cr]])
    m2i = np.block([[cr, ci], [-ci, cr]])
    as_bf = lambda a: jnp.asarray(a, BF16)
    return as_bf(g_fwd), as_bf(g_inv), as_bf(m2), as_bf(m2i)


def _dft_a_kernel(x_ref, g_ref, o_ref):
    n1 = o_ref.shape[1]
    for j in range(SUBLANES):
        xj = x_ref[0, :, j, :].astype(BF16)
        r = _dot(g_ref[j], xj)
        o_ref[0, :, 0, j, :] = r[:n1]
        o_ref[0, :, 1, j, :] = r[n1:]


def _dft_stage_a(x4, lane_block, g_fwd, C):
    S, K1, N2, _ = x4.shape
    N1 = 2 * K1
    return pl.pallas_call(
        _dft_a_kernel,
        out_shape=jax.ShapeDtypeStruct((S, N1, 2, N2, C), F32),
        grid=(S, N2 // SUBLANES),
        in_specs=[
            pl.BlockSpec((1, K1, SUBLANES, C), lambda s, j: (s, 0, j, lane_block)),
            pl.BlockSpec((SUBLANES, 2 * N1, K1), lambda s, j: (j, 0, 0)),
        ],
        out_specs=pl.BlockSpec((1, N1, 2, SUBLANES, C), lambda s, j: (s, 0, 0, j, 0)),
        compiler_params=_cparams(("parallel", "parallel")),
        name="hy_dft_a",
    )(x4, g_fwd)


def _dft_b_filter_kernel(s1_ref, s2_ref, m2_ref, asum_ref, o_ref, *, tk1):
    n2 = s1_ref.shape[3]
    C = s1_ref.shape[4]
    scale = 1.0 / asum_ref[0]
    for j in range(tk1):
        sign = 1.0 if j % 2 == 0 else -1.0
        s = s1_ref[0, j] + sign * s2_ref[0, j]
        xk = _dot(m2_ref[...], s.reshape(2 * n2, C).astype(BF16)) * scale
        o_ref[0, j] = xk.reshape(2, n2, C)


def _dft_stage_b_filter(sa, m2, asum):
    _, N1, _, N2, C = sa.shape
    tk1 = min(8, N1)
    blk = (1, tk1, 2, N2, C)
    return pl.pallas_call(
        functools.partial(_dft_b_filter_kernel, tk1=tk1),
        out_shape=jax.ShapeDtypeStruct((HY_ORDER, N1, 2, N2, C), F32),
        grid=(HY_ORDER, N1 // tk1),
        in_specs=[
            pl.BlockSpec(blk, lambda o, i: (o, i, 0, 0, 0)),
            pl.BlockSpec(blk, lambda o, i: (HY_ORDER + o, i, 0, 0, 0)),
            pl.BlockSpec((2 * N2, 2 * N2), lambda o, i: (0, 0)),
            pl.BlockSpec((1, 1, C), lambda o, i: (o, 0, 0)),
        ],
        out_specs=pl.BlockSpec(blk, lambda o, i: (o, i, 0, 0, 0)),
        compiler_params=_cparams(("parallel", "parallel")),
        name="hy_dft_b_filter",
    )(sa, sa, m2, asum.reshape(HY_ORDER, 1, C))


def _dft_b_conv_kernel(s_ref, kf_ref, m2_ref, m2i_ref, o_ref, *, tk1):
    n2 = s_ref.shape[3]
    C = s_ref.shape[4]
    for j in range(tk1):
        x = _dot(m2_ref[...], s_ref[0, j].reshape(2 * n2, C).astype(BF16))
        xr, xi = x[:n2], x[n2:]
        kr = kf_ref[0, j, 0]
        ki = kf_ref[0, j, 1]
        y = jnp.concatenate([xr * kr - xi * ki, xr * ki + xi * kr], axis=0).astype(BF16)
        o_ref[0, j] = _dot(m2i_ref[...], y).reshape(2, n2, C)


def _dft_stage_b_conv(sa, kspec, order, m2, m2i):
    S, N1, _, N2, C = sa.shape
    tk1 = min(8, N1)
    blk = (1, tk1, 2, N2, C)
    return pl.pallas_call(
        functools.partial(_dft_b_conv_kernel, tk1=tk1),
        out_shape=jax.ShapeDtypeStruct(sa.shape, F32),
        grid=(S, N1 // tk1),
        in_specs=[
            pl.BlockSpec(blk, lambda s, i: (s, i, 0, 0, 0)),
            pl.BlockSpec(blk, lambda s, i: (order, i, 0, 0, 0)),
            pl.BlockSpec((2 * N2, 2 * N2), lambda s, i: (0, 0)),
            pl.BlockSpec((2 * N2, 2 * N2), lambda s, i: (0, 0)),
        ],
        out_specs=pl.BlockSpec(blk, lambda s, i: (s, i, 0, 0, 0)),
        compiler_params=_cparams(("parallel", "parallel")),
        name="hy_dft_b_conv",
    )(sa, kspec, m2, m2i)


def _dft_c_kernel(c_ref, gi_ref, z_ref, g_ref, d_ref, o_ref):
    for j in range(SUBLANES):
        cat = jnp.concatenate([c_ref[0, :, 0, j, :], c_ref[0, :, 1, j, :]], axis=0).astype(BF16)
        y = _dot(gi_ref[j], cat)
        o_ref[0, :, j, :] = g_ref[0, :, j, :] * (y + z_ref[0, :, j, :] * d_ref[0])


def _dft_stage_c(sc, g_inv, z4, z_block, gate4, gate_block, d):
    S, N1, _, N2, C = sc.shape
    K1 = N1 // 2
    seq = lambda blk: pl.BlockSpec((1, K1, SUBLANES, C), lambda s, j: (s, 0, j, blk))
    return pl.pallas_call(
        _dft_c_kernel,
        out_shape=jax.ShapeDtypeStruct((S, K1, N2, C), F32),
        grid=(S, N2 // SUBLANES),
        in_specs=[
            pl.BlockSpec((1, N1, 2, SUBLANES, C), lambda s, j: (s, 0, 0, j, 0)),
            pl.BlockSpec((SUBLANES, K1, 2 * N1), lambda s, j: (j, 0, 0)),
            seq(z_block),
            seq(gate_block),
            pl.BlockSpec((1, 1, C), lambda s, j: (0, 0, 0)),
        ],
        out_specs=seq(0),
        compiler_params=_cparams(("parallel", "parallel")),
        name="hy_dft_c",
    )(sc, g_inv, z4, gate4, d.reshape(1, 1, C))


def _hyena(hy, conv_w, conv_b, w1, b1, w2, b2, w3, b3, freq, wout, bias, tables):
    B, L, _ = hy.shape
    C = HY_WIDTH
    g_fwd, g_inv, m2, m2i = tables
    N2 = DFT_N2
    K1 = L // N2
    fs, asum = _hyena_filter_time(L, w1, b1, w2, b2, w3, b3, freq, wout)
    fa = _dft_stage_a(fs.reshape(2 * HY_ORDER, K1, N2, C), 0, g_fwd, C)
    kspec = _dft_stage_b_filter(fa, m2, asum)
    hc = _short_conv(hy, conv_w, conv_b)
    hc4 = hc.reshape(B, K1, N2, (HY_ORDER + 1) * C)
    z4, z_block = hc4, 0
    for o in range(HY_ORDER):
        sa = _dft_stage_a(z4, z_block, g_fwd, C)
        sc = _dft_stage_b_conv(sa, kspec, o, m2, m2i)
        z4 = _dft_stage_c(sc, g_inv, z4, z_block, hc4, 1 + o, bias[o])
        z_block = 0
    return z4.reshape(B, L, C)


def _outproj_kernel(x_ref, at_ref, h_ref, gr_ref, zy_ref, w_ref, gt_ref, g_ref, b_ref, o_ref, *, alpha):
    lru = (h_ref[0, 0] + h_ref[1, 0]) * jax.nn.gelu(gr_ref[0])
    na = at_ref.shape[2]
    nl = lru.shape[1]
    m = (_dot(at_ref[0], w_ref[0:na]) + _dot(lru.astype(BF16), w_ref[na:na + nl])
         + _dot(zy_ref[0].astype(BF16), w_ref[na + nl:]))
    o_ref[0] = _layer_norm(alpha * x_ref[0] + gt_ref[0] * m, g_ref[...], b_ref[...])


def _out_proj(x, attn, h, xg, zy, w_out, gt, ln_g, ln_b, alpha):
    B, L, D = x.shape
    T = min(512, L)
    C = LRU_WIDTH
    row = lambda b, i: (b, i, 0)
    vec = lambda b, i: (b, 0, 0)
    const = lambda b, i: (0, 0)
    return pl.pallas_call(
        functools.partial(_outproj_kernel, alpha=alpha),
        out_shape=jax.ShapeDtypeStruct((B, L, D), F32),
        grid=(B, L // T),
        in_specs=[
            pl.BlockSpec((1, T, D), row),
            pl.BlockSpec((1, T, IN_Q), row),
            pl.BlockSpec((2, 1, T, C), lambda b, i: (0, b, i, 0)),
            pl.BlockSpec((1, T, C), lambda b, i: (b, i, 1)),
            pl.BlockSpec((1, T, HY_WIDTH), row),
            pl.BlockSpec(w_out.shape, const),
            pl.BlockSpec((1, 1, D), vec),
            pl.BlockSpec((1, D), const),
            pl.BlockSpec((1, D), const),
        ],
        out_specs=pl.BlockSpec((1, T, D), row),
        compiler_params=_cparams(("parallel", "parallel")),
        name="out_proj",
    )(x, attn, h, xg, zy, w_out, gt, ln_g.reshape(1, D), ln_b.reshape(1, D))


FFN_CHUNK = 256


def _ffn_kernel(xp_ref, x_ref, xn_ref, sc_ref, sh_ref, gt_ref, wu_ref, cw_ref, cb_ref, wd_ref, g_ref, b_ref,
                o_ref, u_sc, h_sc, acc_sc, *, T, nt, nf, alpha):
    i = pl.program_id(1)
    H = SUBLANES
    R = T + 2 * H
    F = FFN_CHUNK
    sc = 1.0 + sc_ref[0]
    sh = sh_ref[0]
    u_sc[H:H + T, :] = (x_ref[0] * sc + sh).astype(BF16)
    u_sc[0:H, :] = jnp.where(i > 0, xp_ref[0] * sc + sh, 0.0).astype(BF16)
    u_sc[H + T:R, :] = jnp.where(i < nt - 1, xn_ref[0] * sc + sh, 0.0).astype(BF16)
    acc_sc[...] = jnp.zeros(acc_sc.shape, F32)

    def up(f, slot):
        h_sc[slot] = _dot(u_sc[...], wu_ref[f])

    def down(f, slot):
        h = h_sc[slot]
        cw = cw_ref[f]
        y = cb_ref[f] + pltpu.roll(h, 1, 0) * cw[0:1] + h * cw[1:2] + pltpu.roll(h, R - 1, 0) * cw[2:3]
        y = y[H:H + T]
        act = (jax.nn.gelu(y[:, :F]) * y[:, F:]).astype(BF16)
        acc_sc[...] += _dot(act, wd_ref[f])

    def pair(c2, carry):
        f = 2 * c2
        up(f + 1, 1)
        down(f, 0)
        up(jnp.minimum(f + 2, nf - 1), 0)
        down(f + 1, 1)
        return carry

    up(0, 0)
    lax.fori_loop(0, nf // 2, pair, 0)
    if nf % 2 == 1:
        down(nf - 1, 0)
    o_ref[0] = _layer_norm(alpha * x_ref[0] + gt_ref[0] * acc_sc[...], g_ref[...], b_ref[...])


def _conv_ffn(x, sc, sh, gt, w_up, conv_w, conv_b, w_down, ln_g, ln_b, alpha):
    B, L, D = x.shape
    d_ff = w_down.shape[0]
    T = min(512, L)
    F = FFN_CHUNK
    nt = L // T
    nf = d_ff // F
    hb = T // SUBLANES
    n_halo = L // SUBLANES
    nk = conv_w.shape[0]
    chunked = lambda w: jnp.concatenate([w[..., :d_ff].reshape(w.shape[0], nf, F),
                                         w[..., d_ff:].reshape(w.shape[0], nf, F)], axis=-1).transpose(1, 0, 2)
    wu = chunked(w_up)
    cw = chunked(conv_w)
    cb = chunked(conv_b.reshape(1, 2 * d_ff))
    wd = w_down.reshape(nf, F, D)
    row = lambda b, i: (b, i, 0)
    vec = lambda b, i: (b, 0, 0)
    const2 = lambda b, i: (0, 0)
    const3 = lambda b, i: (0, 0, 0)
    resident = lambda shape, imap: pl.BlockSpec(shape, imap, pipeline_mode=pl.Buffered(1))
    return pl.pallas_call(
        functools.partial(_ffn_kernel, T=T, nt=nt, nf=nf, alpha=alpha),
        out_shape=jax.ShapeDtypeStruct((B, L, D), F32),
        grid=(B, nt),
        in_specs=[
            pl.BlockSpec((1, SUBLANES, D), lambda b, i: (b, jnp.maximum(i * hb - 1, 0), 0)),
            pl.BlockSpec((1, T, D), row),
            pl.BlockSpec((1, SUBLANES, D), lambda b, i: (b, jnp.minimum((i + 1) * hb, n_halo - 1), 0)),
            pl.BlockSpec((1, 1, D), vec),
            pl.BlockSpec((1, 1, D), vec),
            pl.BlockSpec((1, 1, D), vec),
            resident((nf, D, 2 * F), const3),
            resident((nf, nk, 2 * F), const3),
            resident((nf, 1, 2 * F), const3),
            resident((nf, F, D), const3),
            pl.BlockSpec((1, D), const2),
            pl.BlockSpec((1, D), const2),
        ],
        out_specs=pl.BlockSpec((1, T, D), row),
        scratch_shapes=[
            pltpu.VMEM((T + 2 * SUBLANES, D), BF16),
            pltpu.VMEM((2, T + 2 * SUBLANES, 2 * F), F32),
            pltpu.VMEM((T, D), F32),
        ],
        compiler_params=_cparams(("parallel", "parallel")),
        name="conv_ffn",
    )(x, x, x, sc, sh, gt, wu, cw, cb, wd, ln_g.reshape(1, D), ln_b.reshape(1, D))


def _trunk(x, mod, p):
    B, L, D = x.shape
    depth = mod.shape[0]
    alpha = (2 * depth) ** 0.25
    rope = _rope_tables(L)
    tables = _dft_tables(L)
    for l in range(depth):
        m6 = mod[l].reshape(B, 6, 1, D)
        sh1, sc1, gt1, sh2, sc2, gt2 = (m6[:, j] for j in range(6))
        q, k, v, xg, hy = _in_proj(x, sc1, sh1, p['w_in'][l], p['q_gain'][l], p['k_gain'][l], rope)
        attn = _attention(q, k, v)
        h = _rglru(xg, p['lru_conv_w'][l], p['lru_conv_b'][l], p['lru_wa'][l], p['lru_ba'][l],
                   p['lru_wx'][l], p['lru_bx'][l], p['lru_lambda'][l])
        zy = _hyena(hy, p['hy_conv_w'][l], p['hy_conv_b'][l], p['hy_w1'][l], p['hy_b1'][l], p['hy_w2'][l],
                    p['hy_b2'][l], p['hy_w3'][l], p['hy_b3'][l], p['hy_freq'][l], p['hy_wout'][l],
                    p['hy_bias'][l], tables)
        x = _out_proj(x, attn, h, xg, zy, p['w_out'][l], gt1, p['ln1_g'][l], p['ln1_b'][l], alpha)
        x = _conv_ffn(x, sc2, sh2, gt2, p['ffn_w_up'][l], p['ffn_conv_w'][l], p['ffn_conv_b'][l],
                      p['ffn_w_down'][l], p['ln2_g'][l], p['ln2_b'][l], alpha)
    return x


def kernel(x_prompt, x_sample, c_prompt, c_sample, ada_w, ada_b, w_in, q_gain, k_gain, lru_conv_w, lru_conv_b, lru_wa, lru_ba, lru_wx, lru_bx, lru_lambda, hy_conv_w, hy_conv_b, hy_w1, hy_b1, hy_w2, hy_b2, hy_w3, hy_b3, hy_freq, hy_wout, hy_bias, w_out, ln1_g, ln1_b, ffn_w_up, ffn_conv_w, ffn_conv_b, ffn_w_down, ln2_g, ln2_b):
    p = dict(
        w_in=w_in.astype(BF16), q_gain=q_gain, k_gain=k_gain, lru_conv_w=lru_conv_w, lru_conv_b=lru_conv_b,
        lru_wa=lru_wa, lru_ba=lru_ba, lru_wx=lru_wx, lru_bx=lru_bx, lru_lambda=lru_lambda,
        hy_conv_w=hy_conv_w, hy_conv_b=hy_conv_b, hy_w1=hy_w1, hy_b1=hy_b1, hy_w2=hy_w2, hy_b2=hy_b2,
        hy_w3=hy_w3, hy_b3=hy_b3, hy_freq=hy_freq, hy_wout=hy_wout, hy_bias=hy_bias,
        w_out=w_out.astype(BF16), ln1_g=ln1_g, ln1_b=ln1_b, ffn_w_up=ffn_w_up.astype(BF16),
        ffn_conv_w=ffn_conv_w, ffn_conv_b=ffn_conv_b, ffn_w_down=ffn_w_down.astype(BF16), ln2_g=ln2_g, ln2_b=ln2_b,
    )
    nb = x_prompt.shape[0]
    mod = _ada_mod(jnp.concatenate([c_prompt, c_sample], axis=0), ada_w, ada_b)
    y_prompt = _trunk(x_prompt, mod[:, :nb], p)
    y_sample = _trunk(x_sample, mod[:, nb:], p)
    return (y_prompt, y_sample)
```

```python
import functools
import math

import numpy as np
import jax
import jax.numpy as jnp
from jax import lax
from jax.experimental import pallas as pl
from jax.experimental.pallas import tpu as pltpu

F32 = jnp.float32
BF16 = jnp.bfloat16
HIGHEST = lax.Precision.HIGHEST

GRID_W = 64
HEAD_DIM = 64
N_HEADS = 8
N_KV_HEADS = 2
KV_GROUP = N_HEADS // N_KV_HEADS
ROPE_THETA = 10000.0
ROPE_FREQS = HEAD_DIM // 4
QK_EPS = 1e-6
LRU_WIDTH = 256
LRU_HEADS = 4
LRU_C = 8.0
HY_WIDTH = 256
HY_ORDER = 2
HY_BANDS = 16
HY_MIN_DECAY = abs(math.log(1e-2)) / 1.5
HY_MAX_DECAY = abs(math.log(1e-2)) / 0.3
LN_EPS = 1e-5
IN_Q = N_HEADS * HEAD_DIM
IN_KV = N_KV_HEADS * HEAD_DIM

LANES = 128
SUBLANES = 8
DFT_N2 = 128
VMEM_LIMIT = 48 * 1024 * 1024


def _cparams(sem):
    return pltpu.CompilerParams(dimension_semantics=sem, vmem_limit_bytes=VMEM_LIMIT)


def _dot(a, b):
    return jnp.dot(a, b, preferred_element_type=F32)


def _layer_norm(y, g, b):
    mu = jnp.mean(y, axis=-1, keepdims=True)
    yc = y - mu
    var = jnp.mean(yc * yc, axis=-1, keepdims=True)
    return yc * lax.rsqrt(var + LN_EPS) * g + b


def _ada_kernel(c_ref, w_ref, b_ref, o_ref):
    c = c_ref[...]
    s = c * jax.nn.sigmoid(c)
    o_ref[0] = jnp.dot(s, w_ref[0], precision=HIGHEST, preferred_element_type=F32) + b_ref[0]


def _ada_mod(c_all, ada_w, ada_b):
    depth, d, n = ada_w.shape
    rows = c_all.shape[0]
    tn = 768
    return pl.pallas_call(
        _ada_kernel,
        out_shape=jax.ShapeDtypeStruct((depth, rows, n), F32),
        grid=(depth, n // tn),
        in_specs=[
            pl.BlockSpec((rows, d), lambda l, j: (0, 0)),
            pl.BlockSpec((1, d, tn), lambda l, j: (l, 0, j)),
            pl.BlockSpec((1, 1, tn), lambda l, j: (l, 0, j)),
        ],
        out_specs=pl.BlockSpec((1, rows, tn), lambda l, j: (l, 0, j)),
        compiler_params=_cparams(("parallel", "parallel")),
        name="ada_mod",
    )(c_all, ada_w, ada_b.reshape(depth, 1, n))


def _rope_tables(L):
    rows = L // GRID_W
    row = np.repeat(np.arange(rows, dtype=np.float64), GRID_W)
    col = np.tile(np.arange(GRID_W, dtype=np.float64), rows)
    inv = ROPE_THETA ** (-np.arange(ROPE_FREQS, dtype=np.float64) / ROPE_FREQS)
    ar = row[:, None] * inv
    ac = col[:, None] * inv
    zeros = np.zeros_like(ar)
    cos = np.concatenate([np.cos(ar), np.cos(ar), np.cos(ac), np.cos(ac)], axis=1)
    sin_up = np.concatenate([-np.sin(ar), zeros, -np.sin(ac), zeros], axis=1)
    sin_dn = np.concatenate([zeros, np.sin(ar), zeros, np.sin(ac)], axis=1)
    two = lambda t: jnp.asarray(np.concatenate([t, t], axis=1), F32)
    return two(cos), two(sin_up), two(sin_dn)


def _inproj_kernel(x_ref, sc_ref, sh_ref, w_ref, qg_ref, kg_ref, cos_ref, sup_ref, sdn_ref, bd_ref,
                   q_ref, k_ref, v_ref, xg_ref, hy_ref):
    u = (x_ref[0] * (1.0 + sc_ref[0]) + sh_ref[0]).astype(BF16)
    proj = _dot(u, w_ref[...])
    cos = cos_ref[...]
    sup = sup_ref[...]
    sdn = sdn_ref[...]
    bd = bd_ref[...]
    half = ROPE_FREQS

    def norm_rope(t, gain):
        sq = t * t
        hi = sq.astype(BF16)
        lo = (sq - hi.astype(F32)).astype(BF16)
        ms = _dot(hi, bd) + _dot(lo, bd)
        tn = t * lax.rsqrt(ms + QK_EPS) * gain
        return (tn * cos + pltpu.roll(tn, LANES - half, 1) * sup + pltpu.roll(tn, half, 1) * sdn)

    qg = qg_ref[...]
    for j in range(IN_Q // LANES):
        sl = slice(j * LANES, (j + 1) * LANES)
        q_ref[0, :, sl] = (norm_rope(proj[:, sl], qg) * (HEAD_DIM ** -0.5 * math.log2(math.e))).astype(BF16)
    k_ref[0] = norm_rope(proj[:, IN_Q:IN_Q + IN_KV], kg_ref[...]).astype(BF16)
    v_ref[0] = proj[:, IN_Q + IN_KV:IN_Q + 2 * IN_KV].astype(BF16)
    o = IN_Q + 2 * IN_KV
    xg_ref[0] = proj[:, o:o + 2 * LRU_WIDTH]
    hy_ref[0] = proj[:, o + 2 * LRU_WIDTH:]


def _in_proj(x, sc, sh, w_in, q_gain, k_gain, rope):
    B, L, D = x.shape
    n_in = w_in.shape[1]
    T = min(512, L)
    cos, sup, sdn = rope
    bd = jnp.asarray(np.kron(np.eye(2), np.full((HEAD_DIM, HEAD_DIM), 1.0 / HEAD_DIM)), BF16)
    qg = jnp.tile(q_gain, 2).reshape(1, LANES)
    kg = jnp.tile(k_gain, 2).reshape(1, LANES)
    n_hy = n_in - IN_Q - 2 * IN_KV - 2 * LRU_WIDTH
    row = lambda b, i: (b, i, 0)
    vec = lambda b, i: (b, 0, 0)
    tab = lambda b, i: (i, 0)
    const = lambda b, i: (0, 0)
    return pl.pallas_call(
        _inproj_kernel,
        out_shape=(
            jax.ShapeDtypeStruct((B, L, IN_Q), BF16),
            jax.ShapeDtypeStruct((B, L, IN_KV), BF16),
            jax.ShapeDtypeStruct((B, L, IN_KV), BF16),
            jax.ShapeDtypeStruct((B, L, 2 * LRU_WIDTH), F32),
            jax.ShapeDtypeStruct((B, L, n_hy), F32),
        ),
        grid=(B, L // T),
        in_specs=[
            pl.BlockSpec((1, T, D), row),
            pl.BlockSpec((1, 1, D), vec),
            pl.BlockSpec((1, 1, D), vec),
            pl.BlockSpec((D, n_in), const),
            pl.BlockSpec((1, LANES), const),
            pl.BlockSpec((1, LANES), const),
            pl.BlockSpec((T, LANES), tab),
            pl.BlockSpec((T, LANES), tab),
            pl.BlockSpec((T, LANES), tab),
            pl.BlockSpec((LANES, LANES), const),
        ],
        out_specs=(
            pl.BlockSpec((1, T, IN_Q), row),
            pl.BlockSpec((1, T, IN_KV), row),
            pl.BlockSpec((1, T, IN_KV), row),
            pl.BlockSpec((1, T, 2 * LRU_WIDTH), row),
            pl.BlockSpec((1, T, n_hy), row),
        ),
        compiler_params=_cparams(("parallel", "parallel")),
        name="in_proj",
    )(x, sc, sh, w_in, qg, kg, cos, sup, sdn, bd)


V_ROWS = HEAD_DIM + 16


def _attn_kernel(qt_ref, k_ref, vt_ref, o_ref, acc_sc, s_sc, *, tk, nk):
    tq = qt_ref.shape[-1]
    acc_sc[...] = jnp.zeros(acc_sc.shape, F32)

    def scores(c, slot):
        kc = k_ref[0, 0, pl.ds(pl.multiple_of(c * tk, tk), tk), :]
        for h in range(KV_GROUP):
            s_sc[slot, h] = _dot(kc, qt_ref[0, 0, h])

    def consume(c, slot, ms):
        vc = vt_ref[0, 0, c]
        new = []
        for h in range(KV_GROUP):
            s = s_sc[slot, h]
            m_new = jnp.maximum(ms[h], jnp.max(s, axis=0, keepdims=True))
            p = jnp.exp2(s - m_new).astype(BF16)
            alpha = jnp.exp2(ms[h] - m_new)
            acc_sc[h] = alpha * acc_sc[h] + _dot(vc, p)
            new.append(m_new)
        return tuple(new)

    def pair(c2, ms):
        c = 2 * c2
        scores(c + 1, 1)
        ms = consume(c, 0, ms)
        scores(jnp.minimum(c + 2, nk - 1), 0)
        return consume(c + 1, 1, ms)

    scores(0, 0)
    m0 = tuple(jnp.full((1, tq), -jnp.inf, F32) for _ in range(KV_GROUP))
    lax.fori_loop(0, nk // 2, pair, m0)
    for h in range(KV_GROUP):
        acc = acc_sc[h]
        o_ref[0, 0, h] = (acc[:HEAD_DIM] / acc[HEAD_DIM:HEAD_DIM + 1]).astype(o_ref.dtype)


def _attention(q, k, v):
    B, L, _ = q.shape
    tq = min(256, L)
    tk = min(512, L // 2)
    nk = L // tk
    assert nk % 2 == 0
    qt = q.reshape(B, L, N_KV_HEADS, KV_GROUP, HEAD_DIM).transpose(0, 2, 3, 4, 1)
    kh = k.reshape(B, L, N_KV_HEADS, HEAD_DIM).transpose(0, 2, 1, 3)
    vt = v.reshape(B, nk, tk, N_KV_HEADS, HEAD_DIM).transpose(0, 3, 1, 4, 2)
    ones = jnp.ones((B, N_KV_HEADS, nk, 1, tk), BF16)
    pad = jnp.zeros((B, N_KV_HEADS, nk, V_ROWS - HEAD_DIM - 1, tk), BF16)
    vt = jnp.concatenate([vt, ones, pad], axis=3)
    ot = pl.pallas_call(
        functools.partial(_attn_kernel, tk=tk, nk=nk),
        out_shape=jax.ShapeDtypeStruct((B, N_KV_HEADS, KV_GROUP, HEAD_DIM, L), BF16),
        grid=(B, N_KV_HEADS, L // tq),
        in_specs=[
            pl.BlockSpec((1, 1, KV_GROUP, HEAD_DIM, tq), lambda b, g, i: (b, g, 0, 0, i)),
            pl.BlockSpec((1, 1, L, HEAD_DIM), lambda b, g, i: (b, g, 0, 0)),
            pl.BlockSpec((1, 1, nk, V_ROWS, tk), lambda b, g, i: (b, g, 0, 0, 0)),
        ],
        out_specs=pl.BlockSpec((1, 1, KV_GROUP, HEAD_DIM, tq), lambda b, g, i: (b, g, 0, 0, i)),
        scratch_shapes=[pltpu.VMEM((KV_GROUP, V_ROWS, tq), F32), pltpu.VMEM((2, KV_GROUP, tk, tq), F32)],
        compiler_params=_cparams(("parallel", "parallel", "parallel")),
        name="attention",
    )(qt, kh, vt)
    return ot.transpose(0, 4, 1, 2, 3).reshape(B, L, IN_Q)


def _lru_kernel(xp_ref, x_ref, xn_ref, cw_ref, cb_ref, wa_ref, ba_ref, wx_ref, bx_ref, lam_ref,
                o_ref, xe_sc, a_sc, b_sc, hs_sc, h_sc, *, T, nt):
    d = pl.program_id(0)
    i = pl.program_id(1)
    tile = jnp.where(d == 0, i, nt - 1 - i)
    nb, _, C = x_ref.shape
    nh = C // LANES
    H = SUBLANES
    keep_prev = (tile > 0).astype(F32)
    keep_next = (tile < nt - 1).astype(F32)
    for b in range(nb):
        for hf in range(nh):
            lanes = slice(hf * LANES, (hf + 1) * LANES)
            xe_sc[hf, pl.ds(b, H, stride=nb), :] = xp_ref[b, :, lanes] * keep_prev
            xe_sc[hf, pl.ds(H * nb + b, T, stride=nb), :] = x_ref[b, :, lanes]
            xe_sc[hf, pl.ds((H + T) * nb + b, H, stride=nb), :] = xn_ref[b, :, lanes] * keep_next
    cw = cw_ref[...]
    halves = []
    for hf in range(nh):
        lanes = slice(hf * LANES, (hf + 1) * LANES)
        acc = cb_ref[:, lanes]
        for k in range(cw.shape[0]):
            acc = acc + xe_sc[hf, (H - 2 + k) * nb:(H - 2 + k + T) * nb, :] * cw[k:k + 1, lanes]
        halves.append(acc)
    xc = jnp.concatenate(halves, axis=1)
    xb = xc.astype(BF16)
    sigmoid = lambda v: 0.5 * jnp.tanh(0.5 * v) + 0.5
    r = sigmoid(_dot(xb, wa_ref[0]) + ba_ref[0])
    ig = sigmoid(_dot(xb, wx_ref[0]) + bx_ref[0])
    lam = lam_ref[0]
    softplus_neg = jnp.maximum(-lam, 0.0) + jnp.log1p(jnp.exp(-jnp.abs(lam)))
    log_a = -LRU_C * r * softplus_neg
    a = jnp.exp(log_a)
    bb = jnp.sqrt(-jnp.tanh(log_a) * (1.0 + a * a)) * (ig * xc)
    for hf in range(nh):
        a_sc[hf] = a[:, hf * LANES:(hf + 1) * LANES]
        b_sc[hf] = bb[:, hf * LANES:(hf + 1) * LANES]

    @pl.when(i == 0)
    def _():
        h_sc[...] = jnp.zeros(h_sc.shape, F32)

    def body(s, hs):
        t = jnp.where(d == 0, s, T - 1 - s)
        rows = pl.ds(pl.multiple_of(t * nb, nb), nb)
        new = []
        for hf in range(nh):
            h = a_sc[hf, rows, :] * hs[hf] + b_sc[hf, rows, :]
            hs_sc[hf, rows, :] = h
            new.append(h)
        return tuple(new)

    hs = lax.fori_loop(0, T, body, tuple(h_sc[hf] for hf in range(nh)), unroll=8)
    for hf in range(nh):
        h_sc[hf] = hs[hf]
        for b in range(nb):
            o_ref[0, b, :, hf * LANES:(hf + 1) * LANES] = hs_sc[hf, pl.ds(b, T, stride=nb), :]


def _rglru(xg, conv_w, conv_b, wa, ba, wx, bx, lam):
    B, L, _ = xg.shape
    C = LRU_WIDTH
    assert B == SUBLANES
    T = min(256, L)
    nt = L // T
    hb = T // SUBLANES
    n_halo = L // SUBLANES

    def tile_of(d, i):
        return jnp.where(d == 0, i, nt - 1 - i)

    def blockdiag(w):
        eye = jnp.eye(LRU_HEADS, dtype=w.dtype)
        return jnp.einsum('dhij,hg->dhigj', w, eye).reshape(2, C, C).astype(BF16)

    kern = functools.partial(_lru_kernel, T=T, nt=nt)
    return pl.pallas_call(
        kern,
        out_shape=jax.ShapeDtypeStruct((2, B, L, C), F32),
        grid=(2, nt),
        in_specs=[
            pl.BlockSpec((B, SUBLANES, C), lambda d, i: (0, jnp.maximum(tile_of(d, i) * hb - 1, 0), 0)),
            pl.BlockSpec((B, T, C), lambda d, i: (0, tile_of(d, i), 0)),
            pl.BlockSpec((B, SUBLANES, C), lambda d, i: (0, jnp.minimum((tile_of(d, i) + 1) * hb, n_halo - 1), 0)),
            pl.BlockSpec(conv_w.shape, lambda d, i: (0, 0)),
            pl.BlockSpec((1, C), lambda d, i: (0, 0)),
            pl.BlockSpec((1, C, C), lambda d, i: (d, 0, 0)),
            pl.BlockSpec((1, 1, C), lambda d, i: (d, 0, 0)),
            pl.BlockSpec((1, C, C), lambda d, i: (d, 0, 0)),
            pl.BlockSpec((1, 1, C), lambda d, i: (d, 0, 0)),
            pl.BlockSpec((1, 1, C), lambda d, i: (d, 0, 0)),
        ],
        out_specs=pl.BlockSpec((1, B, T, C), lambda d, i: (d, 0, tile_of(d, i), 0)),
        scratch_shapes=[
            pltpu.VMEM((C // LANES, (T + 2 * SUBLANES) * B, LANES), F32),
            pltpu.VMEM((C // LANES, T * B, LANES), F32),
            pltpu.VMEM((C // LANES, T * B, LANES), F32),
            pltpu.VMEM((C // LANES, T * B, LANES), F32),
            pltpu.VMEM((C // LANES, B, LANES), F32),
        ],
        compiler_params=_cparams(("arbitrary", "arbitrary")),
        name="rglru",
    )(xg, xg, xg, conv_w, conv_b.reshape(1, C), blockdiag(wa), ba.reshape(2, 1, C),
      blockdiag(wx), bx.reshape(2, 1, C), lam.reshape(2, 1, C))


def _shortconv_kernel(xp_ref, x_ref, xn_ref, w_ref, b_ref, o_ref, xe_sc, *, T, nt):
    i = pl.program_id(1)
    H = SUBLANES
    xe_sc[H:H + T, :] = x_ref[0]
    xe_sc[0:H, :] = jnp.where(i > 0, xp_ref[0], 0.0)
    xe_sc[H + T:2 * H + T, :] = jnp.where(i < nt - 1, xn_ref[0], 0.0)
    w = w_ref[...]
    y = b_ref[...]
    for k in range(w.shape[0]):
        y = y + xe_sc[H - 1 + k:H - 1 + k + T, :] * w[k:k + 1]
    o_ref[0] = y


def _short_conv(x, w, b):
    B, L, C = x.shape
    T = min(512, L)
    nt = L // T
    hb = T // SUBLANES
    n_halo = L // SUBLANES
    return pl.pallas_call(
        functools.partial(_shortconv_kernel, T=T, nt=nt),
        out_shape=jax.ShapeDtypeStruct((B, L, C), F32),
        grid=(B, nt),
        in_specs=[
            pl.BlockSpec((1, SUBLANES, C), lambda b_, i: (b_, jnp.maximum(i * hb - 1, 0), 0)),
            pl.BlockSpec((1, T, C), lambda b_, i: (b_, i, 0)),
            pl.BlockSpec((1, SUBLANES, C), lambda b_, i: (b_, jnp.minimum((i + 1) * hb, n_halo - 1), 0)),
            pl.BlockSpec(w.shape, lambda b_, i: (0, 0)),
            pl.BlockSpec((1, C), lambda b_, i: (0, 0)),
        ],
        out_specs=pl.BlockSpec((1, T, C), lambda b_, i: (b_, i, 0)),
        scratch_shapes=[pltpu.VMEM((T + 2 * SUBLANES, C), F32)],
        compiler_params=_cparams(("parallel", "parallel")),
        name="hy_short_conv",
    )(x, x, x, w, b.reshape(1, C))


def _filter_positions(L):
    t = np.linspace(0.0, 1.0, L)[:, None]
    w = 2.0 * math.pi * np.arange(L, dtype=np.float64)[:, None] / L
    f = np.linspace(1e-4, HY_BANDS - 1, HY_BANDS)[None, :]
    z = np.concatenate([t, np.cos(f * w), -np.sin(f * w)], axis=-1)
    zrev = np.concatenate([z[:1], z[:0:-1]], axis=0)
    return jnp.asarray(z, F32), jnp.asarray(zrev, F32)


def _filter_kernel(z_ref, zr_ref, w1_ref, b1_ref, w2_ref, b2_ref, w3_ref, b3_ref, fr_ref, wo_ref, dl_ref,
                   fs_ref, asum_ref, *, T):
    i = pl.program_id(0)
    fr = fr_ref[...]
    hd = lambda a, b: jnp.dot(a, b, precision=HIGHEST, preferred_element_type=F32)

    def mlp(z):
        h = jnp.sin(fr * (hd(z, w1_ref[...]) + b1_ref[...]))
        h = jnp.sin(fr * (hd(h, w2_ref[...]) + b2_ref[...]))
        h = jnp.sin(fr * (hd(h, w3_ref[...]) + b3_ref[...]))
        return hd(h, wo_ref[...])

    z = z_ref[...]
    zr = zr_ref[...]
    kz = mlp(z)
    kr = mlp(zr)
    dl = dl_ref[...]
    dec_f = jnp.exp(-z[:, 0:1] * dl)
    dec_r = jnp.exp(-zr[:, 0:1] * dl)
    C = HY_WIDTH
    first_row = (i * T + lax.broadcasted_iota(jnp.int32, (T, 1), 0)) == 0

    @pl.when(i == 0)
    def _():
        asum_ref[...] = jnp.zeros(asum_ref.shape, F32)

    for o in range(HY_ORDER):
        base = o * 2 * C
        kf = kz[:, base:base + C] * dec_f
        kb_here = kz[:, base + C:base + 2 * C] * dec_f
        kb_rev = kr[:, base + C:base + 2 * C] * dec_r
        first = kf + jnp.where(first_row, kb_here, 0.0)
        second = jnp.where(first_row, 0.0, kb_rev)
        fs_ref[o] = first
        fs_ref[HY_ORDER + o] = second
        asum_ref[o:o + 1, :] += jnp.sum(jnp.abs(first) + jnp.abs(second), axis=0, keepdims=True)


def _hyena_filter_time(L, w1, b1, w2, b2, w3, b3, freq, wout):
    z, zrev = _filter_positions(L)
    T = min(512, L)
    C = HY_WIDTH
    nemb = z.shape[1]
    nf = w2.shape[0]
    deltas = jnp.asarray(np.linspace(HY_MIN_DECAY, HY_MAX_DECAY, C)[None, :], F32)
    const = lambda i: (0, 0)
    return pl.pallas_call(
        functools.partial(_filter_kernel, T=T),
        out_shape=(jax.ShapeDtypeStruct((2 * HY_ORDER, L, C), F32), jax.ShapeDtypeStruct((HY_ORDER, C), F32)),
        grid=(L // T,),
        in_specs=[
            pl.BlockSpec((T, nemb), lambda i: (i, 0)),
            pl.BlockSpec((T, nemb), lambda i: (i, 0)),
            pl.BlockSpec((nemb, nf), const), pl.BlockSpec((1, nf), const),
            pl.BlockSpec((nf, nf), const), pl.BlockSpec((1, nf), const),
            pl.BlockSpec((nf, nf), const), pl.BlockSpec((1, nf), const),
            pl.BlockSpec((1, nf), const),
            pl.BlockSpec(wout.shape, const),
            pl.BlockSpec((1, C), const),
        ],
        out_specs=(pl.BlockSpec((2 * HY_ORDER, T, C), lambda i: (0, i, 0)),
                   pl.BlockSpec((HY_ORDER, C), const)),
        compiler_params=_cparams(("arbitrary",)),
        name="hy_filter_mlp",
    )(z, zrev, w1, b1.reshape(1, nf), w2, b2.reshape(1, nf), w3, b3.reshape(1, nf), freq.reshape(1, nf), wout, deltas)


def _dft_tables(L):
    N = 2 * L
    N2 = DFT_N2
    N1 = N // N2
    K1 = N1 // 2
    n1 = np.arange(K1)[None, None, :]
    k1 = np.arange(N1)[None, :, None]
    n2 = np.arange(N2)[:, None, None]
    ang = 2.0 * np.pi * (((n1 * N2 + n2) * k1) % N) / N
    g_fwd = np.concatenate([np.cos(ang), -np.sin(ang)], axis=1)
    angt = np.transpose(ang, (0, 2, 1))
    g_inv = np.concatenate([np.cos(angt), -np.sin(angt)], axis=2) / N
    a2 = 2.0 * np.pi * ((np.arange(N2)[:, None] * np.arange(N2)[None, :]) % N2) / N2
    cr, ci = np.cos(a2), -np.sin(a2)
    m2 = np.block([[cr, -ci], [ci, cr]])
    m2i = np.block([[cr, ci], [-ci, cr]])
    as_bf = lambda a: jnp.asarray(a, BF16)
    return as_bf(g_fwd), as_bf(g_inv), as_bf(m2), as_bf(m2i)


DFT_NB = 8
N_HALF = HY_WIDTH // LANES


def _lane_half_specs(k1n, nb, part):
    return [pl.BlockSpec((1, k1n, nb, LANES), lambda s, j, h=h: (s, 0, j, part * N_HALF + h)) for h in range(N_HALF)]


def _major_half_specs(k1n, nb):
    return [pl.BlockSpec((1, 1, k1n, nb, LANES), lambda s, j, h=h: (s, h, 0, j, 0)) for h in range(N_HALF)]


def _ld_time(ref, j, nb):
    k1n = ref.shape[-3]
    return ref.reshape(k1n * nb, LANES)[pl.ds(j, k1n, stride=nb), :]


def _st_time(ref, h, j, nb, val):
    k1n = ref.shape[2]
    ref.reshape(N_HALF * k1n * nb, LANES)[pl.ds(h * k1n * nb + j, k1n, stride=nb), :] = val


def _dft_a_kernel(x0_ref, x1_ref, g_ref, o_ref, *, nb):
    n1 = o_ref.shape[4]
    for j in range(nb):
        parts = []
        for x_ref in (x0_ref, x1_ref):
            parts.append(_ld_time(x_ref, j, nb))
        xj = jnp.concatenate(parts, axis=1).astype(BF16)
        r = _dot(g_ref[j], xj)
        for h in range(N_HALF):
            o_ref[0, h, 0, j] = r[:n1, h * LANES:(h + 1) * LANES]
            o_ref[0, h, 1, j] = r[n1:, h * LANES:(h + 1) * LANES]


def _dft_stage_a(x, x_specs, g_fwd):
    S = x.shape[0]
    N2, two_n1, K1 = g_fwd.shape
    N1 = two_n1 // 2
    nb = DFT_NB
    return pl.pallas_call(
        functools.partial(_dft_a_kernel, nb=nb),
        out_shape=jax.ShapeDtypeStruct((S, N_HALF, 2, N2, N1, LANES), F32),
        grid=(S, N2 // nb),
        in_specs=x_specs + [pl.BlockSpec((nb, 2 * N1, K1), lambda s, j: (j, 0, 0))],
        out_specs=pl.BlockSpec((1, N_HALF, 2, nb, N1, LANES), lambda s, j: (s, 0, 0, j, 0, 0)),
        compiler_params=_cparams(("parallel", "parallel")),
        name="hy_dft_a",
    )(x, x, g_fwd)


def _k1_rows(ref, h, j):
    _, nh, _, n2, tk1, _ = ref.shape
    return ref.reshape(nh * 2 * n2 * tk1, LANES), pl.ds(h * 2 * n2 * tk1 + j, 2 * n2, stride=tk1)


def _ld_k1(ref, h, j):
    r2, rows = _k1_rows(ref, h, j)
    return r2[rows, :]


def _st_k1(ref, h, j, val):
    r2, rows = _k1_rows(ref, h, j)
    r2[rows, :] = val


def _k1_slab(ref, j):
    return jnp.concatenate([_ld_k1(ref, h, j) for h in range(N_HALF)], axis=1)


def _dft_b_filter_kernel(s1_ref, s2_ref, m2_ref, asum_ref, o_ref, *, tk1):
    n2 = s1_ref.shape[3]
    C = o_ref.shape[4]
    scale = 1.0 / asum_ref[0]
    for j in range(tk1):
        sign = 1.0 if j % 2 == 0 else -1.0
        s = _k1_slab(s1_ref, j) + sign * _k1_slab(s2_ref, j)
        xk = _dot(m2_ref[...], s.astype(BF16)) * scale
        o_ref[0, j] = xk.reshape(2, n2, C)


def _dft_stage_b_filter(sa, m2, asum):
    _, _, _, N2, N1, _ = sa.shape
    C = HY_WIDTH
    tk1 = SUBLANES
    blk = (1, N_HALF, 2, N2, tk1, LANES)
    return pl.pallas_call(
        functools.partial(_dft_b_filter_kernel, tk1=tk1),
        out_shape=jax.ShapeDtypeStruct((HY_ORDER, N1, 2, N2, C), F32),
        grid=(HY_ORDER, N1 // tk1),
        in_specs=[
            pl.BlockSpec(blk, lambda o, i: (o, 0, 0, 0, i, 0)),
            pl.BlockSpec(blk, lambda o, i: (HY_ORDER + o, 0, 0, 0, i, 0)),
            pl.BlockSpec((2 * N2, 2 * N2), lambda o, i: (0, 0)),
            pl.BlockSpec((1, 1, C), lambda o, i: (o, 0, 0)),
        ],
        out_specs=pl.BlockSpec((1, tk1, 2, N2, C), lambda o, i: (o, i, 0, 0, 0)),
        compiler_params=_cparams(("parallel", "parallel")),
        name="hy_dft_b_filter",
    )(sa, sa, m2, asum.reshape(HY_ORDER, 1, C))


def _dft_b_conv_kernel(s_ref, kf_ref, m2_ref, m2i_ref, o_ref, *, tk1):
    n2 = s_ref.shape[3]
    for j in range(tk1):
        x = _dot(m2_ref[...], _k1_slab(s_ref, j).astype(BF16))
        xr, xi = x[:n2], x[n2:]
        kr = kf_ref[0, j, 0]
        ki = kf_ref[0, j, 1]
        y = jnp.concatenate([xr * kr - xi * ki, xr * ki + xi * kr], axis=0).astype(BF16)
        cc = _dot(m2i_ref[...], y)
        for h in range(N_HALF):
            _st_k1(o_ref, h, j, cc[:, h * LANES:(h + 1) * LANES])


def _dft_stage_b_conv(sa, kspec, order, m2, m2i):
    S, _, _, N2, N1, _ = sa.shape
    C = HY_WIDTH
    tk1 = SUBLANES
    blk = (1, N_HALF, 2, N2, tk1, LANES)
    return pl.pallas_call(
        functools.partial(_dft_b_conv_kernel, tk1=tk1),
        out_shape=jax.ShapeDtypeStruct(sa.shape, F32),
        grid=(N1 // tk1, S),
        in_specs=[
            pl.BlockSpec(blk, lambda i, s: (s, 0, 0, 0, i, 0)),
            pl.BlockSpec((1, tk1, 2, N2, C), lambda i, s: (order, i, 0, 0, 0)),
            pl.BlockSpec((2 * N2, 2 * N2), lambda i, s: (0, 0)),
            pl.BlockSpec((2 * N2, 2 * N2), lambda i, s: (0, 0)),
        ],
        out_specs=pl.BlockSpec(blk, lambda i, s: (s, 0, 0, 0, i, 0)),
        compiler_params=_cparams(("parallel", "parallel")),
        name="hy_dft_b_conv",
    )(sa, kspec, m2, m2i)


def _dft_c_kernel(c_ref, gi_ref, z0_ref, z1_ref, g0_ref, g1_ref, d_ref, o_ref, *, nb):
    d = d_ref[0]
    for j in range(nb):
        cat = jnp.concatenate(
            [jnp.concatenate([c_ref[0, h, 0, j], c_ref[0, h, 1, j]], axis=0) for h in range(N_HALF)], axis=1)
        y = _dot(gi_ref[j], cat.astype(BF16))
        for h, (z_ref, g_ref) in enumerate(((z0_ref, g0_ref), (z1_ref, g1_ref))):
            lanes = slice(h * LANES, (h + 1) * LANES)
            _st_time(o_ref, h, j, nb, _ld_time(g_ref, j, nb) * (y[:, lanes] + _ld_time(z_ref, j, nb) * d[:, lanes]))


def _dft_stage_c(sc, g_inv, z, z_specs, gate, gate_specs, d):
    S, _, _, N2, N1, _ = sc.shape
    C = HY_WIDTH
    K1 = N1 // 2
    nb = DFT_NB
    return pl.pallas_call(
        functools.partial(_dft_c_kernel, nb=nb),
        out_shape=jax.ShapeDtypeStruct((S, N_HALF, K1, N2, LANES), F32),
        grid=(S, N2 // nb),
        in_specs=[
            pl.BlockSpec((1, N_HALF, 2, nb, N1, LANES), lambda s, j: (s, 0, 0, j, 0, 0)),
            pl.BlockSpec((nb, K1, 2 * N1), lambda s, j: (j, 0, 0)),
        ] + z_specs + gate_specs + [pl.BlockSpec((1, 1, C), lambda s, j: (0, 0, 0))],
        out_specs=pl.BlockSpec((1, N_HALF, K1, nb, LANES), lambda s, j: (s, 0, 0, j, 0)),
        compiler_params=_cparams(("parallel", "parallel")),
        name="hy_dft_c",
    )(sc, g_inv, z, z, gate, gate, d.reshape(1, 1, C))


def _hyena(hy, conv_w, conv_b, w1, b1, w2, b2, w3, b3, freq, wout, bias, tables):
    B, L, _ = hy.shape
    C = HY_WIDTH
    g_fwd, g_inv, m2, m2i = tables
    N2 = DFT_N2
    K1 = L // N2
    nb = DFT_NB
    fs, asum = _hyena_filter_time(L, w1, b1, w2, b2, w3, b3, freq, wout)
    fa = _dft_stage_a(fs.reshape(2 * HY_ORDER, K1, N2, C), _lane_half_specs(K1, nb, 0), g_fwd)
    kspec = _dft_stage_b_filter(fa, m2, asum)
    hc = _short_conv(hy, conv_w, conv_b)
    hc4 = hc.reshape(B, K1, N2, (HY_ORDER + 1) * C)
    z, z_specs = hc4, _lane_half_specs(K1, nb, 0)
    for o in range(HY_ORDER):
        sa = _dft_stage_a(z, z_specs, g_fwd)
        sc = _dft_stage_b_conv(sa, kspec, o, m2, m2i)
        z = _dft_stage_c(sc, g_inv, z, z_specs, hc4, _lane_half_specs(K1, nb, 1 + o), bias[o])
        z_specs = _major_half_specs(K1, nb)
    return z.reshape(B, N_HALF, L, LANES)


def _outproj_kernel(x_ref, at_ref, h_ref, gr_ref, zy_ref, w_ref, gt_ref, g_ref, b_ref, o_ref, *, alpha):
    lru = (h_ref[0, 0] + h_ref[1, 0]) * jax.nn.gelu(gr_ref[0])
    na = at_ref.shape[2]
    nl = lru.shape[1]
    m = (_dot(at_ref[0], w_ref[0:na]) + _dot(lru.astype(BF16), w_ref[na:na + nl])
         + sum(_dot(zy_ref[0, h].astype(BF16), w_ref[na + nl + h * LANES:na + nl + (h + 1) * LANES])
               for h in range(N_HALF)))
    o_ref[0] = _layer_norm(alpha * x_ref[0] + gt_ref[0] * m, g_ref[...], b_ref[...])


def _out_proj(x, attn, h, xg, zy, w_out, gt, ln_g, ln_b, alpha):
    B, L, D = x.shape
    T = min(512, L)
    C = LRU_WIDTH
    row = lambda b, i: (b, i, 0)
    vec = lambda b, i: (b, 0, 0)
    const = lambda b, i: (0, 0)
    return pl.pallas_call(
        functools.partial(_outproj_kernel, alpha=alpha),
        out_shape=jax.ShapeDtypeStruct((B, L, D), F32),
        grid=(B, L // T),
        in_specs=[
            pl.BlockSpec((1, T, D), row),
            pl.BlockSpec((1, T, IN_Q), row),
            pl.BlockSpec((2, 1, T, C), lambda b, i: (0, b, i, 0)),
            pl.BlockSpec((1, T, C), lambda b, i: (b, i, 1)),
            pl.BlockSpec((1, N_HALF, T, LANES), lambda b, i: (b, 0, i, 0)),
            pl.BlockSpec(w_out.shape, const),
            pl.BlockSpec((1, 1, D), vec),
            pl.BlockSpec((1, D), const),
            pl.BlockSpec((1, D), const),
        ],
        out_specs=pl.BlockSpec((1, T, D), row),
        compiler_params=_cparams(("parallel", "parallel")),
        name="out_proj",
    )(x, attn, h, xg, zy, w_out, gt, ln_g.reshape(1, D), ln_b.reshape(1, D))


FFN_CHUNK = 256


def _ffn_kernel(xp_ref, x_ref, xn_ref, sc_ref, sh_ref, gt_ref, wu_ref, cw_ref, cb_ref, wd_ref, g_ref, b_ref,
                o_ref, u_sc, h_sc, acc_sc, *, T, nt, nf, alpha):
    i = pl.program_id(1)
    H = SUBLANES
    R = T + 2 * H
    F = FFN_CHUNK
    sc = 1.0 + sc_ref[0]
    sh = sh_ref[0]
    u_sc[H:H + T, :] = (x_ref[0] * sc + sh).astype(BF16)
    u_sc[0:H, :] = jnp.where(i > 0, xp_ref[0] * sc + sh, 0.0).astype(BF16)
    u_sc[H + T:R, :] = jnp.where(i < nt - 1, xn_ref[0] * sc + sh, 0.0).astype(BF16)
    acc_sc[...] = jnp.zeros(acc_sc.shape, F32)

    def up(f, slot):
        h_sc[slot] = _dot(u_sc[...], wu_ref[f])

    def down(f, slot):
        h = h_sc[slot]
        cw = cw_ref[f]
        y = cb_ref[f] + pltpu.roll(h, 1, 0) * cw[0:1] + h * cw[1:2] + pltpu.roll(h, R - 1, 0) * cw[2:3]
        y = y[H:H + T]
        act = (jax.nn.gelu(y[:, :F]) * y[:, F:]).astype(BF16)
        acc_sc[...] += _dot(act, wd_ref[f])

    def pair(c2, carry):
        f = 2 * c2
        up(f + 1, 1)
        down(f, 0)
        up(jnp.minimum(f + 2, nf - 1), 0)
        down(f + 1, 1)
        return carry

    up(0, 0)
    lax.fori_loop(0, nf // 2, pair, 0)
    if nf % 2 == 1:
        down(nf - 1, 0)
    o_ref[0] = _layer_norm(alpha * x_ref[0] + gt_ref[0] * acc_sc[...], g_ref[...], b_ref[...])


def _conv_ffn(x, sc, sh, gt, w_up, conv_w, conv_b, w_down, ln_g, ln_b, alpha):
    B, L, D = x.shape
    d_ff = w_down.shape[0]
    T = min(512, L)
    F = FFN_CHUNK
    nt = L // T
    nf = d_ff // F
    hb = T // SUBLANES
    n_halo = L // SUBLANES
    nk = conv_w.shape[0]
    chunked = lambda w: jnp.concatenate([w[..., :d_ff].reshape(w.shape[0], nf, F),
                                         w[..., d_ff:].reshape(w.shape[0], nf, F)], axis=-1).transpose(1, 0, 2)
    wu = chunked(w_up)
    cw = chunked(conv_w)
    cb = chunked(conv_b.reshape(1, 2 * d_ff))
    wd = w_down.reshape(nf, F, D)
    row = lambda b, i: (b, i, 0)
    vec = lambda b, i: (b, 0, 0)
    const2 = lambda b, i: (0, 0)
    const3 = lambda b, i: (0, 0, 0)
    resident = lambda shape, imap: pl.BlockSpec(shape, imap, pipeline_mode=pl.Buffered(1))
    return pl.pallas_call(
        functools.partial(_ffn_kernel, T=T, nt=nt, nf=nf, alpha=alpha),
        out_shape=jax.ShapeDtypeStruct((B, L, D), F32),
        grid=(B, nt),
        in_specs=[
            pl.BlockSpec((1, SUBLANES, D), lambda b, i: (b, jnp.maximum(i * hb - 1, 0), 0)),
            pl.BlockSpec((1, T, D), row),
            pl.BlockSpec((1, SUBLANES, D), lambda b, i: (b, jnp.minimum((i + 1) * hb, n_halo - 1), 0)),
            pl.BlockSpec((1, 1, D), vec),
            pl.BlockSpec((1, 1, D), vec),
            pl.BlockSpec((1, 1, D), vec),
            resident((nf, D, 2 * F), const3),
            resident((nf, nk, 2 * F), const3),
            resident((nf, 1, 2 * F), const3),
            resident((nf, F, D), const3),
            pl.BlockSpec((1, D), const2),
            pl.BlockSpec((1, D), const2),
        ],
        out_specs=pl.BlockSpec((1, T, D), row),
        scratch_shapes=[
            pltpu.VMEM((T + 2 * SUBLANES, D), BF16),
            pltpu.VMEM((2, T + 2 * SUBLANES, 2 * F), F32),
            pltpu.VMEM((T, D), F32),
        ],
        compiler_params=_cparams(("parallel", "parallel")),
        name="conv_ffn",
    )(x, x, x, sc, sh, gt, wu, cw, cb, wd, ln_g.reshape(1, D), ln_b.reshape(1, D))


def _trunk(x, mod, p):
    B, L, D = x.shape
    depth = mod.shape[0]
    alpha = (2 * depth) ** 0.25
    rope = _rope_tables(L)
    tables = _dft_tables(L)
    for l in range(depth):
        m6 = mod[l].reshape(B, 6, 1, D)
        sh1, sc1, gt1, sh2, sc2, gt2 = (m6[:, j] for j in range(6))
        q, k, v, xg, hy = _in_proj(x, sc1, sh1, p['w_in'][l], p['q_gain'][l], p['k_gain'][l], rope)
        attn = _attention(q, k, v)
        h = _rglru(xg, p['lru_conv_w'][l], p['lru_conv_b'][l], p['lru_wa'][l], p['lru_ba'][l],
                   p['lru_wx'][l], p['lru_bx'][l], p['lru_lambda'][l])
        zy = _hyena(hy, p['hy_conv_w'][l], p['hy_conv_b'][l], p['hy_w1'][l], p['hy_b1'][l], p['hy_w2'][l],
                    p['hy_b2'][l], p['hy_w3'][l], p['hy_b3'][l], p['hy_freq'][l], p['hy_wout'][l],
                    p['hy_bias'][l], tables)
        x = _out_proj(x, attn, h, xg, zy, p['w_out'][l], gt1, p['ln1_g'][l], p['ln1_b'][l], alpha)
        x = _conv_ffn(x, sc2, sh2, gt2, p['ffn_w_up'][l], p['ffn_conv_w'][l], p['ffn_conv_b'][l],
                      p['ffn_w_down'][l], p['ln2_g'][l], p['ln2_b'][l], alpha)
    return x


def kernel(x_prompt, x_sample, c_prompt, c_sample, ada_w, ada_b, w_in, q_gain, k_gain, lru_conv_w, lru_conv_b, lru_wa, lru_ba, lru_wx, lru_bx, lru_lambda, hy_conv_w, hy_conv_b, hy_w1, hy_b1, hy_w2, hy_b2, hy_w3, hy_b3, hy_freq, hy_wout, hy_bias, w_out, ln1_g, ln1_b, ffn_w_up, ffn_conv_w, ffn_conv_b, ffn_w_down, ln2_g, ln2_b):
    p = dict(
        w_in=w_in.astype(BF16), q_gain=q_gain, k_gain=k_gain, lru_conv_w=lru_conv_w, lru_conv_b=lru_conv_b,
        lru_wa=lru_wa, lru_ba=lru_ba, lru_wx=lru_wx, lru_bx=lru_bx, lru_lambda=lru_lambda,
        hy_conv_w=hy_conv_w, hy_conv_b=hy_conv_b, hy_w1=hy_w1, hy_b1=hy_b1, hy_w2=hy_w2, hy_b2=hy_b2,
        hy_w3=hy_w3, hy_b3=hy_b3, hy_freq=hy_freq, hy_wout=hy_wout, hy_bias=hy_bias,
        w_out=w_out.astype(BF16), ln1_g=ln1_g, ln1_b=ln1_b, ffn_w_up=ffn_w_up.astype(BF16),
        ffn_conv_w=ffn_conv_w, ffn_conv_b=ffn_conv_b, ffn_w_down=ffn_w_down.astype(BF16), ln2_g=ln2_g, ln2_b=ln2_b,
    )
    nb = x_prompt.shape[0]
    mod = _ada_mod(jnp.concatenate([c_prompt, c_sample], axis=0), ada_w, ada_b)
    y_prompt = _trunk(x_prompt, mod[:, :nb], p)
    y_sample = _trunk(x_sample, mod[:, nb:], p)
    return (y_prompt, y_sample)
```

```python
import functools
import math

import numpy as np
import jax
import jax.numpy as jnp
from jax import lax
from jax.experimental import pallas as pl
from jax.experimental.pallas import tpu as pltpu

F32 = jnp.float32
BF16 = jnp.bfloat16
HIGHEST = lax.Precision.HIGHEST

GRID_W = 64
HEAD_DIM = 64
N_HEADS = 8
N_KV_HEADS = 2
KV_GROUP = N_HEADS // N_KV_HEADS
ROPE_THETA = 10000.0
ROPE_FREQS = HEAD_DIM // 4
QK_EPS = 1e-6
LRU_WIDTH = 256
LRU_HEADS = 4
LRU_C = 8.0
HY_WIDTH = 256
HY_ORDER = 2
HY_BANDS = 16
HY_MIN_DECAY = abs(math.log(1e-2)) / 1.5
HY_MAX_DECAY = abs(math.log(1e-2)) / 0.3
LN_EPS = 1e-5
IN_Q = N_HEADS * HEAD_DIM
IN_KV = N_KV_HEADS * HEAD_DIM

LANES = 128
SUBLANES = 8
DFT_N2 = 128
VMEM_LIMIT = 48 * 1024 * 1024


def _cparams(sem):
    return pltpu.CompilerParams(dimension_semantics=sem, vmem_limit_bytes=VMEM_LIMIT)


def _dot(a, b):
    return jnp.dot(a, b, preferred_element_type=F32)


def _layer_norm(y, g, b):
    mu = jnp.mean(y, axis=-1, keepdims=True)
    yc = y - mu
    var = jnp.mean(yc * yc, axis=-1, keepdims=True)
    return yc * lax.rsqrt(var + LN_EPS) * g + b


def _ada_kernel(c_ref, w_ref, b_ref, o_ref):
    c = c_ref[...]
    s = c * jax.nn.sigmoid(c)
    o_ref[0] = jnp.dot(s, w_ref[0], precision=HIGHEST, preferred_element_type=F32) + b_ref[0]


def _ada_mod(c_all, ada_w, ada_b):
    depth, d, n = ada_w.shape
    rows = c_all.shape[0]
    tn = 768
    return pl.pallas_call(
        _ada_kernel,
        out_shape=jax.ShapeDtypeStruct((depth, rows, n), F32),
        grid=(depth, n // tn),
        in_specs=[
            pl.BlockSpec((rows, d), lambda l, j: (0, 0)),
            pl.BlockSpec((1, d, tn), lambda l, j: (l, 0, j)),
            pl.BlockSpec((1, 1, tn), lambda l, j: (l, 0, j)),
        ],
        out_specs=pl.BlockSpec((1, rows, tn), lambda l, j: (l, 0, j)),
        compiler_params=_cparams(("parallel", "parallel")),
        name="ada_mod",
    )(c_all, ada_w, ada_b.reshape(depth, 1, n))


def _rope_tables(L):
    rows = L // GRID_W
    row = np.repeat(np.arange(rows, dtype=np.float64), GRID_W)
    col = np.tile(np.arange(GRID_W, dtype=np.float64), rows)
    inv = ROPE_THETA ** (-np.arange(ROPE_FREQS, dtype=np.float64) / ROPE_FREQS)
    ar = row[:, None] * inv
    ac = col[:, None] * inv
    zeros = np.zeros_like(ar)
    cos = np.concatenate([np.cos(ar), np.cos(ar), np.cos(ac), np.cos(ac)], axis=1)
    sin_up = np.concatenate([-np.sin(ar), zeros, -np.sin(ac), zeros], axis=1)
    sin_dn = np.concatenate([zeros, np.sin(ar), zeros, np.sin(ac)], axis=1)
    two = lambda t: jnp.asarray(np.concatenate([t, t], axis=1), F32)
    return two(cos), two(sin_up), two(sin_dn)


def _inproj_kernel(x_ref, sc_ref, sh_ref, w_ref, qg_ref, kg_ref, cos_ref, sup_ref, sdn_ref, bd_ref,
                   q_ref, k_ref, v_ref, xg_ref, hy_ref):
    u = (x_ref[0] * (1.0 + sc_ref[0]) + sh_ref[0]).astype(BF16)
    proj = _dot(u, w_ref[...])
    cos = cos_ref[...]
    sup = sup_ref[...]
    sdn = sdn_ref[...]
    bd = bd_ref[...]
    half = ROPE_FREQS

    def norm_rope(t, gain):
        sq = t * t
        hi = sq.astype(BF16)
        lo = (sq - hi.astype(F32)).astype(BF16)
        ms = _dot(hi, bd) + _dot(lo, bd)
        tn = t * lax.rsqrt(ms + QK_EPS) * gain
        return (tn * cos + pltpu.roll(tn, LANES - half, 1) * sup + pltpu.roll(tn, half, 1) * sdn)

    qg = qg_ref[...]
    for j in range(IN_Q // LANES):
        sl = slice(j * LANES, (j + 1) * LANES)
        q_ref[0, :, sl] = (norm_rope(proj[:, sl], qg) * (HEAD_DIM ** -0.5 * math.log2(math.e))).astype(BF16)
    k_ref[0] = norm_rope(proj[:, IN_Q:IN_Q + IN_KV], kg_ref[...]).astype(BF16)
    v_ref[0] = proj[:, IN_Q + IN_KV:IN_Q + 2 * IN_KV].astype(BF16)
    o = IN_Q + 2 * IN_KV
    xg_ref[0] = proj[:, o:o + 2 * LRU_WIDTH]
    hy_ref[0] = proj[:, o + 2 * LRU_WIDTH:]


def _in_proj(x, sc, sh, w_in, q_gain, k_gain, rope):
    B, L, D = x.shape
    n_in = w_in.shape[1]
    T = min(512, L)
    cos, sup, sdn = rope
    bd = jnp.asarray(np.kron(np.eye(2), np.full((HEAD_DIM, HEAD_DIM), 1.0 / HEAD_DIM)), BF16)
    qg = jnp.tile(q_gain, 2).reshape(1, LANES)
    kg = jnp.tile(k_gain, 2).reshape(1, LANES)
    n_hy = n_in - IN_Q - 2 * IN_KV - 2 * LRU_WIDTH
    row = lambda b, i: (b, i, 0)
    vec = lambda b, i: (b, 0, 0)
    tab = lambda b, i: (i, 0)
    const = lambda b, i: (0, 0)
    return pl.pallas_call(
        _inproj_kernel,
        out_shape=(
            jax.ShapeDtypeStruct((B, L, IN_Q), BF16),
            jax.ShapeDtypeStruct((B, L, IN_KV), BF16),
            jax.ShapeDtypeStruct((B, L, IN_KV), BF16),
            jax.ShapeDtypeStruct((B, L, 2 * LRU_WIDTH), F32),
            jax.ShapeDtypeStruct((B, L, n_hy), F32),
        ),
        grid=(B, L // T),
        in_specs=[
            pl.BlockSpec((1, T, D), row),
            pl.BlockSpec((1, 1, D), vec),
            pl.BlockSpec((1, 1, D), vec),
            pl.BlockSpec((D, n_in), const),
            pl.BlockSpec((1, LANES), const),
            pl.BlockSpec((1, LANES), const),
            pl.BlockSpec((T, LANES), tab),
            pl.BlockSpec((T, LANES), tab),
            pl.BlockSpec((T, LANES), tab),
            pl.BlockSpec((LANES, LANES), const),
        ],
        out_specs=(
            pl.BlockSpec((1, T, IN_Q), row),
            pl.BlockSpec((1, T, IN_KV), row),
            pl.BlockSpec((1, T, IN_KV), row),
            pl.BlockSpec((1, T, 2 * LRU_WIDTH), row),
            pl.BlockSpec((1, T, n_hy), row),
        ),
        compiler_params=_cparams(("parallel", "parallel")),
        name="in_proj",
    )(x, sc, sh, w_in, qg, kg, cos, sup, sdn, bd)


V_ROWS = HEAD_DIM + 16


def _attn_kernel(qt_ref, k_ref, vt_ref, o_ref, acc_sc, s_sc, *, tk, nk):
    tq = qt_ref.shape[-1]
    acc_sc[...] = jnp.zeros(acc_sc.shape, F32)

    def scores_h(c, slot, h):
        kc = k_ref[0, 0, pl.ds(pl.multiple_of(c * tk, tk), tk), :]
        s_sc[slot, h] = _dot(kc, qt_ref[0, 0, h])

    def scores(c, slot):
        for h in range(KV_GROUP):
            scores_h(c, slot, h)

    def consume_h(c, slot, h, m_prev):
        vc = vt_ref[0, 0, c]
        s = s_sc[slot, h]
        m_new = jnp.maximum(m_prev, jnp.max(s, axis=0, keepdims=True))
        p = jnp.exp2(s - m_new).astype(BF16)
        alpha = jnp.exp2(m_prev - m_new)
        acc_sc[h] = alpha * acc_sc[h] + _dot(vc, p)
        return m_new

    def step(c, c_next, slot, ms):
        new = []
        for h in range(KV_GROUP):
            scores_h(c_next, 1 - slot, h)
            new.append(consume_h(c, slot, h, ms[h]))
        return tuple(new)

    def pair(c2, ms):
        c = 2 * c2
        ms = step(c, c + 1, 0, ms)
        return step(c + 1, jnp.minimum(c + 2, nk - 1), 1, ms)

    scores(0, 0)
    m0 = tuple(jnp.full((1, tq), -jnp.inf, F32) for _ in range(KV_GROUP))
    lax.fori_loop(0, nk // 2, pair, m0)
    for h in range(KV_GROUP):
        acc = acc_sc[h]
        o_ref[0, 0, h] = (acc[:HEAD_DIM] / acc[HEAD_DIM:HEAD_DIM + 1]).astype(o_ref.dtype)


def _attention(q, k, v):
    B, L, _ = q.shape
    tq = min(256, L)
    tk = min(512, L // 2)
    nk = L // tk
    assert nk % 2 == 0
    qt = q.reshape(B, L, N_KV_HEADS, KV_GROUP, HEAD_DIM).transpose(0, 2, 3, 4, 1)
    kh = k.reshape(B, L, N_KV_HEADS, HEAD_DIM).transpose(0, 2, 1, 3)
    vt = v.reshape(B, nk, tk, N_KV_HEADS, HEAD_DIM).transpose(0, 3, 1, 4, 2)
    ones = jnp.ones((B, N_KV_HEADS, nk, 1, tk), BF16)
    pad = jnp.zeros((B, N_KV_HEADS, nk, V_ROWS - HEAD_DIM - 1, tk), BF16)
    vt = jnp.concatenate([vt, ones, pad], axis=3)
    ot = pl.pallas_call(
        functools.partial(_attn_kernel, tk=tk, nk=nk),
        out_shape=jax.ShapeDtypeStruct((B, N_KV_HEADS, KV_GROUP, HEAD_DIM, L), BF16),
        grid=(B, N_KV_HEADS, L // tq),
        in_specs=[
            pl.BlockSpec((1, 1, KV_GROUP, HEAD_DIM, tq), lambda b, g, i: (b, g, 0, 0, i)),
            pl.BlockSpec((1, 1, L, HEAD_DIM), lambda b, g, i: (b, g, 0, 0)),
            pl.BlockSpec((1, 1, nk, V_ROWS, tk), lambda b, g, i: (b, g, 0, 0, 0)),
        ],
        out_specs=pl.BlockSpec((1, 1, KV_GROUP, HEAD_DIM, tq), lambda b, g, i: (b, g, 0, 0, i)),
        scratch_shapes=[pltpu.VMEM((KV_GROUP, V_ROWS, tq), F32), pltpu.VMEM((2, KV_GROUP, tk, tq), F32)],
        compiler_params=_cparams(("parallel", "parallel", "parallel")),
        name="attention",
    )(qt, kh, vt)
    return ot.transpose(0, 4, 1, 2, 3).reshape(B, L, IN_Q)


def _lru_kernel(xp_ref, x_ref, xn_ref, cw_ref, cb_ref, wa_ref, ba_ref, wx_ref, bx_ref, lam_ref,
                o_ref, xe_sc, a_sc, b_sc, hs_sc, h_sc, *, T, nt):
    d = pl.program_id(0)
    i = pl.program_id(1)
    tile = jnp.where(d == 0, i, nt - 1 - i)
    nb, _, C = x_ref.shape
    nh = C // LANES
    H = SUBLANES
    keep_prev = (tile > 0).astype(F32)
    keep_next = (tile < nt - 1).astype(F32)
    for b in range(nb):
        for hf in range(nh):
            lanes = slice(hf * LANES, (hf + 1) * LANES)
            xe_sc[hf, pl.ds(b, H, stride=nb), :] = xp_ref[b, :, lanes] * keep_prev
            xe_sc[hf, pl.ds(H * nb + b, T, stride=nb), :] = x_ref[b, :, lanes]
            xe_sc[hf, pl.ds((H + T) * nb + b, H, stride=nb), :] = xn_ref[b, :, lanes] * keep_next
    cw = cw_ref[...]
    halves = []
    for hf in range(nh):
        lanes = slice(hf * LANES, (hf + 1) * LANES)
        acc = cb_ref[:, lanes]
        for k in range(cw.shape[0]):
            acc = acc + xe_sc[hf, (H - 2 + k) * nb:(H - 2 + k + T) * nb, :] * cw[k:k + 1, lanes]
        halves.append(acc)
    xc = jnp.concatenate(halves, axis=1)
    xb = xc.astype(BF16)
    sigmoid = lambda v: 0.5 * jnp.tanh(0.5 * v) + 0.5
    r = sigmoid(_dot(xb, wa_ref[0]) + ba_ref[0])
    ig = sigmoid(_dot(xb, wx_ref[0]) + bx_ref[0])
    lam = lam_ref[0]
    softplus_neg = jnp.maximum(-lam, 0.0) + jnp.log1p(jnp.exp(-jnp.abs(lam)))
    log_a = -LRU_C * r * softplus_neg
    a = jnp.exp(log_a)
    bb = jnp.sqrt(-jnp.tanh(log_a) * (1.0 + a * a)) * (ig * xc)
    for hf in range(nh):
        a_sc[hf] = a[:, hf * LANES:(hf + 1) * LANES]
        b_sc[hf] = bb[:, hf * LANES:(hf + 1) * LANES]

    @pl.when(i == 0)
    def _():
        h_sc[...] = jnp.zeros(h_sc.shape, F32)

    def body(s, hs):
        t = jnp.where(d == 0, s, T - 1 - s)
        rows = pl.ds(pl.multiple_of(t * nb, nb), nb)
        new = []
        for hf in range(nh):
            h = a_sc[hf, rows, :] * hs[hf] + b_sc[hf, rows, :]
            hs_sc[hf, rows, :] = h
            new.append(h)
        return tuple(new)

    hs = lax.fori_loop(0, T, body, tuple(h_sc[hf] for hf in range(nh)), unroll=8)
    for hf in range(nh):
        h_sc[hf] = hs[hf]
        for b in range(nb):
            o_ref[0, b, :, hf * LANES:(hf + 1) * LANES] = hs_sc[hf, pl.ds(b, T, stride=nb), :]


def _rglru(xg, conv_w, conv_b, wa, ba, wx, bx, lam):
    B, L, _ = xg.shape
    C = LRU_WIDTH
    assert B == SUBLANES
    T = min(256, L)
    nt = L // T
    hb = T // SUBLANES
    n_halo = L // SUBLANES

    def tile_of(d, i):
        return jnp.where(d == 0, i, nt - 1 - i)

    def blockdiag(w):
        eye = jnp.eye(LRU_HEADS, dtype=w.dtype)
        return jnp.einsum('dhij,hg->dhigj', w, eye).reshape(2, C, C).astype(BF16)

    kern = functools.partial(_lru_kernel, T=T, nt=nt)
    return pl.pallas_call(
        kern,
        out_shape=jax.ShapeDtypeStruct((2, B, L, C), F32),
        grid=(2, nt),
        in_specs=[
            pl.BlockSpec((B, SUBLANES, C), lambda d, i: (0, jnp.maximum(tile_of(d, i) * hb - 1, 0), 0)),
            pl.BlockSpec((B, T, C), lambda d, i: (0, tile_of(d, i), 0)),
            pl.BlockSpec((B, SUBLANES, C), lambda d, i: (0, jnp.minimum((tile_of(d, i) + 1) * hb, n_halo - 1), 0)),
            pl.BlockSpec(conv_w.shape, lambda d, i: (0, 0)),
            pl.BlockSpec((1, C), lambda d, i: (0, 0)),
            pl.BlockSpec((1, C, C), lambda d, i: (d, 0, 0)),
            pl.BlockSpec((1, 1, C), lambda d, i: (d, 0, 0)),
            pl.BlockSpec((1, C, C), lambda d, i: (d, 0, 0)),
            pl.BlockSpec((1, 1, C), lambda d, i: (d, 0, 0)),
            pl.BlockSpec((1, 1, C), lambda d, i: (d, 0, 0)),
        ],
        out_specs=pl.BlockSpec((1, B, T, C), lambda d, i: (d, 0, tile_of(d, i), 0)),
        scratch_shapes=[
            pltpu.VMEM((C // LANES, (T + 2 * SUBLANES) * B, LANES), F32),
            pltpu.VMEM((C // LANES, T * B, LANES), F32),
            pltpu.VMEM((C // LANES, T * B, LANES), F32),
            pltpu.VMEM((C // LANES, T * B, LANES), F32),
            pltpu.VMEM((C // LANES, B, LANES), F32),
        ],
        compiler_params=_cparams(("arbitrary", "arbitrary")),
        name="rglru",
    )(xg, xg, xg, conv_w, conv_b.reshape(1, C), blockdiag(wa), ba.reshape(2, 1, C),
      blockdiag(wx), bx.reshape(2, 1, C), lam.reshape(2, 1, C))


def _shortconv_kernel(xp_ref, x_ref, xn_ref, w_ref, b_ref, o_ref, xe_sc, *, T, nt):
    i = pl.program_id(1)
    H = SUBLANES
    xe_sc[H:H + T, :] = x_ref[0]
    xe_sc[0:H, :] = jnp.where(i > 0, xp_ref[0], 0.0)
    xe_sc[H + T:2 * H + T, :] = jnp.where(i < nt - 1, xn_ref[0], 0.0)
    w = w_ref[...]
    y = b_ref[...]
    for k in range(w.shape[0]):
        y = y + xe_sc[H - 1 + k:H - 1 + k + T, :] * w[k:k + 1]
    o_ref[0] = y


def _short_conv(x, w, b):
    B, L, C = x.shape
    T = min(512, L)
    nt = L // T
    hb = T // SUBLANES
    n_halo = L // SUBLANES
    return pl.pallas_call(
        functools.partial(_shortconv_kernel, T=T, nt=nt),
        out_shape=jax.ShapeDtypeStruct((B, L, C), F32),
        grid=(B, nt),
        in_specs=[
            pl.BlockSpec((1, SUBLANES, C), lambda b_, i: (b_, jnp.maximum(i * hb - 1, 0), 0)),
            pl.BlockSpec((1, T, C), lambda b_, i: (b_, i, 0)),
            pl.BlockSpec((1, SUBLANES, C), lambda b_, i: (b_, jnp.minimum((i + 1) * hb, n_halo - 1), 0)),
            pl.BlockSpec(w.shape, lambda b_, i: (0, 0)),
            pl.BlockSpec((1, C), lambda b_, i: (0, 0)),
        ],
        out_specs=pl.BlockSpec((1, T, C), lambda b_, i: (b_, i, 0)),
        scratch_shapes=[pltpu.VMEM((T + 2 * SUBLANES, C), F32)],
        compiler_params=_cparams(("parallel", "parallel")),
        name="hy_short_conv",
    )(x, x, x, w, b.reshape(1, C))


def _filter_positions(L):
    t = np.linspace(0.0, 1.0, L)[:, None]
    w = 2.0 * math.pi * np.arange(L, dtype=np.float64)[:, None] / L
    f = np.linspace(1e-4, HY_BANDS - 1, HY_BANDS)[None, :]
    z = np.concatenate([t, np.cos(f * w), -np.sin(f * w)], axis=-1)
    zrev = np.concatenate([z[:1], z[:0:-1]], axis=0)
    return jnp.asarray(z, F32), jnp.asarray(zrev, F32)


def _filter_kernel(z_ref, zr_ref, w1_ref, b1_ref, w2_ref, b2_ref, w3_ref, b3_ref, fr_ref, wo_ref, dl_ref,
                   fs_ref, asum_ref, *, T):
    i = pl.program_id(0)
    fr = fr_ref[...]
    hd = lambda a, b: jnp.dot(a, b, precision=HIGHEST, preferred_element_type=F32)

    def mlp(z):
        h = jnp.sin(fr * (hd(z, w1_ref[...]) + b1_ref[...]))
        h = jnp.sin(fr * (hd(h, w2_ref[...]) + b2_ref[...]))
        h = jnp.sin(fr * (hd(h, w3_ref[...]) + b3_ref[...]))
        return hd(h, wo_ref[...])

    z = z_ref[...]
    zr = zr_ref[...]
    kz = mlp(z)
    kr = mlp(zr)
    dl = dl_ref[...]
    dec_f = jnp.exp(-z[:, 0:1] * dl)
    dec_r = jnp.exp(-zr[:, 0:1] * dl)
    C = HY_WIDTH
    first_row = (i * T + lax.broadcasted_iota(jnp.int32, (T, 1), 0)) == 0

    @pl.when(i == 0)
    def _():
        asum_ref[...] = jnp.zeros(asum_ref.shape, F32)

    for o in range(HY_ORDER):
        base = o * 2 * C
        kf = kz[:, base:base + C] * dec_f
        kb_here = kz[:, base + C:base + 2 * C] * dec_f
        kb_rev = kr[:, base + C:base + 2 * C] * dec_r
        first = kf + jnp.where(first_row, kb_here, 0.0)
        second = jnp.where(first_row, 0.0, kb_rev)
        fs_ref[o] = first
        fs_ref[HY_ORDER + o] = second
        asum_ref[o:o + 1, :] += jnp.sum(jnp.abs(first) + jnp.abs(second), axis=0, keepdims=True)


def _hyena_filter_time(L, w1, b1, w2, b2, w3, b3, freq, wout):
    z, zrev = _filter_positions(L)
    T = min(512, L)
    C = HY_WIDTH
    nemb = z.shape[1]
    nf = w2.shape[0]
    deltas = jnp.asarray(np.linspace(HY_MIN_DECAY, HY_MAX_DECAY, C)[None, :], F32)
    const = lambda i: (0, 0)
    return pl.pallas_call(
        functools.partial(_filter_kernel, T=T),
        out_shape=(jax.ShapeDtypeStruct((2 * HY_ORDER, L, C), F32), jax.ShapeDtypeStruct((HY_ORDER, C), F32)),
        grid=(L // T,),
        in_specs=[
            pl.BlockSpec((T, nemb), lambda i: (i, 0)),
            pl.BlockSpec((T, nemb), lambda i: (i, 0)),
            pl.BlockSpec((nemb, nf), const), pl.BlockSpec((1, nf), const),
            pl.BlockSpec((nf, nf), const), pl.BlockSpec((1, nf), const),
            pl.BlockSpec((nf, nf), const), pl.BlockSpec((1, nf), const),
            pl.BlockSpec((1, nf), const),
            pl.BlockSpec(wout.shape, const),
            pl.BlockSpec((1, C), const),
        ],
        out_specs=(pl.BlockSpec((2 * HY_ORDER, T, C), lambda i: (0, i, 0)),
                   pl.BlockSpec((HY_ORDER, C), const)),
        compiler_params=_cparams(("arbitrary",)),
        name="hy_filter_mlp",
    )(z, zrev, w1, b1.reshape(1, nf), w2, b2.reshape(1, nf), w3, b3.reshape(1, nf), freq.reshape(1, nf), wout, deltas)


def _dft_tables(L):
    N = 2 * L
    N2 = DFT_N2
    N1 = N // N2
    K1 = N1 // 2
    kept = K1 + 1
    rows = -(-kept // SUBLANES) * SUBLANES
    live = (np.arange(rows) < kept).astype(np.float64)[None, :, None]
    pair = np.where((np.arange(rows) == 0) | (np.arange(rows) == K1), 1.0, 2.0)[None, :, None] * live
    n1 = np.arange(K1)[None, None, :]
    k1 = np.arange(rows)[None, :, None]
    n2 = np.arange(N2)[:, None, None]
    ang = 2.0 * np.pi * (((n1 * N2 + n2) * k1) % N) / N
    g_fwd = np.concatenate([np.cos(ang) * live, -np.sin(ang) * live], axis=1)
    t = lambda a: np.transpose(a, (0, 2, 1))
    g_inv = np.concatenate([t(np.cos(ang) * pair), t(-np.sin(ang) * pair)], axis=2) / N
    a2 = 2.0 * np.pi * ((np.arange(N2)[:, None] * np.arange(N2)[None, :]) % N2) / N2
    cr, ci = np.cos(a2), -np.sin(a2)
    m2 = np.block([[cr, -ci], [ci, cr]])
    m2i = np.block([[cr, ci], [-ci, cr]])
    as_bf = lambda a: jnp.asarray(a, BF16)
    return as_bf(g_fwd), as_bf(g_inv), as_bf(m2), as_bf(m2i)


DFT_NB = 8
N_HALF = HY_WIDTH // LANES


def _lane_half_specs(k1n, nb, part):
    return [pl.BlockSpec((1, k1n, nb, LANES), lambda s, j, h=h: (s, 0, j, part * N_HALF + h)) for h in range(N_HALF)]


def _major_half_specs(k1n, nb):
    return [pl.BlockSpec((1, 1, k1n, nb, LANES), lambda s, j, h=h: (s, h, 0, j, 0)) for h in range(N_HALF)]


def _ld_time(ref, j, nb):
    k1n = ref.shape[-3]
    return ref.reshape(k1n * nb, LANES)[pl.ds(j, k1n, stride=nb), :]


def _st_time(ref, h, j, nb, val):
    k1n = ref.shape[2]
    ref.reshape(N_HALF * k1n * nb, LANES)[pl.ds(h * k1n * nb + j, k1n, stride=nb), :] = val


def _dft_a_kernel(x0_ref, x1_ref, g_ref, o_ref, *, nb):
    n1 = o_ref.shape[4]
    for j in range(nb):
        parts = []
        for x_ref in (x0_ref, x1_ref):
            parts.append(_ld_time(x_ref, j, nb))
        xj = jnp.concatenate(parts, axis=1).astype(BF16)
        r = _dot(g_ref[j], xj)
        for h in range(N_HALF):
            o_ref[0, h, 0, j] = r[:n1, h * LANES:(h + 1) * LANES]
            o_ref[0, h, 1, j] = r[n1:, h * LANES:(h + 1) * LANES]


def _dft_stage_a(x, x_specs, g_fwd):
    S = x.shape[0]
    N2, two_n1, K1 = g_fwd.shape
    N1 = two_n1 // 2
    nb = DFT_NB
    return pl.pallas_call(
        functools.partial(_dft_a_kernel, nb=nb),
        out_shape=jax.ShapeDtypeStruct((S, N_HALF, 2, N2, N1, LANES), F32),
        grid=(S, N2 // nb),
        in_specs=x_specs + [pl.BlockSpec((nb, 2 * N1, K1), lambda s, j: (j, 0, 0))],
        out_specs=pl.BlockSpec((1, N_HALF, 2, nb, N1, LANES), lambda s, j: (s, 0, 0, j, 0, 0)),
        compiler_params=_cparams(("parallel", "parallel")),
        name="hy_dft_a",
    )(x, x, g_fwd)


def _k1_rows(ref, h, j):
    _, nh, _, n2, tk1, _ = ref.shape
    return ref.reshape(nh * 2 * n2 * tk1, LANES), pl.ds(h * 2 * n2 * tk1 + j, 2 * n2, stride=tk1)


def _ld_k1(ref, h, j):
    r2, rows = _k1_rows(ref, h, j)
    return r2[rows, :]


def _st_k1(ref, h, j, val):
    r2, rows = _k1_rows(ref, h, j)
    r2[rows, :] = val


def _k1_slab(ref, j):
    return jnp.concatenate([_ld_k1(ref, h, j) for h in range(N_HALF)], axis=1)


def _dft_b_filter_kernel(s1_ref, s2_ref, m2_ref, asum_ref, o_ref, *, tk1):
    n2 = s1_ref.shape[3]
    C = o_ref.shape[4]
    scale = 1.0 / asum_ref[0]
    for j in range(tk1):
        sign = 1.0 if j % 2 == 0 else -1.0
        s = _k1_slab(s1_ref, j) + sign * _k1_slab(s2_ref, j)
        xk = _dot(m2_ref[...], s.astype(BF16)) * scale
        o_ref[0, j] = xk.reshape(2, n2, C)


def _dft_stage_b_filter(sa, m2, asum):
    _, _, _, N2, N1, _ = sa.shape
    C = HY_WIDTH
    tk1 = SUBLANES
    blk = (1, N_HALF, 2, N2, tk1, LANES)
    return pl.pallas_call(
        functools.partial(_dft_b_filter_kernel, tk1=tk1),
        out_shape=jax.ShapeDtypeStruct((HY_ORDER, N1, 2, N2, C), F32),
        grid=(HY_ORDER, N1 // tk1),
        in_specs=[
            pl.BlockSpec(blk, lambda o, i: (o, 0, 0, 0, i, 0)),
            pl.BlockSpec(blk, lambda o, i: (HY_ORDER + o, 0, 0, 0, i, 0)),
            pl.BlockSpec((2 * N2, 2 * N2), lambda o, i: (0, 0)),
            pl.BlockSpec((1, 1, C), lambda o, i: (o, 0, 0)),
        ],
        out_specs=pl.BlockSpec((1, tk1, 2, N2, C), lambda o, i: (o, i, 0, 0, 0)),
        compiler_params=_cparams(("parallel", "parallel")),
        name="hy_dft_b_filter",
    )(sa, sa, m2, asum.reshape(HY_ORDER, 1, C))


def _dft_b_conv_kernel(s_ref, kf_ref, m2_ref, m2i_ref, o_ref, *, tk1):
    n2 = s_ref.shape[3]
    for j in range(tk1):
        x = _dot(m2_ref[...], _k1_slab(s_ref, j).astype(BF16))
        xr, xi = x[:n2], x[n2:]
        kr = kf_ref[0, j, 0]
        ki = kf_ref[0, j, 1]
        y = jnp.concatenate([xr * kr - xi * ki, xr * ki + xi * kr], axis=0).astype(BF16)
        cc = _dot(m2i_ref[...], y)
        for h in range(N_HALF):
            _st_k1(o_ref, h, j, cc[:, h * LANES:(h + 1) * LANES])


def _dft_stage_b_conv(sa, kspec, order, m2, m2i):
    S, _, _, N2, N1, _ = sa.shape
    C = HY_WIDTH
    tk1 = SUBLANES
    blk = (1, N_HALF, 2, N2, tk1, LANES)
    return pl.pallas_call(
        functools.partial(_dft_b_conv_kernel, tk1=tk1),
        out_shape=jax.ShapeDtypeStruct(sa.shape, F32),
        grid=(N1 // tk1, S),
        in_specs=[
            pl.BlockSpec(blk, lambda i, s: (s, 0, 0, 0, i, 0)),
            pl.BlockSpec((1, tk1, 2, N2, C), lambda i, s: (order, i, 0, 0, 0)),
            pl.BlockSpec((2 * N2, 2 * N2), lambda i, s: (0, 0)),
            pl.BlockSpec((2 * N2, 2 * N2), lambda i, s: (0, 0)),
        ],
        out_specs=pl.BlockSpec(blk, lambda i, s: (s, 0, 0, 0, i, 0)),
        compiler_params=_cparams(("parallel", "parallel")),
        name="hy_dft_b_conv",
    )(sa, kspec, m2, m2i)


def _dft_c_kernel(c_ref, gi_ref, z0_ref, z1_ref, g0_ref, g1_ref, d_ref, o_ref, *, nb):
    d = d_ref[0]
    for j in range(nb):
        cat = jnp.concatenate(
            [jnp.concatenate([c_ref[0, h, 0, j], c_ref[0, h, 1, j]], axis=0) for h in range(N_HALF)], axis=1)
        y = _dot(gi_ref[j], cat.astype(BF16))
        for h, (z_ref, g_ref) in enumerate(((z0_ref, g0_ref), (z1_ref, g1_ref))):
            lanes = slice(h * LANES, (h + 1) * LANES)
            _st_time(o_ref, h, j, nb, _ld_time(g_ref, j, nb) * (y[:, lanes] + _ld_time(z_ref, j, nb) * d[:, lanes]))


def _dft_stage_c(sc, g_inv, z, z_specs, gate, gate_specs, d):
    S, _, _, N2, N1, _ = sc.shape
    C = HY_WIDTH
    K1 = g_inv.shape[1]
    nb = DFT_NB
    return pl.pallas_call(
        functools.partial(_dft_c_kernel, nb=nb),
        out_shape=jax.ShapeDtypeStruct((S, N_HALF, K1, N2, LANES), F32),
        grid=(S, N2 // nb),
        in_specs=[
            pl.BlockSpec((1, N_HALF, 2, nb, N1, LANES), lambda s, j: (s, 0, 0, j, 0, 0)),
            pl.BlockSpec((nb, K1, 2 * N1), lambda s, j: (j, 0, 0)),
        ] + z_specs + gate_specs + [pl.BlockSpec((1, 1, C), lambda s, j: (0, 0, 0))],
        out_specs=pl.BlockSpec((1, N_HALF, K1, nb, LANES), lambda s, j: (s, 0, 0, j, 0)),
        compiler_params=_cparams(("parallel", "parallel")),
        name="hy_dft_c",
    )(sc, g_inv, z, z, gate, gate, d.reshape(1, 1, C))


def _hyena(hy, conv_w, conv_b, w1, b1, w2, b2, w3, b3, freq, wout, bias, tables):
    B, L, _ = hy.shape
    C = HY_WIDTH
    g_fwd, g_inv, m2, m2i = tables
    N2 = DFT_N2
    K1 = L // N2
    nb = DFT_NB
    fs, asum = _hyena_filter_time(L, w1, b1, w2, b2, w3, b3, freq, wout)
    fa = _dft_stage_a(fs.reshape(2 * HY_ORDER, K1, N2, C), _lane_half_specs(K1, nb, 0), g_fwd)
    kspec = _dft_stage_b_filter(fa, m2, asum)
    hc = _short_conv(hy, conv_w, conv_b)
    hc4 = hc.reshape(B, K1, N2, (HY_ORDER + 1) * C)
    z, z_specs = hc4, _lane_half_specs(K1, nb, 0)
    for o in range(HY_ORDER):
        sa = _dft_stage_a(z, z_specs, g_fwd)
        sc = _dft_stage_b_conv(sa, kspec, o, m2, m2i)
        z = _dft_stage_c(sc, g_inv, z, z_specs, hc4, _lane_half_specs(K1, nb, 1 + o), bias[o])
        z_specs = _major_half_specs(K1, nb)
    return z.reshape(B, N_HALF, L, LANES)


def _outproj_kernel(x_ref, at_ref, h_ref, gr_ref, zy_ref, w_ref, gt_ref, g_ref, b_ref, o_ref, *, alpha):
    lru = (h_ref[0, 0] + h_ref[1, 0]) * jax.nn.gelu(gr_ref[0])
    na = at_ref.shape[2]
    nl = lru.shape[1]
    m = (_dot(at_ref[0], w_ref[0:na]) + _dot(lru.astype(BF16), w_ref[na:na + nl])
         + sum(_dot(zy_ref[0, h].astype(BF16), w_ref[na + nl + h * LANES:na + nl + (h + 1) * LANES])
               for h in range(N_HALF)))
    o_ref[0] = _layer_norm(alpha * x_ref[0] + gt_ref[0] * m, g_ref[...], b_ref[...])


def _out_proj(x, attn, h, xg, zy, w_out, gt, ln_g, ln_b, alpha):
    B, L, D = x.shape
    T = min(512, L)
    C = LRU_WIDTH
    row = lambda b, i: (b, i, 0)
    vec = lambda b, i: (b, 0, 0)
    const = lambda b, i: (0, 0)
    return pl.pallas_call(
        functools.partial(_outproj_kernel, alpha=alpha),
        out_shape=jax.ShapeDtypeStruct((B, L, D), F32),
        grid=(B, L // T),
        in_specs=[
            pl.BlockSpec((1, T, D), row),
            pl.BlockSpec((1, T, IN_Q), row),
            pl.BlockSpec((2, 1, T, C), lambda b, i: (0, b, i, 0)),
            pl.BlockSpec((1, T, C), lambda b, i: (b, i, 1)),
            pl.BlockSpec((1, N_HALF, T, LANES), lambda b, i: (b, 0, i, 0)),
            pl.BlockSpec(w_out.shape, const),
            pl.BlockSpec((1, 1, D), vec),
            pl.BlockSpec((1, D), const),
            pl.BlockSpec((1, D), const),
        ],
        out_specs=pl.BlockSpec((1, T, D), row),
        compiler_params=_cparams(("parallel", "parallel")),
        name="out_proj",
    )(x, attn, h, xg, zy, w_out, gt, ln_g.reshape(1, D), ln_b.reshape(1, D))


FFN_CHUNK = 256


def _ffn_kernel(xp_ref, x_ref, xn_ref, sc_ref, sh_ref, gt_ref, wu_ref, cw_ref, cb_ref, wd_ref, g_ref, b_ref,
                o_ref, u_sc, h_sc, acc_sc, *, T, nt, nf, alpha):
    i = pl.program_id(1)
    H = SUBLANES
    R = T + 2 * H
    F = FFN_CHUNK
    sc = 1.0 + sc_ref[0]
    sh = sh_ref[0]
    u_sc[H:H + T, :] = (x_ref[0] * sc + sh).astype(BF16)
    u_sc[0:H, :] = jnp.where(i > 0, xp_ref[0] * sc + sh, 0.0).astype(BF16)
    u_sc[H + T:R, :] = jnp.where(i < nt - 1, xn_ref[0] * sc + sh, 0.0).astype(BF16)
    acc_sc[...] = jnp.zeros(acc_sc.shape, F32)

    def up(f, slot):
        h_sc[slot] = _dot(u_sc[...], wu_ref[f])

    def down(f, slot):
        h = h_sc[slot]
        cw = cw_ref[f]
        y = cb_ref[f] + pltpu.roll(h, 1, 0) * cw[0:1] + h * cw[1:2] + pltpu.roll(h, R - 1, 0) * cw[2:3]
        y = y[H:H + T]
        act = (jax.nn.gelu(y[:, :F]) * y[:, F:]).astype(BF16)
        acc_sc[...] += _dot(act, wd_ref[f])

    def pair(c2, carry):
        f = 2 * c2
        up(f + 1, 1)
        down(f, 0)
        up(jnp.minimum(f + 2, nf - 1), 0)
        down(f + 1, 1)
        return carry

    up(0, 0)
    lax.fori_loop(0, nf // 2, pair, 0)
    if nf % 2 == 1:
        down(nf - 1, 0)
    o_ref[0] = _layer_norm(alpha * x_ref[0] + gt_ref[0] * acc_sc[...], g_ref[...], b_ref[...])


def _conv_ffn(x, sc, sh, gt, w_up, conv_w, conv_b, w_down, ln_g, ln_b, alpha):
    B, L, D = x.shape
    d_ff = w_down.shape[0]
    T = min(512, L)
    F = FFN_CHUNK
    nt = L // T
    nf = d_ff // F
    hb = T // SUBLANES
    n_halo = L // SUBLANES
    nk = conv_w.shape[0]
    chunked = lambda w: jnp.concatenate([w[..., :d_ff].reshape(w.shape[0], nf, F),
                                         w[..., d_ff:].reshape(w.shape[0], nf, F)], axis=-1).transpose(1, 0, 2)
    wu = chunked(w_up)
    cw = chunked(conv_w)
    cb = chunked(conv_b.reshape(1, 2 * d_ff))
    wd = w_down.reshape(nf, F, D)
    row = lambda b, i: (b, i, 0)
    vec = lambda b, i: (b, 0, 0)
    const2 = lambda b, i: (0, 0)
    const3 = lambda b, i: (0, 0, 0)
    resident = lambda shape, imap: pl.BlockSpec(shape, imap, pipeline_mode=pl.Buffered(1))
    return pl.pallas_call(
        functools.partial(_ffn_kernel, T=T, nt=nt, nf=nf, alpha=alpha),
        out_shape=jax.ShapeDtypeStruct((B, L, D), F32),
        grid=(B, nt),
        in_specs=[
            pl.BlockSpec((1, SUBLANES, D), lambda b, i: (b, jnp.maximum(i * hb - 1, 0), 0)),
            pl.BlockSpec((1, T, D), row),
            pl.BlockSpec((1, SUBLANES, D), lambda b, i: (b, jnp.minimum((i + 1) * hb, n_halo - 1), 0)),
            pl.BlockSpec((1, 1, D), vec),
            pl.BlockSpec((1, 1, D), vec),
            pl.BlockSpec((1, 1, D), vec),
            resident((nf, D, 2 * F), const3),
            resident((nf, nk, 2 * F), const3),
            resident((nf, 1, 2 * F), const3),
            resident((nf, F, D), const3),
            pl.BlockSpec((1, D), const2),
            pl.BlockSpec((1, D), const2),
        ],
        out_specs=pl.BlockSpec((1, T, D), row),
        scratch_shapes=[
            pltpu.VMEM((T + 2 * SUBLANES, D), BF16),
            pltpu.VMEM((2, T + 2 * SUBLANES, 2 * F), F32),
            pltpu.VMEM((T, D), F32),
        ],
        compiler_params=_cparams(("parallel", "parallel")),
        name="conv_ffn",
    )(x, x, x, sc, sh, gt, wu, cw, cb, wd, ln_g.reshape(1, D), ln_b.reshape(1, D))


def _trunk(x, mod, p):
    B, L, D = x.shape
    depth = mod.shape[0]
    alpha = (2 * depth) ** 0.25
    rope = _rope_tables(L)
    tables = _dft_tables(L)
    for l in range(depth):
        m6 = mod[l].reshape(B, 6, 1, D)
        sh1, sc1, gt1, sh2, sc2, gt2 = (m6[:, j] for j in range(6))
        q, k, v, xg, hy = _in_proj(x, sc1, sh1, p['w_in'][l], p['q_gain'][l], p['k_gain'][l], rope)
        attn = _attention(q, k, v)
        h = _rglru(xg, p['lru_conv_w'][l], p['lru_conv_b'][l], p['lru_wa'][l], p['lru_ba'][l],
                   p['lru_wx'][l], p['lru_bx'][l], p['lru_lambda'][l])
        zy = _hyena(hy, p['hy_conv_w'][l], p['hy_conv_b'][l], p['hy_w1'][l], p['hy_b1'][l], p['hy_w2'][l],
                    p['hy_b2'][l], p['hy_w3'][l], p['hy_b3'][l], p['hy_freq'][l], p['hy_wout'][l],
                    p['hy_bias'][l], tables)
        x = _out_proj(x, attn, h, xg, zy, p['w_out'][l], gt1, p['ln1_g'][l], p['ln1_b'][l], alpha)
        x = _conv_ffn(x, sc2, sh2, gt2, p['ffn_w_up'][l], p['ffn_conv_w'][l], p['ffn_conv_b'][l],
                      p['ffn_w_down'][l], p['ln2_g'][l], p['ln2_b'][l], alpha)
    return x


def kernel(x_prompt, x_sample, c_prompt, c_sample, ada_w, ada_b, w_in, q_gain, k_gain, lru_conv_w, lru_conv_b, lru_wa, lru_ba, lru_wx, lru_bx, lru_lambda, hy_conv_w, hy_conv_b, hy_w1, hy_b1, hy_w2, hy_b2, hy_w3, hy_b3, hy_freq, hy_wout, hy_bias, w_out, ln1_g, ln1_b, ffn_w_up, ffn_conv_w, ffn_conv_b, ffn_w_down, ln2_g, ln2_b):
    p = dict(
        w_in=w_in.astype(BF16), q_gain=q_gain, k_gain=k_gain, lru_conv_w=lru_conv_w, lru_conv_b=lru_conv_b,
        lru_wa=lru_wa, lru_ba=lru_ba, lru_wx=lru_wx, lru_bx=lru_bx, lru_lambda=lru_lambda,
        hy_conv_w=hy_conv_w, hy_conv_b=hy_conv_b, hy_w1=hy_w1, hy_b1=hy_b1, hy_w2=hy_w2, hy_b2=hy_b2,
        hy_w3=hy_w3, hy_b3=hy_b3, hy_freq=hy_freq, hy_wout=hy_wout, hy_bias=hy_bias,
        w_out=w_out.astype(BF16), ln1_g=ln1_g, ln1_b=ln1_b, ffn_w_up=ffn_w_up.astype(BF16),
        ffn_conv_w=ffn_conv_w, ffn_conv_b=ffn_conv_b, ffn_w_down=ffn_w_down.astype(BF16), ln2_g=ln2_g, ln2_b=ln2_b,
    )
    nb = x_prompt.shape[0]
    mod = _ada_mod(jnp.concatenate([c_prompt, c_sample], axis=0), ada_w, ada_b)
    y_prompt = _trunk(x_prompt, mod[:, :nb], p)
    y_sample = _trunk(x_sample, mod[:, nb:], p)
    return (y_prompt, y_sample)
```

```python
import functools
import math

import numpy as np
import jax
import jax.numpy as jnp
from jax import lax
from jax.experimental import pallas as pl
from jax.experimental.pallas import tpu as pltpu

F32 = jnp.float32
BF16 = jnp.bfloat16
HIGHEST = lax.Precision.HIGHEST

GRID_W = 64
HEAD_DIM = 64
N_HEADS = 8
N_KV_HEADS = 2
KV_GROUP = N_HEADS // N_KV_HEADS
ROPE_THETA = 10000.0
ROPE_FREQS = HEAD_DIM // 4
QK_EPS = 1e-6
LRU_WIDTH = 256
LRU_HEADS = 4
LRU_C = 8.0
HY_WIDTH = 256
HY_ORDER = 2
HY_BANDS = 16
HY_MIN_DECAY = abs(math.log(1e-2)) / 1.5
HY_MAX_DECAY = abs(math.log(1e-2)) / 0.3
LN_EPS = 1e-5
IN_Q = N_HEADS * HEAD_DIM
IN_KV = N_KV_HEADS * HEAD_DIM

LANES = 128
SUBLANES = 8
DFT_N2 = 128
VMEM_LIMIT = 48 * 1024 * 1024


def _cparams(sem):
    return pltpu.CompilerParams(dimension_semantics=sem, vmem_limit_bytes=VMEM_LIMIT)


def _dot(a, b):
    return jnp.dot(a, b, preferred_element_type=F32)


def _layer_norm(y, g, b):
    mu = jnp.mean(y, axis=-1, keepdims=True)
    yc = y - mu
    var = jnp.mean(yc * yc, axis=-1, keepdims=True)
    return yc * lax.rsqrt(var + LN_EPS) * g + b


def _ada_kernel(c_ref, w_ref, b_ref, o_ref):
    c = c_ref[...]
    s = c * jax.nn.sigmoid(c)
    o_ref[0] = jnp.dot(s, w_ref[0], precision=HIGHEST, preferred_element_type=F32) + b_ref[0]


def _ada_mod(c_all, ada_w, ada_b):
    depth, d, n = ada_w.shape
    rows = c_all.shape[0]
    tn = 768
    return pl.pallas_call(
        _ada_kernel,
        out_shape=jax.ShapeDtypeStruct((depth, rows, n), F32),
        grid=(depth, n // tn),
        in_specs=[
            pl.BlockSpec((rows, d), lambda l, j: (0, 0)),
            pl.BlockSpec((1, d, tn), lambda l, j: (l, 0, j)),
            pl.BlockSpec((1, 1, tn), lambda l, j: (l, 0, j)),
        ],
        out_specs=pl.BlockSpec((1, rows, tn), lambda l, j: (l, 0, j)),
        compiler_params=_cparams(("parallel", "parallel")),
        name="ada_mod",
    )(c_all, ada_w, ada_b.reshape(depth, 1, n))


def _rope_tables(L):
    rows = L // GRID_W
    row = np.repeat(np.arange(rows, dtype=np.float64), GRID_W)
    col = np.tile(np.arange(GRID_W, dtype=np.float64), rows)
    inv = ROPE_THETA ** (-np.arange(ROPE_FREQS, dtype=np.float64) / ROPE_FREQS)
    ar = row[:, None] * inv
    ac = col[:, None] * inv
    zeros = np.zeros_like(ar)
    cos = np.concatenate([np.cos(ar), np.cos(ar), np.cos(ac), np.cos(ac)], axis=1)
    sin_up = np.concatenate([-np.sin(ar), zeros, -np.sin(ac), zeros], axis=1)
    sin_dn = np.concatenate([zeros, np.sin(ar), zeros, np.sin(ac)], axis=1)
    two = lambda t: jnp.asarray(np.concatenate([t, t], axis=1), F32)
    return two(cos), two(sin_up), two(sin_dn)


def _inproj_kernel(x_ref, sc_ref, sh_ref, w_ref, qg_ref, kg_ref, cos_ref, sup_ref, sdn_ref, bd_ref,
                   q_ref, k_ref, v_ref, xg_ref, hy_ref):
    u = (x_ref[0] * (1.0 + sc_ref[0]) + sh_ref[0]).astype(BF16)
    proj = _dot(u, w_ref[...])
    cos = cos_ref[...]
    sup = sup_ref[...]
    sdn = sdn_ref[...]
    bd = bd_ref[...]
    half = ROPE_FREQS

    def norm_rope(t, gain):
        ms = _dot((t * t).astype(BF16), bd)
        tn = t * lax.rsqrt(ms + QK_EPS) * gain
        return (tn * cos + pltpu.roll(tn, LANES - half, 1) * sup + pltpu.roll(tn, half, 1) * sdn)

    qg = qg_ref[...]
    for j in range(IN_Q // LANES):
        sl = slice(j * LANES, (j + 1) * LANES)
        q_ref[0, :, sl] = (norm_rope(proj[:, sl], qg) * (HEAD_DIM ** -0.5 * math.log2(math.e))).astype(BF16)
    k_ref[0] = norm_rope(proj[:, IN_Q:IN_Q + IN_KV], kg_ref[...]).astype(BF16)
    v_ref[0] = proj[:, IN_Q + IN_KV:IN_Q + 2 * IN_KV].astype(BF16)
    o = IN_Q + 2 * IN_KV
    xg_ref[0] = proj[:, o:o + 2 * LRU_WIDTH]
    hy_ref[0] = proj[:, o + 2 * LRU_WIDTH:]


def _in_proj(x, sc, sh, w_in, q_gain, k_gain, rope):
    B, L, D = x.shape
    n_in = w_in.shape[1]
    T = min(512, L)
    cos, sup, sdn = rope
    bd = jnp.asarray(np.kron(np.eye(2), np.full((HEAD_DIM, HEAD_DIM), 1.0 / HEAD_DIM)), BF16)
    qg = jnp.tile(q_gain, 2).reshape(1, LANES)
    kg = jnp.tile(k_gain, 2).reshape(1, LANES)
    n_hy = n_in - IN_Q - 2 * IN_KV - 2 * LRU_WIDTH
    row = lambda b, i: (b, i, 0)
    vec = lambda b, i: (b, 0, 0)
    tab = lambda b, i: (i, 0)
    const = lambda b, i: (0, 0)
    return pl.pallas_call(
        _inproj_kernel,
        out_shape=(
            jax.ShapeDtypeStruct((B, L, IN_Q), BF16),
            jax.ShapeDtypeStruct((B, L, IN_KV), BF16),
            jax.ShapeDtypeStruct((B, L, IN_KV), BF16),
            jax.ShapeDtypeStruct((B, L, 2 * LRU_WIDTH), F32),
            jax.ShapeDtypeStruct((B, L, n_hy), F32),
        ),
        grid=(B, L // T),
        in_specs=[
            pl.BlockSpec((1, T, D), row),
            pl.BlockSpec((1, 1, D), vec),
            pl.BlockSpec((1, 1, D), vec),
            pl.BlockSpec((D, n_in), const),
            pl.BlockSpec((1, LANES), const),
            pl.BlockSpec((1, LANES), const),
            pl.BlockSpec((T, LANES), tab),
            pl.BlockSpec((T, LANES), tab),
            pl.BlockSpec((T, LANES), tab),
            pl.BlockSpec((LANES, LANES), const),
        ],
        out_specs=(
            pl.BlockSpec((1, T, IN_Q), row),
            pl.BlockSpec((1, T, IN_KV), row),
            pl.BlockSpec((1, T, IN_KV), row),
            pl.BlockSpec((1, T, 2 * LRU_WIDTH), row),
            pl.BlockSpec((1, T, n_hy), row),
        ),
        compiler_params=_cparams(("parallel", "parallel")),
        name="in_proj",
    )(x, sc, sh, w_in, qg, kg, cos, sup, sdn, bd)


V_ROWS = HEAD_DIM + 16


def _attn_kernel(qt_ref, k_ref, vt_ref, o_ref, acc_sc, s_sc, *, tk, nk):
    tq = qt_ref.shape[-1]
    acc_sc[...] = jnp.zeros(acc_sc.shape, F32)

    def scores_h(c, slot, h):
        kc = k_ref[0, 0, pl.ds(pl.multiple_of(c * tk, tk), tk), :]
        s = _dot(kc, qt_ref[0, 0, h])
        s_sc[slot, h] = s
        return jnp.max(s, axis=0, keepdims=True)

    def consume_h(c, slot, h, m_prev, m_chunk):
        vc = vt_ref[0, 0, c]
        m_new = jnp.maximum(m_prev, m_chunk)
        p = jnp.exp2(s_sc[slot, h] - m_new).astype(BF16)
        alpha = jnp.exp2(m_prev - m_new)
        acc_sc[h] = alpha * acc_sc[h] + _dot(vc, p)
        return m_new

    def step(c, c_next, slot, carry):
        ms, mc = carry
        new_m, new_c = [], []
        for h in range(KV_GROUP):
            new_c.append(scores_h(c_next, 1 - slot, h))
            new_m.append(consume_h(c, slot, h, ms[h], mc[h]))
        return tuple(new_m), tuple(new_c)

    def pair(c2, carry):
        c = 2 * c2
        carry = step(c, c + 1, 0, carry)
        return step(c + 1, jnp.minimum(c + 2, nk - 1), 1, carry)

    m0 = tuple(jnp.full((1, tq), -jnp.inf, F32) for _ in range(KV_GROUP))
    c0 = tuple(scores_h(0, 0, h) for h in range(KV_GROUP))
    lax.fori_loop(0, nk // 2, pair, (m0, c0))
    for h in range(KV_GROUP):
        acc = acc_sc[h]
        o_ref[0, 0, h] = (acc[:HEAD_DIM] / acc[HEAD_DIM:HEAD_DIM + 1]).astype(o_ref.dtype)


def _attention(q, k, v):
    B, L, _ = q.shape
    tq = min(512, L)
    tk = min(512, L // 2)
    nk = L // tk
    assert nk % 2 == 0
    qt = q.reshape(B, L, N_KV_HEADS, KV_GROUP, HEAD_DIM).transpose(0, 2, 3, 4, 1)
    kh = k.reshape(B, L, N_KV_HEADS, HEAD_DIM).transpose(0, 2, 1, 3)
    vt = v.reshape(B, nk, tk, N_KV_HEADS, HEAD_DIM).transpose(0, 3, 1, 4, 2)
    ones = jnp.ones((B, N_KV_HEADS, nk, 1, tk), BF16)
    pad = jnp.zeros((B, N_KV_HEADS, nk, V_ROWS - HEAD_DIM - 1, tk), BF16)
    vt = jnp.concatenate([vt, ones, pad], axis=3)
    ot = pl.pallas_call(
        functools.partial(_attn_kernel, tk=tk, nk=nk),
        out_shape=jax.ShapeDtypeStruct((B, N_KV_HEADS, KV_GROUP, HEAD_DIM, L), BF16),
        grid=(B, N_KV_HEADS, L // tq),
        in_specs=[
            pl.BlockSpec((1, 1, KV_GROUP, HEAD_DIM, tq), lambda b, g, i: (b, g, 0, 0, i)),
            pl.BlockSpec((1, 1, L, HEAD_DIM), lambda b, g, i: (b, g, 0, 0)),
            pl.BlockSpec((1, 1, nk, V_ROWS, tk), lambda b, g, i: (b, g, 0, 0, 0)),
        ],
        out_specs=pl.BlockSpec((1, 1, KV_GROUP, HEAD_DIM, tq), lambda b, g, i: (b, g, 0, 0, i)),
        scratch_shapes=[pltpu.VMEM((KV_GROUP, V_ROWS, tq), F32), pltpu.VMEM((2, KV_GROUP, tk, tq), F32)],
        compiler_params=_cparams(("parallel", "parallel", "parallel")),
        name="attention",
    )(qt, kh, vt)
    return ot.transpose(0, 4, 1, 2, 3).reshape(B, L, IN_Q)


def _lru_kernel(xp_ref, x_ref, xn_ref, cw_ref, cb_ref, wa_ref, ba_ref, wx_ref, bx_ref, lam_ref,
                o_ref, xe_sc, a_sc, b_sc, hs_sc, h_sc, *, T, nt):
    d = pl.program_id(0)
    i = pl.program_id(1)
    tile = jnp.where(d == 0, i, nt - 1 - i)
    nb, _, C = x_ref.shape
    nh = C // LANES
    H = SUBLANES
    keep_prev = (tile > 0).astype(F32)
    keep_next = (tile < nt - 1).astype(F32)
    for b in range(nb):
        for hf in range(nh):
            lanes = slice(hf * LANES, (hf + 1) * LANES)
            xe_sc[hf, pl.ds(b, H, stride=nb), :] = xp_ref[b, :, lanes] * keep_prev
            xe_sc[hf, pl.ds(H * nb + b, T, stride=nb), :] = x_ref[b, :, lanes]
            xe_sc[hf, pl.ds((H + T) * nb + b, H, stride=nb), :] = xn_ref[b, :, lanes] * keep_next
    cw = cw_ref[...]
    halves = []
    for hf in range(nh):
        lanes = slice(hf * LANES, (hf + 1) * LANES)
        acc = cb_ref[:, lanes]
        for k in range(cw.shape[0]):
            acc = acc + xe_sc[hf, (H - 2 + k) * nb:(H - 2 + k + T) * nb, :] * cw[k:k + 1, lanes]
        halves.append(acc)
    xc = jnp.concatenate(halves, axis=1)
    xb = xc.astype(BF16)
    sigmoid = lambda v: 0.5 * jnp.tanh(0.5 * v) + 0.5
    r = sigmoid(_dot(xb, wa_ref[0]) + ba_ref[0])
    ig = sigmoid(_dot(xb, wx_ref[0]) + bx_ref[0])
    lam = lam_ref[0]
    softplus_neg = jnp.maximum(-lam, 0.0) + jnp.log1p(jnp.exp(-jnp.abs(lam)))
    log_a = -LRU_C * r * softplus_neg
    a = jnp.exp(log_a)
    bb = jnp.sqrt(-jnp.tanh(log_a) * (1.0 + a * a)) * (ig * xc)
    for hf in range(nh):
        a_sc[hf] = a[:, hf * LANES:(hf + 1) * LANES]
        b_sc[hf] = bb[:, hf * LANES:(hf + 1) * LANES]

    @pl.when(i == 0)
    def _():
        h_sc[...] = jnp.zeros(h_sc.shape, F32)

    def body(s, hs):
        t = jnp.where(d == 0, s, T - 1 - s)
        rows = pl.ds(pl.multiple_of(t * nb, nb), nb)
        new = []
        for hf in range(nh):
            h = a_sc[hf, rows, :] * hs[hf] + b_sc[hf, rows, :]
            hs_sc[hf, rows, :] = h
            new.append(h)
        return tuple(new)

    hs = lax.fori_loop(0, T, body, tuple(h_sc[hf] for hf in range(nh)), unroll=8)
    for hf in range(nh):
        h_sc[hf] = hs[hf]
        for b in range(nb):
            o_ref[0, b, :, hf * LANES:(hf + 1) * LANES] = hs_sc[hf, pl.ds(b, T, stride=nb), :]


def _rglru(xg, conv_w, conv_b, wa, ba, wx, bx, lam):
    B, L, _ = xg.shape
    C = LRU_WIDTH
    assert B == SUBLANES
    T = min(256, L)
    nt = L // T
    hb = T // SUBLANES
    n_halo = L // SUBLANES

    def tile_of(d, i):
        return jnp.where(d == 0, i, nt - 1 - i)

    def blockdiag(w):
        eye = jnp.eye(LRU_HEADS, dtype=w.dtype)
        return jnp.einsum('dhij,hg->dhigj', w, eye).reshape(2, C, C).astype(BF16)

    kern = functools.partial(_lru_kernel, T=T, nt=nt)
    return pl.pallas_call(
        kern,
        out_shape=jax.ShapeDtypeStruct((2, B, L, C), F32),
        grid=(2, nt),
        in_specs=[
            pl.BlockSpec((B, SUBLANES, C), lambda d, i: (0, jnp.maximum(tile_of(d, i) * hb - 1, 0), 0)),
            pl.BlockSpec((B, T, C), lambda d, i: (0, tile_of(d, i), 0)),
            pl.BlockSpec((B, SUBLANES, C), lambda d, i: (0, jnp.minimum((tile_of(d, i) + 1) * hb, n_halo - 1), 0)),
            pl.BlockSpec(conv_w.shape, lambda d, i: (0, 0)),
            pl.BlockSpec((1, C), lambda d, i: (0, 0)),
            pl.BlockSpec((1, C, C), lambda d, i: (d, 0, 0)),
            pl.BlockSpec((1, 1, C), lambda d, i: (d, 0, 0)),
            pl.BlockSpec((1, C, C), lambda d, i: (d, 0, 0)),
            pl.BlockSpec((1, 1, C), lambda d, i: (d, 0, 0)),
            pl.BlockSpec((1, 1, C), lambda d, i: (d, 0, 0)),
        ],
        out_specs=pl.BlockSpec((1, B, T, C), lambda d, i: (d, 0, tile_of(d, i), 0)),
        scratch_shapes=[
            pltpu.VMEM((C // LANES, (T + 2 * SUBLANES) * B, LANES), F32),
            pltpu.VMEM((C // LANES, T * B, LANES), F32),
            pltpu.VMEM((C // LANES, T * B, LANES), F32),
            pltpu.VMEM((C // LANES, T * B, LANES), F32),
            pltpu.VMEM((C // LANES, B, LANES), F32),
        ],
        compiler_params=_cparams(("arbitrary", "arbitrary")),
        name="rglru",
    )(xg, xg, xg, conv_w, conv_b.reshape(1, C), blockdiag(wa), ba.reshape(2, 1, C),
      blockdiag(wx), bx.reshape(2, 1, C), lam.reshape(2, 1, C))


def _shortconv_kernel(xp_ref, x_ref, xn_ref, w_ref, b_ref, o_ref, xe_sc, *, T, nt):
    i = pl.program_id(1)
    H = SUBLANES
    xe_sc[H:H + T, :] = x_ref[0]
    xe_sc[0:H, :] = jnp.where(i > 0, xp_ref[0], 0.0)
    xe_sc[H + T:2 * H + T, :] = jnp.where(i < nt - 1, xn_ref[0], 0.0)
    w = w_ref[...]
    y = b_ref[...]
    for k in range(w.shape[0]):
        y = y + xe_sc[H - 1 + k:H - 1 + k + T, :] * w[k:k + 1]
    o_ref[0] = y


def _short_conv(x, w, b):
    B, L, C = x.shape
    T = min(512, L)
    nt = L // T
    hb = T // SUBLANES
    n_halo = L // SUBLANES
    return pl.pallas_call(
        functools.partial(_shortconv_kernel, T=T, nt=nt),
        out_shape=jax.ShapeDtypeStruct((B, L, C), F32),
        grid=(B, nt),
        in_specs=[
            pl.BlockSpec((1, SUBLANES, C), lambda b_, i: (b_, jnp.maximum(i * hb - 1, 0), 0)),
            pl.BlockSpec((1, T, C), lambda b_, i: (b_, i, 0)),
            pl.BlockSpec((1, SUBLANES, C), lambda b_, i: (b_, jnp.minimum((i + 1) * hb, n_halo - 1), 0)),
            pl.BlockSpec(w.shape, lambda b_, i: (0, 0)),
            pl.BlockSpec((1, C), lambda b_, i: (0, 0)),
        ],
        out_specs=pl.BlockSpec((1, T, C), lambda b_, i: (b_, i, 0)),
        scratch_shapes=[pltpu.VMEM((T + 2 * SUBLANES, C), F32)],
        compiler_params=_cparams(("parallel", "parallel")),
        name="hy_short_conv",
    )(x, x, x, w, b.reshape(1, C))


def _filter_positions(L):
    t = np.linspace(0.0, 1.0, L)[:, None]
    w = 2.0 * math.pi * np.arange(L, dtype=np.float64)[:, None] / L
    f = np.linspace(1e-4, HY_BANDS - 1, HY_BANDS)[None, :]
    z = np.concatenate([t, np.cos(f * w), -np.sin(f * w)], axis=-1)
    zrev = np.concatenate([z[:1], z[:0:-1]], axis=0)
    return jnp.asarray(z, F32), jnp.asarray(zrev, F32)


def _filter_kernel(z_ref, zr_ref, w1_ref, b1_ref, w2_ref, b2_ref, w3_ref, b3_ref, fr_ref, wo_ref, dl_ref,
                   fs_ref, asum_ref, *, T):
    i = pl.program_id(0)
    fr = fr_ref[...]
    hd = lambda a, b: jnp.dot(a, b, precision=HIGHEST, preferred_element_type=F32)

    def mlp(z):
        h = jnp.sin(fr * (hd(z, w1_ref[...]) + b1_ref[...]))
        h = jnp.sin(fr * (hd(h, w2_ref[...]) + b2_ref[...]))
        h = jnp.sin(fr * (hd(h, w3_ref[...]) + b3_ref[...]))
        return hd(h, wo_ref[...])

    z = z_ref[...]
    zr = zr_ref[...]
    kz = mlp(z)
    kr = mlp(zr)
    dl = dl_ref[...]
    dec_f = jnp.exp(-z[:, 0:1] * dl)
    dec_r = jnp.exp(-zr[:, 0:1] * dl)
    C = HY_WIDTH
    first_row = (i * T + lax.broadcasted_iota(jnp.int32, (T, 1), 0)) == 0

    @pl.when(i == 0)
    def _():
        asum_ref[...] = jnp.zeros(asum_ref.shape, F32)

    for o in range(HY_ORDER):
        base = o * 2 * C
        kf = kz[:, base:base + C] * dec_f
        kb_here = kz[:, base + C:base + 2 * C] * dec_f
        kb_rev = kr[:, base + C:base + 2 * C] * dec_r
        first = kf + jnp.where(first_row, kb_here, 0.0)
        second = jnp.where(first_row, 0.0, kb_rev)
        fs_ref[o] = first
        fs_ref[HY_ORDER + o] = second
        asum_ref[o:o + 1, :] += jnp.sum(jnp.abs(first) + jnp.abs(second), axis=0, keepdims=True)


def _hyena_filter_time(L, w1, b1, w2, b2, w3, b3, freq, wout):
    z, zrev = _filter_positions(L)
    T = min(512, L)
    C = HY_WIDTH
    nemb = z.shape[1]
    nf = w2.shape[0]
    deltas = jnp.asarray(np.linspace(HY_MIN_DECAY, HY_MAX_DECAY, C)[None, :], F32)
    const = lambda i: (0, 0)
    return pl.pallas_call(
        functools.partial(_filter_kernel, T=T),
        out_shape=(jax.ShapeDtypeStruct((2 * HY_ORDER, L, C), F32), jax.ShapeDtypeStruct((HY_ORDER, C), F32)),
        grid=(L // T,),
        in_specs=[
            pl.BlockSpec((T, nemb), lambda i: (i, 0)),
            pl.BlockSpec((T, nemb), lambda i: (i, 0)),
            pl.BlockSpec((nemb, nf), const), pl.BlockSpec((1, nf), const),
            pl.BlockSpec((nf, nf), const), pl.BlockSpec((1, nf), const),
            pl.BlockSpec((nf, nf), const), pl.BlockSpec((1, nf), const),
            pl.BlockSpec((1, nf), const),
            pl.BlockSpec(wout.shape, const),
            pl.BlockSpec((1, C), const),
        ],
        out_specs=(pl.BlockSpec((2 * HY_ORDER, T, C), lambda i: (0, i, 0)),
                   pl.BlockSpec((HY_ORDER, C), const)),
        compiler_params=_cparams(("arbitrary",)),
        name="hy_filter_mlp",
    )(z, zrev, w1, b1.reshape(1, nf), w2, b2.reshape(1, nf), w3, b3.reshape(1, nf), freq.reshape(1, nf), wout, deltas)


def _dft_tables(L):
    N = 2 * L
    N2 = DFT_N2
    N1 = N // N2
    K1 = N1 // 2
    kept = K1 + 1
    rows = -(-kept // SUBLANES) * SUBLANES
    live = (np.arange(rows) < kept).astype(np.float64)[None, :, None]
    pair = np.where((np.arange(rows) == 0) | (np.arange(rows) == K1), 1.0, 2.0)[None, :, None] * live
    n1 = np.arange(K1)[None, None, :]
    k1 = np.arange(rows)[None, :, None]
    n2 = np.arange(N2)[:, None, None]
    ang = 2.0 * np.pi * (((n1 * N2 + n2) * k1) % N) / N
    g_fwd = np.concatenate([np.cos(ang) * live, -np.sin(ang) * live], axis=1)
    t = lambda a: np.transpose(a, (0, 2, 1))
    g_inv = np.concatenate([t(np.cos(ang) * pair), t(-np.sin(ang) * pair)], axis=2) / N
    a2 = 2.0 * np.pi * ((np.arange(N2)[:, None] * np.arange(N2)[None, :]) % N2) / N2
    cr, ci = np.cos(a2), -np.sin(a2)
    m2 = np.block([[cr, -ci], [ci, cr]])
    m2i = np.block([[cr, ci], [-ci, cr]])
    as_bf = lambda a: jnp.asarray(a, BF16)
    return as_bf(g_fwd), as_bf(g_inv), as_bf(m2), as_bf(m2i)


DFT_NB = 8
N_HALF = HY_WIDTH // LANES


def _lane_half_specs(k1n, nb, part):
    return [pl.BlockSpec((1, k1n, nb, LANES), lambda s, j, h=h: (s, 0, j, part * N_HALF + h)) for h in range(N_HALF)]


def _major_half_specs(k1n, nb):
    return [pl.BlockSpec((1, 1, k1n, nb, LANES), lambda s, j, h=h: (s, h, 0, j, 0)) for h in range(N_HALF)]


def _ld_time(ref, j, nb):
    k1n = ref.shape[-3]
    return ref.reshape(k1n * nb, LANES)[pl.ds(j, k1n, stride=nb), :]


def _st_time(ref, h, j, nb, val):
    k1n = ref.shape[2]
    ref.reshape(N_HALF * k1n * nb, LANES)[pl.ds(h * k1n * nb + j, k1n, stride=nb), :] = val


def _dft_a_kernel(x0_ref, x1_ref, g_ref, o_ref, *, nb):
    n1 = o_ref.shape[4]
    for j in range(nb):
        parts = []
        for x_ref in (x0_ref, x1_ref):
            parts.append(_ld_time(x_ref, j, nb))
        xj = jnp.concatenate(parts, axis=1).astype(BF16)
        r = _dot(g_ref[j], xj)
        for h in range(N_HALF):
            o_ref[0, h, 0, j] = r[:n1, h * LANES:(h + 1) * LANES]
            o_ref[0, h, 1, j] = r[n1:, h * LANES:(h + 1) * LANES]


def _dft_stage_a(x, x_specs, g_fwd):
    S = x.shape[0]
    N2, two_n1, K1 = g_fwd.shape
    N1 = two_n1 // 2
    nb = DFT_NB
    return pl.pallas_call(
        functools.partial(_dft_a_kernel, nb=nb),
        out_shape=jax.ShapeDtypeStruct((S, N_HALF, 2, N2, N1, LANES), F32),
        grid=(S, N2 // nb),
        in_specs=x_specs + [pl.BlockSpec((nb, 2 * N1, K1), lambda s, j: (j, 0, 0))],
        out_specs=pl.BlockSpec((1, N_HALF, 2, nb, N1, LANES), lambda s, j: (s, 0, 0, j, 0, 0)),
        compiler_params=_cparams(("parallel", "parallel")),
        name="hy_dft_a",
    )(x, x, g_fwd)


def _k1_rows(ref, h, j):
    _, nh, _, n2, tk1, _ = ref.shape
    return ref.reshape(nh * 2 * n2 * tk1, LANES), pl.ds(h * 2 * n2 * tk1 + j, 2 * n2, stride=tk1)


def _ld_k1(ref, h, j):
    r2, rows = _k1_rows(ref, h, j)
    return r2[rows, :]


def _st_k1(ref, h, j, val):
    r2, rows = _k1_rows(ref, h, j)
    r2[rows, :] = val


def _k1_slab(ref, j):
    return jnp.concatenate([_ld_k1(ref, h, j) for h in range(N_HALF)], axis=1)


def _dft_b_filter_kernel(s1_ref, s2_ref, m2_ref, asum_ref, o_ref, *, tk1):
    n2 = s1_ref.shape[3]
    C = o_ref.shape[4]
    scale = 1.0 / asum_ref[0]
    for j in range(tk1):
        sign = 1.0 if j % 2 == 0 else -1.0
        s = _k1_slab(s1_ref, j) + sign * _k1_slab(s2_ref, j)
        xk = _dot(m2_ref[...], s.astype(BF16)) * scale
        o_ref[0, j] = xk.reshape(2, n2, C)


def _dft_stage_b_filter(sa, m2, asum):
    _, _, _, N2, N1, _ = sa.shape
    C = HY_WIDTH
    tk1 = SUBLANES
    blk = (1, N_HALF, 2, N2, tk1, LANES)
    return pl.pallas_call(
        functools.partial(_dft_b_filter_kernel, tk1=tk1),
        out_shape=jax.ShapeDtypeStruct((HY_ORDER, N1, 2, N2, C), F32),
        grid=(HY_ORDER, N1 // tk1),
        in_specs=[
            pl.BlockSpec(blk, lambda o, i: (o, 0, 0, 0, i, 0)),
            pl.BlockSpec(blk, lambda o, i: (HY_ORDER + o, 0, 0, 0, i, 0)),
            pl.BlockSpec((2 * N2, 2 * N2), lambda o, i: (0, 0)),
            pl.BlockSpec((1, 1, C), lambda o, i: (o, 0, 0)),
        ],
        out_specs=pl.BlockSpec((1, tk1, 2, N2, C), lambda o, i: (o, i, 0, 0, 0)),
        compiler_params=_cparams(("parallel", "parallel")),
        name="hy_dft_b_filter",
    )(sa, sa, m2, asum.reshape(HY_ORDER, 1, C))


def _dft_b_conv_kernel(s_ref, kf_ref, m2_ref, m2i_ref, o_ref, *, tk1):
    n2 = s_ref.shape[3]
    for j in range(tk1):
        x = _dot(m2_ref[...], _k1_slab(s_ref, j).astype(BF16))
        xr, xi = x[:n2], x[n2:]
        kr = kf_ref[0, j, 0]
        ki = kf_ref[0, j, 1]
        y = jnp.concatenate([xr * kr - xi * ki, xr * ki + xi * kr], axis=0).astype(BF16)
        cc = _dot(m2i_ref[...], y)
        for h in range(N_HALF):
            _st_k1(o_ref, h, j, cc[:, h * LANES:(h + 1) * LANES])


def _dft_stage_b_conv(sa, kspec, order, m2, m2i):
    S, _, _, N2, N1, _ = sa.shape
    C = HY_WIDTH
    tk1 = SUBLANES
    blk = (1, N_HALF, 2, N2, tk1, LANES)
    return pl.pallas_call(
        functools.partial(_dft_b_conv_kernel, tk1=tk1),
        out_shape=jax.ShapeDtypeStruct(sa.shape, F32),
        grid=(N1 // tk1, S),
        in_specs=[
            pl.BlockSpec(blk, lambda i, s: (s, 0, 0, 0, i, 0)),
            pl.BlockSpec((1, tk1, 2, N2, C), lambda i, s: (order, i, 0, 0, 0)),
            pl.BlockSpec((2 * N2, 2 * N2), lambda i, s: (0, 0)),
            pl.BlockSpec((2 * N2, 2 * N2), lambda i, s: (0, 0)),
        ],
        out_specs=pl.BlockSpec(blk, lambda i, s: (s, 0, 0, 0, i, 0)),
        compiler_params=_cparams(("parallel", "parallel")),
        name="hy_dft_b_conv",
    )(sa, kspec, m2, m2i)


def _dft_c_kernel(c_ref, gi_ref, z0_ref, z1_ref, g0_ref, g1_ref, d_ref, o_ref, *, nb):
    d = d_ref[0]
    for j in range(nb):
        cat = jnp.concatenate(
            [jnp.concatenate([c_ref[0, h, 0, j], c_ref[0, h, 1, j]], axis=0) for h in range(N_HALF)], axis=1)
        y = _dot(gi_ref[j], cat.astype(BF16))
        for h, (z_ref, g_ref) in enumerate(((z0_ref, g0_ref), (z1_ref, g1_ref))):
            lanes = slice(h * LANES, (h + 1) * LANES)
            _st_time(o_ref, h, j, nb, _ld_time(g_ref, j, nb) * (y[:, lanes] + _ld_time(z_ref, j, nb) * d[:, lanes]))


def _dft_stage_c(sc, g_inv, z, z_specs, gate, gate_specs, d):
    S, _, _, N2, N1, _ = sc.shape
    C = HY_WIDTH
    K1 = g_inv.shape[1]
    nb = DFT_NB
    return pl.pallas_call(
        functools.partial(_dft_c_kernel, nb=nb),
        out_shape=jax.ShapeDtypeStruct((S, N_HALF, K1, N2, LANES), F32),
        grid=(S, N2 // nb),
        in_specs=[
            pl.BlockSpec((1, N_HALF, 2, nb, N1, LANES), lambda s, j: (s, 0, 0, j, 0, 0)),
            pl.BlockSpec((nb, K1, 2 * N1), lambda s, j: (j, 0, 0)),
        ] + z_specs + gate_specs + [pl.BlockSpec((1, 1, C), lambda s, j: (0, 0, 0))],
        out_specs=pl.BlockSpec((1, N_HALF, K1, nb, LANES), lambda s, j: (s, 0, 0, j, 0)),
        compiler_params=_cparams(("parallel", "parallel")),
        name="hy_dft_c",
    )(sc, g_inv, z, z, gate, gate, d.reshape(1, 1, C))


def _hyena(hy, conv_w, conv_b, w1, b1, w2, b2, w3, b3, freq, wout, bias, tables):
    B, L, _ = hy.shape
    C = HY_WIDTH
    g_fwd, g_inv, m2, m2i = tables
    N2 = DFT_N2
    K1 = L // N2
    nb = DFT_NB
    fs, asum = _hyena_filter_time(L, w1, b1, w2, b2, w3, b3, freq, wout)
    fa = _dft_stage_a(fs.reshape(2 * HY_ORDER, K1, N2, C), _lane_half_specs(K1, nb, 0), g_fwd)
    kspec = _dft_stage_b_filter(fa, m2, asum)
    hc = _short_conv(hy, conv_w, conv_b)
    hc4 = hc.reshape(B, K1, N2, (HY_ORDER + 1) * C)
    z, z_specs = hc4, _lane_half_specs(K1, nb, 0)
    for o in range(HY_ORDER):
        sa = _dft_stage_a(z, z_specs, g_fwd)
        sc = _dft_stage_b_conv(sa, kspec, o, m2, m2i)
        z = _dft_stage_c(sc, g_inv, z, z_specs, hc4, _lane_half_specs(K1, nb, 1 + o), bias[o])
        z_specs = _major_half_specs(K1, nb)
    return z.reshape(B, N_HALF, L, LANES)


def _outproj_kernel(x_ref, at_ref, h_ref, gr_ref, zy_ref, w_ref, gt_ref, g_ref, b_ref, o_ref, *, alpha):
    lru = (h_ref[0, 0] + h_ref[1, 0]) * jax.nn.gelu(gr_ref[0])
    mixed = jnp.concatenate([at_ref[0], lru.astype(BF16)] + [zy_ref[0, h].astype(BF16) for h in range(N_HALF)], axis=1)
    m = _dot(mixed, w_ref[...])
    o_ref[0] = _layer_norm(alpha * x_ref[0] + gt_ref[0] * m, g_ref[...], b_ref[...])


def _out_proj(x, attn, h, xg, zy, w_out, gt, ln_g, ln_b, alpha):
    B, L, D = x.shape
    T = min(512, L)
    C = LRU_WIDTH
    row = lambda b, i: (b, i, 0)
    vec = lambda b, i: (b, 0, 0)
    const = lambda b, i: (0, 0)
    return pl.pallas_call(
        functools.partial(_outproj_kernel, alpha=alpha),
        out_shape=jax.ShapeDtypeStruct((B, L, D), F32),
        grid=(B, L // T),
        in_specs=[
            pl.BlockSpec((1, T, D), row),
            pl.BlockSpec((1, T, IN_Q), row),
            pl.BlockSpec((2, 1, T, C), lambda b, i: (0, b, i, 0)),
            pl.BlockSpec((1, T, C), lambda b, i: (b, i, 1)),
            pl.BlockSpec((1, N_HALF, T, LANES), lambda b, i: (b, 0, i, 0)),
            pl.BlockSpec(w_out.shape, const),
            pl.BlockSpec((1, 1, D), vec),
            pl.BlockSpec((1, D), const),
            pl.BlockSpec((1, D), const),
        ],
        out_specs=pl.BlockSpec((1, T, D), row),
        compiler_params=_cparams(("parallel", "parallel")),
        name="out_proj",
    )(x, attn, h, xg, zy, w_out, gt, ln_g.reshape(1, D), ln_b.reshape(1, D))


FFN_CHUNK = 256


def _ffn_kernel(xp_ref, x_ref, xn_ref, sc_ref, sh_ref, gt_ref, wu_ref, cw_ref, cb_ref, wd_ref, g_ref, b_ref,
                o_ref, u_sc, h_sc, acc_sc, *, T, nt, nf, alpha):
    i = pl.program_id(1)
    H = SUBLANES
    R = T + 2 * H
    F = FFN_CHUNK
    sc = 1.0 + sc_ref[0]
    sh = sh_ref[0]
    u_sc[H:H + T, :] = (x_ref[0] * sc + sh).astype(BF16)
    u_sc[0:H, :] = jnp.where(i > 0, xp_ref[0] * sc + sh, 0.0).astype(BF16)
    u_sc[H + T:R, :] = jnp.where(i < nt - 1, xn_ref[0] * sc + sh, 0.0).astype(BF16)
    acc_sc[...] = jnp.zeros(acc_sc.shape, F32)

    def up(f, slot):
        h_sc[slot] = _dot(u_sc[...], wu_ref[f])

    def down(f, slot):
        h = h_sc[slot]
        cw = cw_ref[f]
        y = cb_ref[f] + pltpu.roll(h, 1, 0) * cw[0:1] + h * cw[1:2] + pltpu.roll(h, R - 1, 0) * cw[2:3]
        y = y[H:H + T]
        act = (jax.nn.gelu(y[:, :F]) * y[:, F:]).astype(BF16)
        acc_sc[...] += _dot(act, wd_ref[f])

    def pair(c2, carry):
        f = 2 * c2
        up(f + 1, 1)
        down(f, 0)
        up(jnp.minimum(f + 2, nf - 1), 0)
        down(f + 1, 1)
        return carry

    up(0, 0)
    lax.fori_loop(0, nf // 2, pair, 0)
    if nf % 2 == 1:
        down(nf - 1, 0)
    o_ref[0] = _layer_norm(alpha * x_ref[0] + gt_ref[0] * acc_sc[...], g_ref[...], b_ref[...])


def _conv_ffn(x, sc, sh, gt, w_up, conv_w, conv_b, w_down, ln_g, ln_b, alpha):
    B, L, D = x.shape
    d_ff = w_down.shape[0]
    T = min(512, L)
    F = FFN_CHUNK
    nt = L // T
    nf = d_ff // F
    hb = T // SUBLANES
    n_halo = L // SUBLANES
    nk = conv_w.shape[0]
    chunked = lambda w: jnp.concatenate([w[..., :d_ff].reshape(w.shape[0], nf, F),
                                         w[..., d_ff:].reshape(w.shape[0], nf, F)], axis=-1).transpose(1, 0, 2)
    wu = chunked(w_up)
    cw = chunked(conv_w)
    cb = chunked(conv_b.reshape(1, 2 * d_ff))
    wd = w_down.reshape(nf, F, D)
    row = lambda b, i: (b, i, 0)
    vec = lambda b, i: (b, 0, 0)
    const2 = lambda b, i: (0, 0)
    const3 = lambda b, i: (0, 0, 0)
    resident = lambda shape, imap: pl.BlockSpec(shape, imap, pipeline_mode=pl.Buffered(1))
    return pl.pallas_call(
        functools.partial(_ffn_kernel, T=T, nt=nt, nf=nf, alpha=alpha),
        out_shape=jax.ShapeDtypeStruct((B, L, D), F32),
        grid=(B, nt),
        in_specs=[
            pl.BlockSpec((1, SUBLANES, D), lambda b, i: (b, jnp.maximum(i * hb - 1, 0), 0)),
            pl.BlockSpec((1, T, D), row),
            pl.BlockSpec((1, SUBLANES, D), lambda b, i: (b, jnp.minimum((i + 1) * hb, n_halo - 1), 0)),
            pl.BlockSpec((1, 1, D), vec),
            pl.BlockSpec((1, 1, D), vec),
            pl.BlockSpec((1, 1, D), vec),
            resident((nf, D, 2 * F), const3),
            resident((nf, nk, 2 * F), const3),
            resident((nf, 1, 2 * F), const3),
            resident((nf, F, D), const3),
            pl.BlockSpec((1, D), const2),
            pl.BlockSpec((1, D), const2),
        ],
        out_specs=pl.BlockSpec((1, T, D), row),
        scratch_shapes=[
            pltpu.VMEM((T + 2 * SUBLANES, D), BF16),
            pltpu.VMEM((2, T + 2 * SUBLANES, 2 * F), F32),
            pltpu.VMEM((T, D), F32),
        ],
        compiler_params=_cparams(("parallel", "parallel")),
        name="conv_ffn",
    )(x, x, x, sc, sh, gt, wu, cw, cb, wd, ln_g.reshape(1, D), ln_b.reshape(1, D))


def _trunk(x, mod, p):
    B, L, D = x.shape
    depth = mod.shape[0]
    alpha = (2 * depth) ** 0.25
    rope = _rope_tables(L)
    tables = _dft_tables(L)
    for l in range(depth):
        m6 = mod[l].reshape(B, 6, 1, D)
        sh1, sc1, gt1, sh2, sc2, gt2 = (m6[:, j] for j in range(6))
        q, k, v, xg, hy = _in_proj(x, sc1, sh1, p['w_in'][l], p['q_gain'][l], p['k_gain'][l], rope)
        attn = _attention(q, k, v)
        h = _rglru(xg, p['lru_conv_w'][l], p['lru_conv_b'][l], p['lru_wa'][l], p['lru_ba'][l],
                   p['lru_wx'][l], p['lru_bx'][l], p['lru_lambda'][l])
        zy = _hyena(hy, p['hy_conv_w'][l], p['hy_conv_b'][l], p['hy_w1'][l], p['hy_b1'][l], p['hy_w2'][l],
                    p['hy_b2'][l], p['hy_w3'][l], p['hy_b3'][l], p['hy_freq'][l], p['hy_wout'][l],
                    p['hy_bias'][l], tables)
        x = _out_proj(x, attn, h, xg, zy, p['w_out'][l], gt1, p['ln1_g'][l], p['ln1_b'][l], alpha)
        x = _conv_ffn(x, sc2, sh2, gt2, p['ffn_w_up'][l], p['ffn_conv_w'][l], p['ffn_conv_b'][l],
                      p['ffn_w_down'][l], p['ln2_g'][l], p['ln2_b'][l], alpha)
    return x


def kernel(x_prompt, x_sample, c_prompt, c_sample, ada_w, ada_b, w_in, q_gain, k_gain, lru_conv_w, lru_conv_b, lru_wa, lru_ba, lru_wx, lru_bx, lru_lambda, hy_conv_w, hy_conv_b, hy_w1, hy_b1, hy_w2, hy_b2, hy_w3, hy_b3, hy_freq, hy_wout, hy_bias, w_out, ln1_g, ln1_b, ffn_w_up, ffn_conv_w, ffn_conv_b, ffn_w_down, ln2_g, ln2_b):
    p = dict(
        w_in=w_in.astype(BF16), q_gain=q_gain, k_gain=k_gain, lru_conv_w=lru_conv_w, lru_conv_b=lru_conv_b,
        lru_wa=lru_wa, lru_ba=lru_ba, lru_wx=lru_wx, lru_bx=lru_bx, lru_lambda=lru_lambda,
        hy_conv_w=hy_conv_w, hy_conv_b=hy_conv_b, hy_w1=hy_w1, hy_b1=hy_b1, hy_w2=hy_w2, hy_b2=hy_b2,
        hy_w3=hy_w3, hy_b3=hy_b3, hy_freq=hy_freq, hy_wout=hy_wout, hy_bias=hy_bias,
        w_out=w_out.astype(BF16), ln1_g=ln1_g, ln1_b=ln1_b, ffn_w_up=ffn_w_up.astype(BF16),
        ffn_conv_w=ffn_conv_w, ffn_conv_b=ffn_conv_b, ffn_w_down=ffn_w_down.astype(BF16), ln2_g=ln2_g, ln2_b=ln2_b,
    )
    nb = x_prompt.shape[0]
    mod = _ada_mod(jnp.concatenate([c_prompt, c_sample], axis=0), ada_w, ada_b)
    y_prompt = _trunk(x_prompt, mod[:, :nb], p)
    y_sample = _trunk(x_sample, mod[:, nb:], p)
    return (y_prompt, y_sample)
```

```python
import functools
import math

import numpy as np
import jax
import jax.numpy as jnp
from jax import lax
from jax.experimental import pallas as pl
from jax.experimental.pallas import tpu as pltpu

F32 = jnp.float32
BF16 = jnp.bfloat16
HIGHEST = lax.Precision.HIGHEST

GRID_W = 64
HEAD_DIM = 64
N_HEADS = 8
N_KV_HEADS = 2
KV_GROUP = N_HEADS // N_KV_HEADS
ROPE_THETA = 10000.0
ROPE_FREQS = HEAD_DIM // 4
QK_EPS = 1e-6
LRU_WIDTH = 256
LRU_HEADS = 4
LRU_C = 8.0
HY_WIDTH = 256
HY_ORDER = 2
HY_BANDS = 16
HY_MIN_DECAY = abs(math.log(1e-2)) / 1.5
HY_MAX_DECAY = abs(math.log(1e-2)) / 0.3
LN_EPS = 1e-5
IN_Q = N_HEADS * HEAD_DIM
IN_KV = N_KV_HEADS * HEAD_DIM

LANES = 128
SUBLANES = 8
DFT_N2 = 128
VMEM_LIMIT = 48 * 1024 * 1024


def _cparams(sem):
    return pltpu.CompilerParams(dimension_semantics=sem, vmem_limit_bytes=VMEM_LIMIT)


def _dot(a, b):
    return jnp.dot(a, b, preferred_element_type=F32)


def _layer_norm(y, g, b):
    mu = jnp.mean(y, axis=-1, keepdims=True)
    yc = y - mu
    var = jnp.mean(yc * yc, axis=-1, keepdims=True)
    return yc * lax.rsqrt(var + LN_EPS) * g + b


def _ada_kernel(c_ref, w_ref, b_ref, o_ref):
    c = c_ref[...]
    s = c * jax.nn.sigmoid(c)
    o_ref[0] = jnp.dot(s, w_ref[0], precision=HIGHEST, preferred_element_type=F32) + b_ref[0]


def _ada_mod(c_all, ada_w, ada_b):
    depth, d, n = ada_w.shape
    rows = c_all.shape[0]
    tn = 768
    return pl.pallas_call(
        _ada_kernel,
        out_shape=jax.ShapeDtypeStruct((depth, rows, n), F32),
        grid=(depth, n // tn),
        in_specs=[
            pl.BlockSpec((rows, d), lambda l, j: (0, 0)),
            pl.BlockSpec((1, d, tn), lambda l, j: (l, 0, j)),
            pl.BlockSpec((1, 1, tn), lambda l, j: (l, 0, j)),
        ],
        out_specs=pl.BlockSpec((1, rows, tn), lambda l, j: (l, 0, j)),
        compiler_params=_cparams(("parallel", "parallel")),
        name="ada_mod",
    )(c_all, ada_w, ada_b.reshape(depth, 1, n))


def _rope_tables(L):
    rows = L // GRID_W
    row = np.repeat(np.arange(rows, dtype=np.float64), GRID_W)
    col = np.tile(np.arange(GRID_W, dtype=np.float64), rows)
    inv = ROPE_THETA ** (-np.arange(ROPE_FREQS, dtype=np.float64) / ROPE_FREQS)
    ar = row[:, None] * inv
    ac = col[:, None] * inv
    zeros = np.zeros_like(ar)
    cos = np.concatenate([np.cos(ar), np.cos(ar), np.cos(ac), np.cos(ac)], axis=1)
    sin_up = np.concatenate([-np.sin(ar), zeros, -np.sin(ac), zeros], axis=1)
    sin_dn = np.concatenate([zeros, np.sin(ar), zeros, np.sin(ac)], axis=1)
    two = lambda t: jnp.asarray(np.concatenate([t, t], axis=1), F32)
    return two(cos), two(sin_up), two(sin_dn)


def _inproj_kernel(xp_ref, x_ref, xn_ref, sc_ref, sh_ref, w_ref, qg_ref, kg_ref, cos_ref, sup_ref, sdn_ref, bd_ref,
                   hcw_ref, hcb_ref, q_ref, k_ref, v_ref, xg_ref, hc_ref, *, T, nt):
    i = pl.program_id(1)
    H = SUBLANES
    R = T + 2 * H
    sc = 1.0 + sc_ref[0]
    sh = sh_ref[0]
    u_ext = jnp.concatenate([jnp.where(i > 0, xp_ref[0] * sc + sh, 0.0), x_ref[0] * sc + sh,
                             jnp.where(i < nt - 1, xn_ref[0] * sc + sh, 0.0)], axis=0).astype(BF16)
    proj_ext = _dot(u_ext, w_ref[...])
    proj = proj_ext[H:H + T]
    cos = cos_ref[...]
    sup = sup_ref[...]
    sdn = sdn_ref[...]
    bd = bd_ref[...]
    half = ROPE_FREQS

    def norm_rope(t, gain):
        ms = _dot((t * t).astype(BF16), bd)
        tn = t * lax.rsqrt(ms + QK_EPS) * gain
        return (tn * cos + pltpu.roll(tn, LANES - half, 1) * sup + pltpu.roll(tn, half, 1) * sdn)

    qg = qg_ref[...]
    for j in range(IN_Q // LANES):
        sl = slice(j * LANES, (j + 1) * LANES)
        q_ref[0, :, sl] = (norm_rope(proj[:, sl], qg) * (HEAD_DIM ** -0.5 * math.log2(math.e))).astype(BF16)
    k_ref[0] = norm_rope(proj[:, IN_Q:IN_Q + IN_KV], kg_ref[...]).astype(BF16)
    v_ref[0] = proj[:, IN_Q + IN_KV:IN_Q + 2 * IN_KV].astype(BF16)
    o = IN_Q + 2 * IN_KV
    xg_ref[0] = proj[:, o:o + 2 * LRU_WIDTH]
    hy = proj_ext[:, o + 2 * LRU_WIDTH:]
    cw = hcw_ref[...]
    hc = hcb_ref[...] + pltpu.roll(hy, 1, 0) * cw[0:1] + hy * cw[1:2] + pltpu.roll(hy, R - 1, 0) * cw[2:3]
    hc_ref[0] = hc[H:H + T]


def _in_proj(x, sc, sh, w_in, q_gain, k_gain, rope, hy_conv_w, hy_conv_b):
    B, L, D = x.shape
    n_in = w_in.shape[1]
    T = min(512, L)
    nt = L // T
    hb = T // SUBLANES
    n_halo = L // SUBLANES
    cos, sup, sdn = rope
    bd = jnp.asarray(np.kron(np.eye(2), np.full((HEAD_DIM, HEAD_DIM), 1.0 / HEAD_DIM)), BF16)
    qg = jnp.tile(q_gain, 2).reshape(1, LANES)
    kg = jnp.tile(k_gain, 2).reshape(1, LANES)
    n_hy = n_in - IN_Q - 2 * IN_KV - 2 * LRU_WIDTH
    row = lambda b, i: (b, i, 0)
    vec = lambda b, i: (b, 0, 0)
    tab = lambda b, i: (i, 0)
    const = lambda b, i: (0, 0)
    return pl.pallas_call(
        functools.partial(_inproj_kernel, T=T, nt=nt),
        out_shape=(
            jax.ShapeDtypeStruct((B, L, IN_Q), BF16),
            jax.ShapeDtypeStruct((B, L, IN_KV), BF16),
            jax.ShapeDtypeStruct((B, L, IN_KV), BF16),
            jax.ShapeDtypeStruct((B, L, 2 * LRU_WIDTH), F32),
            jax.ShapeDtypeStruct((B, L, n_hy), F32),
        ),
        grid=(B, nt),
        in_specs=[
            pl.BlockSpec((1, SUBLANES, D), lambda b, i: (b, jnp.maximum(i * hb - 1, 0), 0)),
            pl.BlockSpec((1, T, D), row),
            pl.BlockSpec((1, SUBLANES, D), lambda b, i: (b, jnp.minimum((i + 1) * hb, n_halo - 1), 0)),
            pl.BlockSpec((1, 1, D), vec),
            pl.BlockSpec((1, 1, D), vec),
            pl.BlockSpec((D, n_in), const),
            pl.BlockSpec((1, LANES), const),
            pl.BlockSpec((1, LANES), const),
            pl.BlockSpec((T, LANES), tab),
            pl.BlockSpec((T, LANES), tab),
            pl.BlockSpec((T, LANES), tab),
            pl.BlockSpec((LANES, LANES), const),
            pl.BlockSpec(hy_conv_w.shape, const),
            pl.BlockSpec((1, n_hy), const),
        ],
        out_specs=(
            pl.BlockSpec((1, T, IN_Q), row),
            pl.BlockSpec((1, T, IN_KV), row),
            pl.BlockSpec((1, T, IN_KV), row),
            pl.BlockSpec((1, T, 2 * LRU_WIDTH), row),
            pl.BlockSpec((1, T, n_hy), row),
        ),
        compiler_params=_cparams(("parallel", "parallel")),
        name="in_proj",
    )(x, x, x, sc, sh, w_in, qg, kg, cos, sup, sdn, bd, hy_conv_w, hy_conv_b.reshape(1, n_hy))


V_ROWS = HEAD_DIM + 16


def _attn_kernel(qt_ref, k_ref, vt_ref, o_ref, acc_sc, s_sc, *, tk, nk):
    tq = qt_ref.shape[-1]
    acc_sc[...] = jnp.zeros(acc_sc.shape, F32)

    def scores_h(c, slot, h):
        kc = k_ref[0, 0, pl.ds(pl.multiple_of(c * tk, tk), tk), :]
        s = _dot(kc, qt_ref[0, 0, h])
        s_sc[slot, h] = s
        return jnp.max(s, axis=0, keepdims=True)

    def consume_h(c, slot, h, m_prev, m_chunk):
        vc = vt_ref[0, 0, c]
        m_new = jnp.maximum(m_prev, m_chunk)
        p = jnp.exp2(s_sc[slot, h] - m_new).astype(BF16)
        alpha = jnp.exp2(m_prev - m_new)
        acc_sc[h] = alpha * acc_sc[h] + _dot(vc, p)
        return m_new

    def step(c, c_next, slot, carry):
        ms, mc = carry
        new_m, new_c = [], []
        for h in range(KV_GROUP):
            new_c.append(scores_h(c_next, 1 - slot, h))
            new_m.append(consume_h(c, slot, h, ms[h], mc[h]))
        return tuple(new_m), tuple(new_c)

    def pair(c2, carry):
        c = 2 * c2
        carry = step(c, c + 1, 0, carry)
        return step(c + 1, jnp.minimum(c + 2, nk - 1), 1, carry)

    m0 = tuple(jnp.full((1, tq), -jnp.inf, F32) for _ in range(KV_GROUP))
    c0 = tuple(scores_h(0, 0, h) for h in range(KV_GROUP))
    lax.fori_loop(0, nk // 2, pair, (m0, c0))
    for h in range(KV_GROUP):
        acc = acc_sc[h]
        o_ref[0, 0, h] = (acc[:HEAD_DIM] / acc[HEAD_DIM:HEAD_DIM + 1]).astype(o_ref.dtype)


def _attention(q, k, v):
    B, L, _ = q.shape
    tq = min(512, L)
    tk = min(512, L // 2)
    nk = L // tk
    assert nk % 2 == 0
    qt = q.reshape(B, L, N_KV_HEADS, KV_GROUP, HEAD_DIM).transpose(0, 2, 3, 4, 1)
    kh = k.reshape(B, L, N_KV_HEADS, HEAD_DIM).transpose(0, 2, 1, 3)
    vt = v.reshape(B, nk, tk, N_KV_HEADS, HEAD_DIM).transpose(0, 3, 1, 4, 2)
    ones = jnp.ones((B, N_KV_HEADS, nk, 1, tk), BF16)
    pad = jnp.zeros((B, N_KV_HEADS, nk, V_ROWS - HEAD_DIM - 1, tk), BF16)
    vt = jnp.concatenate([vt, ones, pad], axis=3)
    ot = pl.pallas_call(
        functools.partial(_attn_kernel, tk=tk, nk=nk),
        out_shape=jax.ShapeDtypeStruct((B, N_KV_HEADS, KV_GROUP, HEAD_DIM, L), BF16),
        grid=(B, N_KV_HEADS, L // tq),
        in_specs=[
            pl.BlockSpec((1, 1, KV_GROUP, HEAD_DIM, tq), lambda b, g, i: (b, g, 0, 0, i)),
            pl.BlockSpec((1, 1, L, HEAD_DIM), lambda b, g, i: (b, g, 0, 0)),
            pl.BlockSpec((1, 1, nk, V_ROWS, tk), lambda b, g, i: (b, g, 0, 0, 0)),
        ],
        out_specs=pl.BlockSpec((1, 1, KV_GROUP, HEAD_DIM, tq), lambda b, g, i: (b, g, 0, 0, i)),
        scratch_shapes=[pltpu.VMEM((KV_GROUP, V_ROWS, tq), F32), pltpu.VMEM((2, KV_GROUP, tk, tq), F32)],
        compiler_params=_cparams(("parallel", "parallel", "parallel")),
        name="attention",
    )(qt, kh, vt)
    return ot.transpose(0, 4, 1, 2, 3).reshape(B, L, IN_Q)


def _lru_kernel(xp_ref, x_ref, xn_ref, cw_ref, cb_ref, wa_ref, ba_ref, wx_ref, bx_ref, lam_ref,
                o_ref, xe_sc, a_sc, b_sc, hs_sc, h_sc, *, T, nt):
    d = pl.program_id(0)
    i = pl.program_id(1)
    tile = jnp.where(d == 0, i, nt - 1 - i)
    nb, _, C = x_ref.shape
    nh = C // LANES
    H = SUBLANES
    keep_prev = (tile > 0).astype(F32)
    keep_next = (tile < nt - 1).astype(F32)
    for b in range(nb):
        for hf in range(nh):
            lanes = slice(hf * LANES, (hf + 1) * LANES)
            xe_sc[hf, pl.ds(b, H, stride=nb), :] = xp_ref[b, :, lanes] * keep_prev
            xe_sc[hf, pl.ds(H * nb + b, T, stride=nb), :] = x_ref[b, :, lanes]
            xe_sc[hf, pl.ds((H + T) * nb + b, H, stride=nb), :] = xn_ref[b, :, lanes] * keep_next
    cw = cw_ref[...]
    halves = []
    for hf in range(nh):
        lanes = slice(hf * LANES, (hf + 1) * LANES)
        acc = cb_ref[:, lanes]
        for k in range(cw.shape[0]):
            acc = acc + xe_sc[hf, (H - 2 + k) * nb:(H - 2 + k + T) * nb, :] * cw[k:k + 1, lanes]
        halves.append(acc)
    xc = jnp.concatenate(halves, axis=1)
    xb = xc.astype(BF16)
    sigmoid = lambda v: 0.5 * jnp.tanh(0.5 * v) + 0.5
    r = sigmoid(_dot(xb, wa_ref[0]) + ba_ref[0])
    ig = sigmoid(_dot(xb, wx_ref[0]) + bx_ref[0])
    lam = lam_ref[0]
    softplus_neg = jnp.maximum(-lam, 0.0) + jnp.log1p(jnp.exp(-jnp.abs(lam)))
    log_a = -LRU_C * r * softplus_neg
    a = jnp.exp(log_a)
    bb = jnp.sqrt(-jnp.tanh(log_a) * (1.0 + a * a)) * (ig * xc)
    for hf in range(nh):
        a_sc[hf] = a[:, hf * LANES:(hf + 1) * LANES]
        b_sc[hf] = bb[:, hf * LANES:(hf + 1) * LANES]

    @pl.when(i == 0)
    def _():
        h_sc[...] = jnp.zeros(h_sc.shape, F32)

    def body(s, hs):
        t = jnp.where(d == 0, s, T - 1 - s)
        rows = pl.ds(pl.multiple_of(t * nb, nb), nb)
        new = []
        for hf in range(nh):
            h = a_sc[hf, rows, :] * hs[hf] + b_sc[hf, rows, :]
            hs_sc[hf, rows, :] = h
            new.append(h)
        return tuple(new)

    hs = lax.fori_loop(0, T, body, tuple(h_sc[hf] for hf in range(nh)), unroll=8)
    for hf in range(nh):
        h_sc[hf] = hs[hf]
        for b in range(nb):
            o_ref[0, b, :, hf * LANES:(hf + 1) * LANES] = hs_sc[hf, pl.ds(b, T, stride=nb), :]


def _rglru(xg, conv_w, conv_b, wa, ba, wx, bx, lam):
    B, L, _ = xg.shape
    C = LRU_WIDTH
    assert B == SUBLANES
    T = min(256, L)
    nt = L // T
    hb = T // SUBLANES
    n_halo = L // SUBLANES

    def tile_of(d, i):
        return jnp.where(d == 0, i, nt - 1 - i)

    def blockdiag(w):
        eye = jnp.eye(LRU_HEADS, dtype=w.dtype)
        return jnp.einsum('dhij,hg->dhigj', w, eye).reshape(2, C, C).astype(BF16)

    kern = functools.partial(_lru_kernel, T=T, nt=nt)
    return pl.pallas_call(
        kern,
        out_shape=jax.ShapeDtypeStruct((2, B, L, C), F32),
        grid=(2, nt),
        in_specs=[
            pl.BlockSpec((B, SUBLANES, C), lambda d, i: (0, jnp.maximum(tile_of(d, i) * hb - 1, 0), 0)),
            pl.BlockSpec((B, T, C), lambda d, i: (0, tile_of(d, i), 0)),
            pl.BlockSpec((B, SUBLANES, C), lambda d, i: (0, jnp.minimum((tile_of(d, i) + 1) * hb, n_halo - 1), 0)),
            pl.BlockSpec(conv_w.shape, lambda d, i: (0, 0)),
            pl.BlockSpec((1, C), lambda d, i: (0, 0)),
            pl.BlockSpec((1, C, C), lambda d, i: (d, 0, 0)),
            pl.BlockSpec((1, 1, C), lambda d, i: (d, 0, 0)),
            pl.BlockSpec((1, C, C), lambda d, i: (d, 0, 0)),
            pl.BlockSpec((1, 1, C), lambda d, i: (d, 0, 0)),
            pl.BlockSpec((1, 1, C), lambda d, i: (d, 0, 0)),
        ],
        out_specs=pl.BlockSpec((1, B, T, C), lambda d, i: (d, 0, tile_of(d, i), 0)),
        scratch_shapes=[
            pltpu.VMEM((C // LANES, (T + 2 * SUBLANES) * B, LANES), F32),
            pltpu.VMEM((C // LANES, T * B, LANES), F32),
            pltpu.VMEM((C // LANES, T * B, LANES), F32),
            pltpu.VMEM((C // LANES, T * B, LANES), F32),
            pltpu.VMEM((C // LANES, B, LANES), F32),
        ],
        compiler_params=_cparams(("arbitrary", "arbitrary")),
        name="rglru",
    )(xg, xg, xg, conv_w, conv_b.reshape(1, C), blockdiag(wa), ba.reshape(2, 1, C),
      blockdiag(wx), bx.reshape(2, 1, C), lam.reshape(2, 1, C))


def _filter_positions(L):
    t = np.linspace(0.0, 1.0, L)[:, None]
    w = 2.0 * math.pi * np.arange(L, dtype=np.float64)[:, None] / L
    f = np.linspace(1e-4, HY_BANDS - 1, HY_BANDS)[None, :]
    z = np.concatenate([t, np.cos(f * w), -np.sin(f * w)], axis=-1)
    zrev = np.concatenate([z[:1], z[:0:-1]], axis=0)
    return jnp.asarray(np.concatenate([z, zrev], axis=1), F32)


def _filter_kernel(zz_ref, w1_ref, b1_ref, w2_ref, b2_ref, w3_ref, b3_ref, fr_ref, wo_ref, dl_ref,
                   fs_ref, asum_ref, *, T, nemb):
    i = pl.program_id(0)
    fr = fr_ref[...]
    hd = lambda a, b: jnp.dot(a, b, precision=HIGHEST, preferred_element_type=F32)
    zz = zz_ref[...]
    h = jnp.sin(fr * (hd(zz, w1_ref[...]) + b1_ref[...]))
    h = jnp.sin(fr * (hd(h, w2_ref[...]) + b2_ref[...]))
    h = jnp.sin(fr * (hd(h, w3_ref[...]) + b3_ref[...]))
    k = hd(h, wo_ref[...])
    dl = dl_ref[...]
    dec_f = jnp.exp(-zz[:, 0:1] * dl)
    dec_r = jnp.exp(-zz[:, nemb:nemb + 1] * dl)
    C = HY_WIDTH
    first_row = (i * T + lax.broadcasted_iota(jnp.int32, (T, 1), 0)) == 0

    @pl.when(i == 0)
    def _():
        asum_ref[...] = jnp.zeros(asum_ref.shape, F32)

    for o in range(HY_ORDER):
        base = o * 2 * C
        kf = k[:, base:base + C] * dec_f
        kb_here = k[:, base + C:base + 2 * C] * dec_f
        kb_rev = k[:, HY_ORDER * 2 * C + o * C:HY_ORDER * 2 * C + (o + 1) * C] * dec_r
        first = kf + jnp.where(first_row, kb_here, 0.0)
        second = jnp.where(first_row, 0.0, kb_rev)
        fs_ref[o] = first
        fs_ref[HY_ORDER + o] = second
        asum_ref[o:o + 1, :] += jnp.sum(jnp.abs(first) + jnp.abs(second), axis=0, keepdims=True)


def _hyena_filter_time(L, w1, b1, w2, b2, w3, b3, freq, wout):
    zz = _filter_positions(L)
    T = min(512, L)
    C = HY_WIDTH
    nemb = zz.shape[1] // 2
    nf = w2.shape[0]
    deltas = jnp.asarray(np.linspace(HY_MIN_DECAY, HY_MAX_DECAY, C)[None, :], F32)
    twice = lambda w: jnp.kron(jnp.eye(2, dtype=w.dtype), w)
    both = lambda v: jnp.tile(v, 2).reshape(1, 2 * nf)
    w_back = wout.reshape(nf, HY_ORDER, 2, C)[:, :, 1, :].reshape(nf, HY_ORDER * C)
    wo = jnp.concatenate([jnp.concatenate([wout, jnp.zeros((nf, HY_ORDER * C), wout.dtype)], axis=1),
                          jnp.concatenate([jnp.zeros_like(wout), w_back], axis=1)], axis=0)
    const = lambda i: (0, 0)
    return pl.pallas_call(
        functools.partial(_filter_kernel, T=T, nemb=nemb),
        out_shape=(jax.ShapeDtypeStruct((2 * HY_ORDER, L, C), F32), jax.ShapeDtypeStruct((HY_ORDER, C), F32)),
        grid=(L // T,),
        in_specs=[
            pl.BlockSpec((T, 2 * nemb), lambda i: (i, 0)),
            pl.BlockSpec((2 * nemb, 2 * nf), const), pl.BlockSpec((1, 2 * nf), const),
            pl.BlockSpec((2 * nf, 2 * nf), const), pl.BlockSpec((1, 2 * nf), const),
            pl.BlockSpec((2 * nf, 2 * nf), const), pl.BlockSpec((1, 2 * nf), const),
            pl.BlockSpec((1, 2 * nf), const),
            pl.BlockSpec(wo.shape, const),
            pl.BlockSpec((1, C), const),
        ],
        out_specs=(pl.BlockSpec((2 * HY_ORDER, T, C), lambda i: (0, i, 0)),
                   pl.BlockSpec((HY_ORDER, C), const)),
        compiler_params=_cparams(("arbitrary",)),
        name="hy_filter_mlp",
    )(zz, twice(w1), both(b1), twice(w2), both(b2), twice(w3), both(b3), both(freq), wo, deltas)


def _dft_tables(L):
    N = 2 * L
    N2 = DFT_N2
    N1 = N // N2
    K1 = N1 // 2
    kept = K1 + 1
    rows = -(-kept // SUBLANES) * SUBLANES
    live = (np.arange(rows) < kept).astype(np.float64)[None, :, None]
    pair = np.where((np.arange(rows) == 0) | (np.arange(rows) == K1), 1.0, 2.0)[None, :, None] * live
    n1 = np.arange(K1)[None, None, :]
    k1 = np.arange(rows)[None, :, None]
    n2 = np.arange(N2)[:, None, None]
    ang = 2.0 * np.pi * (((n1 * N2 + n2) * k1) % N) / N
    g_fwd = np.concatenate([np.cos(ang) * live, -np.sin(ang) * live], axis=1)
    t = lambda a: np.transpose(a, (0, 2, 1))
    g_inv = np.concatenate([t(np.cos(ang) * pair), t(-np.sin(ang) * pair)], axis=2) / N
    a2 = 2.0 * np.pi * ((np.arange(N2)[:, None] * np.arange(N2)[None, :]) % N2) / N2
    cr, ci = np.cos(a2), -np.sin(a2)
    m2 = np.block([[cr, -ci], [ci, cr]])
    m2i = np.block([[cr, ci], [-ci, cr]])
    as_bf = lambda a: jnp.asarray(a, BF16)
    return as_bf(g_fwd), as_bf(g_inv), as_bf(m2), as_bf(m2i)


DFT_NB = 8
N_HALF = HY_WIDTH // LANES


def _lane_half_specs(k1n, nb, part):
    return [pl.BlockSpec((1, k1n, nb, LANES), lambda s, j, h=h: (s, 0, j, part * N_HALF + h)) for h in range(N_HALF)]


def _major_half_specs(k1n, nb):
    return [pl.BlockSpec((1, 1, k1n, nb, LANES), lambda s, j, h=h: (s, h, 0, j, 0)) for h in range(N_HALF)]


def _ld_time(ref, j, nb):
    k1n = ref.shape[-3]
    return ref.reshape(k1n * nb, LANES)[pl.ds(j, k1n, stride=nb), :]


def _st_time(ref, h, j, nb, val):
    k1n = ref.shape[2]
    ref.reshape(N_HALF * k1n * nb, LANES)[pl.ds(h * k1n * nb + j, k1n, stride=nb), :] = val


def _dft_a_kernel(x0_ref, x1_ref, g_ref, o_ref, *, nb):
    n1 = o_ref.shape[4]
    for j in range(nb):
        parts = []
        for x_ref in (x0_ref, x1_ref):
            parts.append(_ld_time(x_ref, j, nb))
        xj = jnp.concatenate(parts, axis=1).astype(BF16)
        r = _dot(g_ref[j], xj)
        for h in range(N_HALF):
            o_ref[0, h, 0, j] = r[:n1, h * LANES:(h + 1) * LANES]
            o_ref[0, h, 1, j] = r[n1:, h * LANES:(h + 1) * LANES]


def _dft_stage_a(x, x_specs, g_fwd):
    S = x.shape[0]
    N2, two_n1, K1 = g_fwd.shape
    N1 = two_n1 // 2
    nb = DFT_NB
    return pl.pallas_call(
        functools.partial(_dft_a_kernel, nb=nb),
        out_shape=jax.ShapeDtypeStruct((S, N_HALF, 2, N2, N1, LANES), F32),
        grid=(S, N2 // nb),
        in_specs=x_specs + [pl.BlockSpec((nb, 2 * N1, K1), lambda s, j: (j, 0, 0))],
        out_specs=pl.BlockSpec((1, N_HALF, 2, nb, N1, LANES), lambda s, j: (s, 0, 0, j, 0, 0)),
        compiler_params=_cparams(("parallel", "parallel")),
        name="hy_dft_a",
    )(x, x, g_fwd)


def _k1_rows(ref, h, j):
    _, nh, _, n2, tk1, _ = ref.shape
    return ref.reshape(nh * 2 * n2 * tk1, LANES), pl.ds(h * 2 * n2 * tk1 + j, 2 * n2, stride=tk1)


def _ld_k1(ref, h, j):
    r2, rows = _k1_rows(ref, h, j)
    return r2[rows, :]


def _st_k1(ref, h, j, val):
    r2, rows = _k1_rows(ref, h, j)
    r2[rows, :] = val


def _k1_slab(ref, j):
    return jnp.concatenate([_ld_k1(ref, h, j) for h in range(N_HALF)], axis=1)


def _dft_b_filter_kernel(s1_ref, s2_ref, m2_ref, asum_ref, o_ref, *, tk1):
    n2 = s1_ref.shape[3]
    C = o_ref.shape[4]
    scale = 1.0 / asum_ref[0]
    for j in range(tk1):
        sign = 1.0 if j % 2 == 0 else -1.0
        s = _k1_slab(s1_ref, j) + sign * _k1_slab(s2_ref, j)
        xk = _dot(m2_ref[...], s.astype(BF16)) * scale
        o_ref[0, j] = xk.reshape(2, n2, C)


def _dft_stage_b_filter(sa, m2, asum):
    _, _, _, N2, N1, _ = sa.shape
    C = HY_WIDTH
    tk1 = SUBLANES
    blk = (1, N_HALF, 2, N2, tk1, LANES)
    return pl.pallas_call(
        functools.partial(_dft_b_filter_kernel, tk1=tk1),
        out_shape=jax.ShapeDtypeStruct((HY_ORDER, N1, 2, N2, C), F32),
        grid=(HY_ORDER, N1 // tk1),
        in_specs=[
            pl.BlockSpec(blk, lambda o, i: (o, 0, 0, 0, i, 0)),
            pl.BlockSpec(blk, lambda o, i: (HY_ORDER + o, 0, 0, 0, i, 0)),
            pl.BlockSpec((2 * N2, 2 * N2), lambda o, i: (0, 0)),
            pl.BlockSpec((1, 1, C), lambda o, i: (o, 0, 0)),
        ],
        out_specs=pl.BlockSpec((1, tk1, 2, N2, C), lambda o, i: (o, i, 0, 0, 0)),
        compiler_params=_cparams(("parallel", "parallel")),
        name="hy_dft_b_filter",
    )(sa, sa, m2, asum.reshape(HY_ORDER, 1, C))


def _dft_b_conv_kernel(s_ref, kf_ref, m2_ref, m2i_ref, o_ref, *, tk1):
    n2 = s_ref.shape[3]
    for j in range(tk1):
        x = _dot(m2_ref[...], _k1_slab(s_ref, j).astype(BF16))
        xr, xi = x[:n2], x[n2:]
        kr = kf_ref[0, j, 0]
        ki = kf_ref[0, j, 1]
        y = jnp.concatenate([xr * kr - xi * ki, xr * ki + xi * kr], axis=0).astype(BF16)
        cc = _dot(m2i_ref[...], y)
        for h in range(N_HALF):
            _st_k1(o_ref, h, j, cc[:, h * LANES:(h + 1) * LANES])


def _dft_stage_b_conv(sa, kspec, order, m2, m2i):
    S, _, _, N2, N1, _ = sa.shape
    C = HY_WIDTH
    tk1 = SUBLANES
    blk = (1, N_HALF, 2, N2, tk1, LANES)
    return pl.pallas_call(
        functools.partial(_dft_b_conv_kernel, tk1=tk1),
        out_shape=jax.ShapeDtypeStruct(sa.shape, F32),
        grid=(N1 // tk1, S),
        in_specs=[
            pl.BlockSpec(blk, lambda i, s: (s, 0, 0, 0, i, 0)),
            pl.BlockSpec((1, tk1, 2, N2, C), lambda i, s: (order, i, 0, 0, 0)),
            pl.BlockSpec((2 * N2, 2 * N2), lambda i, s: (0, 0)),
            pl.BlockSpec((2 * N2, 2 * N2), lambda i, s: (0, 0)),
        ],
        out_specs=pl.BlockSpec(blk, lambda i, s: (s, 0, 0, 0, i, 0)),
        compiler_params=_cparams(("parallel", "parallel")),
        name="hy_dft_b_conv",
    )(sa, kspec, m2, m2i)


def _dft_c_kernel(c_ref, gi_ref, z0_ref, z1_ref, g0_ref, g1_ref, d_ref, o_ref, *, nb):
    d = d_ref[0]
    for j in range(nb):
        cat = jnp.concatenate(
            [jnp.concatenate([c_ref[0, h, 0, j], c_ref[0, h, 1, j]], axis=0) for h in range(N_HALF)], axis=1)
        y = _dot(gi_ref[j], cat.astype(BF16))
        for h, (z_ref, g_ref) in enumerate(((z0_ref, g0_ref), (z1_ref, g1_ref))):
            lanes = slice(h * LANES, (h + 1) * LANES)
            _st_time(o_ref, h, j, nb, _ld_time(g_ref, j, nb) * (y[:, lanes] + _ld_time(z_ref, j, nb) * d[:, lanes]))


def _dft_stage_c(sc, g_inv, z, z_specs, gate, gate_specs, d):
    S, _, _, N2, N1, _ = sc.shape
    C = HY_WIDTH
    K1 = g_inv.shape[1]
    nb = DFT_NB
    return pl.pallas_call(
        functools.partial(_dft_c_kernel, nb=nb),
        out_shape=jax.ShapeDtypeStruct((S, N_HALF, K1, N2, LANES), F32),
        grid=(S, N2 // nb),
        in_specs=[
            pl.BlockSpec((1, N_HALF, 2, nb, N1, LANES), lambda s, j: (s, 0, 0, j, 0, 0)),
            pl.BlockSpec((nb, K1, 2 * N1), lambda s, j: (j, 0, 0)),
        ] + z_specs + gate_specs + [pl.BlockSpec((1, 1, C), lambda s, j: (0, 0, 0))],
        out_specs=pl.BlockSpec((1, N_HALF, K1, nb, LANES), lambda s, j: (s, 0, 0, j, 0)),
        compiler_params=_cparams(("parallel", "parallel")),
        name="hy_dft_c",
    )(sc, g_inv, z, z, gate, gate, d.reshape(1, 1, C))


def _hyena(hc, w1, b1, w2, b2, w3, b3, freq, wout, bias, tables):
    B, L, _ = hc.shape
    C = HY_WIDTH
    g_fwd, g_inv, m2, m2i = tables
    N2 = DFT_N2
    K1 = L // N2
    nb = DFT_NB
    fs, asum = _hyena_filter_time(L, w1, b1, w2, b2, w3, b3, freq, wout)
    fa = _dft_stage_a(fs.reshape(2 * HY_ORDER, K1, N2, C), _lane_half_specs(K1, nb, 0), g_fwd)
    kspec = _dft_stage_b_filter(fa, m2, asum)
    hc4 = hc.reshape(B, K1, N2, (HY_ORDER + 1) * C)
    z, z_specs = hc4, _lane_half_specs(K1, nb, 0)
    for o in range(HY_ORDER):
        sa = _dft_stage_a(z, z_specs, g_fwd)
        sc = _dft_stage_b_conv(sa, kspec, o, m2, m2i)
        z = _dft_stage_c(sc, g_inv, z, z_specs, hc4, _lane_half_specs(K1, nb, 1 + o), bias[o])
        z_specs = _major_half_specs(K1, nb)
    return z.reshape(B, N_HALF, L, LANES)


def _outproj_kernel(x_ref, at_ref, h_ref, gr_ref, zy_ref, w_ref, gt_ref, g_ref, b_ref, o_ref, *, alpha):
    lru = (h_ref[0, 0] + h_ref[1, 0]) * jax.nn.gelu(gr_ref[0])
    mixed = jnp.concatenate([at_ref[0], lru.astype(BF16)] + [zy_ref[0, h].astype(BF16) for h in range(N_HALF)], axis=1)
    m = _dot(mixed, w_ref[...])
    o_ref[0] = _layer_norm(alpha * x_ref[0] + gt_ref[0] * m, g_ref[...], b_ref[...])


def _out_proj(x, attn, h, xg, zy, w_out, gt, ln_g, ln_b, alpha):
    B, L, D = x.shape
    T = min(512, L)
    C = LRU_WIDTH
    row = lambda b, i: (b, i, 0)
    vec = lambda b, i: (b, 0, 0)
    const = lambda b, i: (0, 0)
    return pl.pallas_call(
        functools.partial(_outproj_kernel, alpha=alpha),
        out_shape=jax.ShapeDtypeStruct((B, L, D), F32),
        grid=(B, L // T),
        in_specs=[
            pl.BlockSpec((1, T, D), row),
            pl.BlockSpec((1, T, IN_Q), row),
            pl.BlockSpec((2, 1, T, C), lambda b, i: (0, b, i, 0)),
            pl.BlockSpec((1, T, C), lambda b, i: (b, i, 1)),
            pl.BlockSpec((1, N_HALF, T, LANES), lambda b, i: (b, 0, i, 0)),
            pl.BlockSpec(w_out.shape, const),
            pl.BlockSpec((1, 1, D), vec),
            pl.BlockSpec((1, D), const),
            pl.BlockSpec((1, D), const),
        ],
        out_specs=pl.BlockSpec((1, T, D), row),
        compiler_params=_cparams(("parallel", "parallel")),
        name="out_proj",
    )(x, attn, h, xg, zy, w_out, gt, ln_g.reshape(1, D), ln_b.reshape(1, D))


FFN_CHUNK = 256


def _ffn_kernel(xp_ref, x_ref, xn_ref, sc_ref, sh_ref, gt_ref, wu_ref, cw_ref, cb_ref, wd_ref, g_ref, b_ref,
                o_ref, u_sc, h_sc, acc_sc, *, T, nt, nf, alpha):
    i = pl.program_id(1)
    H = SUBLANES
    R = T + 2 * H
    F = FFN_CHUNK
    sc = 1.0 + sc_ref[0]
    sh = sh_ref[0]
    u_sc[H:H + T, :] = (x_ref[0] * sc + sh).astype(BF16)
    u_sc[0:H, :] = jnp.where(i > 0, xp_ref[0] * sc + sh, 0.0).astype(BF16)
    u_sc[H + T:R, :] = jnp.where(i < nt - 1, xn_ref[0] * sc + sh, 0.0).astype(BF16)
    acc_sc[...] = jnp.zeros(acc_sc.shape, F32)

    def up(f, slot):
        h_sc[slot] = _dot(u_sc[...], wu_ref[f])

    def down(f, slot):
        h = h_sc[slot]
        cw = cw_ref[f]
        y = cb_ref[f] + pltpu.roll(h, 1, 0) * cw[0:1] + h * cw[1:2] + pltpu.roll(h, R - 1, 0) * cw[2:3]
        y = y[H:H + T]
        act = (jax.nn.gelu(y[:, :F]) * y[:, F:]).astype(BF16)
        acc_sc[...] += _dot(act, wd_ref[f])

    def pair(c2, carry):
        f = 2 * c2
        up(f + 1, 1)
        down(f, 0)
        up(jnp.minimum(f + 2, nf - 1), 0)
        down(f + 1, 1)
        return carry

    up(0, 0)
    lax.fori_loop(0, nf // 2, pair, 0)
    if nf % 2 == 1:
        down(nf - 1, 0)
    o_ref[0] = _layer_norm(alpha * x_ref[0] + gt_ref[0] * acc_sc[...], g_ref[...], b_ref[...])


def _conv_ffn(x, sc, sh, gt, w_up, conv_w, conv_b, w_down, ln_g, ln_b, alpha):
    B, L, D = x.shape
    d_ff = w_down.shape[0]
    T = min(512, L)
    F = FFN_CHUNK
    nt = L // T
    nf = d_ff // F
    hb = T // SUBLANES
    n_halo = L // SUBLANES
    nk = conv_w.shape[0]
    chunked = lambda w: jnp.concatenate([w[..., :d_ff].reshape(w.shape[0], nf, F),
                                         w[..., d_ff:].reshape(w.shape[0], nf, F)], axis=-1).transpose(1, 0, 2)
    wu = chunked(w_up)
    cw = chunked(conv_w)
    cb = chunked(conv_b.reshape(1, 2 * d_ff))
    wd = w_down.reshape(nf, F, D)
    row = lambda b, i: (b, i, 0)
    vec = lambda b, i: (b, 0, 0)
    const2 = lambda b, i: (0, 0)
    const3 = lambda b, i: (0, 0, 0)
    resident = lambda shape, imap: pl.BlockSpec(shape, imap, pipeline_mode=pl.Buffered(1))
    return pl.pallas_call(
        functools.partial(_ffn_kernel, T=T, nt=nt, nf=nf, alpha=alpha),
        out_shape=jax.ShapeDtypeStruct((B, L, D), F32),
        grid=(B, nt),
        in_specs=[
            pl.BlockSpec((1, SUBLANES, D), lambda b, i: (b, jnp.maximum(i * hb - 1, 0), 0)),
            pl.BlockSpec((1, T, D), row),
            pl.BlockSpec((1, SUBLANES, D), lambda b, i: (b, jnp.minimum((i + 1) * hb, n_halo - 1), 0)),
            pl.BlockSpec((1, 1, D), vec),
            pl.BlockSpec((1, 1, D), vec),
            pl.BlockSpec((1, 1, D), vec),
            resident((nf, D, 2 * F), const3),
            resident((nf, nk, 2 * F), const3),
            resident((nf, 1, 2 * F), const3),
            resident((nf, F, D), const3),
            pl.BlockSpec((1, D), const2),
            pl.BlockSpec((1, D), const2),
        ],
        out_specs=pl.BlockSpec((1, T, D), row),
        scratch_shapes=[
            pltpu.VMEM((T + 2 * SUBLANES, D), BF16),
            pltpu.VMEM((2, T + 2 * SUBLANES, 2 * F), F32),
            pltpu.VMEM((T, D), F32),
        ],
        compiler_params=_cparams(("parallel", "parallel")),
        name="conv_ffn",
    )(x, x, x, sc, sh, gt, wu, cw, cb, wd, ln_g.reshape(1, D), ln_b.reshape(1, D))


def _trunk(x, mod, p):
    B, L, D = x.shape
    depth = mod.shape[0]
    alpha = (2 * depth) ** 0.25
    rope = _rope_tables(L)
    tables = _dft_tables(L)
    for l in range(depth):
        m6 = mod[l].reshape(B, 6, 1, D)
        sh1, sc1, gt1, sh2, sc2, gt2 = (m6[:, j] for j in range(6))
        q, k, v, xg, hc = _in_proj(x, sc1, sh1, p['w_in'][l], p['q_gain'][l], p['k_gain'][l], rope,
                                   p['hy_conv_w'][l], p['hy_conv_b'][l])
        attn = _attention(q, k, v)
        h = _rglru(xg, p['lru_conv_w'][l], p['lru_conv_b'][l], p['lru_wa'][l], p['lru_ba'][l],
                   p['lru_wx'][l], p['lru_bx'][l], p['lru_lambda'][l])
        zy = _hyena(hc, p['hy_w1'][l], p['hy_b1'][l], p['hy_w2'][l], p['hy_b2'][l], p['hy_w3'][l], p['hy_b3'][l],
                    p['hy_freq'][l], p['hy_wout'][l], p['hy_bias'][l], tables)
        x = _out_proj(x, attn, h, xg, zy, p['w_out'][l], gt1, p['ln1_g'][l], p['ln1_b'][l], alpha)
        x = _conv_ffn(x, sc2, sh2, gt2, p['ffn_w_up'][l], p['ffn_conv_w'][l], p['ffn_conv_b'][l],
                      p['ffn_w_down'][l], p['ln2_g'][l], p['ln2_b'][l], alpha)
    return x


def kernel(x_prompt, x_sample, c_prompt, c_sample, ada_w, ada_b, w_in, q_gain, k_gain, lru_conv_w, lru_conv_b, lru_wa, lru_ba, lru_wx, lru_bx, lru_lambda, hy_conv_w, hy_conv_b, hy_w1, hy_b1, hy_w2, hy_b2, hy_w3, hy_b3, hy_freq, hy_wout, hy_bias, w_out, ln1_g, ln1_b, ffn_w_up, ffn_conv_w, ffn_conv_b, ffn_w_down, ln2_g, ln2_b):
    p = dict(
        w_in=w_in.astype(BF16), q_gain=q_gain, k_gain=k_gain, lru_conv_w=lru_conv_w, lru_conv_b=lru_conv_b,
        lru_wa=lru_wa, lru_ba=lru_ba, lru_wx=lru_wx, lru_bx=lru_bx, lru_lambda=lru_lambda,
        hy_conv_w=hy_conv_w, hy_conv_b=hy_conv_b, hy_w1=hy_w1, hy_b1=hy_b1, hy_w2=hy_w2, hy_b2=hy_b2,
        hy_w3=hy_w3, hy_b3=hy_b3, hy_freq=hy_freq, hy_wout=hy_wout, hy_bias=hy_bias,
        w_out=w_out.astype(BF16), ln1_g=ln1_g, ln1_b=ln1_b, ffn_w_up=ffn_w_up.astype(BF16),
        ffn_conv_w=ffn_conv_w, ffn_conv_b=ffn_conv_b, ffn_w_down=ffn_w_down.astype(BF16), ln2_g=ln2_g, ln2_b=ln2_b,
    )
    nb = x_prompt.shape[0]
    mod = _ada_mod(jnp.concatenate([c_prompt, c_sample], axis=0), ada_w, ada_b)
    y_prompt = _trunk(x_prompt, mod[:, :nb], p)
    y_sample = _trunk(x_sample, mod[:, nb:], p)
    return (y_prompt, y_sample)
```

```python
import functools
import math

import numpy as np
import jax
import jax.numpy as jnp
from jax import lax
from jax.experimental import pallas as pl
from jax.experimental.pallas import tpu as pltpu

F32 = jnp.float32
BF16 = jnp.bfloat16
HIGHEST = lax.Precision.HIGHEST

GRID_W = 64
HEAD_DIM = 64
N_HEADS = 8
N_KV_HEADS = 2
KV_GROUP = N_HEADS // N_KV_HEADS
ROPE_THETA = 10000.0
ROPE_FREQS = HEAD_DIM // 4
QK_EPS = 1e-6
LRU_WIDTH = 256
LRU_HEADS = 4
LRU_C = 8.0
HY_WIDTH = 256
HY_ORDER = 2
HY_BANDS = 16
HY_MIN_DECAY = abs(math.log(1e-2)) / 1.5
HY_MAX_DECAY = abs(math.log(1e-2)) / 0.3
LN_EPS = 1e-5
IN_Q = N_HEADS * HEAD_DIM
IN_KV = N_KV_HEADS * HEAD_DIM

LANES = 128
SUBLANES = 8
DFT_N2 = 128
VMEM_LIMIT = 48 * 1024 * 1024


def _cparams(sem):
    return pltpu.CompilerParams(dimension_semantics=sem, vmem_limit_bytes=VMEM_LIMIT)


def _dot(a, b):
    return jnp.dot(a, b, preferred_element_type=F32)


def _layer_norm(y, g, b):
    mu = jnp.mean(y, axis=-1, keepdims=True)
    yc = y - mu
    var = jnp.mean(yc * yc, axis=-1, keepdims=True)
    return yc * lax.rsqrt(var + LN_EPS) * g + b


def _ada_kernel(c_ref, w_ref, b_ref, o_ref):
    c = c_ref[...]
    s = c * jax.nn.sigmoid(c)
    o_ref[0] = jnp.dot(s, w_ref[0], precision=HIGHEST, preferred_element_type=F32) + b_ref[0]


def _ada_mod(c_all, ada_w, ada_b):
    depth, d, n = ada_w.shape
    rows = c_all.shape[0]
    tn = 768
    return pl.pallas_call(
        _ada_kernel,
        out_shape=jax.ShapeDtypeStruct((depth, rows, n), F32),
        grid=(depth, n // tn),
        in_specs=[
            pl.BlockSpec((rows, d), lambda l, j: (0, 0)),
            pl.BlockSpec((1, d, tn), lambda l, j: (l, 0, j)),
            pl.BlockSpec((1, 1, tn), lambda l, j: (l, 0, j)),
        ],
        out_specs=pl.BlockSpec((1, rows, tn), lambda l, j: (l, 0, j)),
        compiler_params=_cparams(("parallel", "parallel")),
        name="ada_mod",
    )(c_all, ada_w, ada_b.reshape(depth, 1, n))


def _rope_tables(L):
    rows = L // GRID_W
    row = np.repeat(np.arange(rows, dtype=np.float64), GRID_W)
    col = np.tile(np.arange(GRID_W, dtype=np.float64), rows)
    inv = ROPE_THETA ** (-np.arange(ROPE_FREQS, dtype=np.float64) / ROPE_FREQS)
    ar = row[:, None] * inv
    ac = col[:, None] * inv
    zeros = np.zeros_like(ar)
    cos = np.concatenate([np.cos(ar), np.cos(ar), np.cos(ac), np.cos(ac)], axis=1)
    sin_up = np.concatenate([-np.sin(ar), zeros, -np.sin(ac), zeros], axis=1)
    sin_dn = np.concatenate([zeros, np.sin(ar), zeros, np.sin(ac)], axis=1)
    two = lambda t: jnp.asarray(np.concatenate([t, t], axis=1), F32)
    return two(cos), two(sin_up), two(sin_dn)


def _inproj_kernel(xp_ref, x_ref, xn_ref, sc_ref, sh_ref, w_ref, qg_ref, kg_ref, cos_ref, sup_ref, sdn_ref, bd_ref,
                   hcw_ref, hcb_ref, q_ref, k_ref, v_ref, xg_ref, hc_ref, *, T, nt):
    i = pl.program_id(1)
    H = SUBLANES
    R = T + 2 * H
    sc = 1.0 + sc_ref[0]
    sh = sh_ref[0]
    u_ext = jnp.concatenate([jnp.where(i > 0, xp_ref[0] * sc + sh, 0.0), x_ref[0] * sc + sh,
                             jnp.where(i < nt - 1, xn_ref[0] * sc + sh, 0.0)], axis=0).astype(BF16)
    proj_ext = _dot(u_ext, w_ref[...])
    proj = proj_ext[H:H + T]
    cos = cos_ref[...]
    sup = sup_ref[...]
    sdn = sdn_ref[...]
    bd = bd_ref[...]
    half = ROPE_FREQS

    def norm_rope(t, gain):
        ms = _dot((t * t).astype(BF16), bd)
        tn = t * lax.rsqrt(ms + QK_EPS) * gain
        return (tn * cos + pltpu.roll(tn, LANES - half, 1) * sup + pltpu.roll(tn, half, 1) * sdn)

    qg = qg_ref[...]
    for j in range(IN_Q // LANES):
        sl = slice(j * LANES, (j + 1) * LANES)
        q_ref[0, :, sl] = (norm_rope(proj[:, sl], qg) * (HEAD_DIM ** -0.5 * math.log2(math.e))).astype(BF16)
    k_ref[0] = norm_rope(proj[:, IN_Q:IN_Q + IN_KV], kg_ref[...]).astype(BF16)
    v_ref[0] = proj[:, IN_Q + IN_KV:IN_Q + 2 * IN_KV].astype(BF16)
    o = IN_Q + 2 * IN_KV
    xg_ref[0] = proj[:, o:o + 2 * LRU_WIDTH]
    hy = proj_ext[:, o + 2 * LRU_WIDTH:]
    cw = hcw_ref[...]
    hc = hcb_ref[...] + pltpu.roll(hy, 1, 0) * cw[0:1] + hy * cw[1:2] + pltpu.roll(hy, R - 1, 0) * cw[2:3]
    hc_ref[0] = hc[H:H + T]


def _in_proj(x, sc, sh, w_in, q_gain, k_gain, rope, hy_conv_w, hy_conv_b):
    B, L, D = x.shape
    n_in = w_in.shape[1]
    T = min(512, L)
    nt = L // T
    hb = T // SUBLANES
    n_halo = L // SUBLANES
    cos, sup, sdn = rope
    bd = jnp.asarray(np.kron(np.eye(2), np.full((HEAD_DIM, HEAD_DIM), 1.0 / HEAD_DIM)), BF16)
    qg = jnp.tile(q_gain, 2).reshape(1, LANES)
    kg = jnp.tile(k_gain, 2).reshape(1, LANES)
    n_hy = n_in - IN_Q - 2 * IN_KV - 2 * LRU_WIDTH
    row = lambda b, i: (b, i, 0)
    vec = lambda b, i: (b, 0, 0)
    tab = lambda b, i: (i, 0)
    const = lambda b, i: (0, 0)
    return pl.pallas_call(
        functools.partial(_inproj_kernel, T=T, nt=nt),
        out_shape=(
            jax.ShapeDtypeStruct((B, L, IN_Q), BF16),
            jax.ShapeDtypeStruct((B, L, IN_KV), BF16),
            jax.ShapeDtypeStruct((B, L, IN_KV), BF16),
            jax.ShapeDtypeStruct((B, L, 2 * LRU_WIDTH), F32),
            jax.ShapeDtypeStruct((B, L, n_hy), F32),
        ),
        grid=(B, nt),
        in_specs=[
            pl.BlockSpec((1, SUBLANES, D), lambda b, i: (b, jnp.maximum(i * hb - 1, 0), 0)),
            pl.BlockSpec((1, T, D), row),
            pl.BlockSpec((1, SUBLANES, D), lambda b, i: (b, jnp.minimum((i + 1) * hb, n_halo - 1), 0)),
            pl.BlockSpec((1, 1, D), vec),
            pl.BlockSpec((1, 1, D), vec),
            pl.BlockSpec((D, n_in), const),
            pl.BlockSpec((1, LANES), const),
            pl.BlockSpec((1, LANES), const),
            pl.BlockSpec((T, LANES), tab),
            pl.BlockSpec((T, LANES), tab),
            pl.BlockSpec((T, LANES), tab),
            pl.BlockSpec((LANES, LANES), const),
            pl.BlockSpec(hy_conv_w.shape, const),
            pl.BlockSpec((1, n_hy), const),
        ],
        out_specs=(
            pl.BlockSpec((1, T, IN_Q), row),
            pl.BlockSpec((1, T, IN_KV), row),
            pl.BlockSpec((1, T, IN_KV), row),
            pl.BlockSpec((1, T, 2 * LRU_WIDTH), row),
            pl.BlockSpec((1, T, n_hy), row),
        ),
        compiler_params=_cparams(("parallel", "parallel")),
        name="in_proj",
    )(x, x, x, sc, sh, w_in, qg, kg, cos, sup, sdn, bd, hy_conv_w, hy_conv_b.reshape(1, n_hy))


V_ROWS = HEAD_DIM + 16


def _attn_kernel(qt_ref, k_ref, vt_ref, o_ref, acc_sc, s_sc, *, tk, nk):
    tq = qt_ref.shape[-1]
    acc_sc[...] = jnp.zeros(acc_sc.shape, F32)

    def scores_h(c, slot, h):
        kc = k_ref[0, 0, pl.ds(pl.multiple_of(c * tk, tk), tk), :]
        s = _dot(kc, qt_ref[0, 0, h])
        s_sc[slot, h] = s
        return jnp.max(s, axis=0, keepdims=True)

    def consume_h(c, slot, h, m_prev, m_chunk):
        vc = vt_ref[0, 0, c]
        m_new = jnp.maximum(m_prev, m_chunk)
        p = jnp.exp2(s_sc[slot, h] - m_new).astype(BF16)
        alpha = jnp.exp2(m_prev - m_new)
        acc_sc[h] = alpha * acc_sc[h] + _dot(vc, p)
        return m_new

    def step(c, c_next, slot, carry):
        ms, mc = carry
        new_m, new_c = [], []
        for h in range(KV_GROUP):
            new_c.append(scores_h(c_next, 1 - slot, h))
            new_m.append(consume_h(c, slot, h, ms[h], mc[h]))
        return tuple(new_m), tuple(new_c)

    def pair(c2, carry):
        c = 2 * c2
        carry = step(c, c + 1, 0, carry)
        return step(c + 1, jnp.minimum(c + 2, nk - 1), 1, carry)

    m0 = tuple(jnp.full((1, tq), -jnp.inf, F32) for _ in range(KV_GROUP))
    c0 = tuple(scores_h(0, 0, h) for h in range(KV_GROUP))
    lax.fori_loop(0, nk // 2, pair, (m0, c0))
    for h in range(KV_GROUP):
        acc = acc_sc[h]
        o_ref[0, 0, h] = (acc[:HEAD_DIM] / acc[HEAD_DIM:HEAD_DIM + 1]).astype(o_ref.dtype)


def _attention(q, k, v):
    B, L, _ = q.shape
    tq = min(512, L)
    tk = min(512, L // 2)
    nk = L // tk
    assert nk % 2 == 0
    qt = q.reshape(B, L, N_KV_HEADS, KV_GROUP, HEAD_DIM).transpose(0, 2, 3, 4, 1)
    kh = k.reshape(B, L, N_KV_HEADS, HEAD_DIM).transpose(0, 2, 1, 3)
    vt = v.reshape(B, nk, tk, N_KV_HEADS, HEAD_DIM).transpose(0, 3, 1, 4, 2)
    ones = jnp.ones((B, N_KV_HEADS, nk, 1, tk), BF16)
    pad = jnp.zeros((B, N_KV_HEADS, nk, V_ROWS - HEAD_DIM - 1, tk), BF16)
    vt = jnp.concatenate([vt, ones, pad], axis=3)
    ot = pl.pallas_call(
        functools.partial(_attn_kernel, tk=tk, nk=nk),
        out_shape=jax.ShapeDtypeStruct((B, N_KV_HEADS, KV_GROUP, HEAD_DIM, L), BF16),
        grid=(B, N_KV_HEADS, L // tq),
        in_specs=[
            pl.BlockSpec((1, 1, KV_GROUP, HEAD_DIM, tq), lambda b, g, i: (b, g, 0, 0, i)),
            pl.BlockSpec((1, 1, L, HEAD_DIM), lambda b, g, i: (b, g, 0, 0)),
            pl.BlockSpec((1, 1, nk, V_ROWS, tk), lambda b, g, i: (b, g, 0, 0, 0)),
        ],
        out_specs=pl.BlockSpec((1, 1, KV_GROUP, HEAD_DIM, tq), lambda b, g, i: (b, g, 0, 0, i)),
        scratch_shapes=[pltpu.VMEM((KV_GROUP, V_ROWS, tq), F32), pltpu.VMEM((2, KV_GROUP, tk, tq), F32)],
        compiler_params=_cparams(("parallel", "parallel", "parallel")),
        name="attention",
    )(qt, kh, vt)
    return ot.transpose(0, 4, 1, 2, 3).reshape(B, L, IN_Q)


def _lru_kernel(xp_ref, x_ref, xn_ref, cw_ref, cb_ref, wa_ref, ba_ref, wx_ref, bx_ref, lam_ref,
                o_ref, xe_sc, a_sc, b_sc, hs_sc, h_sc, *, T, nt):
    d = pl.program_id(0)
    i = pl.program_id(1)
    tile = jnp.where(d == 0, i, nt - 1 - i)
    nb, _, C = x_ref.shape
    nh = C // LANES
    H = SUBLANES
    keep_prev = (tile > 0).astype(F32)
    keep_next = (tile < nt - 1).astype(F32)
    for b in range(nb):
        for hf in range(nh):
            lanes = slice(hf * LANES, (hf + 1) * LANES)
            xe_sc[hf, pl.ds(b, H, stride=nb), :] = xp_ref[b, :, lanes] * keep_prev
            xe_sc[hf, pl.ds(H * nb + b, T, stride=nb), :] = x_ref[b, :, lanes]
            xe_sc[hf, pl.ds((H + T) * nb + b, H, stride=nb), :] = xn_ref[b, :, lanes] * keep_next
    cw = cw_ref[...]
    halves = []
    for hf in range(nh):
        lanes = slice(hf * LANES, (hf + 1) * LANES)
        acc = cb_ref[:, lanes]
        for k in range(cw.shape[0]):
            acc = acc + xe_sc[hf, (H - 2 + k) * nb:(H - 2 + k + T) * nb, :] * cw[k:k + 1, lanes]
        halves.append(acc)
    xc = jnp.concatenate(halves, axis=1)
    xb = xc.astype(BF16)
    sigmoid = lambda v: 0.5 * jnp.tanh(0.5 * v) + 0.5
    r = sigmoid(_dot(xb, wa_ref[0]) + ba_ref[0])
    ig = sigmoid(_dot(xb, wx_ref[0]) + bx_ref[0])
    lam = lam_ref[0]
    softplus_neg = jnp.maximum(-lam, 0.0) + jnp.log1p(jnp.exp(-jnp.abs(lam)))
    log_a = -LRU_C * r * softplus_neg
    a = jnp.exp(log_a)
    bb = jnp.sqrt(-jnp.tanh(log_a) * (1.0 + a * a)) * (ig * xc)
    for hf in range(nh):
        a_sc[hf] = a[:, hf * LANES:(hf + 1) * LANES]
        b_sc[hf] = bb[:, hf * LANES:(hf + 1) * LANES]

    @pl.when(i == 0)
    def _():
        h_sc[...] = jnp.zeros(h_sc.shape, F32)

    def body(s, hs):
        t = jnp.where(d == 0, s, T - 1 - s)
        rows = pl.ds(pl.multiple_of(t * nb, nb), nb)
        new = []
        for hf in range(nh):
            h = a_sc[hf, rows, :] * hs[hf] + b_sc[hf, rows, :]
            hs_sc[hf, rows, :] = h
            new.append(h)
        return tuple(new)

    hs = lax.fori_loop(0, T, body, tuple(h_sc[hf] for hf in range(nh)), unroll=8)
    for hf in range(nh):
        h_sc[hf] = hs[hf]
        for b in range(nb):
            o_ref[0, b, :, hf * LANES:(hf + 1) * LANES] = hs_sc[hf, pl.ds(b, T, stride=nb), :].astype(o_ref.dtype)


def _rglru(xg, conv_w, conv_b, wa, ba, wx, bx, lam):
    B, L, _ = xg.shape
    C = LRU_WIDTH
    assert B == SUBLANES
    T = min(256, L)
    nt = L // T
    hb = T // SUBLANES
    n_halo = L // SUBLANES

    def tile_of(d, i):
        return jnp.where(d == 0, i, nt - 1 - i)

    def blockdiag(w):
        eye = jnp.eye(LRU_HEADS, dtype=w.dtype)
        return jnp.einsum('dhij,hg->dhigj', w, eye).reshape(2, C, C).astype(BF16)

    kern = functools.partial(_lru_kernel, T=T, nt=nt)
    return pl.pallas_call(
        kern,
        out_shape=jax.ShapeDtypeStruct((2, B, L, C), BF16),
        grid=(2, nt),
        in_specs=[
            pl.BlockSpec((B, SUBLANES, C), lambda d, i: (0, jnp.maximum(tile_of(d, i) * hb - 1, 0), 0)),
            pl.BlockSpec((B, T, C), lambda d, i: (0, tile_of(d, i), 0)),
            pl.BlockSpec((B, SUBLANES, C), lambda d, i: (0, jnp.minimum((tile_of(d, i) + 1) * hb, n_halo - 1), 0)),
            pl.BlockSpec(conv_w.shape, lambda d, i: (0, 0)),
            pl.BlockSpec((1, C), lambda d, i: (0, 0)),
            pl.BlockSpec((1, C, C), lambda d, i: (d, 0, 0)),
            pl.BlockSpec((1, 1, C), lambda d, i: (d, 0, 0)),
            pl.BlockSpec((1, C, C), lambda d, i: (d, 0, 0)),
            pl.BlockSpec((1, 1, C), lambda d, i: (d, 0, 0)),
            pl.BlockSpec((1, 1, C), lambda d, i: (d, 0, 0)),
        ],
        out_specs=pl.BlockSpec((1, B, T, C), lambda d, i: (d, 0, tile_of(d, i), 0)),
        scratch_shapes=[
            pltpu.VMEM((C // LANES, (T + 2 * SUBLANES) * B, LANES), F32),
            pltpu.VMEM((C // LANES, T * B, LANES), F32),
            pltpu.VMEM((C // LANES, T * B, LANES), F32),
            pltpu.VMEM((C // LANES, T * B, LANES), F32),
            pltpu.VMEM((C // LANES, B, LANES), F32),
        ],
        compiler_params=_cparams(("arbitrary", "arbitrary")),
        name="rglru",
    )(xg, xg, xg, conv_w, conv_b.reshape(1, C), blockdiag(wa), ba.reshape(2, 1, C),
      blockdiag(wx), bx.reshape(2, 1, C), lam.reshape(2, 1, C))


def _filter_positions(L):
    t = np.linspace(0.0, 1.0, L)[:, None]
    w = 2.0 * math.pi * np.arange(L, dtype=np.float64)[:, None] / L
    f = np.linspace(1e-4, HY_BANDS - 1, HY_BANDS)[None, :]
    z = np.concatenate([t, np.cos(f * w), -np.sin(f * w)], axis=-1)
    zrev = np.concatenate([z[:1], z[:0:-1]], axis=0)
    return jnp.asarray(np.concatenate([z, zrev], axis=1), F32)


def _filter_kernel(zz_ref, w1_ref, b1_ref, w2_ref, b2_ref, w3_ref, b3_ref, fr_ref, wo_ref, dl_ref,
                   fs_ref, asum_ref, *, T, nemb):
    i = pl.program_id(0)
    fr = fr_ref[...]
    hd = lambda a, b: jnp.dot(a, b, precision=HIGHEST, preferred_element_type=F32)
    zz = zz_ref[...]
    h = jnp.sin(fr * (hd(zz, w1_ref[...]) + b1_ref[...]))
    h = jnp.sin(fr * (hd(h, w2_ref[...]) + b2_ref[...]))
    h = jnp.sin(fr * (hd(h, w3_ref[...]) + b3_ref[...]))
    k = hd(h, wo_ref[...])
    dl = dl_ref[...]
    dec_f = jnp.exp(-zz[:, 0:1] * dl)
    dec_r = jnp.exp(-zz[:, nemb:nemb + 1] * dl)
    C = HY_WIDTH
    first_row = (i * T + lax.broadcasted_iota(jnp.int32, (T, 1), 0)) == 0

    @pl.when(i == 0)
    def _():
        asum_ref[...] = jnp.zeros(asum_ref.shape, F32)

    for o in range(HY_ORDER):
        base = o * 2 * C
        kf = k[:, base:base + C] * dec_f
        kb_here = k[:, base + C:base + 2 * C] * dec_f
        kb_rev = k[:, HY_ORDER * 2 * C + o * C:HY_ORDER * 2 * C + (o + 1) * C] * dec_r
        first = kf + jnp.where(first_row, kb_here, 0.0)
        second = jnp.where(first_row, 0.0, kb_rev)
        fs_ref[o] = first
        fs_ref[HY_ORDER + o] = second
        asum_ref[o:o + 1, :] += jnp.sum(jnp.abs(first) + jnp.abs(second), axis=0, keepdims=True)


def _hyena_filter_time(L, w1, b1, w2, b2, w3, b3, freq, wout):
    zz = _filter_positions(L)
    T = min(512, L)
    C = HY_WIDTH
    nemb = zz.shape[1] // 2
    nf = w2.shape[0]
    deltas = jnp.asarray(np.linspace(HY_MIN_DECAY, HY_MAX_DECAY, C)[None, :], F32)
    twice = lambda w: jnp.kron(jnp.eye(2, dtype=w.dtype), w)
    both = lambda v: jnp.tile(v, 2).reshape(1, 2 * nf)
    w_back = wout.reshape(nf, HY_ORDER, 2, C)[:, :, 1, :].reshape(nf, HY_ORDER * C)
    wo = jnp.concatenate([jnp.concatenate([wout, jnp.zeros((nf, HY_ORDER * C), wout.dtype)], axis=1),
                          jnp.concatenate([jnp.zeros_like(wout), w_back], axis=1)], axis=0)
    const = lambda i: (0, 0)
    return pl.pallas_call(
        functools.partial(_filter_kernel, T=T, nemb=nemb),
        out_shape=(jax.ShapeDtypeStruct((2 * HY_ORDER, L, C), F32), jax.ShapeDtypeStruct((HY_ORDER, C), F32)),
        grid=(L // T,),
        in_specs=[
            pl.BlockSpec((T, 2 * nemb), lambda i: (i, 0)),
            pl.BlockSpec((2 * nemb, 2 * nf), const), pl.BlockSpec((1, 2 * nf), const),
            pl.BlockSpec((2 * nf, 2 * nf), const), pl.BlockSpec((1, 2 * nf), const),
            pl.BlockSpec((2 * nf, 2 * nf), const), pl.BlockSpec((1, 2 * nf), const),
            pl.BlockSpec((1, 2 * nf), const),
            pl.BlockSpec(wo.shape, const),
            pl.BlockSpec((1, C), const),
        ],
        out_specs=(pl.BlockSpec((2 * HY_ORDER, T, C), lambda i: (0, i, 0)),
                   pl.BlockSpec((HY_ORDER, C), const)),
        compiler_params=_cparams(("arbitrary",)),
        name="hy_filter_mlp",
    )(zz, twice(w1), both(b1), twice(w2), both(b2), twice(w3), both(b3), both(freq), wo, deltas)


def _dft_tables(L):
    N = 2 * L
    N2 = DFT_N2
    N1 = N // N2
    K1 = N1 // 2
    kept = K1 + 1
    rows = -(-kept // SUBLANES) * SUBLANES
    live = (np.arange(rows) < kept).astype(np.float64)[None, :, None]
    pair = np.where((np.arange(rows) == 0) | (np.arange(rows) == K1), 1.0, 2.0)[None, :, None] * live
    n1 = np.arange(K1)[None, None, :]
    k1 = np.arange(rows)[None, :, None]
    n2 = np.arange(N2)[:, None, None]
    ang = 2.0 * np.pi * (((n1 * N2 + n2) * k1) % N) / N
    g_fwd = np.concatenate([np.cos(ang) * live, -np.sin(ang) * live], axis=1)
    t = lambda a: np.transpose(a, (0, 2, 1))
    g_inv = np.concatenate([t(np.cos(ang) * pair), t(-np.sin(ang) * pair)], axis=2) / N
    a2 = 2.0 * np.pi * ((np.arange(N2)[:, None] * np.arange(N2)[None, :]) % N2) / N2
    cr, ci = np.cos(a2), -np.sin(a2)
    m2 = np.block([[cr, -ci], [ci, cr]])
    m2i = np.block([[cr, ci], [-ci, cr]])
    as_bf = lambda a: jnp.asarray(a, BF16)
    return as_bf(g_fwd), as_bf(g_inv), as_bf(m2), as_bf(m2i)


DFT_NB = 8
N_HALF = HY_WIDTH // LANES


def _lane_half_specs(k1n, nb, part):
    return [pl.BlockSpec((1, k1n, nb, LANES), lambda s, j, h=h: (s, 0, j, part * N_HALF + h)) for h in range(N_HALF)]


def _major_half_specs(k1n, nb):
    return [pl.BlockSpec((1, 1, k1n, nb, LANES), lambda s, j, h=h: (s, h, 0, j, 0)) for h in range(N_HALF)]


def _ld_time(ref, j, nb):
    k1n = ref.shape[-3]
    return ref.reshape(k1n * nb, LANES)[pl.ds(j, k1n, stride=nb), :]


def _st_time(ref, h, j, nb, val):
    k1n = ref.shape[2]
    ref.reshape(N_HALF * k1n * nb, LANES)[pl.ds(h * k1n * nb + j, k1n, stride=nb), :] = val


U32 = jnp.uint32


def _pack_halves(x):
    lo = lax.bitcast_convert_type(x[:, :LANES], U32)
    hi = lax.bitcast_convert_type(x[:, LANES:], U32)
    rnd = jnp.uint32(0x8000)
    return ((lo + rnd) >> 16) | ((hi + rnd) & jnp.uint32(0xFFFF0000))


def _unpack_halves(w):
    lo = lax.bitcast_convert_type(w << 16, F32)
    hi = lax.bitcast_convert_type(w & jnp.uint32(0xFFFF0000), F32)
    return jnp.concatenate([lo, hi], axis=1)


def _dft_a_kernel(x0_ref, x1_ref, g_ref, o_ref, *, nb):
    n1 = o_ref.shape[3]
    for j in range(nb):
        parts = []
        for x_ref in (x0_ref, x1_ref):
            parts.append(_ld_time(x_ref, j, nb))
        xj = jnp.concatenate(parts, axis=1).astype(BF16)
        r = _pack_halves(_dot(g_ref[j], xj))
        o_ref[0, 0, j] = r[:n1]
        o_ref[0, 1, j] = r[n1:]


def _dft_stage_a(x, x_specs, g_fwd):
    S = x.shape[0]
    N2, two_n1, K1 = g_fwd.shape
    N1 = two_n1 // 2
    nb = DFT_NB
    return pl.pallas_call(
        functools.partial(_dft_a_kernel, nb=nb),
        out_shape=jax.ShapeDtypeStruct((S, 2, N2, N1, LANES), U32),
        grid=(S, N2 // nb),
        in_specs=x_specs + [pl.BlockSpec((nb, 2 * N1, K1), lambda s, j: (j, 0, 0))],
        out_specs=pl.BlockSpec((1, 2, nb, N1, LANES), lambda s, j: (s, 0, j, 0, 0)),
        compiler_params=_cparams(("parallel", "parallel")),
        name="hy_dft_a",
    )(x, x, g_fwd)


def _k1_rows(ref, j):
    _, _, n2, tk1, _ = ref.shape
    return ref.reshape(2 * n2 * tk1, LANES), pl.ds(j, 2 * n2, stride=tk1)


def _ld_k1(ref, j):
    r2, rows = _k1_rows(ref, j)
    return r2[rows, :]


def _st_k1(ref, j, val):
    r2, rows = _k1_rows(ref, j)
    r2[rows, :] = val


def _dft_b_filter_kernel(s1_ref, s2_ref, m2_ref, asum_ref, o_ref, *, tk1):
    n2 = s1_ref.shape[2]
    C = o_ref.shape[4]
    scale = 1.0 / asum_ref[0]
    for j in range(tk1):
        sign = 1.0 if j % 2 == 0 else -1.0
        s = _unpack_halves(_ld_k1(s1_ref, j)) + sign * _unpack_halves(_ld_k1(s2_ref, j))
        xk = _dot(m2_ref[...], s.astype(BF16)) * scale
        o_ref[0, j] = xk.reshape(2, n2, C)


def _dft_stage_b_filter(sa, m2, asum):
    _, _, N2, N1, _ = sa.shape
    C = HY_WIDTH
    tk1 = SUBLANES
    blk = (1, 2, N2, tk1, LANES)
    return pl.pallas_call(
        functools.partial(_dft_b_filter_kernel, tk1=tk1),
        out_shape=jax.ShapeDtypeStruct((HY_ORDER, N1, 2, N2, C), F32),
        grid=(HY_ORDER, N1 // tk1),
        in_specs=[
            pl.BlockSpec(blk, lambda o, i: (o, 0, 0, i, 0)),
            pl.BlockSpec(blk, lambda o, i: (HY_ORDER + o, 0, 0, i, 0)),
            pl.BlockSpec((2 * N2, 2 * N2), lambda o, i: (0, 0)),
            pl.BlockSpec((1, 1, C), lambda o, i: (o, 0, 0)),
        ],
        out_specs=pl.BlockSpec((1, tk1, 2, N2, C), lambda o, i: (o, i, 0, 0, 0)),
        compiler_params=_cparams(("parallel", "parallel")),
        name="hy_dft_b_filter",
    )(sa, sa, m2, asum.reshape(HY_ORDER, 1, C))


def _dft_b_conv_kernel(s_ref, kf_ref, m2_ref, m2i_ref, o_ref, *, tk1):
    n2 = s_ref.shape[2]
    for j in range(tk1):
        x = _dot(m2_ref[...], _unpack_halves(_ld_k1(s_ref, j)).astype(BF16))
        xr, xi = x[:n2], x[n2:]
        kr = kf_ref[0, j, 0]
        ki = kf_ref[0, j, 1]
        y = jnp.concatenate([xr * kr - xi * ki, xr * ki + xi * kr], axis=0).astype(BF16)
        _st_k1(o_ref, j, _pack_halves(_dot(m2i_ref[...], y)))


def _dft_stage_b_conv(sa, kspec, order, m2, m2i):
    S, _, N2, N1, _ = sa.shape
    C = HY_WIDTH
    tk1 = SUBLANES
    blk = (1, 2, N2, tk1, LANES)
    return pl.pallas_call(
        functools.partial(_dft_b_conv_kernel, tk1=tk1),
        out_shape=jax.ShapeDtypeStruct(sa.shape, U32),
        grid=(N1 // tk1, S),
        in_specs=[
            pl.BlockSpec(blk, lambda i, s: (s, 0, 0, i, 0)),
            pl.BlockSpec((1, tk1, 2, N2, C), lambda i, s: (order, i, 0, 0, 0)),
            pl.BlockSpec((2 * N2, 2 * N2), lambda i, s: (0, 0)),
            pl.BlockSpec((2 * N2, 2 * N2), lambda i, s: (0, 0)),
        ],
        out_specs=pl.BlockSpec(blk, lambda i, s: (s, 0, 0, i, 0)),
        compiler_params=_cparams(("parallel", "parallel")),
        name="hy_dft_b_conv",
    )(sa, kspec, m2, m2i)


def _dft_c_kernel(c_ref, gi_ref, z0_ref, z1_ref, g0_ref, g1_ref, d_ref, o_ref, *, nb):
    d = d_ref[0]
    for j in range(nb):
        cat = _unpack_halves(jnp.concatenate([c_ref[0, 0, j], c_ref[0, 1, j]], axis=0))
        y = _dot(gi_ref[j], cat.astype(BF16))
        for h, (z_ref, g_ref) in enumerate(((z0_ref, g0_ref), (z1_ref, g1_ref))):
            lanes = slice(h * LANES, (h + 1) * LANES)
            _st_time(o_ref, h, j, nb, _ld_time(g_ref, j, nb) * (y[:, lanes] + _ld_time(z_ref, j, nb) * d[:, lanes]))


def _dft_stage_c(sc, g_inv, z, z_specs, gate, gate_specs, d):
    S, _, N2, N1, _ = sc.shape
    C = HY_WIDTH
    K1 = g_inv.shape[1]
    nb = DFT_NB
    return pl.pallas_call(
        functools.partial(_dft_c_kernel, nb=nb),
        out_shape=jax.ShapeDtypeStruct((S, N_HALF, K1, N2, LANES), F32),
        grid=(S, N2 // nb),
        in_specs=[
            pl.BlockSpec((1, 2, nb, N1, LANES), lambda s, j: (s, 0, j, 0, 0)),
            pl.BlockSpec((nb, K1, 2 * N1), lambda s, j: (j, 0, 0)),
        ] + z_specs + gate_specs + [pl.BlockSpec((1, 1, C), lambda s, j: (0, 0, 0))],
        out_specs=pl.BlockSpec((1, N_HALF, K1, nb, LANES), lambda s, j: (s, 0, 0, j, 0)),
        compiler_params=_cparams(("parallel", "parallel")),
        name="hy_dft_c",
    )(sc, g_inv, z, z, gate, gate, d.reshape(1, 1, C))


def _hyena(hc, w1, b1, w2, b2, w3, b3, freq, wout, bias, tables):
    B, L, _ = hc.shape
    C = HY_WIDTH
    g_fwd, g_inv, m2, m2i = tables
    N2 = DFT_N2
    K1 = L // N2
    nb = DFT_NB
    fs, asum = _hyena_filter_time(L, w1, b1, w2, b2, w3, b3, freq, wout)
    fa = _dft_stage_a(fs.reshape(2 * HY_ORDER, K1, N2, C), _lane_half_specs(K1, nb, 0), g_fwd)
    kspec = _dft_stage_b_filter(fa, m2, asum)
    hc4 = hc.reshape(B, K1, N2, (HY_ORDER + 1) * C)
    z, z_specs = hc4, _lane_half_specs(K1, nb, 0)
    for o in range(HY_ORDER):
        sa = _dft_stage_a(z, z_specs, g_fwd)
        sc = _dft_stage_b_conv(sa, kspec, o, m2, m2i)
        z = _dft_stage_c(sc, g_inv, z, z_specs, hc4, _lane_half_specs(K1, nb, 1 + o), bias[o])
        z_specs = _major_half_specs(K1, nb)
    return z.reshape(B, N_HALF, L, LANES)


def _outproj_kernel(x_ref, at_ref, h_ref, gr_ref, zy_ref, w_ref, gt_ref, g_ref, b_ref, o_ref, *, alpha):
    lru = (h_ref[0, 0].astype(F32) + h_ref[1, 0].astype(F32)) * jax.nn.gelu(gr_ref[0])
    mixed = jnp.concatenate([at_ref[0], lru.astype(BF16)] + [zy_ref[0, h].astype(BF16) for h in range(N_HALF)], axis=1)
    m = _dot(mixed, w_ref[...])
    o_ref[0] = _layer_norm(alpha * x_ref[0] + gt_ref[0] * m, g_ref[...], b_ref[...])


def _out_proj(x, attn, h, xg, zy, w_out, gt, ln_g, ln_b, alpha):
    B, L, D = x.shape
    T = min(512, L)
    C = LRU_WIDTH
    row = lambda b, i: (b, i, 0)
    vec = lambda b, i: (b, 0, 0)
    const = lambda b, i: (0, 0)
    return pl.pallas_call(
        functools.partial(_outproj_kernel, alpha=alpha),
        out_shape=jax.ShapeDtypeStruct((B, L, D), F32),
        grid=(B, L // T),
        in_specs=[
            pl.BlockSpec((1, T, D), row),
            pl.BlockSpec((1, T, IN_Q), row),
            pl.BlockSpec((2, 1, T, C), lambda b, i: (0, b, i, 0)),
            pl.BlockSpec((1, T, C), lambda b, i: (b, i, 1)),
            pl.BlockSpec((1, N_HALF, T, LANES), lambda b, i: (b, 0, i, 0)),
            pl.BlockSpec(w_out.shape, const),
            pl.BlockSpec((1, 1, D), vec),
            pl.BlockSpec((1, D), const),
            pl.BlockSpec((1, D), const),
        ],
        out_specs=pl.BlockSpec((1, T, D), row),
        compiler_params=_cparams(("parallel", "parallel")),
        name="out_proj",
    )(x, attn, h, xg, zy, w_out, gt, ln_g.reshape(1, D), ln_b.reshape(1, D))


FFN_CHUNK = 256


def _ffn_kernel(xp_ref, x_ref, xn_ref, sc_ref, sh_ref, gt_ref, wu_ref, cw_ref, cb_ref, wd_ref, g_ref, b_ref,
                o_ref, u_sc, h_sc, acc_sc, *, T, nt, nf, alpha):
    i = pl.program_id(1)
    H = SUBLANES
    R = T + 2 * H
    F = FFN_CHUNK
    sc = 1.0 + sc_ref[0]
    sh = sh_ref[0]
    u_sc[H:H + T, :] = (x_ref[0] * sc + sh).astype(BF16)
    u_sc[0:H, :] = jnp.where(i > 0, xp_ref[0] * sc + sh, 0.0).astype(BF16)
    u_sc[H + T:R, :] = jnp.where(i < nt - 1, xn_ref[0] * sc + sh, 0.0).astype(BF16)
    acc_sc[...] = jnp.zeros(acc_sc.shape, F32)

    def up(f, slot):
        h_sc[slot] = _dot(u_sc[...], wu_ref[f])

    def down(f, slot):
        h = h_sc[slot]
        cw = cw_ref[f]
        y = cb_ref[f] + pltpu.roll(h, 1, 0) * cw[0:1] + h * cw[1:2] + pltpu.roll(h, R - 1, 0) * cw[2:3]
        y = y[H:H + T]
        act = (jax.nn.gelu(y[:, :F]) * y[:, F:]).astype(BF16)
        acc_sc[...] += _dot(act, wd_ref[f])

    def pair(c2, carry):
        f = 2 * c2
        up(f + 1, 1)
        down(f, 0)
        up(jnp.minimum(f + 2, nf - 1), 0)
        down(f + 1, 1)
        return carry

    up(0, 0)
    lax.fori_loop(0, nf // 2, pair, 0)
    if nf % 2 == 1:
        down(nf - 1, 0)
    o_ref[0] = _layer_norm(alpha * x_ref[0] + gt_ref[0] * acc_sc[...], g_ref[...], b_ref[...])


def _conv_ffn(x, sc, sh, gt, w_up, conv_w, conv_b, w_down, ln_g, ln_b, alpha):
    B, L, D = x.shape
    d_ff = w_down.shape[0]
    T = min(512, L)
    F = FFN_CHUNK
    nt = L // T
    nf = d_ff // F
    hb = T // SUBLANES
    n_halo = L // SUBLANES
    nk = conv_w.shape[0]
    chunked = lambda w: jnp.concatenate([w[..., :d_ff].reshape(w.shape[0], nf, F),
                                         w[..., d_ff:].reshape(w.shape[0], nf, F)], axis=-1).transpose(1, 0, 2)
    wu = chunked(w_up)
    cw = chunked(conv_w)
    cb = chunked(conv_b.reshape(1, 2 * d_ff))
    wd = w_down.reshape(nf, F, D)
    row = lambda b, i: (b, i, 0)
    vec = lambda b, i: (b, 0, 0)
    const2 = lambda b, i: (0, 0)
    const3 = lambda b, i: (0, 0, 0)
    resident = lambda shape, imap: pl.BlockSpec(shape, imap, pipeline_mode=pl.Buffered(1))
    return pl.pallas_call(
        functools.partial(_ffn_kernel, T=T, nt=nt, nf=nf, alpha=alpha),
        out_shape=jax.ShapeDtypeStruct((B, L, D), F32),
        grid=(B, nt),
        in_specs=[
            pl.BlockSpec((1, SUBLANES, D), lambda b, i: (b, jnp.maximum(i * hb - 1, 0), 0)),
            pl.BlockSpec((1, T, D), row),
            pl.BlockSpec((1, SUBLANES, D), lambda b, i: (b, jnp.minimum((i + 1) * hb, n_halo - 1), 0)),
            pl.BlockSpec((1, 1, D), vec),
            pl.BlockSpec((1, 1, D), vec),
            pl.BlockSpec((1, 1, D), vec),
            resident((nf, D, 2 * F), const3),
            resident((nf, nk, 2 * F), const3),
            resident((nf, 1, 2 * F), const3),
            resident((nf, F, D), const3),
            pl.BlockSpec((1, D), const2),
            pl.BlockSpec((1, D), const2),
        ],
        out_specs=pl.BlockSpec((1, T, D), row),
        scratch_shapes=[
            pltpu.VMEM((T + 2 * SUBLANES, D), BF16),
            pltpu.VMEM((2, T + 2 * SUBLANES, 2 * F), F32),
            pltpu.VMEM((T, D), F32),
        ],
        compiler_params=_cparams(("parallel", "parallel")),
        name="conv_ffn",
    )(x, x, x, sc, sh, gt, wu, cw, cb, wd, ln_g.reshape(1, D), ln_b.reshape(1, D))


def _trunk(x, mod, p):
    B, L, D = x.shape
    depth = mod.shape[0]
    alpha = (2 * depth) ** 0.25
    rope = _rope_tables(L)
    tables = _dft_tables(L)
    for l in range(depth):
        m6 = mod[l].reshape(B, 6, 1, D)
        sh1, sc1, gt1, sh2, sc2, gt2 = (m6[:, j] for j in range(6))
        q, k, v, xg, hc = _in_proj(x, sc1, sh1, p['w_in'][l], p['q_gain'][l], p['k_gain'][l], rope,
                                   p['hy_conv_w'][l], p['hy_conv_b'][l])
        attn = _attention(q, k, v)
        h = _rglru(xg, p['lru_conv_w'][l], p['lru_conv_b'][l], p['lru_wa'][l], p['lru_ba'][l],
                   p['lru_wx'][l], p['lru_bx'][l], p['lru_lambda'][l])
        zy = _hyena(hc, p['hy_w1'][l], p['hy_b1'][l], p['hy_w2'][l], p['hy_b2'][l], p['hy_w3'][l], p['hy_b3'][l],
                    p['hy_freq'][l], p['hy_wout'][l], p['hy_bias'][l], tables)
        x = _out_proj(x, attn, h, xg, zy, p['w_out'][l], gt1, p['ln1_g'][l], p['ln1_b'][l], alpha)
        x = _conv_ffn(x, sc2, sh2, gt2, p['ffn_w_up'][l], p['ffn_conv_w'][l], p['ffn_conv_b'][l],
                      p['ffn_w_down'][l], p['ln2_g'][l], p['ln2_b'][l], alpha)
    return x


def kernel(x_prompt, x_sample, c_prompt, c_sample, ada_w, ada_b, w_in, q_gain, k_gain, lru_conv_w, lru_conv_b, lru_wa, lru_ba, lru_wx, lru_bx, lru_lambda, hy_conv_w, hy_conv_b, hy_w1, hy_b1, hy_w2, hy_b2, hy_w3, hy_b3, hy_freq, hy_wout, hy_bias, w_out, ln1_g, ln1_b, ffn_w_up, ffn_conv_w, ffn_conv_b, ffn_w_down, ln2_g, ln2_b):
    p = dict(
        w_in=w_in.astype(BF16), q_gain=q_gain, k_gain=k_gain, lru_conv_w=lru_conv_w, lru_conv_b=lru_conv_b,
        lru_wa=lru_wa, lru_ba=lru_ba, lru_wx=lru_wx, lru_bx=lru_bx, lru_lambda=lru_lambda,
        hy_conv_w=hy_conv_w, hy_conv_b=hy_conv_b, hy_w1=hy_w1, hy_b1=hy_b1, hy_w2=hy_w2, hy_b2=hy_b2,
        hy_w3=hy_w3, hy_b3=hy_b3, hy_freq=hy_freq, hy_wout=hy_wout, hy_bias=hy_bias,
        w_out=w_out.astype(BF16), ln1_g=ln1_g, ln1_b=ln1_b, ffn_w_up=ffn_w_up.astype(BF16),
        ffn_conv_w=ffn_conv_w, ffn_conv_b=ffn_conv_b, ffn_w_down=ffn_w_down.astype(BF16), ln2_g=ln2_g, ln2_b=ln2_b,
    )
    nb = x_prompt.shape[0]
    mod = _ada_mod(jnp.concatenate([c_prompt, c_sample], axis=0), ada_w, ada_b)
    y_prompt = _trunk(x_prompt, mod[:, :nb], p)
    y_sample = _trunk(x_sample, mod[:, nb:], p)
    return (y_prompt, y_sample)
```

```python
import functools
import math

import numpy as np
import jax
import jax.numpy as jnp
from jax import lax
from jax.experimental import pallas as pl
from jax.experimental.pallas import tpu as pltpu

F32 = jnp.float32
BF16 = jnp.bfloat16
HIGHEST = lax.Precision.HIGHEST

GRID_W = 64
HEAD_DIM = 64
N_HEADS = 8
N_KV_HEADS = 2
KV_GROUP = N_HEADS // N_KV_HEADS
ROPE_THETA = 10000.0
ROPE_FREQS = HEAD_DIM // 4
QK_EPS = 1e-6
LRU_WIDTH = 256
LRU_HEADS = 4
LRU_C = 8.0
HY_WIDTH = 256
HY_ORDER = 2
HY_BANDS = 16
HY_MIN_DECAY = abs(math.log(1e-2)) / 1.5
HY_MAX_DECAY = abs(math.log(1e-2)) / 0.3
LN_EPS = 1e-5
IN_Q = N_HEADS * HEAD_DIM
IN_KV = N_KV_HEADS * HEAD_DIM

LANES = 128
SUBLANES = 8
DFT_N2 = 128
VMEM_LIMIT = 48 * 1024 * 1024


def _cparams(sem):
    return pltpu.CompilerParams(dimension_semantics=sem, vmem_limit_bytes=VMEM_LIMIT)


def _dot(a, b):
    return jnp.dot(a, b, preferred_element_type=F32)


def _layer_norm(y, g, b):
    mu = jnp.mean(y, axis=-1, keepdims=True)
    yc = y - mu
    var = jnp.mean(yc * yc, axis=-1, keepdims=True)
    return yc * lax.rsqrt(var + LN_EPS) * g + b


def _ada_kernel(c_ref, w_ref, b_ref, o_ref):
    c = c_ref[...]
    s = c * jax.nn.sigmoid(c)
    o_ref[0] = jnp.dot(s, w_ref[0], precision=HIGHEST, preferred_element_type=F32) + b_ref[0]


def _ada_mod(c_all, ada_w, ada_b):
    depth, d, n = ada_w.shape
    rows = c_all.shape[0]
    tn = 768
    return pl.pallas_call(
        _ada_kernel,
        out_shape=jax.ShapeDtypeStruct((depth, rows, n), F32),
        grid=(depth, n // tn),
        in_specs=[
            pl.BlockSpec((rows, d), lambda l, j: (0, 0)),
            pl.BlockSpec((1, d, tn), lambda l, j: (l, 0, j)),
            pl.BlockSpec((1, 1, tn), lambda l, j: (l, 0, j)),
        ],
        out_specs=pl.BlockSpec((1, rows, tn), lambda l, j: (l, 0, j)),
        compiler_params=_cparams(("parallel", "parallel")),
        name="ada_mod",
    )(c_all, ada_w, ada_b.reshape(depth, 1, n))


def _rope_tables(L):
    rows = L // GRID_W
    row = np.repeat(np.arange(rows, dtype=np.float64), GRID_W)
    col = np.tile(np.arange(GRID_W, dtype=np.float64), rows)
    inv = ROPE_THETA ** (-np.arange(ROPE_FREQS, dtype=np.float64) / ROPE_FREQS)
    ar = row[:, None] * inv
    ac = col[:, None] * inv
    zeros = np.zeros_like(ar)
    cos = np.concatenate([np.cos(ar), np.cos(ar), np.cos(ac), np.cos(ac)], axis=1)
    sin_up = np.concatenate([-np.sin(ar), zeros, -np.sin(ac), zeros], axis=1)
    sin_dn = np.concatenate([zeros, np.sin(ar), zeros, np.sin(ac)], axis=1)
    two = lambda t: jnp.asarray(np.concatenate([t, t], axis=1), F32)
    return two(cos), two(sin_up), two(sin_dn)


def _inproj_kernel(xp_ref, x_ref, xn_ref, sc_ref, sh_ref, w_ref, qg_ref, kg_ref, cos_ref, sup_ref, sdn_ref, bd_ref,
                   hcw_ref, hcb_ref, q_ref, k_ref, vt_ref, xg_ref, hc_ref, *, T, nt):
    i = pl.program_id(1)
    H = SUBLANES
    R = T + 2 * H
    sc = 1.0 + sc_ref[0]
    sh = sh_ref[0]
    u_ext = jnp.concatenate([jnp.where(i > 0, xp_ref[0] * sc + sh, 0.0), x_ref[0] * sc + sh,
                             jnp.where(i < nt - 1, xn_ref[0] * sc + sh, 0.0)], axis=0).astype(BF16)
    proj_ext = _dot(u_ext, w_ref[...])
    proj = proj_ext[H:H + T]
    cos = cos_ref[...]
    sup = sup_ref[...]
    sdn = sdn_ref[...]
    bd = bd_ref[...]
    half = ROPE_FREQS

    def norm_rope(t, gain):
        ms = _dot((t * t).astype(BF16), bd)
        tn = t * lax.rsqrt(ms + QK_EPS) * gain
        return (tn * cos + pltpu.roll(tn, LANES - half, 1) * sup + pltpu.roll(tn, half, 1) * sdn)

    qg = qg_ref[...]
    for j in range(IN_Q // LANES):
        sl = slice(j * LANES, (j + 1) * LANES)
        q_ref[0, :, sl] = (norm_rope(proj[:, sl], qg) * (HEAD_DIM ** -0.5 * math.log2(math.e))).astype(BF16)
    kn = norm_rope(proj[:, IN_Q:IN_Q + IN_KV], kg_ref[...])
    v_t = proj[:, IN_Q + IN_KV:IN_Q + 2 * IN_KV].T
    tk = vt_ref.shape[-1]
    tail = (lax.broadcasted_iota(jnp.int32, (V_ROWS - HEAD_DIM, tk), 0) == 0).astype(BF16)
    for g in range(N_KV_HEADS):
        k_ref[0, g] = kn[:, g * HEAD_DIM:(g + 1) * HEAD_DIM].astype(BF16)
        for c in range(T // tk):
            vt_ref[0, g, c, 0:HEAD_DIM, :] = v_t[g * HEAD_DIM:(g + 1) * HEAD_DIM, c * tk:(c + 1) * tk].astype(BF16)
            vt_ref[0, g, c, HEAD_DIM:, :] = tail
    o = IN_Q + 2 * IN_KV
    xg_ref[0] = proj[:, o:o + 2 * LRU_WIDTH]
    hy = proj_ext[:, o + 2 * LRU_WIDTH:]
    cw = hcw_ref[...]
    hc = hcb_ref[...] + pltpu.roll(hy, 1, 0) * cw[0:1] + hy * cw[1:2] + pltpu.roll(hy, R - 1, 0) * cw[2:3]
    hc_ref[0] = hc[H:H + T]


def _in_proj(x, sc, sh, w_in, q_gain, k_gain, rope, hy_conv_w, hy_conv_b):
    B, L, D = x.shape
    n_in = w_in.shape[1]
    T = min(512, L)
    nt = L // T
    hb = T // SUBLANES
    n_halo = L // SUBLANES
    _, tk = _attn_tiles(L)
    cos, sup, sdn = rope
    bd = jnp.asarray(np.kron(np.eye(2), np.full((HEAD_DIM, HEAD_DIM), 1.0 / HEAD_DIM)), BF16)
    qg = jnp.tile(q_gain, 2).reshape(1, LANES)
    kg = jnp.tile(k_gain, 2).reshape(1, LANES)
    n_hy = n_in - IN_Q - 2 * IN_KV - 2 * LRU_WIDTH
    row = lambda b, i: (b, i, 0)
    vec = lambda b, i: (b, 0, 0)
    tab = lambda b, i: (i, 0)
    const = lambda b, i: (0, 0)
    return pl.pallas_call(
        functools.partial(_inproj_kernel, T=T, nt=nt),
        out_shape=(
            jax.ShapeDtypeStruct((B, L, IN_Q), BF16),
            jax.ShapeDtypeStruct((B, N_KV_HEADS, L, HEAD_DIM), BF16),
            jax.ShapeDtypeStruct((B, N_KV_HEADS, L // tk, V_ROWS, tk), BF16),
            jax.ShapeDtypeStruct((B, L, 2 * LRU_WIDTH), F32),
            jax.ShapeDtypeStruct((B, L, n_hy), F32),
        ),
        grid=(B, nt),
        in_specs=[
            pl.BlockSpec((1, SUBLANES, D), lambda b, i: (b, jnp.maximum(i * hb - 1, 0), 0)),
            pl.BlockSpec((1, T, D), row),
            pl.BlockSpec((1, SUBLANES, D), lambda b, i: (b, jnp.minimum((i + 1) * hb, n_halo - 1), 0)),
            pl.BlockSpec((1, 1, D), vec),
            pl.BlockSpec((1, 1, D), vec),
            pl.BlockSpec((D, n_in), const),
            pl.BlockSpec((1, LANES), const),
            pl.BlockSpec((1, LANES), const),
            pl.BlockSpec((T, LANES), tab),
            pl.BlockSpec((T, LANES), tab),
            pl.BlockSpec((T, LANES), tab),
            pl.BlockSpec((LANES, LANES), const),
            pl.BlockSpec(hy_conv_w.shape, const),
            pl.BlockSpec((1, n_hy), const),
        ],
        out_specs=(
            pl.BlockSpec((1, T, IN_Q), row),
            pl.BlockSpec((1, N_KV_HEADS, T, HEAD_DIM), lambda b, i: (b, 0, i, 0)),
            pl.BlockSpec((1, N_KV_HEADS, T // tk, V_ROWS, tk), lambda b, i: (b, 0, i, 0, 0)),
            pl.BlockSpec((1, T, 2 * LRU_WIDTH), row),
            pl.BlockSpec((1, T, n_hy), row),
        ),
        compiler_params=_cparams(("parallel", "parallel")),
        name="in_proj",
    )(x, x, x, sc, sh, w_in, qg, kg, cos, sup, sdn, bd, hy_conv_w, hy_conv_b.reshape(1, n_hy))


V_ROWS = HEAD_DIM + 16


def _attn_kernel(qt_ref, k_ref, vt_ref, o_ref, acc_sc, s_sc, *, tk, nk):
    tq = qt_ref.shape[-1]
    acc_sc[...] = jnp.zeros(acc_sc.shape, F32)

    def scores_h(c, slot, h):
        kc = k_ref[0, 0, pl.ds(pl.multiple_of(c * tk, tk), tk), :]
        s = _dot(kc, qt_ref[0, 0, h])
        s_sc[slot, h] = s
        return jnp.max(s, axis=0, keepdims=True)

    def consume_h(c, slot, h, m_prev, m_chunk):
        vc = vt_ref[0, 0, c]
        m_new = jnp.maximum(m_prev, m_chunk)
        p = jnp.exp2(s_sc[slot, h] - m_new).astype(BF16)
        alpha = jnp.exp2(m_prev - m_new)
        acc_sc[h] = alpha * acc_sc[h] + _dot(vc, p)
        return m_new

    def step(c, c_next, slot, carry):
        ms, mc = carry
        new_m, new_c = [], []
        for h in range(KV_GROUP):
            new_c.append(scores_h(c_next, 1 - slot, h))
            new_m.append(consume_h(c, slot, h, ms[h], mc[h]))
        return tuple(new_m), tuple(new_c)

    def pair(c2, carry):
        c = 2 * c2
        carry = step(c, c + 1, 0, carry)
        return step(c + 1, jnp.minimum(c + 2, nk - 1), 1, carry)

    m0 = tuple(jnp.full((1, tq), -jnp.inf, F32) for _ in range(KV_GROUP))
    c0 = tuple(scores_h(0, 0, h) for h in range(KV_GROUP))
    lax.fori_loop(0, nk // 2, pair, (m0, c0))
    for h in range(KV_GROUP):
        acc = acc_sc[h]
        o_ref[0, 0, h] = (acc[:HEAD_DIM] / acc[HEAD_DIM:HEAD_DIM + 1]).astype(o_ref.dtype)


def _attn_tiles(L):
    return min(512, L), min(512, L // 2)


def _attention(q, kh, vt):
    B, L, _ = q.shape
    tq, tk = _attn_tiles(L)
    nk = L // tk
    assert nk % 2 == 0
    qt = q.reshape(B, L, N_KV_HEADS, KV_GROUP, HEAD_DIM).transpose(0, 2, 3, 4, 1)
    return pl.pallas_call(
        functools.partial(_attn_kernel, tk=tk, nk=nk),
        out_shape=jax.ShapeDtypeStruct((B, N_KV_HEADS, KV_GROUP, HEAD_DIM, L), BF16),
        grid=(B, N_KV_HEADS, L // tq),
        in_specs=[
            pl.BlockSpec((1, 1, KV_GROUP, HEAD_DIM, tq), lambda b, g, i: (b, g, 0, 0, i)),
            pl.BlockSpec((1, 1, L, HEAD_DIM), lambda b, g, i: (b, g, 0, 0)),
            pl.BlockSpec((1, 1, nk, V_ROWS, tk), lambda b, g, i: (b, g, 0, 0, 0)),
        ],
        out_specs=pl.BlockSpec((1, 1, KV_GROUP, HEAD_DIM, tq), lambda b, g, i: (b, g, 0, 0, i)),
        scratch_shapes=[pltpu.VMEM((KV_GROUP, V_ROWS, tq), F32), pltpu.VMEM((2, KV_GROUP, tk, tq), F32)],
        compiler_params=_cparams(("parallel", "parallel", "parallel")),
        name="attention",
    )(qt, kh, vt)


def _lru_kernel(xp_ref, x_ref, xn_ref, cw_ref, cb_ref, wa_ref, ba_ref, wx_ref, bx_ref, lam_ref,
                o_ref, xe_sc, a_sc, b_sc, hs_sc, h_sc, *, T, nt):
    d = pl.program_id(0)
    i = pl.program_id(1)
    tile = jnp.where(d == 0, i, nt - 1 - i)
    nb, _, C = x_ref.shape
    nh = C // LANES
    H = SUBLANES
    keep_prev = (tile > 0).astype(F32)
    keep_next = (tile < nt - 1).astype(F32)
    for b in range(nb):
        for hf in range(nh):
            lanes = slice(hf * LANES, (hf + 1) * LANES)
            xe_sc[hf, pl.ds(b, H, stride=nb), :] = xp_ref[b, :, lanes] * keep_prev
            xe_sc[hf, pl.ds(H * nb + b, T, stride=nb), :] = x_ref[b, :, lanes]
            xe_sc[hf, pl.ds((H + T) * nb + b, H, stride=nb), :] = xn_ref[b, :, lanes] * keep_next
    cw = cw_ref[...]
    halves = []
    for hf in range(nh):
        lanes = slice(hf * LANES, (hf + 1) * LANES)
        acc = cb_ref[:, lanes]
        for k in range(cw.shape[0]):
            acc = acc + xe_sc[hf, (H - 2 + k) * nb:(H - 2 + k + T) * nb, :] * cw[k:k + 1, lanes]
        halves.append(acc)
    xc = jnp.concatenate(halves, axis=1)
    xb = xc.astype(BF16)
    sigmoid = lambda v: 0.5 * jnp.tanh(0.5 * v) + 0.5
    r = sigmoid(_dot(xb, wa_ref[0]) + ba_ref[0])
    ig = sigmoid(_dot(xb, wx_ref[0]) + bx_ref[0])
    lam = lam_ref[0]
    softplus_neg = jnp.maximum(-lam, 0.0) + jnp.log1p(jnp.exp(-jnp.abs(lam)))
    log_a = -LRU_C * r * softplus_neg
    a = jnp.exp(log_a)
    bb = jnp.sqrt(-jnp.tanh(log_a) * (1.0 + a * a)) * (ig * xc)
    for hf in range(nh):
        a_sc[hf] = a[:, hf * LANES:(hf + 1) * LANES]
        b_sc[hf] = bb[:, hf * LANES:(hf + 1) * LANES]

    @pl.when(i == 0)
    def _():
        h_sc[...] = jnp.zeros(h_sc.shape, F32)

    def body(s, hs):
        t = jnp.where(d == 0, s, T - 1 - s)
        rows = pl.ds(pl.multiple_of(t * nb, nb), nb)
        new = []
        for hf in range(nh):
            h = a_sc[hf, rows, :] * hs[hf] + b_sc[hf, rows, :]
            hs_sc[hf, rows, :] = h
            new.append(h)
        return tuple(new)

    hs = lax.fori_loop(0, T, body, tuple(h_sc[hf] for hf in range(nh)), unroll=8)
    for hf in range(nh):
        h_sc[hf] = hs[hf]
        for b in range(nb):
            o_ref[0, b, :, hf * LANES:(hf + 1) * LANES] = hs_sc[hf, pl.ds(b, T, stride=nb), :].astype(o_ref.dtype)


def _rglru(xg, conv_w, conv_b, wa, ba, wx, bx, lam):
    B, L, _ = xg.shape
    C = LRU_WIDTH
    assert B == SUBLANES
    T = min(256, L)
    nt = L // T
    hb = T // SUBLANES
    n_halo = L // SUBLANES

    def tile_of(d, i):
        return jnp.where(d == 0, i, nt - 1 - i)

    def blockdiag(w):
        eye = jnp.eye(LRU_HEADS, dtype=w.dtype)
        return jnp.einsum('dhij,hg->dhigj', w, eye).reshape(2, C, C).astype(BF16)

    kern = functools.partial(_lru_kernel, T=T, nt=nt)
    return pl.pallas_call(
        kern,
        out_shape=jax.ShapeDtypeStruct((2, B, L, C), BF16),
        grid=(2, nt),
        in_specs=[
            pl.BlockSpec((B, SUBLANES, C), lambda d, i: (0, jnp.maximum(tile_of(d, i) * hb - 1, 0), 0)),
            pl.BlockSpec((B, T, C), lambda d, i: (0, tile_of(d, i), 0)),
            pl.BlockSpec((B, SUBLANES, C), lambda d, i: (0, jnp.minimum((tile_of(d, i) + 1) * hb, n_halo - 1), 0)),
            pl.BlockSpec(conv_w.shape, lambda d, i: (0, 0)),
            pl.BlockSpec((1, C), lambda d, i: (0, 0)),
            pl.BlockSpec((1, C, C), lambda d, i: (d, 0, 0)),
            pl.BlockSpec((1, 1, C), lambda d, i: (d, 0, 0)),
            pl.BlockSpec((1, C, C), lambda d, i: (d, 0, 0)),
            pl.BlockSpec((1, 1, C), lambda d, i: (d, 0, 0)),
            pl.BlockSpec((1, 1, C), lambda d, i: (d, 0, 0)),
        ],
        out_specs=pl.BlockSpec((1, B, T, C), lambda d, i: (d, 0, tile_of(d, i), 0)),
        scratch_shapes=[
            pltpu.VMEM((C // LANES, (T + 2 * SUBLANES) * B, LANES), F32),
            pltpu.VMEM((C // LANES, T * B, LANES), F32),
            pltpu.VMEM((C // LANES, T * B, LANES), F32),
            pltpu.VMEM((C // LANES, T * B, LANES), F32),
            pltpu.VMEM((C // LANES, B, LANES), F32),
        ],
        compiler_params=_cparams(("arbitrary", "arbitrary")),
        name="rglru",
    )(xg, xg, xg, conv_w, conv_b.reshape(1, C), blockdiag(wa), ba.reshape(2, 1, C),
      blockdiag(wx), bx.reshape(2, 1, C), lam.reshape(2, 1, C))


def _filter_positions(L):
    t = np.linspace(0.0, 1.0, L)[:, None]
    w = 2.0 * math.pi * np.arange(L, dtype=np.float64)[:, None] / L
    f = np.linspace(1e-4, HY_BANDS - 1, HY_BANDS)[None, :]
    z = np.concatenate([t, np.cos(f * w), -np.sin(f * w)], axis=-1)
    zrev = np.concatenate([z[:1], z[:0:-1]], axis=0)
    return jnp.asarray(np.concatenate([z, zrev], axis=1), F32)


def _filter_kernel(zz_ref, w1_ref, b1_ref, w2_ref, b2_ref, w3_ref, b3_ref, fr_ref, wo_ref, dl_ref,
                   fs_ref, asum_ref, *, T, nemb):
    i = pl.program_id(0)
    fr = fr_ref[...]
    hd = lambda a, b: jnp.dot(a, b, precision=HIGHEST, preferred_element_type=F32)
    zz = zz_ref[...]
    h = jnp.sin(fr * (hd(zz, w1_ref[...]) + b1_ref[...]))
    h = jnp.sin(fr * (hd(h, w2_ref[...]) + b2_ref[...]))
    h = jnp.sin(fr * (hd(h, w3_ref[...]) + b3_ref[...]))
    k = hd(h, wo_ref[...])
    dl = dl_ref[...]
    dec_f = jnp.exp(-zz[:, 0:1] * dl)
    dec_r = jnp.exp(-zz[:, nemb:nemb + 1] * dl)
    C = HY_WIDTH
    first_row = (i * T + lax.broadcasted_iota(jnp.int32, (T, 1), 0)) == 0

    @pl.when(i == 0)
    def _():
        asum_ref[...] = jnp.zeros(asum_ref.shape, F32)

    for o in range(HY_ORDER):
        base = o * 2 * C
        kf = k[:, base:base + C] * dec_f
        kb_here = k[:, base + C:base + 2 * C] * dec_f
        kb_rev = k[:, HY_ORDER * 2 * C + o * C:HY_ORDER * 2 * C + (o + 1) * C] * dec_r
        first = kf + jnp.where(first_row, kb_here, 0.0)
        second = jnp.where(first_row, 0.0, kb_rev)
        fs_ref[o] = first
        fs_ref[HY_ORDER + o] = second
        asum_ref[o:o + 1, :] += jnp.sum(jnp.abs(first) + jnp.abs(second), axis=0, keepdims=True)


def _hyena_filter_time(L, w1, b1, w2, b2, w3, b3, freq, wout):
    zz = _filter_positions(L)
    T = min(512, L)
    C = HY_WIDTH
    nemb = zz.shape[1] // 2
    nf = w2.shape[0]
    deltas = jnp.asarray(np.linspace(HY_MIN_DECAY, HY_MAX_DECAY, C)[None, :], F32)
    twice = lambda w: jnp.kron(jnp.eye(2, dtype=w.dtype), w)
    both = lambda v: jnp.tile(v, 2).reshape(1, 2 * nf)
    w_back = wout.reshape(nf, HY_ORDER, 2, C)[:, :, 1, :].reshape(nf, HY_ORDER * C)
    wo = jnp.concatenate([jnp.concatenate([wout, jnp.zeros((nf, HY_ORDER * C), wout.dtype)], axis=1),
                          jnp.concatenate([jnp.zeros_like(wout), w_back], axis=1)], axis=0)
    const = lambda i: (0, 0)
    return pl.pallas_call(
        functools.partial(_filter_kernel, T=T, nemb=nemb),
        out_shape=(jax.ShapeDtypeStruct((2 * HY_ORDER, L, C), F32), jax.ShapeDtypeStruct((HY_ORDER, C), F32)),
        grid=(L // T,),
        in_specs=[
            pl.BlockSpec((T, 2 * nemb), lambda i: (i, 0)),
            pl.BlockSpec((2 * nemb, 2 * nf), const), pl.BlockSpec((1, 2 * nf), const),
            pl.BlockSpec((2 * nf, 2 * nf), const), pl.BlockSpec((1, 2 * nf), const),
            pl.BlockSpec((2 * nf, 2 * nf), const), pl.BlockSpec((1, 2 * nf), const),
            pl.BlockSpec((1, 2 * nf), const),
            pl.BlockSpec(wo.shape, const),
            pl.BlockSpec((1, C), const),
        ],
        out_specs=(pl.BlockSpec((2 * HY_ORDER, T, C), lambda i: (0, i, 0)),
                   pl.BlockSpec((HY_ORDER, C), const)),
        compiler_params=_cparams(("arbitrary",)),
        name="hy_filter_mlp",
    )(zz, twice(w1), both(b1), twice(w2), both(b2), twice(w3), both(b3), both(freq), wo, deltas)


def _dft_tables(L):
    N = 2 * L
    N2 = DFT_N2
    N1 = N // N2
    K1 = N1 // 2
    kept = K1 + 1
    rows = -(-kept // SUBLANES) * SUBLANES
    live = (np.arange(rows) < kept).astype(np.float64)[None, :, None]
    pair = np.where((np.arange(rows) == 0) | (np.arange(rows) == K1), 1.0, 2.0)[None, :, None] * live
    n1 = np.arange(K1)[None, None, :]
    k1 = np.arange(rows)[None, :, None]
    n2 = np.arange(N2)[:, None, None]
    ang = 2.0 * np.pi * (((n1 * N2 + n2) * k1) % N) / N
    g_fwd = np.concatenate([np.cos(ang) * live, -np.sin(ang) * live], axis=1)
    t = lambda a: np.transpose(a, (0, 2, 1))
    g_inv = np.concatenate([t(np.cos(ang) * pair), t(-np.sin(ang) * pair)], axis=2) / N
    a2 = 2.0 * np.pi * ((np.arange(N2)[:, None] * np.arange(N2)[None, :]) % N2) / N2
    cr, ci = np.cos(a2), -np.sin(a2)
    m2 = np.block([[cr, -ci], [ci, cr]])
    m2i = np.block([[cr, ci], [-ci, cr]])
    as_bf = lambda a: jnp.asarray(a, BF16)
    return as_bf(g_fwd), as_bf(g_inv), as_bf(m2), as_bf(m2i)


DFT_NB = 16
N_HALF = HY_WIDTH // LANES


def _lane_half_specs(k1n, nb, part):
    return [pl.BlockSpec((1, k1n, nb, LANES), lambda s, j, h=h: (s, 0, j, part * N_HALF + h)) for h in range(N_HALF)]


def _major_half_specs(k1n, nb):
    return [pl.BlockSpec((1, 1, k1n, nb, LANES), lambda s, j, h=h: (s, h, 0, j, 0)) for h in range(N_HALF)]


def _ld_time(ref, j, nb):
    k1n = ref.shape[-3]
    return ref.reshape(k1n * nb, LANES)[pl.ds(j, k1n, stride=nb), :]


def _st_time(ref, h, j, nb, val):
    k1n = ref.shape[2]
    ref.reshape(N_HALF * k1n * nb, LANES)[pl.ds(h * k1n * nb + j, k1n, stride=nb), :] = val


U32 = jnp.uint32


def _pack_halves(x):
    lo = lax.bitcast_convert_type(x[:, :LANES], U32)
    hi = lax.bitcast_convert_type(x[:, LANES:], U32)
    rnd = jnp.uint32(0x8000)
    return ((lo + rnd) >> 16) | ((hi + rnd) & jnp.uint32(0xFFFF0000))


def _unpack_halves(w):
    lo = lax.bitcast_convert_type(w << 16, F32)
    hi = lax.bitcast_convert_type(w & jnp.uint32(0xFFFF0000), F32)
    return jnp.concatenate([lo, hi], axis=1)


def _dft_a_kernel(x0_ref, x1_ref, g_ref, o_ref, *, nb):
    n1 = o_ref.shape[3]
    for j in range(nb):
        parts = []
        for x_ref in (x0_ref, x1_ref):
            parts.append(_ld_time(x_ref, j, nb))
        xj = jnp.concatenate(parts, axis=1).astype(BF16)
        r = _pack_halves(_dot(g_ref[j], xj))
        o_ref[0, 0, j] = r[:n1]
        o_ref[0, 1, j] = r[n1:]


def _dft_stage_a(x, x_specs, g_fwd):
    S = x.shape[0]
    N2, two_n1, K1 = g_fwd.shape
    N1 = two_n1 // 2
    nb = DFT_NB
    return pl.pallas_call(
        functools.partial(_dft_a_kernel, nb=nb),
        out_shape=jax.ShapeDtypeStruct((S, 2, N2, N1, LANES), U32),
        grid=(S, N2 // nb),
        in_specs=x_specs + [pl.BlockSpec((nb, 2 * N1, K1), lambda s, j: (j, 0, 0))],
        out_specs=pl.BlockSpec((1, 2, nb, N1, LANES), lambda s, j: (s, 0, j, 0, 0)),
        compiler_params=_cparams(("parallel", "parallel")),
        name="hy_dft_a",
    )(x, x, g_fwd)


def _k1_rows(ref, j):
    _, _, n2, tk1, _ = ref.shape
    return ref.reshape(2 * n2 * tk1, LANES), pl.ds(j, 2 * n2, stride=tk1)


def _ld_k1(ref, j):
    r2, rows = _k1_rows(ref, j)
    return r2[rows, :]


def _st_k1(ref, j, val):
    r2, rows = _k1_rows(ref, j)
    r2[rows, :] = val


def _dft_b_filter_kernel(s1_ref, s2_ref, m2_ref, asum_ref, o_ref, *, tk1):
    n2 = s1_ref.shape[2]
    C = o_ref.shape[4]
    scale = 1.0 / asum_ref[0]
    for j in range(tk1):
        sign = 1.0 if j % 2 == 0 else -1.0
        s = _unpack_halves(_ld_k1(s1_ref, j)) + sign * _unpack_halves(_ld_k1(s2_ref, j))
        xk = _dot(m2_ref[...], s.astype(BF16)) * scale
        o_ref[0, j] = xk.reshape(2, n2, C)


def _dft_stage_b_filter(sa, m2, asum):
    _, _, N2, N1, _ = sa.shape
    C = HY_WIDTH
    tk1 = SUBLANES
    blk = (1, 2, N2, tk1, LANES)
    return pl.pallas_call(
        functools.partial(_dft_b_filter_kernel, tk1=tk1),
        out_shape=jax.ShapeDtypeStruct((HY_ORDER, N1, 2, N2, C), F32),
        grid=(HY_ORDER, N1 // tk1),
        in_specs=[
            pl.BlockSpec(blk, lambda o, i: (o, 0, 0, i, 0)),
            pl.BlockSpec(blk, lambda o, i: (HY_ORDER + o, 0, 0, i, 0)),
            pl.BlockSpec((2 * N2, 2 * N2), lambda o, i: (0, 0)),
            pl.BlockSpec((1, 1, C), lambda o, i: (o, 0, 0)),
        ],
        out_specs=pl.BlockSpec((1, tk1, 2, N2, C), lambda o, i: (o, i, 0, 0, 0)),
        compiler_params=_cparams(("parallel", "parallel")),
        name="hy_dft_b_filter",
    )(sa, sa, m2, asum.reshape(HY_ORDER, 1, C))


def _dft_b_conv_kernel(s_ref, kf_ref, m2_ref, m2i_ref, o_ref, *, tk1):
    n2 = s_ref.shape[2]
    for j in range(tk1):
        x = _dot(m2_ref[...], _unpack_halves(_ld_k1(s_ref, j)).astype(BF16))
        xr, xi = x[:n2], x[n2:]
        kr = kf_ref[0, j, 0]
        ki = kf_ref[0, j, 1]
        y = jnp.concatenate([xr * kr - xi * ki, xr * ki + xi * kr], axis=0).astype(BF16)
        _st_k1(o_ref, j, _pack_halves(_dot(m2i_ref[...], y)))


def _dft_stage_b_conv(sa, kspec, order, m2, m2i):
    S, _, N2, N1, _ = sa.shape
    C = HY_WIDTH
    tk1 = SUBLANES
    blk = (1, 2, N2, tk1, LANES)
    return pl.pallas_call(
        functools.partial(_dft_b_conv_kernel, tk1=tk1),
        out_shape=jax.ShapeDtypeStruct(sa.shape, U32),
        grid=(N1 // tk1, S),
        in_specs=[
            pl.BlockSpec(blk, lambda i, s: (s, 0, 0, i, 0)),
            pl.BlockSpec((1, tk1, 2, N2, C), lambda i, s: (order, i, 0, 0, 0)),
            pl.BlockSpec((2 * N2, 2 * N2), lambda i, s: (0, 0)),
            pl.BlockSpec((2 * N2, 2 * N2), lambda i, s: (0, 0)),
        ],
        out_specs=pl.BlockSpec(blk, lambda i, s: (s, 0, 0, i, 0)),
        compiler_params=_cparams(("parallel", "parallel")),
        name="hy_dft_b_conv",
    )(sa, kspec, m2, m2i)


def _dft_c_kernel(c_ref, gi_ref, z0_ref, z1_ref, g0_ref, g1_ref, d_ref, o_ref, *, nb):
    d = d_ref[0]
    for j in range(nb):
        cat = _unpack_halves(jnp.concatenate([c_ref[0, 0, j], c_ref[0, 1, j]], axis=0))
        y = _dot(gi_ref[j], cat.astype(BF16))
        for h, (z_ref, g_ref) in enumerate(((z0_ref, g0_ref), (z1_ref, g1_ref))):
            lanes = slice(h * LANES, (h + 1) * LANES)
            _st_time(o_ref, h, j, nb, _ld_time(g_ref, j, nb) * (y[:, lanes] + _ld_time(z_ref, j, nb) * d[:, lanes]))


def _dft_stage_c(sc, g_inv, z, z_specs, gate, gate_specs, d):
    S, _, N2, N1, _ = sc.shape
    C = HY_WIDTH
    K1 = g_inv.shape[1]
    nb = DFT_NB
    return pl.pallas_call(
        functools.partial(_dft_c_kernel, nb=nb),
        out_shape=jax.ShapeDtypeStruct((S, N_HALF, K1, N2, LANES), F32),
        grid=(S, N2 // nb),
        in_specs=[
            pl.BlockSpec((1, 2, nb, N1, LANES), lambda s, j: (s, 0, j, 0, 0)),
            pl.BlockSpec((nb, K1, 2 * N1), lambda s, j: (j, 0, 0)),
        ] + z_specs + gate_specs + [pl.BlockSpec((1, 1, C), lambda s, j: (0, 0, 0))],
        out_specs=pl.BlockSpec((1, N_HALF, K1, nb, LANES), lambda s, j: (s, 0, 0, j, 0)),
        compiler_params=_cparams(("parallel", "parallel")),
        name="hy_dft_c",
    )(sc, g_inv, z, z, gate, gate, d.reshape(1, 1, C))


def _hyena(hc, w1, b1, w2, b2, w3, b3, freq, wout, bias, tables):
    B, L, _ = hc.shape
    C = HY_WIDTH
    g_fwd, g_inv, m2, m2i = tables
    N2 = DFT_N2
    K1 = L // N2
    nb = DFT_NB
    fs, asum = _hyena_filter_time(L, w1, b1, w2, b2, w3, b3, freq, wout)
    fa = _dft_stage_a(fs.reshape(2 * HY_ORDER, K1, N2, C), _lane_half_specs(K1, nb, 0), g_fwd)
    kspec = _dft_stage_b_filter(fa, m2, asum)
    hc4 = hc.reshape(B, K1, N2, (HY_ORDER + 1) * C)
    z, z_specs = hc4, _lane_half_specs(K1, nb, 0)
    for o in range(HY_ORDER):
        sa = _dft_stage_a(z, z_specs, g_fwd)
        sc = _dft_stage_b_conv(sa, kspec, o, m2, m2i)
        z = _dft_stage_c(sc, g_inv, z, z_specs, hc4, _lane_half_specs(K1, nb, 1 + o), bias[o])
        z_specs = _major_half_specs(K1, nb)
    return z.reshape(B, N_HALF, L, LANES)


def _outproj_kernel(x_ref, at_ref, h_ref, gr_ref, zy_ref, w_ref, gt_ref, g_ref, b_ref, o_ref, *, alpha):
    lru = (h_ref[0, 0].astype(F32) + h_ref[1, 0].astype(F32)) * jax.nn.gelu(gr_ref[0])
    na = IN_Q
    at = at_ref[0].reshape(na, at_ref.shape[-1])
    m = lax.dot_general(at, w_ref[0:na], (((0,), (0,)), ((), ())), preferred_element_type=F32)
    rest = jnp.concatenate([lru.astype(BF16)] + [zy_ref[0, h].astype(BF16) for h in range(N_HALF)], axis=1)
    m = m + _dot(rest, w_ref[na:])
    o_ref[0] = _layer_norm(alpha * x_ref[0] + gt_ref[0] * m, g_ref[...], b_ref[...])


def _out_proj(x, attn, h, xg, zy, w_out, gt, ln_g, ln_b, alpha):
    B, L, D = x.shape
    T = min(512, L)
    C = LRU_WIDTH
    row = lambda b, i: (b, i, 0)
    vec = lambda b, i: (b, 0, 0)
    const = lambda b, i: (0, 0)
    return pl.pallas_call(
        functools.partial(_outproj_kernel, alpha=alpha),
        out_shape=jax.ShapeDtypeStruct((B, L, D), F32),
        grid=(B, L // T),
        in_specs=[
            pl.BlockSpec((1, T, D), row),
            pl.BlockSpec((1, N_KV_HEADS, KV_GROUP, HEAD_DIM, T), lambda b, i: (b, 0, 0, 0, i)),
            pl.BlockSpec((2, 1, T, C), lambda b, i: (0, b, i, 0)),
            pl.BlockSpec((1, T, C), lambda b, i: (b, i, 1)),
            pl.BlockSpec((1, N_HALF, T, LANES), lambda b, i: (b, 0, i, 0)),
            pl.BlockSpec(w_out.shape, const),
            pl.BlockSpec((1, 1, D), vec),
            pl.BlockSpec((1, D), const),
            pl.BlockSpec((1, D), const),
        ],
        out_specs=pl.BlockSpec((1, T, D), row),
        compiler_params=_cparams(("parallel", "parallel")),
        name="out_proj",
    )(x, attn, h, xg, zy, w_out, gt, ln_g.reshape(1, D), ln_b.reshape(1, D))


FFN_CHUNK = 256


def _ffn_kernel(xp_ref, x_ref, xn_ref, sc_ref, sh_ref, gt_ref, wu_ref, cw_ref, cb_ref, wd_ref, g_ref, b_ref,
                o_ref, u_sc, h_sc, acc_sc, *, T, nt, nf, alpha):
    i = pl.program_id(1)
    H = SUBLANES
    R = T + 2 * H
    F = FFN_CHUNK
    sc = 1.0 + sc_ref[0]
    sh = sh_ref[0]
    u_sc[H:H + T, :] = (x_ref[0] * sc + sh).astype(BF16)
    u_sc[0:H, :] = jnp.where(i > 0, xp_ref[0] * sc + sh, 0.0).astype(BF16)
    u_sc[H + T:R, :] = jnp.where(i < nt - 1, xn_ref[0] * sc + sh, 0.0).astype(BF16)
    acc_sc[...] = jnp.zeros(acc_sc.shape, F32)

    def up(f, slot):
        h_sc[slot] = _dot(u_sc[...], wu_ref[f])

    def down(f, slot):
        h = h_sc[slot]
        cw = cw_ref[f]
        y = cb_ref[f] + pltpu.roll(h, 1, 0) * cw[0:1] + h * cw[1:2] + pltpu.roll(h, R - 1, 0) * cw[2:3]
        y = y[H:H + T]
        act = (jax.nn.gelu(y[:, :F]) * y[:, F:]).astype(BF16)
        acc_sc[...] += _dot(act, wd_ref[f])

    def pair(c2, carry):
        f = 2 * c2
        up(f + 1, 1)
        down(f, 0)
        up(jnp.minimum(f + 2, nf - 1), 0)
        down(f + 1, 1)
        return carry

    up(0, 0)
    lax.fori_loop(0, nf // 2, pair, 0)
    if nf % 2 == 1:
        down(nf - 1, 0)
    o_ref[0] = _layer_norm(alpha * x_ref[0] + gt_ref[0] * acc_sc[...], g_ref[...], b_ref[...])


def _conv_ffn(x, sc, sh, gt, w_up, conv_w, conv_b, w_down, ln_g, ln_b, alpha):
    B, L, D = x.shape
    d_ff = w_down.shape[0]
    T = min(512, L)
    F = FFN_CHUNK
    nt = L // T
    nf = d_ff // F
    hb = T // SUBLANES
    n_halo = L // SUBLANES
    nk = conv_w.shape[0]
    chunked = lambda w: jnp.concatenate([w[..., :d_ff].reshape(w.shape[0], nf, F),
                                         w[..., d_ff:].reshape(w.shape[0], nf, F)], axis=-1).transpose(1, 0, 2)
    wu = chunked(w_up)
    cw = chunked(conv_w)
    cb = chunked(conv_b.reshape(1, 2 * d_ff))
    wd = w_down.reshape(nf, F, D)
    row = lambda b, i: (b, i, 0)
    vec = lambda b, i: (b, 0, 0)
    const2 = lambda b, i: (0, 0)
    const3 = lambda b, i: (0, 0, 0)
    resident = lambda shape, imap: pl.BlockSpec(shape, imap, pipeline_mode=pl.Buffered(1))
    return pl.pallas_call(
        functools.partial(_ffn_kernel, T=T, nt=nt, nf=nf, alpha=alpha),
        out_shape=jax.ShapeDtypeStruct((B, L, D), F32),
        grid=(B, nt),
        in_specs=[
            pl.BlockSpec((1, SUBLANES, D), lambda b, i: (b, jnp.maximum(i * hb - 1, 0), 0)),
            pl.BlockSpec((1, T, D), row),
            pl.BlockSpec((1, SUBLANES, D), lambda b, i: (b, jnp.minimum((i + 1) * hb, n_halo - 1), 0)),
            pl.BlockSpec((1, 1, D), vec),
            pl.BlockSpec((1, 1, D), vec),
            pl.BlockSpec((1, 1, D), vec),
            resident((nf, D, 2 * F), const3),
            resident((nf, nk, 2 * F), const3),
            resident((nf, 1, 2 * F), const3),
            resident((nf, F, D), const3),
            pl.BlockSpec((1, D), const2),
            pl.BlockSpec((1, D), const2),
        ],
        out_specs=pl.BlockSpec((1, T, D), row),
        scratch_shapes=[
            pltpu.VMEM((T + 2 * SUBLANES, D), BF16),
            pltpu.VMEM((2, T + 2 * SUBLANES, 2 * F), F32),
            pltpu.VMEM((T, D), F32),
        ],
        compiler_params=_cparams(("parallel", "parallel")),
        name="conv_ffn",
    )(x, x, x, sc, sh, gt, wu, cw, cb, wd, ln_g.reshape(1, D), ln_b.reshape(1, D))


def _trunk(x, mod, p):
    B, L, D = x.shape
    depth = mod.shape[0]
    alpha = (2 * depth) ** 0.25
    rope = _rope_tables(L)
    tables = _dft_tables(L)
    for l in range(depth):
        m6 = mod[l].reshape(B, 6, 1, D)
        sh1, sc1, gt1, sh2, sc2, gt2 = (m6[:, j] for j in range(6))
        q, k, v, xg, hc = _in_proj(x, sc1, sh1, p['w_in'][l], p['q_gain'][l], p['k_gain'][l], rope,
                                   p['hy_conv_w'][l], p['hy_conv_b'][l])
        attn = _attention(q, k, v)
        h = _rglru(xg, p['lru_conv_w'][l], p['lru_conv_b'][l], p['lru_wa'][l], p['lru_ba'][l],
                   p['lru_wx'][l], p['lru_bx'][l], p['lru_lambda'][l])
        zy = _hyena(hc, p['hy_w1'][l], p['hy_b1'][l], p['hy_w2'][l], p['hy_b2'][l], p['hy_w3'][l], p['hy_b3'][l],
                    p['hy_freq'][l], p['hy_wout'][l], p['hy_bias'][l], tables)
        x = _out_proj(x, attn, h, xg, zy, p['w_out'][l], gt1, p['ln1_g'][l], p['ln1_b'][l], alpha)
        x = _conv_ffn(x, sc2, sh2, gt2, p['ffn_w_up'][l], p['ffn_conv_w'][l], p['ffn_conv_b'][l],
                      p['ffn_w_down'][l], p['ln2_g'][l], p['ln2_b'][l], alpha)
    return x


def kernel(x_prompt, x_sample, c_prompt, c_sample, ada_w, ada_b, w_in, q_gain, k_gain, lru_conv_w, lru_conv_b, lru_wa, lru_ba, lru_wx, lru_bx, lru_lambda, hy_conv_w, hy_conv_b, hy_w1, hy_b1, hy_w2, hy_b2, hy_w3, hy_b3, hy_freq, hy_wout, hy_bias, w_out, ln1_g, ln1_b, ffn_w_up, ffn_conv_w, ffn_conv_b, ffn_w_down, ln2_g, ln2_b):
    p = dict(
        w_in=w_in.astype(BF16), q_gain=q_gain, k_gain=k_gain, lru_conv_w=lru_conv_w, lru_conv_b=lru_conv_b,
        lru_wa=lru_wa, lru_ba=lru_ba, lru_wx=lru_wx, lru_bx=lru_bx, lru_lambda=lru_lambda,
        hy_conv_w=hy_conv_w, hy_conv_b=hy_conv_b, hy_w1=hy_w1, hy_b1=hy_b1, hy_w2=hy_w2, hy_b2=hy_b2,
        hy_w3=hy_w3, hy_b3=hy_b3, hy_freq=hy_freq, hy_wout=hy_wout, hy_bias=hy_bias,
        w_out=w_out.astype(BF16), ln1_g=ln1_g, ln1_b=ln1_b, ffn_w_up=ffn_w_up.astype(BF16),
        ffn_conv_w=ffn_conv_w, ffn_conv_b=ffn_conv_b, ffn_w_down=ffn_w_down.astype(BF16), ln2_g=ln2_g, ln2_b=ln2_b,
    )
    nb = x_prompt.shape[0]
    mod = _ada_mod(jnp.concatenate([c_prompt, c_sample], axis=0), ada_w, ada_b)
    y_prompt = _trunk(x_prompt, mod[:, :nb], p)
    y_sample = _trunk(x_sample, mod[:, nb:], p)
    return (y_prompt, y_sample)
```

```python
import functools
import math

import numpy as np
import jax
import jax.numpy as jnp
from jax import lax
from jax.experimental import pallas as pl
from jax.experimental.pallas import tpu as pltpu

F32 = jnp.float32
BF16 = jnp.bfloat16
HIGHEST = lax.Precision.HIGHEST

GRID_W = 64
HEAD_DIM = 64
N_HEADS = 8
N_KV_HEADS = 2
KV_GROUP = N_HEADS // N_KV_HEADS
ROPE_THETA = 10000.0
ROPE_FREQS = HEAD_DIM // 4
QK_EPS = 1e-6
LRU_WIDTH = 256
LRU_HEADS = 4
LRU_C = 8.0
HY_WIDTH = 256
HY_ORDER = 2
HY_BANDS = 16
HY_MIN_DECAY = abs(math.log(1e-2)) / 1.5
HY_MAX_DECAY = abs(math.log(1e-2)) / 0.3
LN_EPS = 1e-5
IN_Q = N_HEADS * HEAD_DIM
IN_KV = N_KV_HEADS * HEAD_DIM

LANES = 128
SUBLANES = 8
V7X_VMEM_BYTES = 64 * 1024 * 1024
VMEM_LIMIT = V7X_VMEM_BYTES * 3 // 4
DFT_N2 = 128

ROW_TILE = 1024
LRU_TILE = 256
ATTN_Q_TILE = 512
ATTN_K_CHUNK = 512


def _row_tile(L):
    return min(ROW_TILE, L)


def _cparams(sem):
    return pltpu.CompilerParams(dimension_semantics=sem, vmem_limit_bytes=VMEM_LIMIT)


def _dot(a, b):
    return jnp.dot(a, b, preferred_element_type=F32)


def _layer_norm(y, g, b):
    mu = jnp.mean(y, axis=-1, keepdims=True)
    yc = y - mu
    var = jnp.mean(yc * yc, axis=-1, keepdims=True)
    return yc * lax.rsqrt(var + LN_EPS) * g + b


def _ada_kernel(c_ref, w_ref, b_ref, o_ref):
    c = c_ref[...]
    s = c * jax.nn.sigmoid(c)
    o_ref[0] = jnp.dot(s, w_ref[0], precision=HIGHEST, preferred_element_type=F32) + b_ref[0]


def _ada_mod(c_all, ada_w, ada_b):
    depth, d, n = ada_w.shape
    rows = c_all.shape[0]
    tn = 768
    return pl.pallas_call(
        _ada_kernel,
        out_shape=jax.ShapeDtypeStruct((depth, rows, n), F32),
        grid=(depth, n // tn),
        in_specs=[
            pl.BlockSpec((rows, d), lambda l, j: (0, 0)),
            pl.BlockSpec((1, d, tn), lambda l, j: (l, 0, j)),
            pl.BlockSpec((1, 1, tn), lambda l, j: (l, 0, j)),
        ],
        out_specs=pl.BlockSpec((1, rows, tn), lambda l, j: (l, 0, j)),
        compiler_params=_cparams(("parallel", "parallel")),
        name="ada_mod",
    )(c_all, ada_w, ada_b.reshape(depth, 1, n))


def _rope_tables(L):
    rows = L // GRID_W
    row = np.repeat(np.arange(rows, dtype=np.float64), GRID_W)
    col = np.tile(np.arange(GRID_W, dtype=np.float64), rows)
    inv = ROPE_THETA ** (-np.arange(ROPE_FREQS, dtype=np.float64) / ROPE_FREQS)
    ar = row[:, None] * inv
    ac = col[:, None] * inv
    zeros = np.zeros_like(ar)
    cos = np.concatenate([np.cos(ar), np.cos(ar), np.cos(ac), np.cos(ac)], axis=1)
    sin_up = np.concatenate([-np.sin(ar), zeros, -np.sin(ac), zeros], axis=1)
    sin_dn = np.concatenate([zeros, np.sin(ar), zeros, np.sin(ac)], axis=1)
    two = lambda t: jnp.asarray(np.concatenate([t, t], axis=1), F32)
    return two(cos), two(sin_up), two(sin_dn)


def _inproj_kernel(xp_ref, x_ref, xn_ref, sc_ref, sh_ref, w_ref, qg_ref, kg_ref, cos_ref, sup_ref, sdn_ref, bd_ref,
                   hcw_ref, hcb_ref, q_ref, k_ref, vt_ref, xg_ref, hc_ref, *, T, nt):
    i = pl.program_id(1)
    H = SUBLANES
    R = T + 2 * H
    sc = 1.0 + sc_ref[0]
    sh = sh_ref[0]
    u_ext = jnp.concatenate([jnp.where(i > 0, xp_ref[0] * sc + sh, 0.0), x_ref[0] * sc + sh,
                             jnp.where(i < nt - 1, xn_ref[0] * sc + sh, 0.0)], axis=0).astype(BF16)
    proj_ext = _dot(u_ext, w_ref[...])
    proj = proj_ext[H:H + T]
    cos = cos_ref[...]
    sup = sup_ref[...]
    sdn = sdn_ref[...]
    bd = bd_ref[...]
    half = ROPE_FREQS

    def norm_rope(t, gain):
        ms = _dot((t * t).astype(BF16), bd)
        tn = t * lax.rsqrt(ms + QK_EPS) * gain
        return (tn * cos + pltpu.roll(tn, LANES - half, 1) * sup + pltpu.roll(tn, half, 1) * sdn)

    qg = qg_ref[...]
    for j in range(IN_Q // LANES):
        sl = slice(j * LANES, (j + 1) * LANES)
        q_ref[0, :, sl] = (norm_rope(proj[:, sl], qg) * (HEAD_DIM ** -0.5 * math.log2(math.e))).astype(BF16)
    kn = norm_rope(proj[:, IN_Q:IN_Q + IN_KV], kg_ref[...])
    v_t = proj[:, IN_Q + IN_KV:IN_Q + 2 * IN_KV].T
    tk = vt_ref.shape[-1]
    tail = (lax.broadcasted_iota(jnp.int32, (V_ROWS - HEAD_DIM, tk), 0) == 0).astype(BF16)
    for g in range(N_KV_HEADS):
        k_ref[0, g] = kn[:, g * HEAD_DIM:(g + 1) * HEAD_DIM].astype(BF16)
        for c in range(T // tk):
            vt_ref[0, g, c, 0:HEAD_DIM, :] = v_t[g * HEAD_DIM:(g + 1) * HEAD_DIM, c * tk:(c + 1) * tk].astype(BF16)
            vt_ref[0, g, c, HEAD_DIM:, :] = tail
    o = IN_Q + 2 * IN_KV
    xg_ref[0] = proj[:, o:o + 2 * LRU_WIDTH]
    hy = proj_ext[:, o + 2 * LRU_WIDTH:]
    cw = hcw_ref[...]
    hc = hcb_ref[...] + pltpu.roll(hy, 1, 0) * cw[0:1] + hy * cw[1:2] + pltpu.roll(hy, R - 1, 0) * cw[2:3]
    hc_ref[0] = hc[H:H + T]


def _in_proj(x, sc, sh, w_in, q_gain, k_gain, rope, hy_conv_w, hy_conv_b):
    B, L, D = x.shape
    n_in = w_in.shape[1]
    T = _row_tile(L)
    nt = L // T
    hb = T // SUBLANES
    n_halo = L // SUBLANES
    _, tk = _attn_tiles(L)
    cos, sup, sdn = rope
    bd = jnp.asarray(np.kron(np.eye(2), np.full((HEAD_DIM, HEAD_DIM), 1.0 / HEAD_DIM)), BF16)
    qg = jnp.tile(q_gain, 2).reshape(1, LANES)
    kg = jnp.tile(k_gain, 2).reshape(1, LANES)
    n_hy = n_in - IN_Q - 2 * IN_KV - 2 * LRU_WIDTH
    row = lambda b, i: (b, i, 0)
    vec = lambda b, i: (b, 0, 0)
    tab = lambda b, i: (i, 0)
    const = lambda b, i: (0, 0)
    return pl.pallas_call(
        functools.partial(_inproj_kernel, T=T, nt=nt),
        out_shape=(
            jax.ShapeDtypeStruct((B, L, IN_Q), BF16),
            jax.ShapeDtypeStruct((B, N_KV_HEADS, L, HEAD_DIM), BF16),
            jax.ShapeDtypeStruct((B, N_KV_HEADS, L // tk, V_ROWS, tk), BF16),
            jax.ShapeDtypeStruct((B, L, 2 * LRU_WIDTH), F32),
            jax.ShapeDtypeStruct((B, L, n_hy), F32),
        ),
        grid=(B, nt),
        in_specs=[
            pl.BlockSpec((1, SUBLANES, D), lambda b, i: (b, jnp.maximum(i * hb - 1, 0), 0)),
            pl.BlockSpec((1, T, D), row),
            pl.BlockSpec((1, SUBLANES, D), lambda b, i: (b, jnp.minimum((i + 1) * hb, n_halo - 1), 0)),
            pl.BlockSpec((1, 1, D), vec),
            pl.BlockSpec((1, 1, D), vec),
            pl.BlockSpec((D, n_in), const),
            pl.BlockSpec((1, LANES), const),
            pl.BlockSpec((1, LANES), const),
            pl.BlockSpec((T, LANES), tab),
            pl.BlockSpec((T, LANES), tab),
            pl.BlockSpec((T, LANES), tab),
            pl.BlockSpec((LANES, LANES), const),
            pl.BlockSpec(hy_conv_w.shape, const),
            pl.BlockSpec((1, n_hy), const),
        ],
        out_specs=(
            pl.BlockSpec((1, T, IN_Q), row),
            pl.BlockSpec((1, N_KV_HEADS, T, HEAD_DIM), lambda b, i: (b, 0, i, 0)),
            pl.BlockSpec((1, N_KV_HEADS, T // tk, V_ROWS, tk), lambda b, i: (b, 0, i, 0, 0)),
            pl.BlockSpec((1, T, 2 * LRU_WIDTH), row),
            pl.BlockSpec((1, T, n_hy), row),
        ),
        compiler_params=_cparams(("parallel", "parallel")),
        name="in_proj",
    )(x, x, x, sc, sh, w_in, qg, kg, cos, sup, sdn, bd, hy_conv_w, hy_conv_b.reshape(1, n_hy))


V_ROWS = HEAD_DIM + 16


def _attn_kernel(qt_ref, k_ref, vt_ref, o_ref, acc_sc, s_sc, *, tk, nk):
    tq = qt_ref.shape[-1]
    acc_sc[...] = jnp.zeros(acc_sc.shape, F32)

    def scores_h(c, slot, h):
        kc = k_ref[0, 0, pl.ds(pl.multiple_of(c * tk, tk), tk), :]
        s = _dot(kc, qt_ref[0, 0, h])
        s_sc[slot, h] = s
        return jnp.max(s, axis=0, keepdims=True)

    def consume_h(c, slot, h, m_prev, m_chunk):
        vc = vt_ref[0, 0, c]
        m_new = jnp.maximum(m_prev, m_chunk)
        p = jnp.exp2(s_sc[slot, h] - m_new).astype(BF16)
        alpha = jnp.exp2(m_prev - m_new)
        acc_sc[h] = alpha * acc_sc[h] + _dot(vc, p)
        return m_new

    def step(c, c_next, slot, carry):
        ms, mc = carry
        new_m, new_c = [], []
        for h in range(KV_GROUP):
            new_c.append(scores_h(c_next, 1 - slot, h))
            new_m.append(consume_h(c, slot, h, ms[h], mc[h]))
        return tuple(new_m), tuple(new_c)

    def pair(c2, carry):
        c = 2 * c2
        carry = step(c, c + 1, 0, carry)
        return step(c + 1, jnp.minimum(c + 2, nk - 1), 1, carry)

    m0 = tuple(jnp.full((1, tq), -jnp.inf, F32) for _ in range(KV_GROUP))
    c0 = tuple(scores_h(0, 0, h) for h in range(KV_GROUP))
    lax.fori_loop(0, nk // 2, pair, (m0, c0))
    for h in range(KV_GROUP):
        acc = acc_sc[h]
        o_ref[0, 0, h] = (acc[:HEAD_DIM] / acc[HEAD_DIM:HEAD_DIM + 1]).astype(o_ref.dtype)


def _attn_tiles(L):
    return min(ATTN_Q_TILE, L), min(ATTN_K_CHUNK, L // 2)


def _attention(q, kh, vt):
    B, L, _ = q.shape
    tq, tk = _attn_tiles(L)
    nk = L // tk
    assert nk % 2 == 0
    qt = q.reshape(B, L, N_KV_HEADS, KV_GROUP, HEAD_DIM).transpose(0, 2, 3, 4, 1)
    return pl.pallas_call(
        functools.partial(_attn_kernel, tk=tk, nk=nk),
        out_shape=jax.ShapeDtypeStruct((B, N_KV_HEADS, KV_GROUP, HEAD_DIM, L), BF16),
        grid=(B, N_KV_HEADS, L // tq),
        in_specs=[
            pl.BlockSpec((1, 1, KV_GROUP, HEAD_DIM, tq), lambda b, g, i: (b, g, 0, 0, i)),
            pl.BlockSpec((1, 1, L, HEAD_DIM), lambda b, g, i: (b, g, 0, 0)),
            pl.BlockSpec((1, 1, nk, V_ROWS, tk), lambda b, g, i: (b, g, 0, 0, 0)),
        ],
        out_specs=pl.BlockSpec((1, 1, KV_GROUP, HEAD_DIM, tq), lambda b, g, i: (b, g, 0, 0, i)),
        scratch_shapes=[pltpu.VMEM((KV_GROUP, V_ROWS, tq), F32), pltpu.VMEM((2, KV_GROUP, tk, tq), F32)],
        compiler_params=_cparams(("parallel", "parallel", "parallel")),
        name="attention",
    )(qt, kh, vt)


def _lru_kernel(xp_ref, x_ref, xn_ref, cw_ref, cb_ref, wa_ref, ba_ref, wx_ref, bx_ref, lam_ref,
                o_ref, xe_sc, a_sc, b_sc, hs_sc, h_sc, *, T, nt):
    d = pl.program_id(0)
    i = pl.program_id(1)
    tile = jnp.where(d == 0, i, nt - 1 - i)
    nb, _, C = x_ref.shape
    nh = C // LANES
    H = SUBLANES
    keep_prev = (tile > 0).astype(F32)
    keep_next = (tile < nt - 1).astype(F32)
    for b in range(nb):
        for hf in range(nh):
            lanes = slice(hf * LANES, (hf + 1) * LANES)
            xe_sc[hf, pl.ds(b, H, stride=nb), :] = xp_ref[b, :, lanes] * keep_prev
            xe_sc[hf, pl.ds(H * nb + b, T, stride=nb), :] = x_ref[b, :, lanes]
            xe_sc[hf, pl.ds((H + T) * nb + b, H, stride=nb), :] = xn_ref[b, :, lanes] * keep_next
    cw = cw_ref[...]
    halves = []
    for hf in range(nh):
        lanes = slice(hf * LANES, (hf + 1) * LANES)
        acc = cb_ref[:, lanes]
        for k in range(cw.shape[0]):
            acc = acc + xe_sc[hf, (H - 2 + k) * nb:(H - 2 + k + T) * nb, :] * cw[k:k + 1, lanes]
        halves.append(acc)
    xc = jnp.concatenate(halves, axis=1)
    xb = xc.astype(BF16)
    sigmoid = lambda v: 0.5 * jnp.tanh(0.5 * v) + 0.5
    r = sigmoid(_dot(xb, wa_ref[0]) + ba_ref[0])
    ig = sigmoid(_dot(xb, wx_ref[0]) + bx_ref[0])
    lam = lam_ref[0]
    softplus_neg = jnp.maximum(-lam, 0.0) + jnp.log1p(jnp.exp(-jnp.abs(lam)))
    log_a = -LRU_C * r * softplus_neg
    a = jnp.exp(log_a)
    bb = jnp.sqrt(-jnp.tanh(log_a) * (1.0 + a * a)) * (ig * xc)
    for hf in range(nh):
        a_sc[hf] = a[:, hf * LANES:(hf + 1) * LANES]
        b_sc[hf] = bb[:, hf * LANES:(hf + 1) * LANES]

    @pl.when(i == 0)
    def _():
        h_sc[...] = jnp.zeros(h_sc.shape, F32)

    def body(s, hs):
        t = jnp.where(d == 0, s, T - 1 - s)
        rows = pl.ds(pl.multiple_of(t * nb, nb), nb)
        new = []
        for hf in range(nh):
            h = a_sc[hf, rows, :] * hs[hf] + b_sc[hf, rows, :]
            hs_sc[hf, rows, :] = h
            new.append(h)
        return tuple(new)

    hs = lax.fori_loop(0, T, body, tuple(h_sc[hf] for hf in range(nh)), unroll=8)
    for hf in range(nh):
        h_sc[hf] = hs[hf]
        for b in range(nb):
            o_ref[0, b, :, hf * LANES:(hf + 1) * LANES] = hs_sc[hf, pl.ds(b, T, stride=nb), :].astype(o_ref.dtype)


def _rglru(xg, conv_w, conv_b, wa, ba, wx, bx, lam):
    B, L, _ = xg.shape
    C = LRU_WIDTH
    assert B == SUBLANES
    T = min(LRU_TILE, L)
    nt = L // T
    hb = T // SUBLANES
    n_halo = L // SUBLANES

    def tile_of(d, i):
        return jnp.where(d == 0, i, nt - 1 - i)

    def blockdiag(w):
        eye = jnp.eye(LRU_HEADS, dtype=w.dtype)
        return jnp.einsum('dhij,hg->dhigj', w, eye).reshape(2, C, C).astype(BF16)

    kern = functools.partial(_lru_kernel, T=T, nt=nt)
    return pl.pallas_call(
        kern,
        out_shape=jax.ShapeDtypeStruct((2, B, L, C), BF16),
        grid=(2, nt),
        in_specs=[
            pl.BlockSpec((B, SUBLANES, C), lambda d, i: (0, jnp.maximum(tile_of(d, i) * hb - 1, 0), 0)),
            pl.BlockSpec((B, T, C), lambda d, i: (0, tile_of(d, i), 0)),
            pl.BlockSpec((B, SUBLANES, C), lambda d, i: (0, jnp.minimum((tile_of(d, i) + 1) * hb, n_halo - 1), 0)),
            pl.BlockSpec(conv_w.shape, lambda d, i: (0, 0)),
            pl.BlockSpec((1, C), lambda d, i: (0, 0)),
            pl.BlockSpec((1, C, C), lambda d, i: (d, 0, 0)),
            pl.BlockSpec((1, 1, C), lambda d, i: (d, 0, 0)),
            pl.BlockSpec((1, C, C), lambda d, i: (d, 0, 0)),
            pl.BlockSpec((1, 1, C), lambda d, i: (d, 0, 0)),
            pl.BlockSpec((1, 1, C), lambda d, i: (d, 0, 0)),
        ],
        out_specs=pl.BlockSpec((1, B, T, C), lambda d, i: (d, 0, tile_of(d, i), 0)),
        scratch_shapes=[
            pltpu.VMEM((C // LANES, (T + 2 * SUBLANES) * B, LANES), F32),
            pltpu.VMEM((C // LANES, T * B, LANES), F32),
            pltpu.VMEM((C // LANES, T * B, LANES), F32),
            pltpu.VMEM((C // LANES, T * B, LANES), F32),
            pltpu.VMEM((C // LANES, B, LANES), F32),
        ],
        compiler_params=_cparams(("arbitrary", "arbitrary")),
        name="rglru",
    )(xg, xg, xg, conv_w, conv_b.reshape(1, C), blockdiag(wa), ba.reshape(2, 1, C),
      blockdiag(wx), bx.reshape(2, 1, C), lam.reshape(2, 1, C))


def _filter_positions(L):
    t = np.linspace(0.0, 1.0, L)[:, None]
    w = 2.0 * math.pi * np.arange(L, dtype=np.float64)[:, None] / L
    f = np.linspace(1e-4, HY_BANDS - 1, HY_BANDS)[None, :]
    z = np.concatenate([t, np.cos(f * w), -np.sin(f * w)], axis=-1)
    zrev = np.concatenate([z[:1], z[:0:-1]], axis=0)
    return jnp.asarray(np.concatenate([z, zrev], axis=1), F32)


def _filter_kernel(zz_ref, w1_ref, b1_ref, w2_ref, b2_ref, w3_ref, b3_ref, fr_ref, wo_ref, dl_ref,
                   fs_ref, asum_ref, *, T, nemb):
    i = pl.program_id(0)
    fr = fr_ref[...]
    hd = lambda a, b: jnp.dot(a, b, precision=HIGHEST, preferred_element_type=F32)
    zz = zz_ref[...]
    h = jnp.sin(fr * (hd(zz, w1_ref[...]) + b1_ref[...]))
    h = jnp.sin(fr * (hd(h, w2_ref[...]) + b2_ref[...]))
    h = jnp.sin(fr * (hd(h, w3_ref[...]) + b3_ref[...]))
    k = hd(h, wo_ref[...])
    dl = dl_ref[...]
    dec_f = jnp.exp(-zz[:, 0:1] * dl)
    dec_r = jnp.exp(-zz[:, nemb:nemb + 1] * dl)
    C = HY_WIDTH
    first_row = (i * T + lax.broadcasted_iota(jnp.int32, (T, 1), 0)) == 0

    @pl.when(i == 0)
    def _():
        asum_ref[...] = jnp.zeros(asum_ref.shape, F32)

    for o in range(HY_ORDER):
        base = o * 2 * C
        kf = k[:, base:base + C] * dec_f
        kb_here = k[:, base + C:base + 2 * C] * dec_f
        kb_rev = k[:, HY_ORDER * 2 * C + o * C:HY_ORDER * 2 * C + (o + 1) * C] * dec_r
        first = kf + jnp.where(first_row, kb_here, 0.0)
        second = jnp.where(first_row, 0.0, kb_rev)
        fs_ref[o] = first
        fs_ref[HY_ORDER + o] = second
        asum_ref[o:o + 1, :] += jnp.sum(jnp.abs(first) + jnp.abs(second), axis=0, keepdims=True)


def _hyena_filter_time(L, w1, b1, w2, b2, w3, b3, freq, wout):
    zz = _filter_positions(L)
    T = _row_tile(L)
    C = HY_WIDTH
    nemb = zz.shape[1] // 2
    nf = w2.shape[0]
    deltas = jnp.asarray(np.linspace(HY_MIN_DECAY, HY_MAX_DECAY, C)[None, :], F32)
    twice = lambda w: jnp.kron(jnp.eye(2, dtype=w.dtype), w)
    both = lambda v: jnp.tile(v, 2).reshape(1, 2 * nf)
    w_back = wout.reshape(nf, HY_ORDER, 2, C)[:, :, 1, :].reshape(nf, HY_ORDER * C)
    wo = jnp.concatenate([jnp.concatenate([wout, jnp.zeros((nf, HY_ORDER * C), wout.dtype)], axis=1),
                          jnp.concatenate([jnp.zeros_like(wout), w_back], axis=1)], axis=0)
    const = lambda i: (0, 0)
    return pl.pallas_call(
        functools.partial(_filter_kernel, T=T, nemb=nemb),
        out_shape=(jax.ShapeDtypeStruct((2 * HY_ORDER, L, C), F32), jax.ShapeDtypeStruct((HY_ORDER, C), F32)),
        grid=(L // T,),
        in_specs=[
            pl.BlockSpec((T, 2 * nemb), lambda i: (i, 0)),
            pl.BlockSpec((2 * nemb, 2 * nf), const), pl.BlockSpec((1, 2 * nf), const),
            pl.BlockSpec((2 * nf, 2 * nf), const), pl.BlockSpec((1, 2 * nf), const),
            pl.BlockSpec((2 * nf, 2 * nf), const), pl.BlockSpec((1, 2 * nf), const),
            pl.BlockSpec((1, 2 * nf), const),
            pl.BlockSpec(wo.shape, const),
            pl.BlockSpec((1, C), const),
        ],
        out_specs=(pl.BlockSpec((2 * HY_ORDER, T, C), lambda i: (0, i, 0)),
                   pl.BlockSpec((HY_ORDER, C), const)),
        compiler_params=_cparams(("arbitrary",)),
        name="hy_filter_mlp",
    )(zz, twice(w1), both(b1), twice(w2), both(b2), twice(w3), both(b3), both(freq), wo, deltas)


def _dft_tables(L):
    N = 2 * L
    N2 = DFT_N2
    N1 = N // N2
    K1 = N1 // 2
    kept = K1 + 1
    rows = -(-kept // SUBLANES) * SUBLANES
    live = (np.arange(rows) < kept).astype(np.float64)[None, :, None]
    pair = np.where((np.arange(rows) == 0) | (np.arange(rows) == K1), 1.0, 2.0)[None, :, None] * live
    n1 = np.arange(K1)[None, None, :]
    k1 = np.arange(rows)[None, :, None]
    n2 = np.arange(N2)[:, None, None]
    ang = 2.0 * np.pi * (((n1 * N2 + n2) * k1) % N) / N
    g_fwd = np.concatenate([np.cos(ang) * live, -np.sin(ang) * live], axis=1)
    t = lambda a: np.transpose(a, (0, 2, 1))
    g_inv = np.concatenate([t(np.cos(ang) * pair), t(-np.sin(ang) * pair)], axis=2) / N
    a2 = 2.0 * np.pi * ((np.arange(N2)[:, None] * np.arange(N2)[None, :]) % N2) / N2
    cr, ci = np.cos(a2), -np.sin(a2)
    m2 = np.block([[cr, -ci], [ci, cr]])
    m2i = np.block([[cr, ci], [-ci, cr]])
    as_bf = lambda a: jnp.asarray(a, BF16)
    return as_bf(g_fwd), as_bf(g_inv), as_bf(m2), as_bf(m2i)


DFT_NB = 16
N_HALF = HY_WIDTH // LANES


def _lane_half_specs(k1n, nb, part):
    return [pl.BlockSpec((1, k1n, nb, LANES), lambda s, j, h=h: (s, 0, j, part * N_HALF + h)) for h in range(N_HALF)]


def _major_half_specs(k1n, nb):
    return [pl.BlockSpec((1, 1, k1n, nb, LANES), lambda s, j, h=h: (s, h, 0, j, 0)) for h in range(N_HALF)]


def _ld_time(ref, j, nb):
    k1n = ref.shape[-3]
    return ref.reshape(k1n * nb, LANES)[pl.ds(j, k1n, stride=nb), :]


def _st_time(ref, h, j, nb, val):
    k1n = ref.shape[2]
    ref.reshape(N_HALF * k1n * nb, LANES)[pl.ds(h * k1n * nb + j, k1n, stride=nb), :] = val


U32 = jnp.uint32


def _pack_halves(x):
    lo = lax.bitcast_convert_type(x[:, :LANES], U32)
    hi = lax.bitcast_convert_type(x[:, LANES:], U32)
    rnd = jnp.uint32(0x8000)
    return ((lo + rnd) >> 16) | ((hi + rnd) & jnp.uint32(0xFFFF0000))


def _unpack_halves(w):
    lo = lax.bitcast_convert_type(w << 16, F32)
    hi = lax.bitcast_convert_type(w & jnp.uint32(0xFFFF0000), F32)
    return jnp.concatenate([lo, hi], axis=1)


def _dft_a_kernel(x0_ref, x1_ref, g_ref, o_ref, *, nb):
    n1 = o_ref.shape[3]
    for j in range(nb):
        parts = []
        for x_ref in (x0_ref, x1_ref):
            parts.append(_ld_time(x_ref, j, nb))
        xj = jnp.concatenate(parts, axis=1).astype(BF16)
        r = _pack_halves(_dot(g_ref[j], xj))
        o_ref[0, 0, j] = r[:n1]
        o_ref[0, 1, j] = r[n1:]


def _dft_stage_a(x, x_specs, g_fwd):
    S = x.shape[0]
    N2, two_n1, K1 = g_fwd.shape
    N1 = two_n1 // 2
    nb = DFT_NB
    return pl.pallas_call(
        functools.partial(_dft_a_kernel, nb=nb),
        out_shape=jax.ShapeDtypeStruct((S, 2, N2, N1, LANES), U32),
        grid=(S, N2 // nb),
        in_specs=x_specs + [pl.BlockSpec((nb, 2 * N1, K1), lambda s, j: (j, 0, 0))],
        out_specs=pl.BlockSpec((1, 2, nb, N1, LANES), lambda s, j: (s, 0, j, 0, 0)),
        compiler_params=_cparams(("parallel", "parallel")),
        name="hy_dft_a",
    )(x, x, g_fwd)


def _k1_rows(ref, j):
    _, _, n2, tk1, _ = ref.shape
    return ref.reshape(2 * n2 * tk1, LANES), pl.ds(j, 2 * n2, stride=tk1)


def _ld_k1(ref, j):
    r2, rows = _k1_rows(ref, j)
    return r2[rows, :]


def _st_k1(ref, j, val):
    r2, rows = _k1_rows(ref, j)
    r2[rows, :] = val


def _dft_b_filter_kernel(s1_ref, s2_ref, m2_ref, asum_ref, o_ref, *, tk1):
    n2 = s1_ref.shape[2]
    C = o_ref.shape[4]
    scale = 1.0 / asum_ref[0]
    for j in range(tk1):
        sign = 1.0 if j % 2 == 0 else -1.0
        s = _unpack_halves(_ld_k1(s1_ref, j)) + sign * _unpack_halves(_ld_k1(s2_ref, j))
        xk = _dot(m2_ref[...], s.astype(BF16)) * scale
        o_ref[0, j] = xk.reshape(2, n2, C)


def _dft_stage_b_filter(sa, m2, asum):
    _, _, N2, N1, _ = sa.shape
    C = HY_WIDTH
    tk1 = SUBLANES
    blk = (1, 2, N2, tk1, LANES)
    return pl.pallas_call(
        functools.partial(_dft_b_filter_kernel, tk1=tk1),
        out_shape=jax.ShapeDtypeStruct((HY_ORDER, N1, 2, N2, C), F32),
        grid=(HY_ORDER, N1 // tk1),
        in_specs=[
            pl.BlockSpec(blk, lambda o, i: (o, 0, 0, i, 0)),
            pl.BlockSpec(blk, lambda o, i: (HY_ORDER + o, 0, 0, i, 0)),
            pl.BlockSpec((2 * N2, 2 * N2), lambda o, i: (0, 0)),
            pl.BlockSpec((1, 1, C), lambda o, i: (o, 0, 0)),
        ],
        out_specs=pl.BlockSpec((1, tk1, 2, N2, C), lambda o, i: (o, i, 0, 0, 0)),
        compiler_params=_cparams(("parallel", "parallel")),
        name="hy_dft_b_filter",
    )(sa, sa, m2, asum.reshape(HY_ORDER, 1, C))


def _dft_b_conv_kernel(s_ref, kf_ref, m2_ref, m2i_ref, o_ref, *, tk1):
    n2 = s_ref.shape[2]
    for j in range(tk1):
        x = _dot(m2_ref[...], _unpack_halves(_ld_k1(s_ref, j)).astype(BF16))
        xr, xi = x[:n2], x[n2:]
        kr = kf_ref[0, j, 0]
        ki = kf_ref[0, j, 1]
        y = jnp.concatenate([xr * kr - xi * ki, xr * ki + xi * kr], axis=0).astype(BF16)
        _st_k1(o_ref, j, _pack_halves(_dot(m2i_ref[...], y)))


def _dft_stage_b_conv(sa, kspec, order, m2, m2i):
    S, _, N2, N1, _ = sa.shape
    C = HY_WIDTH
    tk1 = SUBLANES
    blk = (1, 2, N2, tk1, LANES)
    return pl.pallas_call(
        functools.partial(_dft_b_conv_kernel, tk1=tk1),
        out_shape=jax.ShapeDtypeStruct(sa.shape, U32),
        grid=(N1 // tk1, S),
        in_specs=[
            pl.BlockSpec(blk, lambda i, s: (s, 0, 0, i, 0)),
            pl.BlockSpec((1, tk1, 2, N2, C), lambda i, s: (order, i, 0, 0, 0)),
            pl.BlockSpec((2 * N2, 2 * N2), lambda i, s: (0, 0)),
            pl.BlockSpec((2 * N2, 2 * N2), lambda i, s: (0, 0)),
        ],
        out_specs=pl.BlockSpec(blk, lambda i, s: (s, 0, 0, i, 0)),
        compiler_params=_cparams(("parallel", "parallel")),
        name="hy_dft_b_conv",
    )(sa, kspec, m2, m2i)


def _dft_c_kernel(c_ref, gi_ref, z0_ref, z1_ref, g0_ref, g1_ref, d_ref, o_ref, *, nb):
    d = d_ref[0]
    for j in range(nb):
        cat = _unpack_halves(jnp.concatenate([c_ref[0, 0, j], c_ref[0, 1, j]], axis=0))
        y = _dot(gi_ref[j], cat.astype(BF16))
        for h, (z_ref, g_ref) in enumerate(((z0_ref, g0_ref), (z1_ref, g1_ref))):
            lanes = slice(h * LANES, (h + 1) * LANES)
            _st_time(o_ref, h, j, nb, _ld_time(g_ref, j, nb) * (y[:, lanes] + _ld_time(z_ref, j, nb) * d[:, lanes]))


def _dft_stage_c(sc, g_inv, z, z_specs, gate, gate_specs, d):
    S, _, N2, N1, _ = sc.shape
    C = HY_WIDTH
    K1 = g_inv.shape[1]
    nb = DFT_NB
    return pl.pallas_call(
        functools.partial(_dft_c_kernel, nb=nb),
        out_shape=jax.ShapeDtypeStruct((S, N_HALF, K1, N2, LANES), F32),
        grid=(S, N2 // nb),
        in_specs=[
            pl.BlockSpec((1, 2, nb, N1, LANES), lambda s, j: (s, 0, j, 0, 0)),
            pl.BlockSpec((nb, K1, 2 * N1), lambda s, j: (j, 0, 0)),
        ] + z_specs + gate_specs + [pl.BlockSpec((1, 1, C), lambda s, j: (0, 0, 0))],
        out_specs=pl.BlockSpec((1, N_HALF, K1, nb, LANES), lambda s, j: (s, 0, 0, j, 0)),
        compiler_params=_cparams(("parallel", "parallel")),
        name="hy_dft_c",
    )(sc, g_inv, z, z, gate, gate, d.reshape(1, 1, C))


def _hyena(hc, w1, b1, w2, b2, w3, b3, freq, wout, bias, tables):
    B, L, _ = hc.shape
    C = HY_WIDTH
    g_fwd, g_inv, m2, m2i = tables
    N2 = DFT_N2
    K1 = L // N2
    nb = DFT_NB
    fs, asum = _hyena_filter_time(L, w1, b1, w2, b2, w3, b3, freq, wout)
    fa = _dft_stage_a(fs.reshape(2 * HY_ORDER, K1, N2, C), _lane_half_specs(K1, nb, 0), g_fwd)
    kspec = _dft_stage_b_filter(fa, m2, asum)
    hc4 = hc.reshape(B, K1, N2, (HY_ORDER + 1) * C)
    z, z_specs = hc4, _lane_half_specs(K1, nb, 0)
    for o in range(HY_ORDER):
        sa = _dft_stage_a(z, z_specs, g_fwd)
        sc = _dft_stage_b_conv(sa, kspec, o, m2, m2i)
        z = _dft_stage_c(sc, g_inv, z, z_specs, hc4, _lane_half_specs(K1, nb, 1 + o), bias[o])
        z_specs = _major_half_specs(K1, nb)
    return z.reshape(B, N_HALF, L, LANES)


def _outproj_kernel(x_ref, at_ref, h_ref, gr_ref, zy_ref, w_ref, gt_ref, g_ref, b_ref, o_ref, *, alpha):
    lru = (h_ref[0, 0].astype(F32) + h_ref[1, 0].astype(F32)) * jax.nn.gelu(gr_ref[0])
    na = IN_Q
    at = at_ref[0].reshape(na, at_ref.shape[-1])
    m = lax.dot_general(at, w_ref[0:na], (((0,), (0,)), ((), ())), preferred_element_type=F32)
    rest = jnp.concatenate([lru.astype(BF16)] + [zy_ref[0, h].astype(BF16) for h in range(N_HALF)], axis=1)
    m = m + _dot(rest, w_ref[na:])
    o_ref[0] = _layer_norm(alpha * x_ref[0] + gt_ref[0] * m, g_ref[...], b_ref[...])


def _out_proj(x, attn, h, xg, zy, w_out, gt, ln_g, ln_b, alpha):
    B, L, D = x.shape
    T = _row_tile(L)
    C = LRU_WIDTH
    row = lambda b, i: (b, i, 0)
    vec = lambda b, i: (b, 0, 0)
    const = lambda b, i: (0, 0)
    return pl.pallas_call(
        functools.partial(_outproj_kernel, alpha=alpha),
        out_shape=jax.ShapeDtypeStruct((B, L, D), F32),
        grid=(B, L // T),
        in_specs=[
            pl.BlockSpec((1, T, D), row),
            pl.BlockSpec((1, N_KV_HEADS, KV_GROUP, HEAD_DIM, T), lambda b, i: (b, 0, 0, 0, i)),
            pl.BlockSpec((2, 1, T, C), lambda b, i: (0, b, i, 0)),
            pl.BlockSpec((1, T, C), lambda b, i: (b, i, 1)),
            pl.BlockSpec((1, N_HALF, T, LANES), lambda b, i: (b, 0, i, 0)),
            pl.BlockSpec(w_out.shape, const),
            pl.BlockSpec((1, 1, D), vec),
            pl.BlockSpec((1, D), const),
            pl.BlockSpec((1, D), const),
        ],
        out_specs=pl.BlockSpec((1, T, D), row),
        compiler_params=_cparams(("parallel", "parallel")),
        name="out_proj",
    )(x, attn, h, xg, zy, w_out, gt, ln_g.reshape(1, D), ln_b.reshape(1, D))


FFN_CHUNK = 256


def _ffn_kernel(xp_ref, x_ref, xn_ref, sc_ref, sh_ref, gt_ref, wu_ref, cw_ref, cb_ref, wd_ref, g_ref, b_ref,
                o_ref, u_sc, h_sc, acc_sc, *, T, nt, nf, alpha):
    i = pl.program_id(1)
    H = SUBLANES
    R = T + 2 * H
    F = FFN_CHUNK
    sc = 1.0 + sc_ref[0]
    sh = sh_ref[0]
    u_sc[H:H + T, :] = (x_ref[0] * sc + sh).astype(BF16)
    u_sc[0:H, :] = jnp.where(i > 0, xp_ref[0] * sc + sh, 0.0).astype(BF16)
    u_sc[H + T:R, :] = jnp.where(i < nt - 1, xn_ref[0] * sc + sh, 0.0).astype(BF16)
    acc_sc[...] = jnp.zeros(acc_sc.shape, F32)

    def up(f, slot):
        h_sc[slot] = _dot(u_sc[...], wu_ref[f])

    def down(f, slot):
        h = h_sc[slot]
        cw = cw_ref[f]
        y = cb_ref[f] + pltpu.roll(h, 1, 0) * cw[0:1] + h * cw[1:2] + pltpu.roll(h, R - 1, 0) * cw[2:3]
        y = y[H:H + T]
        act = (jax.nn.gelu(y[:, :F]) * y[:, F:]).astype(BF16)
        acc_sc[...] += _dot(act, wd_ref[f])

    def pair(c2, carry):
        f = 2 * c2
        up(f + 1, 1)
        down(f, 0)
        up(jnp.minimum(f + 2, nf - 1), 0)
        down(f + 1, 1)
        return carry

    up(0, 0)
    lax.fori_loop(0, nf // 2, pair, 0)
    if nf % 2 == 1:
        down(nf - 1, 0)
    o_ref[0] = _layer_norm(alpha * x_ref[0] + gt_ref[0] * acc_sc[...], g_ref[...], b_ref[...])


def _conv_ffn(x, sc, sh, gt, w_up, conv_w, conv_b, w_down, ln_g, ln_b, alpha):
    B, L, D = x.shape
    d_ff = w_down.shape[0]
    T = _row_tile(L)
    F = FFN_CHUNK
    nt = L // T
    nf = d_ff // F
    hb = T // SUBLANES
    n_halo = L // SUBLANES
    nk = conv_w.shape[0]
    chunked = lambda w: jnp.concatenate([w[..., :d_ff].reshape(w.shape[0], nf, F),
                                         w[..., d_ff:].reshape(w.shape[0], nf, F)], axis=-1).transpose(1, 0, 2)
    wu = chunked(w_up)
    cw = chunked(conv_w)
    cb = chunked(conv_b.reshape(1, 2 * d_ff))
    wd = w_down.reshape(nf, F, D)
    row = lambda b, i: (b, i, 0)
    vec = lambda b, i: (b, 0, 0)
    const2 = lambda b, i: (0, 0)
    const3 = lambda b, i: (0, 0, 0)
    resident = lambda shape, imap: pl.BlockSpec(shape, imap, pipeline_mode=pl.Buffered(1))
    return pl.pallas_call(
        functools.partial(_ffn_kernel, T=T, nt=nt, nf=nf, alpha=alpha),
        out_shape=jax.ShapeDtypeStruct((B, L, D), F32),
        grid=(B, nt),
        in_specs=[
            pl.BlockSpec((1, SUBLANES, D), lambda b, i: (b, jnp.maximum(i * hb - 1, 0), 0)),
            pl.BlockSpec((1, T, D), row),
            pl.BlockSpec((1, SUBLANES, D), lambda b, i: (b, jnp.minimum((i + 1) * hb, n_halo - 1), 0)),
            pl.BlockSpec((1, 1, D), vec),
            pl.BlockSpec((1, 1, D), vec),
            pl.BlockSpec((1, 1, D), vec),
            resident((nf, D, 2 * F), const3),
            resident((nf, nk, 2 * F), const3),
            resident((nf, 1, 2 * F), const3),
            resident((nf, F, D), const3),
            pl.BlockSpec((1, D), const2),
            pl.BlockSpec((1, D), const2),
        ],
        out_specs=pl.BlockSpec((1, T, D), row),
        scratch_shapes=[
            pltpu.VMEM((T + 2 * SUBLANES, D), BF16),
            pltpu.VMEM((2, T + 2 * SUBLANES, 2 * F), F32),
            pltpu.VMEM((T, D), F32),
        ],
        compiler_params=_cparams(("parallel", "parallel")),
        name="conv_ffn",
    )(x, x, x, sc, sh, gt, wu, cw, cb, wd, ln_g.reshape(1, D), ln_b.reshape(1, D))


def _trunk(x, mod, p):
    B, L, D = x.shape
    depth = mod.shape[0]
    alpha = (2 * depth) ** 0.25
    rope = _rope_tables(L)
    tables = _dft_tables(L)
    for l in range(depth):
        m6 = mod[l].reshape(B, 6, 1, D)
        sh1, sc1, gt1, sh2, sc2, gt2 = (m6[:, j] for j in range(6))
        q, k, v, xg, hc = _in_proj(x, sc1, sh1, p['w_in'][l], p['q_gain'][l], p['k_gain'][l], rope,
                                   p['hy_conv_w'][l], p['hy_conv_b'][l])
        attn = _attention(q, k, v)
        h = _rglru(xg, p['lru_conv_w'][l], p['lru_conv_b'][l], p['lru_wa'][l], p['lru_ba'][l],
                   p['lru_wx'][l], p['lru_bx'][l], p['lru_lambda'][l])
        zy = _hyena(hc, p['hy_w1'][l], p['hy_b1'][l], p['hy_w2'][l], p['hy_b2'][l], p['hy_w3'][l], p['hy_b3'][l],
                    p['hy_freq'][l], p['hy_wout'][l], p['hy_bias'][l], tables)
        x = _out_proj(x, attn, h, xg, zy, p['w_out'][l], gt1, p['ln1_g'][l], p['ln1_b'][l], alpha)
        x = _conv_ffn(x, sc2, sh2, gt2, p['ffn_w_up'][l], p['ffn_conv_w'][l], p['ffn_conv_b'][l],
                      p['ffn_w_down'][l], p['ln2_g'][l], p['ln2_b'][l], alpha)
    return x


def kernel(x_prompt, x_sample, c_prompt, c_sample, ada_w, ada_b, w_in, q_gain, k_gain, lru_conv_w, lru_conv_b, lru_wa, lru_ba, lru_wx, lru_bx, lru_lambda, hy_conv_w, hy_conv_b, hy_w1, hy_b1, hy_w2, hy_b2, hy_w3, hy_b3, hy_freq, hy_wout, hy_bias, w_out, ln1_g, ln1_b, ffn_w_up, ffn_conv_w, ffn_conv_b, ffn_w_down, ln2_g, ln2_b):
    p = dict(
        w_in=w_in.astype(BF16), q_gain=q_gain, k_gain=k_gain, lru_conv_w=lru_conv_w, lru_conv_b=lru_conv_b,
        lru_wa=lru_wa, lru_ba=lru_ba, lru_wx=lru_wx, lru_bx=lru_bx, lru_lambda=lru_lambda,
        hy_conv_w=hy_conv_w, hy_conv_b=hy_conv_b, hy_w1=hy_w1, hy_b1=hy_b1, hy_w2=hy_w2, hy_b2=hy_b2,
        hy_w3=hy_w3, hy_b3=hy_b3, hy_freq=hy_freq, hy_wout=hy_wout, hy_bias=hy_bias,
        w_out=w_out.astype(BF16), ln1_g=ln1_g, ln1_b=ln1_b, ffn_w_up=ffn_w_up.astype(BF16),
        ffn_conv_w=ffn_conv_w, ffn_conv_b=ffn_conv_b, ffn_w_down=ffn_w_down.astype(BF16), ln2_g=ln2_g, ln2_b=ln2_b,
    )
    nb = x_prompt.shape[0]
    mod = _ada_mod(jnp.concatenate([c_prompt, c_sample], axis=0), ada_w, ada_b)
    y_prompt = _trunk(x_prompt, mod[:, :nb], p)
    y_sample = _trunk(x_sample, mod[:, nb:], p)
    return (y_prompt, y_sample)
```

```python
import functools
import math

import numpy as np
import jax
import jax.numpy as jnp
from jax import lax
from jax.experimental import pallas as pl
from jax.experimental.pallas import tpu as pltpu

F32 = jnp.float32
BF16 = jnp.bfloat16
HIGHEST = lax.Precision.HIGHEST

GRID_W = 64
HEAD_DIM = 64
N_HEADS = 8
N_KV_HEADS = 2
KV_GROUP = N_HEADS // N_KV_HEADS
ROPE_THETA = 10000.0
ROPE_FREQS = HEAD_DIM // 4
QK_EPS = 1e-6
LRU_WIDTH = 256
LRU_HEADS = 4
LRU_C = 8.0
HY_WIDTH = 256
HY_ORDER = 2
HY_BANDS = 16
HY_MIN_DECAY = abs(math.log(1e-2)) / 1.5
HY_MAX_DECAY = abs(math.log(1e-2)) / 0.3
LN_EPS = 1e-5
IN_Q = N_HEADS * HEAD_DIM
IN_KV = N_KV_HEADS * HEAD_DIM

LANES = 128
SUBLANES = 8
V7X_VMEM_BYTES = 64 * 1024 * 1024
VMEM_LIMIT = V7X_VMEM_BYTES * 3 // 4
DFT_N2 = 128

ROW_TILE = 1024
LRU_TILE = 256
ATTN_Q_TILE = 512
ATTN_K_CHUNK = 1024


def _row_tile(L):
    return min(ROW_TILE, L)


def _cparams(sem):
    return pltpu.CompilerParams(dimension_semantics=sem, vmem_limit_bytes=VMEM_LIMIT)


def _dot(a, b):
    return jnp.dot(a, b, preferred_element_type=F32)


def _layer_norm(y, g, b):
    mu = jnp.mean(y, axis=-1, keepdims=True)
    yc = y - mu
    var = jnp.mean(yc * yc, axis=-1, keepdims=True)
    return yc * lax.rsqrt(var + LN_EPS) * g + b


def _ada_kernel(c_ref, w_ref, b_ref, o_ref):
    c = c_ref[...]
    s = c * jax.nn.sigmoid(c)
    o_ref[0] = jnp.dot(s, w_ref[0], precision=HIGHEST, preferred_element_type=F32) + b_ref[0]


def _ada_mod(c_all, ada_w, ada_b):
    depth, d, n = ada_w.shape
    rows = c_all.shape[0]
    tn = 768
    return pl.pallas_call(
        _ada_kernel,
        out_shape=jax.ShapeDtypeStruct((depth, rows, n), F32),
        grid=(depth, n // tn),
        in_specs=[
            pl.BlockSpec((rows, d), lambda l, j: (0, 0)),
            pl.BlockSpec((1, d, tn), lambda l, j: (l, 0, j)),
            pl.BlockSpec((1, 1, tn), lambda l, j: (l, 0, j)),
        ],
        out_specs=pl.BlockSpec((1, rows, tn), lambda l, j: (l, 0, j)),
        compiler_params=_cparams(("parallel", "parallel")),
        name="ada_mod",
    )(c_all, ada_w, ada_b.reshape(depth, 1, n))


def _rope_tables(L):
    rows = L // GRID_W
    row = np.repeat(np.arange(rows, dtype=np.float64), GRID_W)
    col = np.tile(np.arange(GRID_W, dtype=np.float64), rows)
    inv = ROPE_THETA ** (-np.arange(ROPE_FREQS, dtype=np.float64) / ROPE_FREQS)
    ar = row[:, None] * inv
    ac = col[:, None] * inv
    zeros = np.zeros_like(ar)
    cos = np.concatenate([np.cos(ar), np.cos(ar), np.cos(ac), np.cos(ac)], axis=1)
    sin_up = np.concatenate([-np.sin(ar), zeros, -np.sin(ac), zeros], axis=1)
    sin_dn = np.concatenate([zeros, np.sin(ar), zeros, np.sin(ac)], axis=1)
    two = lambda t: jnp.asarray(np.concatenate([t, t], axis=1), F32)
    return two(cos), two(sin_up), two(sin_dn)


def _inproj_kernel(xp_ref, x_ref, xn_ref, sc_ref, sh_ref, w_ref, qg_ref, kg_ref, cos_ref, sup_ref, sdn_ref, bd_ref,
                   hcw_ref, hcb_ref, q_ref, k_ref, vt_ref, xg_ref, hc_ref, *, T, nt):
    i = pl.program_id(1)
    H = SUBLANES
    R = T + 2 * H
    sc = 1.0 + sc_ref[0]
    sh = sh_ref[0]
    u_ext = jnp.concatenate([jnp.where(i > 0, xp_ref[0] * sc + sh, 0.0), x_ref[0] * sc + sh,
                             jnp.where(i < nt - 1, xn_ref[0] * sc + sh, 0.0)], axis=0).astype(BF16)
    proj_ext = _dot(u_ext, w_ref[...])
    proj = proj_ext[H:H + T]
    cos = cos_ref[...]
    sup = sup_ref[...]
    sdn = sdn_ref[...]
    bd = bd_ref[...]
    half = ROPE_FREQS

    def norm_rope(t, gain):
        ms = _dot((t * t).astype(BF16), bd)
        tn = t * lax.rsqrt(ms + QK_EPS) * gain
        return (tn * cos + pltpu.roll(tn, LANES - half, 1) * sup + pltpu.roll(tn, half, 1) * sdn)

    qg = qg_ref[...]
    for j in range(IN_Q // LANES):
        sl = slice(j * LANES, (j + 1) * LANES)
        q_ref[0, :, sl] = (norm_rope(proj[:, sl], qg) * (HEAD_DIM ** -0.5 * math.log2(math.e))).astype(BF16)
    kn = norm_rope(proj[:, IN_Q:IN_Q + IN_KV], kg_ref[...])
    v_t = proj[:, IN_Q + IN_KV:IN_Q + 2 * IN_KV].T
    tk = vt_ref.shape[-1]
    tail = (lax.broadcasted_iota(jnp.int32, (V_ROWS - HEAD_DIM, tk), 0) == 0).astype(BF16)
    for g in range(N_KV_HEADS):
        k_ref[0, g] = kn[:, g * HEAD_DIM:(g + 1) * HEAD_DIM].astype(BF16)
        for c in range(T // tk):
            vt_ref[0, g, c, 0:HEAD_DIM, :] = v_t[g * HEAD_DIM:(g + 1) * HEAD_DIM, c * tk:(c + 1) * tk].astype(BF16)
            vt_ref[0, g, c, HEAD_DIM:, :] = tail
    o = IN_Q + 2 * IN_KV
    xg_ref[0] = proj[:, o:o + 2 * LRU_WIDTH]
    hy = proj_ext[:, o + 2 * LRU_WIDTH:]
    cw = hcw_ref[...]
    hc = hcb_ref[...] + pltpu.roll(hy, 1, 0) * cw[0:1] + hy * cw[1:2] + pltpu.roll(hy, R - 1, 0) * cw[2:3]
    hc_ref[0] = hc[H:H + T]


def _in_proj(x, sc, sh, w_in, q_gain, k_gain, rope, hy_conv_w, hy_conv_b):
    B, L, D = x.shape
    n_in = w_in.shape[1]
    T = _row_tile(L)
    nt = L // T
    hb = T // SUBLANES
    n_halo = L // SUBLANES
    _, tk = _attn_tiles(L)
    cos, sup, sdn = rope
    bd = jnp.asarray(np.kron(np.eye(2), np.full((HEAD_DIM, HEAD_DIM), 1.0 / HEAD_DIM)), BF16)
    qg = jnp.tile(q_gain, 2).reshape(1, LANES)
    kg = jnp.tile(k_gain, 2).reshape(1, LANES)
    n_hy = n_in - IN_Q - 2 * IN_KV - 2 * LRU_WIDTH
    row = lambda b, i: (b, i, 0)
    vec = lambda b, i: (b, 0, 0)
    tab = lambda b, i: (i, 0)
    const = lambda b, i: (0, 0)
    return pl.pallas_call(
        functools.partial(_inproj_kernel, T=T, nt=nt),
        out_shape=(
            jax.ShapeDtypeStruct((B, L, IN_Q), BF16),
            jax.ShapeDtypeStruct((B, N_KV_HEADS, L, HEAD_DIM), BF16),
            jax.ShapeDtypeStruct((B, N_KV_HEADS, L // tk, V_ROWS, tk), BF16),
            jax.ShapeDtypeStruct((B, L, 2 * LRU_WIDTH), F32),
            jax.ShapeDtypeStruct((B, L, n_hy), F32),
        ),
        grid=(B, nt),
        in_specs=[
            pl.BlockSpec((1, SUBLANES, D), lambda b, i: (b, jnp.maximum(i * hb - 1, 0), 0)),
            pl.BlockSpec((1, T, D), row),
            pl.BlockSpec((1, SUBLANES, D), lambda b, i: (b, jnp.minimum((i + 1) * hb, n_halo - 1), 0)),
            pl.BlockSpec((1, 1, D), vec),
            pl.BlockSpec((1, 1, D), vec),
            pl.BlockSpec((D, n_in), const),
            pl.BlockSpec((1, LANES), const),
            pl.BlockSpec((1, LANES), const),
            pl.BlockSpec((T, LANES), tab),
            pl.BlockSpec((T, LANES), tab),
            pl.BlockSpec((T, LANES), tab),
            pl.BlockSpec((LANES, LANES), const),
            pl.BlockSpec(hy_conv_w.shape, const),
            pl.BlockSpec((1, n_hy), const),
        ],
        out_specs=(
            pl.BlockSpec((1, T, IN_Q), row),
            pl.BlockSpec((1, N_KV_HEADS, T, HEAD_DIM), lambda b, i: (b, 0, i, 0)),
            pl.BlockSpec((1, N_KV_HEADS, T // tk, V_ROWS, tk), lambda b, i: (b, 0, i, 0, 0)),
            pl.BlockSpec((1, T, 2 * LRU_WIDTH), row),
            pl.BlockSpec((1, T, n_hy), row),
        ),
        compiler_params=_cparams(("parallel", "parallel")),
        name="in_proj",
    )(x, x, x, sc, sh, w_in, qg, kg, cos, sup, sdn, bd, hy_conv_w, hy_conv_b.reshape(1, n_hy))


V_ROWS = HEAD_DIM + 16


def _attn_kernel(qt_ref, k_ref, vt_ref, o_ref, acc_sc, s_sc, *, tk, nk):
    tq = qt_ref.shape[-1]
    acc_sc[...] = jnp.zeros(acc_sc.shape, F32)

    def scores_h(c, slot, h):
        kc = k_ref[0, 0, pl.ds(pl.multiple_of(c * tk, tk), tk), :]
        s = _dot(kc, qt_ref[0, 0, h])
        s_sc[slot, h] = s
        return jnp.max(s, axis=0, keepdims=True)

    def consume_h(c, slot, h, m_prev, m_chunk):
        vc = vt_ref[0, 0, c]
        m_new = jnp.maximum(m_prev, m_chunk)
        p = jnp.exp2(s_sc[slot, h] - m_new).astype(BF16)
        alpha = jnp.exp2(m_prev - m_new)
        acc_sc[h] = alpha * acc_sc[h] + _dot(vc, p)
        return m_new

    def step(c, c_next, slot, carry):
        ms, mc = carry
        new_m, new_c = [], []
        for h in range(KV_GROUP):
            new_c.append(scores_h(c_next, 1 - slot, h))
            new_m.append(consume_h(c, slot, h, ms[h], mc[h]))
        return tuple(new_m), tuple(new_c)

    def pair(c2, carry):
        c = 2 * c2
        carry = step(c, c + 1, 0, carry)
        return step(c + 1, jnp.minimum(c + 2, nk - 1), 1, carry)

    m0 = tuple(jnp.full((1, tq), -jnp.inf, F32) for _ in range(KV_GROUP))
    c0 = tuple(scores_h(0, 0, h) for h in range(KV_GROUP))
    lax.fori_loop(0, nk // 2, pair, (m0, c0))
    for h in range(KV_GROUP):
        acc = acc_sc[h]
        o_ref[0, 0, h] = (acc[:HEAD_DIM] / acc[HEAD_DIM:HEAD_DIM + 1]).astype(o_ref.dtype)


def _attn_tiles(L):
    return min(ATTN_Q_TILE, L), min(ATTN_K_CHUNK, L // 2)


def _attention(q, kh, vt):
    B, L, _ = q.shape
    tq, tk = _attn_tiles(L)
    nk = L // tk
    assert nk % 2 == 0
    qt = q.reshape(B, L, N_KV_HEADS, KV_GROUP, HEAD_DIM).transpose(0, 2, 3, 4, 1)
    return pl.pallas_call(
        functools.partial(_attn_kernel, tk=tk, nk=nk),
        out_shape=jax.ShapeDtypeStruct((B, N_KV_HEADS, KV_GROUP, HEAD_DIM, L), BF16),
        grid=(B, N_KV_HEADS, L // tq),
        in_specs=[
            pl.BlockSpec((1, 1, KV_GROUP, HEAD_DIM, tq), lambda b, g, i: (b, g, 0, 0, i)),
            pl.BlockSpec((1, 1, L, HEAD_DIM), lambda b, g, i: (b, g, 0, 0)),
            pl.BlockSpec((1, 1, nk, V_ROWS, tk), lambda b, g, i: (b, g, 0, 0, 0)),
        ],
        out_specs=pl.BlockSpec((1, 1, KV_GROUP, HEAD_DIM, tq), lambda b, g, i: (b, g, 0, 0, i)),
        scratch_shapes=[pltpu.VMEM((KV_GROUP, V_ROWS, tq), F32), pltpu.VMEM((2, KV_GROUP, tk, tq), F32)],
        compiler_params=_cparams(("parallel", "parallel", "parallel")),
        name="attention",
    )(qt, kh, vt)


def _lru_kernel(xp_ref, x_ref, xn_ref, cw_ref, cb_ref, wa_ref, ba_ref, wx_ref, bx_ref, lam_ref,
                o_ref, xe_sc, a_sc, b_sc, hs_sc, h_sc, *, T, nt):
    d = pl.program_id(0)
    i = pl.program_id(1)
    tile = jnp.where(d == 0, i, nt - 1 - i)
    nb, _, C = x_ref.shape
    nh = C // LANES
    H = SUBLANES
    keep_prev = (tile > 0).astype(F32)
    keep_next = (tile < nt - 1).astype(F32)
    for b in range(nb):
        for hf in range(nh):
            lanes = slice(hf * LANES, (hf + 1) * LANES)
            xe_sc[hf, pl.ds(b, H, stride=nb), :] = xp_ref[b, :, lanes] * keep_prev
            xe_sc[hf, pl.ds(H * nb + b, T, stride=nb), :] = x_ref[b, :, lanes]
            xe_sc[hf, pl.ds((H + T) * nb + b, H, stride=nb), :] = xn_ref[b, :, lanes] * keep_next
    cw = cw_ref[...]
    halves = []
    for hf in range(nh):
        lanes = slice(hf * LANES, (hf + 1) * LANES)
        acc = cb_ref[:, lanes]
        for k in range(cw.shape[0]):
            acc = acc + xe_sc[hf, (H - 2 + k) * nb:(H - 2 + k + T) * nb, :] * cw[k:k + 1, lanes]
        halves.append(acc)
    xc = jnp.concatenate(halves, axis=1)
    xb = xc.astype(BF16)
    sigmoid = lambda v: 0.5 * jnp.tanh(0.5 * v) + 0.5
    r = sigmoid(_dot(xb, wa_ref[0]) + ba_ref[0])
    ig = sigmoid(_dot(xb, wx_ref[0]) + bx_ref[0])
    lam = lam_ref[0]
    softplus_neg = jnp.maximum(-lam, 0.0) + jnp.log1p(jnp.exp(-jnp.abs(lam)))
    log_a = -LRU_C * r * softplus_neg
    a = jnp.exp(log_a)
    bb = jnp.sqrt(-jnp.tanh(log_a) * (1.0 + a * a)) * (ig * xc)
    for hf in range(nh):
        a_sc[hf] = a[:, hf * LANES:(hf + 1) * LANES]
        b_sc[hf] = bb[:, hf * LANES:(hf + 1) * LANES]

    @pl.when(i == 0)
    def _():
        h_sc[...] = jnp.zeros(h_sc.shape, F32)

    def body(s, hs):
        t = jnp.where(d == 0, s, T - 1 - s)
        rows = pl.ds(pl.multiple_of(t * nb, nb), nb)
        new = []
        for hf in range(nh):
            h = a_sc[hf, rows, :] * hs[hf] + b_sc[hf, rows, :]
            hs_sc[hf, rows, :] = h
            new.append(h)
        return tuple(new)

    hs = lax.fori_loop(0, T, body, tuple(h_sc[hf] for hf in range(nh)), unroll=8)
    for hf in range(nh):
        h_sc[hf] = hs[hf]
        for b in range(nb):
            o_ref[0, b, :, hf * LANES:(hf + 1) * LANES] = hs_sc[hf, pl.ds(b, T, stride=nb), :].astype(o_ref.dtype)


def _rglru(xg, conv_w, conv_b, wa, ba, wx, bx, lam):
    B, L, _ = xg.shape
    C = LRU_WIDTH
    assert B == SUBLANES
    T = min(LRU_TILE, L)
    nt = L // T
    hb = T // SUBLANES
    n_halo = L // SUBLANES

    def tile_of(d, i):
        return jnp.where(d == 0, i, nt - 1 - i)

    def blockdiag(w):
        eye = jnp.eye(LRU_HEADS, dtype=w.dtype)
        return jnp.einsum('dhij,hg->dhigj', w, eye).reshape(2, C, C).astype(BF16)

    kern = functools.partial(_lru_kernel, T=T, nt=nt)
    return pl.pallas_call(
        kern,
        out_shape=jax.ShapeDtypeStruct((2, B, L, C), BF16),
        grid=(2, nt),
        in_specs=[
            pl.BlockSpec((B, SUBLANES, C), lambda d, i: (0, jnp.maximum(tile_of(d, i) * hb - 1, 0), 0)),
            pl.BlockSpec((B, T, C), lambda d, i: (0, tile_of(d, i), 0)),
            pl.BlockSpec((B, SUBLANES, C), lambda d, i: (0, jnp.minimum((tile_of(d, i) + 1) * hb, n_halo - 1), 0)),
            pl.BlockSpec(conv_w.shape, lambda d, i: (0, 0)),
            pl.BlockSpec((1, C), lambda d, i: (0, 0)),
            pl.BlockSpec((1, C, C), lambda d, i: (d, 0, 0)),
            pl.BlockSpec((1, 1, C), lambda d, i: (d, 0, 0)),
            pl.BlockSpec((1, C, C), lambda d, i: (d, 0, 0)),
            pl.BlockSpec((1, 1, C), lambda d, i: (d, 0, 0)),
            pl.BlockSpec((1, 1, C), lambda d, i: (d, 0, 0)),
        ],
        out_specs=pl.BlockSpec((1, B, T, C), lambda d, i: (d, 0, tile_of(d, i), 0)),
        scratch_shapes=[
            pltpu.VMEM((C // LANES, (T + 2 * SUBLANES) * B, LANES), F32),
            pltpu.VMEM((C // LANES, T * B, LANES), F32),
            pltpu.VMEM((C // LANES, T * B, LANES), F32),
            pltpu.VMEM((C // LANES, T * B, LANES), F32),
            pltpu.VMEM((C // LANES, B, LANES), F32),
        ],
        compiler_params=_cparams(("arbitrary", "arbitrary")),
        name="rglru",
    )(xg, xg, xg, conv_w, conv_b.reshape(1, C), blockdiag(wa), ba.reshape(2, 1, C),
      blockdiag(wx), bx.reshape(2, 1, C), lam.reshape(2, 1, C))


def _filter_positions(L):
    t = np.linspace(0.0, 1.0, L)[:, None]
    w = 2.0 * math.pi * np.arange(L, dtype=np.float64)[:, None] / L
    f = np.linspace(1e-4, HY_BANDS - 1, HY_BANDS)[None, :]
    z = np.concatenate([t, np.cos(f * w), -np.sin(f * w)], axis=-1)
    zrev = np.concatenate([z[:1], z[:0:-1]], axis=0)
    return jnp.asarray(np.concatenate([z, zrev], axis=1), F32)


def _filter_kernel(zz_ref, w1_ref, b1_ref, w2_ref, b2_ref, w3_ref, b3_ref, fr_ref, wo_ref, dl_ref,
                   fs_ref, asum_ref, *, T, nemb):
    i = pl.program_id(0)
    fr = fr_ref[...]
    hd = lambda a, b: jnp.dot(a, b, precision=HIGHEST, preferred_element_type=F32)
    zz = zz_ref[...]
    h = jnp.sin(fr * (hd(zz, w1_ref[...]) + b1_ref[...]))
    h = jnp.sin(fr * (hd(h, w2_ref[...]) + b2_ref[...]))
    h = jnp.sin(fr * (hd(h, w3_ref[...]) + b3_ref[...]))
    k = hd(h, wo_ref[...])
    dl = dl_ref[...]
    dec_f = jnp.exp(-zz[:, 0:1] * dl)
    dec_r = jnp.exp(-zz[:, nemb:nemb + 1] * dl)
    C = HY_WIDTH
    first_row = (i * T + lax.broadcasted_iota(jnp.int32, (T, 1), 0)) == 0

    @pl.when(i == 0)
    def _():
        asum_ref[...] = jnp.zeros(asum_ref.shape, F32)

    for o in range(HY_ORDER):
        base = o * 2 * C
        kf = k[:, base:base + C] * dec_f
        kb_here = k[:, base + C:base + 2 * C] * dec_f
        kb_rev = k[:, HY_ORDER * 2 * C + o * C:HY_ORDER * 2 * C + (o + 1) * C] * dec_r
        first = kf + jnp.where(first_row, kb_here, 0.0)
        second = jnp.where(first_row, 0.0, kb_rev)
        fs_ref[o] = first
        fs_ref[HY_ORDER + o] = second
        asum_ref[o:o + 1, :] += jnp.sum(jnp.abs(first) + jnp.abs(second), axis=0, keepdims=True)


def _hyena_filter_time(L, w1, b1, w2, b2, w3, b3, freq, wout):
    zz = _filter_positions(L)
    T = _row_tile(L)
    C = HY_WIDTH
    nemb = zz.shape[1] // 2
    nf = w2.shape[0]
    deltas = jnp.asarray(np.linspace(HY_MIN_DECAY, HY_MAX_DECAY, C)[None, :], F32)
    twice = lambda w: jnp.kron(jnp.eye(2, dtype=w.dtype), w)
    both = lambda v: jnp.tile(v, 2).reshape(1, 2 * nf)
    w_back = wout.reshape(nf, HY_ORDER, 2, C)[:, :, 1, :].reshape(nf, HY_ORDER * C)
    wo = jnp.concatenate([jnp.concatenate([wout, jnp.zeros((nf, HY_ORDER * C), wout.dtype)], axis=1),
                          jnp.concatenate([jnp.zeros_like(wout), w_back], axis=1)], axis=0)
    const = lambda i: (0, 0)
    return pl.pallas_call(
        functools.partial(_filter_kernel, T=T, nemb=nemb),
        out_shape=(jax.ShapeDtypeStruct((2 * HY_ORDER, L, C), F32), jax.ShapeDtypeStruct((HY_ORDER, C), F32)),
        grid=(L // T,),
        in_specs=[
            pl.BlockSpec((T, 2 * nemb), lambda i: (i, 0)),
            pl.BlockSpec((2 * nemb, 2 * nf), const), pl.BlockSpec((1, 2 * nf), const),
            pl.BlockSpec((2 * nf, 2 * nf), const), pl.BlockSpec((1, 2 * nf), const),
            pl.BlockSpec((2 * nf, 2 * nf), const), pl.BlockSpec((1, 2 * nf), const),
            pl.BlockSpec((1, 2 * nf), const),
            pl.BlockSpec(wo.shape, const),
            pl.BlockSpec((1, C), const),
        ],
        out_specs=(pl.BlockSpec((2 * HY_ORDER, T, C), lambda i: (0, i, 0)),
                   pl.BlockSpec((HY_ORDER, C), const)),
        compiler_params=_cparams(("arbitrary",)),
        name="hy_filter_mlp",
    )(zz, twice(w1), both(b1), twice(w2), both(b2), twice(w3), both(b3), both(freq), wo, deltas)


def _dft_tables(L):
    N = 2 * L
    N2 = DFT_N2
    N1 = N // N2
    K1 = N1 // 2
    kept = K1 + 1
    rows = -(-kept // SUBLANES) * SUBLANES
    live = (np.arange(rows) < kept).astype(np.float64)[None, :, None]
    pair = np.where((np.arange(rows) == 0) | (np.arange(rows) == K1), 1.0, 2.0)[None, :, None] * live
    n1 = np.arange(K1)[None, None, :]
    k1 = np.arange(rows)[None, :, None]
    n2 = np.arange(N2)[:, None, None]
    ang = 2.0 * np.pi * (((n1 * N2 + n2) * k1) % N) / N
    g_fwd = np.concatenate([np.cos(ang) * live, -np.sin(ang) * live], axis=1)
    t = lambda a: np.transpose(a, (0, 2, 1))
    g_inv = np.concatenate([t(np.cos(ang) * pair), t(-np.sin(ang) * pair)], axis=2) / N
    a2 = 2.0 * np.pi * ((np.arange(N2)[:, None] * np.arange(N2)[None, :]) % N2) / N2
    cr, ci = np.cos(a2), -np.sin(a2)
    m2 = np.block([[cr, -ci], [ci, cr]])
    m2i = np.block([[cr, ci], [-ci, cr]])
    as_bf = lambda a: jnp.asarray(a, BF16)
    return as_bf(g_fwd), as_bf(g_inv), as_bf(m2), as_bf(m2i)


DFT_NB = 16
N_HALF = HY_WIDTH // LANES


def _lane_half_specs(k1n, nb, part):
    return [pl.BlockSpec((1, k1n, nb, LANES), lambda s, j, h=h: (s, 0, j, part * N_HALF + h)) for h in range(N_HALF)]


def _major_half_specs(k1n, nb):
    return [pl.BlockSpec((1, 1, k1n, nb, LANES), lambda s, j, h=h: (s, h, 0, j, 0)) for h in range(N_HALF)]


def _ld_time(ref, j, nb):
    k1n = ref.shape[-3]
    return ref.reshape(k1n * nb, LANES)[pl.ds(j, k1n, stride=nb), :]


def _st_time(ref, h, j, nb, val):
    k1n = ref.shape[2]
    ref.reshape(N_HALF * k1n * nb, LANES)[pl.ds(h * k1n * nb + j, k1n, stride=nb), :] = val


U32 = jnp.uint32


def _pack_halves(x):
    lo = lax.bitcast_convert_type(x[:, :LANES], U32)
    hi = lax.bitcast_convert_type(x[:, LANES:], U32)
    rnd = jnp.uint32(0x8000)
    return ((lo + rnd) >> 16) | ((hi + rnd) & jnp.uint32(0xFFFF0000))


def _unpack_halves(w):
    lo = lax.bitcast_convert_type(w << 16, F32)
    hi = lax.bitcast_convert_type(w & jnp.uint32(0xFFFF0000), F32)
    return jnp.concatenate([lo, hi], axis=1)


def _dft_a_kernel(x0_ref, x1_ref, g_ref, o_ref, *, nb):
    n1 = o_ref.shape[3]
    for j in range(nb):
        parts = []
        for x_ref in (x0_ref, x1_ref):
            parts.append(_ld_time(x_ref, j, nb))
        xj = jnp.concatenate(parts, axis=1).astype(BF16)
        r = _pack_halves(_dot(g_ref[j], xj))
        o_ref[0, 0, j] = r[:n1]
        o_ref[0, 1, j] = r[n1:]


def _dft_stage_a(x, x_specs, g_fwd):
    S = x.shape[0]
    N2, two_n1, K1 = g_fwd.shape
    N1 = two_n1 // 2
    nb = DFT_NB
    return pl.pallas_call(
        functools.partial(_dft_a_kernel, nb=nb),
        out_shape=jax.ShapeDtypeStruct((S, 2, N2, N1, LANES), U32),
        grid=(S, N2 // nb),
        in_specs=x_specs + [pl.BlockSpec((nb, 2 * N1, K1), lambda s, j: (j, 0, 0))],
        out_specs=pl.BlockSpec((1, 2, nb, N1, LANES), lambda s, j: (s, 0, j, 0, 0)),
        compiler_params=_cparams(("parallel", "parallel")),
        name="hy_dft_a",
    )(x, x, g_fwd)


def _k1_rows(ref, j):
    _, _, n2, tk1, _ = ref.shape
    return ref.reshape(2 * n2 * tk1, LANES), pl.ds(j, 2 * n2, stride=tk1)


def _ld_k1(ref, j):
    r2, rows = _k1_rows(ref, j)
    return r2[rows, :]


def _st_k1(ref, j, val):
    r2, rows = _k1_rows(ref, j)
    r2[rows, :] = val


def _dft_b_filter_kernel(s1_ref, s2_ref, m2_ref, asum_ref, o_ref, *, tk1):
    n2 = s1_ref.shape[2]
    C = o_ref.shape[4]
    scale = 1.0 / asum_ref[0]
    for j in range(tk1):
        sign = 1.0 if j % 2 == 0 else -1.0
        s = _unpack_halves(_ld_k1(s1_ref, j)) + sign * _unpack_halves(_ld_k1(s2_ref, j))
        xk = _dot(m2_ref[...], s.astype(BF16)) * scale
        o_ref[0, j] = xk.reshape(2, n2, C)


def _dft_stage_b_filter(sa, m2, asum):
    _, _, N2, N1, _ = sa.shape
    C = HY_WIDTH
    tk1 = SUBLANES
    blk = (1, 2, N2, tk1, LANES)
    return pl.pallas_call(
        functools.partial(_dft_b_filter_kernel, tk1=tk1),
        out_shape=jax.ShapeDtypeStruct((HY_ORDER, N1, 2, N2, C), F32),
        grid=(HY_ORDER, N1 // tk1),
        in_specs=[
            pl.BlockSpec(blk, lambda o, i: (o, 0, 0, i, 0)),
            pl.BlockSpec(blk, lambda o, i: (HY_ORDER + o, 0, 0, i, 0)),
            pl.BlockSpec((2 * N2, 2 * N2), lambda o, i: (0, 0)),
            pl.BlockSpec((1, 1, C), lambda o, i: (o, 0, 0)),
        ],
        out_specs=pl.BlockSpec((1, tk1, 2, N2, C), lambda o, i: (o, i, 0, 0, 0)),
        compiler_params=_cparams(("parallel", "parallel")),
        name="hy_dft_b_filter",
    )(sa, sa, m2, asum.reshape(HY_ORDER, 1, C))


def _dft_b_conv_kernel(s_ref, kf_ref, m2_ref, m2i_ref, o_ref, *, tk1):
    n2 = s_ref.shape[2]
    for j in range(tk1):
        x = _dot(m2_ref[...], _unpack_halves(_ld_k1(s_ref, j)).astype(BF16))
        xr, xi = x[:n2], x[n2:]
        kr = kf_ref[0, j, 0]
        ki = kf_ref[0, j, 1]
        y = jnp.concatenate([xr * kr - xi * ki, xr * ki + xi * kr], axis=0).astype(BF16)
        _st_k1(o_ref, j, _pack_halves(_dot(m2i_ref[...], y)))


def _dft_stage_b_conv(sa, kspec, order, m2, m2i):
    S, _, N2, N1, _ = sa.shape
    C = HY_WIDTH
    tk1 = SUBLANES
    blk = (1, 2, N2, tk1, LANES)
    return pl.pallas_call(
        functools.partial(_dft_b_conv_kernel, tk1=tk1),
        out_shape=jax.ShapeDtypeStruct(sa.shape, U32),
        grid=(N1 // tk1, S),
        in_specs=[
            pl.BlockSpec(blk, lambda i, s: (s, 0, 0, i, 0)),
            pl.BlockSpec((1, tk1, 2, N2, C), lambda i, s: (order, i, 0, 0, 0)),
            pl.BlockSpec((2 * N2, 2 * N2), lambda i, s: (0, 0)),
            pl.BlockSpec((2 * N2, 2 * N2), lambda i, s: (0, 0)),
        ],
        out_specs=pl.BlockSpec(blk, lambda i, s: (s, 0, 0, i, 0)),
        compiler_params=_cparams(("parallel", "parallel")),
        name="hy_dft_b_conv",
    )(sa, kspec, m2, m2i)


def _dft_c_kernel(c_ref, gi_ref, z0_ref, z1_ref, g0_ref, g1_ref, d_ref, o_ref, *, nb):
    d = d_ref[0]
    for j in range(nb):
        cat = _unpack_halves(jnp.concatenate([c_ref[0, 0, j], c_ref[0, 1, j]], axis=0))
        y = _dot(gi_ref[j], cat.astype(BF16))
        for h, (z_ref, g_ref) in enumerate(((z0_ref, g0_ref), (z1_ref, g1_ref))):
            lanes = slice(h * LANES, (h + 1) * LANES)
            _st_time(o_ref, h, j, nb, _ld_time(g_ref, j, nb) * (y[:, lanes] + _ld_time(z_ref, j, nb) * d[:, lanes]))


def _dft_stage_c(sc, g_inv, z, z_specs, gate, gate_specs, d):
    S, _, N2, N1, _ = sc.shape
    C = HY_WIDTH
    K1 = g_inv.shape[1]
    nb = DFT_NB
    return pl.pallas_call(
        functools.partial(_dft_c_kernel, nb=nb),
        out_shape=jax.ShapeDtypeStruct((S, N_HALF, K1, N2, LANES), F32),
        grid=(S, N2 // nb),
        in_specs=[
            pl.BlockSpec((1, 2, nb, N1, LANES), lambda s, j: (s, 0, j, 0, 0)),
            pl.BlockSpec((nb, K1, 2 * N1), lambda s, j: (j, 0, 0)),
        ] + z_specs + gate_specs + [pl.BlockSpec((1, 1, C), lambda s, j: (0, 0, 0))],
        out_specs=pl.BlockSpec((1, N_HALF, K1, nb, LANES), lambda s, j: (s, 0, 0, j, 0)),
        compiler_params=_cparams(("parallel", "parallel")),
        name="hy_dft_c",
    )(sc, g_inv, z, z, gate, gate, d.reshape(1, 1, C))


def _hyena(hc, w1, b1, w2, b2, w3, b3, freq, wout, bias, tables):
    B, L, _ = hc.shape
    C = HY_WIDTH
    g_fwd, g_inv, m2, m2i = tables
    N2 = DFT_N2
    K1 = L // N2
    nb = DFT_NB
    fs, asum = _hyena_filter_time(L, w1, b1, w2, b2, w3, b3, freq, wout)
    fa = _dft_stage_a(fs.reshape(2 * HY_ORDER, K1, N2, C), _lane_half_specs(K1, nb, 0), g_fwd)
    kspec = _dft_stage_b_filter(fa, m2, asum)
    hc4 = hc.reshape(B, K1, N2, (HY_ORDER + 1) * C)
    z, z_specs = hc4, _lane_half_specs(K1, nb, 0)
    for o in range(HY_ORDER):
        sa = _dft_stage_a(z, z_specs, g_fwd)
        sc = _dft_stage_b_conv(sa, kspec, o, m2, m2i)
        z = _dft_stage_c(sc, g_inv, z, z_specs, hc4, _lane_half_specs(K1, nb, 1 + o), bias[o])
        z_specs = _major_half_specs(K1, nb)
    return z.reshape(B, N_HALF, L, LANES)


def _outproj_kernel(x_ref, at_ref, h_ref, gr_ref, zy_ref, w_ref, gt_ref, g_ref, b_ref, o_ref, *, alpha):
    lru = (h_ref[0, 0].astype(F32) + h_ref[1, 0].astype(F32)) * jax.nn.gelu(gr_ref[0])
    na = IN_Q
    at = at_ref[0].reshape(na, at_ref.shape[-1])
    m = lax.dot_general(at, w_ref[0:na], (((0,), (0,)), ((), ())), preferred_element_type=F32)
    rest = jnp.concatenate([lru.astype(BF16)] + [zy_ref[0, h].astype(BF16) for h in range(N_HALF)], axis=1)
    m = m + _dot(rest, w_ref[na:])
    o_ref[0] = _layer_norm(alpha * x_ref[0] + gt_ref[0] * m, g_ref[...], b_ref[...])


def _out_proj(x, attn, h, xg, zy, w_out, gt, ln_g, ln_b, alpha):
    B, L, D = x.shape
    T = _row_tile(L)
    C = LRU_WIDTH
    row = lambda b, i: (b, i, 0)
    vec = lambda b, i: (b, 0, 0)
    const = lambda b, i: (0, 0)
    return pl.pallas_call(
        functools.partial(_outproj_kernel, alpha=alpha),
        out_shape=jax.ShapeDtypeStruct((B, L, D), F32),
        grid=(B, L // T),
        in_specs=[
            pl.BlockSpec((1, T, D), row),
            pl.BlockSpec((1, N_KV_HEADS, KV_GROUP, HEAD_DIM, T), lambda b, i: (b, 0, 0, 0, i)),
            pl.BlockSpec((2, 1, T, C), lambda b, i: (0, b, i, 0)),
            pl.BlockSpec((1, T, C), lambda b, i: (b, i, 1)),
            pl.BlockSpec((1, N_HALF, T, LANES), lambda b, i: (b, 0, i, 0)),
            pl.BlockSpec(w_out.shape, const),
            pl.BlockSpec((1, 1, D), vec),
            pl.BlockSpec((1, D), const),
            pl.BlockSpec((1, D), const),
        ],
        out_specs=pl.BlockSpec((1, T, D), row),
        compiler_params=_cparams(("parallel", "parallel")),
        name="out_proj",
    )(x, attn, h, xg, zy, w_out, gt, ln_g.reshape(1, D), ln_b.reshape(1, D))


FFN_CHUNK = 256


def _ffn_kernel(xp_ref, x_ref, xn_ref, sc_ref, sh_ref, gt_ref, wu_ref, cw_ref, cb_ref, wd_ref, g_ref, b_ref,
                o_ref, u_sc, h_sc, acc_sc, *, T, nt, nf, alpha):
    i = pl.program_id(1)
    H = SUBLANES
    R = T + 2 * H
    F = FFN_CHUNK
    sc = 1.0 + sc_ref[0]
    sh = sh_ref[0]
    u_sc[H:H + T, :] = (x_ref[0] * sc + sh).astype(BF16)
    u_sc[0:H, :] = jnp.where(i > 0, xp_ref[0] * sc + sh, 0.0).astype(BF16)
    u_sc[H + T:R, :] = jnp.where(i < nt - 1, xn_ref[0] * sc + sh, 0.0).astype(BF16)
    acc_sc[...] = jnp.zeros(acc_sc.shape, F32)

    def up(f, slot):
        h_sc[slot] = _dot(u_sc[...], wu_ref[f])

    def down(f, slot):
        h = h_sc[slot]
        cw = cw_ref[f]
        y = cb_ref[f] + pltpu.roll(h, 1, 0) * cw[0:1] + h * cw[1:2] + pltpu.roll(h, R - 1, 0) * cw[2:3]
        y = y[H:H + T]
        act = (jax.nn.gelu(y[:, :F]) * y[:, F:]).astype(BF16)
        acc_sc[...] += _dot(act, wd_ref[f])

    def pair(c2, carry):
        f = 2 * c2
        up(f + 1, 1)
        down(f, 0)
        up(jnp.minimum(f + 2, nf - 1), 0)
        down(f + 1, 1)
        return carry

    up(0, 0)
    lax.fori_loop(0, nf // 2, pair, 0)
    if nf % 2 == 1:
        down(nf - 1, 0)
    o_ref[0] = _layer_norm(alpha * x_ref[0] + gt_ref[0] * acc_sc[...], g_ref[...], b_ref[...])


def _conv_ffn(x, sc, sh, gt, w_up, conv_w, conv_b, w_down, ln_g, ln_b, alpha):
    B, L, D = x.shape
    d_ff = w_down.shape[0]
    T = _row_tile(L)
    F = FFN_CHUNK
    nt = L // T
    nf = d_ff // F
    hb = T // SUBLANES
    n_halo = L // SUBLANES
    nk = conv_w.shape[0]
    chunked = lambda w: jnp.concatenate([w[..., :d_ff].reshape(w.shape[0], nf, F),
                                         w[..., d_ff:].reshape(w.shape[0], nf, F)], axis=-1).transpose(1, 0, 2)
    wu = chunked(w_up)
    cw = chunked(conv_w)
    cb = chunked(conv_b.reshape(1, 2 * d_ff))
    wd = w_down.reshape(nf, F, D)
    row = lambda b, i: (b, i, 0)
    vec = lambda b, i: (b, 0, 0)
    const2 = lambda b, i: (0, 0)
    const3 = lambda b, i: (0, 0, 0)
    resident = lambda shape, imap: pl.BlockSpec(shape, imap, pipeline_mode=pl.Buffered(1))
    return pl.pallas_call(
        functools.partial(_ffn_kernel, T=T, nt=nt, nf=nf, alpha=alpha),
        out_shape=jax.ShapeDtypeStruct((B, L, D), F32),
        grid=(B, nt),
        in_specs=[
            pl.BlockSpec((1, SUBLANES, D), lambda b, i: (b, jnp.maximum(i * hb - 1, 0), 0)),
            pl.BlockSpec((1, T, D), row),
            pl.BlockSpec((1, SUBLANES, D), lambda b, i: (b, jnp.minimum((i + 1) * hb, n_halo - 1), 0)),
            pl.BlockSpec((1, 1, D), vec),
            pl.BlockSpec((1, 1, D), vec),
            pl.BlockSpec((1, 1, D), vec),
            resident((nf, D, 2 * F), const3),
            resident((nf, nk, 2 * F), const3),
            resident((nf, 1, 2 * F), const3),
            resident((nf, F, D), const3),
            pl.BlockSpec((1, D), const2),
            pl.BlockSpec((1, D), const2),
        ],
        out_specs=pl.BlockSpec((1, T, D), row),
        scratch_shapes=[
            pltpu.VMEM((T + 2 * SUBLANES, D), BF16),
            pltpu.VMEM((2, T + 2 * SUBLANES, 2 * F), F32),
            pltpu.VMEM((T, D), F32),
        ],
        compiler_params=_cparams(("parallel", "parallel")),
        name="conv_ffn",
    )(x, x, x, sc, sh, gt, wu, cw, cb, wd, ln_g.reshape(1, D), ln_b.reshape(1, D))


def _trunk(x, mod, p):
    B, L, D = x.shape
    depth = mod.shape[0]
    alpha = (2 * depth) ** 0.25
    rope = _rope_tables(L)
    tables = _dft_tables(L)
    for l in range(depth):
        m6 = mod[l].reshape(B, 6, 1, D)
        sh1, sc1, gt1, sh2, sc2, gt2 = (m6[:, j] for j in range(6))
        q, k, v, xg, hc = _in_proj(x, sc1, sh1, p['w_in'][l], p['q_gain'][l], p['k_gain'][l], rope,
                                   p['hy_conv_w'][l], p['hy_conv_b'][l])
        attn = _attention(q, k, v)
        h = _rglru(xg, p['lru_conv_w'][l], p['lru_conv_b'][l], p['lru_wa'][l], p['lru_ba'][l],
                   p['lru_wx'][l], p['lru_bx'][l], p['lru_lambda'][l])
        zy = _hyena(hc, p['hy_w1'][l], p['hy_b1'][l], p['hy_w2'][l], p['hy_b2'][l], p['hy_w3'][l], p['hy_b3'][l],
                    p['hy_freq'][l], p['hy_wout'][l], p['hy_bias'][l], tables)
        x = _out_proj(x, attn, h, xg, zy, p['w_out'][l], gt1, p['ln1_g'][l], p['ln1_b'][l], alpha)
        x = _conv_ffn(x, sc2, sh2, gt2, p['ffn_w_up'][l], p['ffn_conv_w'][l], p['ffn_conv_b'][l],
                      p['ffn_w_down'][l], p['ln2_g'][l], p['ln2_b'][l], alpha)
    return x


def kernel(x_prompt, x_sample, c_prompt, c_sample, ada_w, ada_b, w_in, q_gain, k_gain, lru_conv_w, lru_conv_b, lru_wa, lru_ba, lru_wx, lru_bx, lru_lambda, hy_conv_w, hy_conv_b, hy_w1, hy_b1, hy_w2, hy_b2, hy_w3, hy_b3, hy_freq, hy_wout, hy_bias, w_out, ln1_g, ln1_b, ffn_w_up, ffn_conv_w, ffn_conv_b, ffn_w_down, ln2_g, ln2_b):
    p = dict(
        w_in=w_in.astype(BF16), q_gain=q_gain, k_gain=k_gain, lru_conv_w=lru_conv_w, lru_conv_b=lru_conv_b,
        lru_wa=lru_wa, lru_ba=lru_ba, lru_wx=lru_wx, lru_bx=lru_bx, lru_lambda=lru_lambda,
        hy_conv_w=hy_conv_w, hy_conv_b=hy_conv_b, hy_w1=hy_w1, hy_b1=hy_b1, hy_w2=hy_w2, hy_b2=hy_b2,
        hy_w3=hy_w3, hy_b3=hy_b3, hy_freq=hy_freq, hy_wout=hy_wout, hy_bias=hy_bias,
        w_out=w_out.astype(BF16), ln1_g=ln1_g, ln1_b=ln1_b, ffn_w_up=ffn_w_up.astype(BF16),
        ffn_conv_w=ffn_conv_w, ffn_conv_b=ffn_conv_b, ffn_w_down=ffn_w_down.astype(BF16), ln2_g=ln2_g, ln2_b=ln2_b,
    )
    nb = x_prompt.shape[0]
    mod = _ada_mod(jnp.concatenate([c_prompt, c_sample], axis=0), ada_w, ada_b)
    y_prompt = _trunk(x_prompt, mod[:, :nb], p)
    y_sample = _trunk(x_sample, mod[:, nb:], p)
    return (y_prompt, y_sample)
```

```python
import functools
import math

import numpy as np
import jax
import jax.numpy as jnp
from jax import lax
from jax.experimental import pallas as pl
from jax.experimental.pallas import tpu as pltpu

F32 = jnp.float32
BF16 = jnp.bfloat16
HIGHEST = lax.Precision.HIGHEST

GRID_W = 64
HEAD_DIM = 64
N_HEADS = 8
N_KV_HEADS = 2
KV_GROUP = N_HEADS // N_KV_HEADS
ROPE_THETA = 10000.0
ROPE_FREQS = HEAD_DIM // 4
QK_EPS = 1e-6
LRU_WIDTH = 256
LRU_HEADS = 4
LRU_C = 8.0
HY_WIDTH = 256
HY_ORDER = 2
HY_BANDS = 16
HY_MIN_DECAY = abs(math.log(1e-2)) / 1.5
HY_MAX_DECAY = abs(math.log(1e-2)) / 0.3
LN_EPS = 1e-5
IN_Q = N_HEADS * HEAD_DIM
IN_KV = N_KV_HEADS * HEAD_DIM

LANES = 128
SUBLANES = 8
V7X_VMEM_BYTES = 64 * 1024 * 1024
VMEM_LIMIT = V7X_VMEM_BYTES * 3 // 4
DFT_N2 = 128

ROW_TILE = 1024
LRU_TILE = 256
ATTN_Q_TILE = 512
ATTN_K_CHUNK = 512
ATTN_UNROLL = 4


def _row_tile(L):
    return min(ROW_TILE, L)


def _cparams(sem):
    return pltpu.CompilerParams(dimension_semantics=sem, vmem_limit_bytes=VMEM_LIMIT)


def _dot(a, b):
    return jnp.dot(a, b, preferred_element_type=F32)


def _layer_norm(y, g, b):
    mu = jnp.mean(y, axis=-1, keepdims=True)
    yc = y - mu
    var = jnp.mean(yc * yc, axis=-1, keepdims=True)
    return yc * lax.rsqrt(var + LN_EPS) * g + b


def _ada_kernel(c_ref, w_ref, b_ref, o_ref):
    c = c_ref[...]
    s = c * jax.nn.sigmoid(c)
    o_ref[0] = jnp.dot(s, w_ref[0], precision=HIGHEST, preferred_element_type=F32) + b_ref[0]


def _ada_mod(c_all, ada_w, ada_b):
    depth, d, n = ada_w.shape
    rows = c_all.shape[0]
    tn = 768
    return pl.pallas_call(
        _ada_kernel,
        out_shape=jax.ShapeDtypeStruct((depth, rows, n), F32),
        grid=(depth, n // tn),
        in_specs=[
            pl.BlockSpec((rows, d), lambda l, j: (0, 0)),
            pl.BlockSpec((1, d, tn), lambda l, j: (l, 0, j)),
            pl.BlockSpec((1, 1, tn), lambda l, j: (l, 0, j)),
        ],
        out_specs=pl.BlockSpec((1, rows, tn), lambda l, j: (l, 0, j)),
        compiler_params=_cparams(("parallel", "parallel")),
        name="ada_mod",
    )(c_all, ada_w, ada_b.reshape(depth, 1, n))


def _rope_tables(L):
    rows = L // GRID_W
    row = np.repeat(np.arange(rows, dtype=np.float64), GRID_W)
    col = np.tile(np.arange(GRID_W, dtype=np.float64), rows)
    inv = ROPE_THETA ** (-np.arange(ROPE_FREQS, dtype=np.float64) / ROPE_FREQS)
    ar = row[:, None] * inv
    ac = col[:, None] * inv
    zeros = np.zeros_like(ar)
    cos = np.concatenate([np.cos(ar), np.cos(ar), np.cos(ac), np.cos(ac)], axis=1)
    sin_up = np.concatenate([-np.sin(ar), zeros, -np.sin(ac), zeros], axis=1)
    sin_dn = np.concatenate([zeros, np.sin(ar), zeros, np.sin(ac)], axis=1)
    two = lambda t: jnp.asarray(np.concatenate([t, t], axis=1), F32)
    return two(cos), two(sin_up), two(sin_dn)


def _inproj_kernel(xp_ref, x_ref, xn_ref, sc_ref, sh_ref, w_ref, qg_ref, kg_ref, cos_ref, sup_ref, sdn_ref, bd_ref,
                   hcw_ref, hcb_ref, q_ref, k_ref, vt_ref, xg_ref, hc_ref, *, T, nt):
    i = pl.program_id(1)
    H = SUBLANES
    R = T + 2 * H
    sc = 1.0 + sc_ref[0]
    sh = sh_ref[0]
    u_ext = jnp.concatenate([jnp.where(i > 0, xp_ref[0] * sc + sh, 0.0), x_ref[0] * sc + sh,
                             jnp.where(i < nt - 1, xn_ref[0] * sc + sh, 0.0)], axis=0).astype(BF16)
    proj_ext = _dot(u_ext, w_ref[...])
    proj = proj_ext[H:H + T]
    cos = cos_ref[...]
    sup = sup_ref[...]
    sdn = sdn_ref[...]
    bd = bd_ref[...]
    half = ROPE_FREQS

    def norm_rope(t, gain):
        ms = _dot((t * t).astype(BF16), bd)
        tn = t * lax.rsqrt(ms + QK_EPS) * gain
        return (tn * cos + pltpu.roll(tn, LANES - half, 1) * sup + pltpu.roll(tn, half, 1) * sdn)

    qg = qg_ref[...]
    for j in range(IN_Q // LANES):
        sl = slice(j * LANES, (j + 1) * LANES)
        q_ref[0, :, sl] = (norm_rope(proj[:, sl], qg) * (HEAD_DIM ** -0.5 * math.log2(math.e))).astype(BF16)
    kn = norm_rope(proj[:, IN_Q:IN_Q + IN_KV], kg_ref[...])
    v_t = proj[:, IN_Q + IN_KV:IN_Q + 2 * IN_KV].T
    tk = vt_ref.shape[-1]
    tail = (lax.broadcasted_iota(jnp.int32, (V_ROWS - HEAD_DIM, tk), 0) == 0).astype(BF16)
    for g in range(N_KV_HEADS):
        k_ref[0, g] = kn[:, g * HEAD_DIM:(g + 1) * HEAD_DIM].astype(BF16)
        for c in range(T // tk):
            vt_ref[0, g, c, 0:HEAD_DIM, :] = v_t[g * HEAD_DIM:(g + 1) * HEAD_DIM, c * tk:(c + 1) * tk].astype(BF16)
            vt_ref[0, g, c, HEAD_DIM:, :] = tail
    o = IN_Q + 2 * IN_KV
    xg_ref[0] = proj[:, o:o + 2 * LRU_WIDTH]
    hy = proj_ext[:, o + 2 * LRU_WIDTH:]
    cw = hcw_ref[...]
    hc = hcb_ref[...] + pltpu.roll(hy, 1, 0) * cw[0:1] + hy * cw[1:2] + pltpu.roll(hy, R - 1, 0) * cw[2:3]
    hc_ref[0] = hc[H:H + T]


def _in_proj(x, sc, sh, w_in, q_gain, k_gain, rope, hy_conv_w, hy_conv_b):
    B, L, D = x.shape
    n_in = w_in.shape[1]
    T = _row_tile(L)
    nt = L // T
    hb = T // SUBLANES
    n_halo = L // SUBLANES
    _, tk = _attn_tiles(L)
    cos, sup, sdn = rope
    bd = jnp.asarray(np.kron(np.eye(2), np.full((HEAD_DIM, HEAD_DIM), 1.0 / HEAD_DIM)), BF16)
    qg = jnp.tile(q_gain, 2).reshape(1, LANES)
    kg = jnp.tile(k_gain, 2).reshape(1, LANES)
    n_hy = n_in - IN_Q - 2 * IN_KV - 2 * LRU_WIDTH
    row = lambda b, i: (b, i, 0)
    vec = lambda b, i: (b, 0, 0)
    tab = lambda b, i: (i, 0)
    const = lambda b, i: (0, 0)
    return pl.pallas_call(
        functools.partial(_inproj_kernel, T=T, nt=nt),
        out_shape=(
            jax.ShapeDtypeStruct((B, L, IN_Q), BF16),
            jax.ShapeDtypeStruct((B, N_KV_HEADS, L, HEAD_DIM), BF16),
            jax.ShapeDtypeStruct((B, N_KV_HEADS, L // tk, V_ROWS, tk), BF16),
            jax.ShapeDtypeStruct((B, L, 2 * LRU_WIDTH), F32),
            jax.ShapeDtypeStruct((B, L, n_hy), F32),
        ),
        grid=(B, nt),
        in_specs=[
            pl.BlockSpec((1, SUBLANES, D), lambda b, i: (b, jnp.maximum(i * hb - 1, 0), 0)),
            pl.BlockSpec((1, T, D), row),
            pl.BlockSpec((1, SUBLANES, D), lambda b, i: (b, jnp.minimum((i + 1) * hb, n_halo - 1), 0)),
            pl.BlockSpec((1, 1, D), vec),
            pl.BlockSpec((1, 1, D), vec),
            pl.BlockSpec((D, n_in), const),
            pl.BlockSpec((1, LANES), const),
            pl.BlockSpec((1, LANES), const),
            pl.BlockSpec((T, LANES), tab),
            pl.BlockSpec((T, LANES), tab),
            pl.BlockSpec((T, LANES), tab),
            pl.BlockSpec((LANES, LANES), const),
            pl.BlockSpec(hy_conv_w.shape, const),
            pl.BlockSpec((1, n_hy), const),
        ],
        out_specs=(
            pl.BlockSpec((1, T, IN_Q), row),
            pl.BlockSpec((1, N_KV_HEADS, T, HEAD_DIM), lambda b, i: (b, 0, i, 0)),
            pl.BlockSpec((1, N_KV_HEADS, T // tk, V_ROWS, tk), lambda b, i: (b, 0, i, 0, 0)),
            pl.BlockSpec((1, T, 2 * LRU_WIDTH), row),
            pl.BlockSpec((1, T, n_hy), row),
        ),
        compiler_params=_cparams(("parallel", "parallel")),
        name="in_proj",
    )(x, x, x, sc, sh, w_in, qg, kg, cos, sup, sdn, bd, hy_conv_w, hy_conv_b.reshape(1, n_hy))


V_ROWS = HEAD_DIM + 16


def _attn_kernel(qt_ref, k_ref, vt_ref, o_ref, acc_sc, s_sc, *, tk, nk):
    tq = qt_ref.shape[-1]
    acc_sc[...] = jnp.zeros(acc_sc.shape, F32)

    def scores_h(c, slot, h):
        kc = k_ref[0, 0, pl.ds(pl.multiple_of(c * tk, tk), tk), :]
        s = _dot(kc, qt_ref[0, 0, h])
        s_sc[slot, h] = s
        return jnp.max(s, axis=0, keepdims=True)

    def consume_h(c, slot, h, m_prev, m_chunk):
        vc = vt_ref[0, 0, c]
        m_new = jnp.maximum(m_prev, m_chunk)
        p = jnp.exp2(s_sc[slot, h] - m_new).astype(BF16)
        alpha = jnp.exp2(m_prev - m_new)
        acc_sc[h] = alpha * acc_sc[h] + _dot(vc, p)
        return m_new

    def step(c, c_next, slot, carry):
        ms, mc = carry
        new_m, new_c = [], []
        for h in range(KV_GROUP):
            new_c.append(scores_h(c_next, 1 - slot, h))
            new_m.append(consume_h(c, slot, h, ms[h], mc[h]))
        return tuple(new_m), tuple(new_c)

    def pair(c2, carry):
        c = 2 * c2
        carry = step(c, c + 1, 0, carry)
        return step(c + 1, jnp.minimum(c + 2, nk - 1), 1, carry)

    m0 = tuple(jnp.full((1, tq), -jnp.inf, F32) for _ in range(KV_GROUP))
    c0 = tuple(scores_h(0, 0, h) for h in range(KV_GROUP))
    lax.fori_loop(0, nk // 2, pair, (m0, c0), unroll=ATTN_UNROLL)
    for h in range(KV_GROUP):
        acc = acc_sc[h]
        o_ref[0, 0, h] = (acc[:HEAD_DIM] / acc[HEAD_DIM:HEAD_DIM + 1]).astype(o_ref.dtype)


def _attn_tiles(L):
    return min(ATTN_Q_TILE, L), min(ATTN_K_CHUNK, L // 2)


def _attention(q, kh, vt):
    B, L, _ = q.shape
    tq, tk = _attn_tiles(L)
    nk = L // tk
    assert nk % 2 == 0
    qt = q.reshape(B, L, N_KV_HEADS, KV_GROUP, HEAD_DIM).transpose(0, 2, 3, 4, 1)
    return pl.pallas_call(
        functools.partial(_attn_kernel, tk=tk, nk=nk),
        out_shape=jax.ShapeDtypeStruct((B, N_KV_HEADS, KV_GROUP, HEAD_DIM, L), BF16),
        grid=(B, N_KV_HEADS, L // tq),
        in_specs=[
            pl.BlockSpec((1, 1, KV_GROUP, HEAD_DIM, tq), lambda b, g, i: (b, g, 0, 0, i)),
            pl.BlockSpec((1, 1, L, HEAD_DIM), lambda b, g, i: (b, g, 0, 0)),
            pl.BlockSpec((1, 1, nk, V_ROWS, tk), lambda b, g, i: (b, g, 0, 0, 0)),
        ],
        out_specs=pl.BlockSpec((1, 1, KV_GROUP, HEAD_DIM, tq), lambda b, g, i: (b, g, 0, 0, i)),
        scratch_shapes=[pltpu.VMEM((KV_GROUP, V_ROWS, tq), F32), pltpu.VMEM((2, KV_GROUP, tk, tq), F32)],
        compiler_params=_cparams(("parallel", "parallel", "parallel")),
        name="attention",
    )(qt, kh, vt)


def _lru_kernel(xp_ref, x_ref, xn_ref, cw_ref, cb_ref, wa_ref, ba_ref, wx_ref, bx_ref, lam_ref,
                o_ref, xe_sc, a_sc, b_sc, hs_sc, h_sc, *, T, nt):
    d = pl.program_id(0)
    i = pl.program_id(1)
    tile = jnp.where(d == 0, i, nt - 1 - i)
    nb, _, C = x_ref.shape
    nh = C // LANES
    H = SUBLANES
    keep_prev = (tile > 0).astype(F32)
    keep_next = (tile < nt - 1).astype(F32)
    for b in range(nb):
        for hf in range(nh):
            lanes = slice(hf * LANES, (hf + 1) * LANES)
            xe_sc[hf, pl.ds(b, H, stride=nb), :] = xp_ref[b, :, lanes] * keep_prev
            xe_sc[hf, pl.ds(H * nb + b, T, stride=nb), :] = x_ref[b, :, lanes]
            xe_sc[hf, pl.ds((H + T) * nb + b, H, stride=nb), :] = xn_ref[b, :, lanes] * keep_next
    cw = cw_ref[...]
    halves = []
    for hf in range(nh):
        lanes = slice(hf * LANES, (hf + 1) * LANES)
        acc = cb_ref[:, lanes]
        for k in range(cw.shape[0]):
            acc = acc + xe_sc[hf, (H - 2 + k) * nb:(H - 2 + k + T) * nb, :] * cw[k:k + 1, lanes]
        halves.append(acc)
    xc = jnp.concatenate(halves, axis=1)
    xb = xc.astype(BF16)
    sigmoid = lambda v: 0.5 * jnp.tanh(0.5 * v) + 0.5
    r = sigmoid(_dot(xb, wa_ref[0]) + ba_ref[0])
    ig = sigmoid(_dot(xb, wx_ref[0]) + bx_ref[0])
    lam = lam_ref[0]
    softplus_neg = jnp.maximum(-lam, 0.0) + jnp.log1p(jnp.exp(-jnp.abs(lam)))
    log_a = -LRU_C * r * softplus_neg
    a = jnp.exp(log_a)
    bb = jnp.sqrt(-jnp.tanh(log_a) * (1.0 + a * a)) * (ig * xc)
    for hf in range(nh):
        a_sc[hf] = a[:, hf * LANES:(hf + 1) * LANES]
        b_sc[hf] = bb[:, hf * LANES:(hf + 1) * LANES]

    @pl.when(i == 0)
    def _():
        h_sc[...] = jnp.zeros(h_sc.shape, F32)

    def body(s, hs):
        t = jnp.where(d == 0, s, T - 1 - s)
        rows = pl.ds(pl.multiple_of(t * nb, nb), nb)
        new = []
        for hf in range(nh):
            h = a_sc[hf, rows, :] * hs[hf] + b_sc[hf, rows, :]
            hs_sc[hf, rows, :] = h
            new.append(h)
        return tuple(new)

    hs = lax.fori_loop(0, T, body, tuple(h_sc[hf] for hf in range(nh)), unroll=8)
    for hf in range(nh):
        h_sc[hf] = hs[hf]
        for b in range(nb):
            o_ref[0, b, :, hf * LANES:(hf + 1) * LANES] = hs_sc[hf, pl.ds(b, T, stride=nb), :].astype(o_ref.dtype)


def _rglru(xg, conv_w, conv_b, wa, ba, wx, bx, lam):
    B, L, _ = xg.shape
    C = LRU_WIDTH
    assert B == SUBLANES
    T = min(LRU_TILE, L)
    nt = L // T
    hb = T // SUBLANES
    n_halo = L // SUBLANES

    def tile_of(d, i):
        return jnp.where(d == 0, i, nt - 1 - i)

    def blockdiag(w):
        eye = jnp.eye(LRU_HEADS, dtype=w.dtype)
        return jnp.einsum('dhij,hg->dhigj', w, eye).reshape(2, C, C).astype(BF16)

    kern = functools.partial(_lru_kernel, T=T, nt=nt)
    return pl.pallas_call(
        kern,
        out_shape=jax.ShapeDtypeStruct((2, B, L, C), BF16),
        grid=(2, nt),
        in_specs=[
            pl.BlockSpec((B, SUBLANES, C), lambda d, i: (0, jnp.maximum(tile_of(d, i) * hb - 1, 0), 0)),
            pl.BlockSpec((B, T, C), lambda d, i: (0, tile_of(d, i), 0)),
            pl.BlockSpec((B, SUBLANES, C), lambda d, i: (0, jnp.minimum((tile_of(d, i) + 1) * hb, n_halo - 1), 0)),
            pl.BlockSpec(conv_w.shape, lambda d, i: (0, 0)),
            pl.BlockSpec((1, C), lambda d, i: (0, 0)),
            pl.BlockSpec((1, C, C), lambda d, i: (d, 0, 0)),
            pl.BlockSpec((1, 1, C), lambda d, i: (d, 0, 0)),
            pl.BlockSpec((1, C, C), lambda d, i: (d, 0, 0)),
            pl.BlockSpec((1, 1, C), lambda d, i: (d, 0, 0)),
            pl.BlockSpec((1, 1, C), lambda d, i: (d, 0, 0)),
        ],
        out_specs=pl.BlockSpec((1, B, T, C), lambda d, i: (d, 0, tile_of(d, i), 0)),
        scratch_shapes=[
            pltpu.VMEM((C // LANES, (T + 2 * SUBLANES) * B, LANES), F32),
            pltpu.VMEM((C // LANES, T * B, LANES), F32),
            pltpu.VMEM((C // LANES, T * B, LANES), F32),
            pltpu.VMEM((C // LANES, T * B, LANES), F32),
            pltpu.VMEM((C // LANES, B, LANES), F32),
        ],
        compiler_params=_cparams(("arbitrary", "arbitrary")),
        name="rglru",
    )(xg, xg, xg, conv_w, conv_b.reshape(1, C), blockdiag(wa), ba.reshape(2, 1, C),
      blockdiag(wx), bx.reshape(2, 1, C), lam.reshape(2, 1, C))


def _filter_positions(L):
    t = np.linspace(0.0, 1.0, L)[:, None]
    w = 2.0 * math.pi * np.arange(L, dtype=np.float64)[:, None] / L
    f = np.linspace(1e-4, HY_BANDS - 1, HY_BANDS)[None, :]
    z = np.concatenate([t, np.cos(f * w), -np.sin(f * w)], axis=-1)
    zrev = np.concatenate([z[:1], z[:0:-1]], axis=0)
    return jnp.asarray(np.concatenate([z, zrev], axis=1), F32)


def _filter_kernel(zz_ref, w1_ref, b1_ref, w2_ref, b2_ref, w3_ref, b3_ref, fr_ref, wo_ref, dl_ref,
                   fs_ref, asum_ref, *, T, nemb):
    i = pl.program_id(0)
    fr = fr_ref[...]
    hd = lambda a, b: jnp.dot(a, b, precision=HIGHEST, preferred_element_type=F32)
    zz = zz_ref[...]
    h = jnp.sin(fr * (hd(zz, w1_ref[...]) + b1_ref[...]))
    h = jnp.sin(fr * (hd(h, w2_ref[...]) + b2_ref[...]))
    h = jnp.sin(fr * (hd(h, w3_ref[...]) + b3_ref[...]))
    k = hd(h, wo_ref[...])
    dl = dl_ref[...]
    dec_f = jnp.exp(-zz[:, 0:1] * dl)
    dec_r = jnp.exp(-zz[:, nemb:nemb + 1] * dl)
    C = HY_WIDTH
    first_row = (i * T + lax.broadcasted_iota(jnp.int32, (T, 1), 0)) == 0

    @pl.when(i == 0)
    def _():
        asum_ref[...] = jnp.zeros(asum_ref.shape, F32)

    for o in range(HY_ORDER):
        base = o * 2 * C
        kf = k[:, base:base + C] * dec_f
        kb_here = k[:, base + C:base + 2 * C] * dec_f
        kb_rev = k[:, HY_ORDER * 2 * C + o * C:HY_ORDER * 2 * C + (o + 1) * C] * dec_r
        first = kf + jnp.where(first_row, kb_here, 0.0)
        second = jnp.where(first_row, 0.0, kb_rev)
        fs_ref[o] = first
        fs_ref[HY_ORDER + o] = second
        asum_ref[o:o + 1, :] += jnp.sum(jnp.abs(first) + jnp.abs(second), axis=0, keepdims=True)


def _hyena_filter_time(L, w1, b1, w2, b2, w3, b3, freq, wout):
    zz = _filter_positions(L)
    T = _row_tile(L)
    C = HY_WIDTH
    nemb = zz.shape[1] // 2
    nf = w2.shape[0]
    deltas = jnp.asarray(np.linspace(HY_MIN_DECAY, HY_MAX_DECAY, C)[None, :], F32)
    twice = lambda w: jnp.kron(jnp.eye(2, dtype=w.dtype), w)
    both = lambda v: jnp.tile(v, 2).reshape(1, 2 * nf)
    w_back = wout.reshape(nf, HY_ORDER, 2, C)[:, :, 1, :].reshape(nf, HY_ORDER * C)
    wo = jnp.concatenate([jnp.concatenate([wout, jnp.zeros((nf, HY_ORDER * C), wout.dtype)], axis=1),
                          jnp.concatenate([jnp.zeros_like(wout), w_back], axis=1)], axis=0)
    const = lambda i: (0, 0)
    return pl.pallas_call(
        functools.partial(_filter_kernel, T=T, nemb=nemb),
        out_shape=(jax.ShapeDtypeStruct((2 * HY_ORDER, L, C), F32), jax.ShapeDtypeStruct((HY_ORDER, C), F32)),
        grid=(L // T,),
        in_specs=[
            pl.BlockSpec((T, 2 * nemb), lambda i: (i, 0)),
            pl.BlockSpec((2 * nemb, 2 * nf), const), pl.BlockSpec((1, 2 * nf), const),
            pl.BlockSpec((2 * nf, 2 * nf), const), pl.BlockSpec((1, 2 * nf), const),
            pl.BlockSpec((2 * nf, 2 * nf), const), pl.BlockSpec((1, 2 * nf), const),
            pl.BlockSpec((1, 2 * nf), const),
            pl.BlockSpec(wo.shape, const),
            pl.BlockSpec((1, C), const),
        ],
        out_specs=(pl.BlockSpec((2 * HY_ORDER, T, C), lambda i: (0, i, 0)),
                   pl.BlockSpec((HY_ORDER, C), const)),
        compiler_params=_cparams(("arbitrary",)),
        name="hy_filter_mlp",
    )(zz, twice(w1), both(b1), twice(w2), both(b2), twice(w3), both(b3), both(freq), wo, deltas)


def _dft_tables(L):
    N = 2 * L
    N2 = DFT_N2
    N1 = N // N2
    K1 = N1 // 2
    kept = K1 + 1
    rows = -(-kept // SUBLANES) * SUBLANES
    live = (np.arange(rows) < kept).astype(np.float64)[None, :, None]
    pair = np.where((np.arange(rows) == 0) | (np.arange(rows) == K1), 1.0, 2.0)[None, :, None] * live
    n1 = np.arange(K1)[None, None, :]
    k1 = np.arange(rows)[None, :, None]
    n2 = np.arange(N2)[:, None, None]
    ang = 2.0 * np.pi * (((n1 * N2 + n2) * k1) % N) / N
    g_fwd = np.concatenate([np.cos(ang) * live, -np.sin(ang) * live], axis=1)
    t = lambda a: np.transpose(a, (0, 2, 1))
    g_inv = np.concatenate([t(np.cos(ang) * pair), t(-np.sin(ang) * pair)], axis=2) / N
    a2 = 2.0 * np.pi * ((np.arange(N2)[:, None] * np.arange(N2)[None, :]) % N2) / N2
    cr, ci = np.cos(a2), -np.sin(a2)
    m2 = np.block([[cr, -ci], [ci, cr]])
    m2i = np.block([[cr, ci], [-ci, cr]])
    as_bf = lambda a: jnp.asarray(a, BF16)
    return as_bf(g_fwd), as_bf(g_inv), as_bf(m2), as_bf(m2i)


DFT_NB = 16
N_HALF = HY_WIDTH // LANES


def _lane_half_specs(k1n, nb, part):
    return [pl.BlockSpec((1, k1n, nb, LANES), lambda s, j, h=h: (s, 0, j, part * N_HALF + h)) for h in range(N_HALF)]


def _major_half_specs(k1n, nb):
    return [pl.BlockSpec((1, 1, k1n, nb, LANES), lambda s, j, h=h: (s, h, 0, j, 0)) for h in range(N_HALF)]


def _ld_time(ref, j, nb):
    k1n = ref.shape[-3]
    return ref.reshape(k1n * nb, LANES)[pl.ds(j, k1n, stride=nb), :]


def _st_time(ref, h, j, nb, val):
    k1n = ref.shape[2]
    ref.reshape(N_HALF * k1n * nb, LANES)[pl.ds(h * k1n * nb + j, k1n, stride=nb), :] = val


U32 = jnp.uint32


def _pack_halves(x):
    lo = lax.bitcast_convert_type(x[:, :LANES], U32)
    hi = lax.bitcast_convert_type(x[:, LANES:], U32)
    rnd = jnp.uint32(0x8000)
    return ((lo + rnd) >> 16) | ((hi + rnd) & jnp.uint32(0xFFFF0000))


def _unpack_halves(w):
    lo = lax.bitcast_convert_type(w << 16, F32)
    hi = lax.bitcast_convert_type(w & jnp.uint32(0xFFFF0000), F32)
    return jnp.concatenate([lo, hi], axis=1)


def _dft_a_kernel(x0_ref, x1_ref, g_ref, o_ref, *, nb):
    n1 = o_ref.shape[3]
    for j in range(nb):
        parts = []
        for x_ref in (x0_ref, x1_ref):
            parts.append(_ld_time(x_ref, j, nb))
        xj = jnp.concatenate(parts, axis=1).astype(BF16)
        r = _pack_halves(_dot(g_ref[j], xj))
        o_ref[0, 0, j] = r[:n1]
        o_ref[0, 1, j] = r[n1:]


def _dft_stage_a(x, x_specs, g_fwd):
    S = x.shape[0]
    N2, two_n1, K1 = g_fwd.shape
    N1 = two_n1 // 2
    nb = DFT_NB
    return pl.pallas_call(
        functools.partial(_dft_a_kernel, nb=nb),
        out_shape=jax.ShapeDtypeStruct((S, 2, N2, N1, LANES), U32),
        grid=(S, N2 // nb),
        in_specs=x_specs + [pl.BlockSpec((nb, 2 * N1, K1), lambda s, j: (j, 0, 0))],
        out_specs=pl.BlockSpec((1, 2, nb, N1, LANES), lambda s, j: (s, 0, j, 0, 0)),
        compiler_params=_cparams(("parallel", "parallel")),
        name="hy_dft_a",
    )(x, x, g_fwd)


def _k1_rows(ref, j):
    _, _, n2, tk1, _ = ref.shape
    return ref.reshape(2 * n2 * tk1, LANES), pl.ds(j, 2 * n2, stride=tk1)


def _ld_k1(ref, j):
    r2, rows = _k1_rows(ref, j)
    return r2[rows, :]


def _st_k1(ref, j, val):
    r2, rows = _k1_rows(ref, j)
    r2[rows, :] = val


def _dft_b_filter_kernel(s1_ref, s2_ref, m2_ref, asum_ref, o_ref, *, tk1):
    n2 = s1_ref.shape[2]
    C = o_ref.shape[4]
    scale = 1.0 / asum_ref[0]
    for j in range(tk1):
        sign = 1.0 if j % 2 == 0 else -1.0
        s = _unpack_halves(_ld_k1(s1_ref, j)) + sign * _unpack_halves(_ld_k1(s2_ref, j))
        xk = _dot(m2_ref[...], s.astype(BF16)) * scale
        o_ref[0, j] = xk.reshape(2, n2, C)


def _dft_stage_b_filter(sa, m2, asum):
    _, _, N2, N1, _ = sa.shape
    C = HY_WIDTH
    tk1 = SUBLANES
    blk = (1, 2, N2, tk1, LANES)
    return pl.pallas_call(
        functools.partial(_dft_b_filter_kernel, tk1=tk1),
        out_shape=jax.ShapeDtypeStruct((HY_ORDER, N1, 2, N2, C), F32),
        grid=(HY_ORDER, N1 // tk1),
        in_specs=[
            pl.BlockSpec(blk, lambda o, i: (o, 0, 0, i, 0)),
            pl.BlockSpec(blk, lambda o, i: (HY_ORDER + o, 0, 0, i, 0)),
            pl.BlockSpec((2 * N2, 2 * N2), lambda o, i: (0, 0)),
            pl.BlockSpec((1, 1, C), lambda o, i: (o, 0, 0)),
        ],
        out_specs=pl.BlockSpec((1, tk1, 2, N2, C), lambda o, i: (o, i, 0, 0, 0)),
        compiler_params=_cparams(("parallel", "parallel")),
        name="hy_dft_b_filter",
    )(sa, sa, m2, asum.reshape(HY_ORDER, 1, C))


def _dft_b_conv_kernel(s_ref, kf_ref, m2_ref, m2i_ref, o_ref, *, tk1):
    n2 = s_ref.shape[2]
    for j in range(tk1):
        x = _dot(m2_ref[...], _unpack_halves(_ld_k1(s_ref, j)).astype(BF16))
        xr, xi = x[:n2], x[n2:]
        kr = kf_ref[0, j, 0]
        ki = kf_ref[0, j, 1]
        y = jnp.concatenate([xr * kr - xi * ki, xr * ki + xi * kr], axis=0).astype(BF16)
        _st_k1(o_ref, j, _pack_halves(_dot(m2i_ref[...], y)))


def _dft_stage_b_conv(sa, kspec, order, m2, m2i):
    S, _, N2, N1, _ = sa.shape
    C = HY_WIDTH
    tk1 = SUBLANES
    blk = (1, 2, N2, tk1, LANES)
    return pl.pallas_call(
        functools.partial(_dft_b_conv_kernel, tk1=tk1),
        out_shape=jax.ShapeDtypeStruct(sa.shape, U32),
        grid=(N1 // tk1, S),
        in_specs=[
            pl.BlockSpec(blk, lambda i, s: (s, 0, 0, i, 0)),
            pl.BlockSpec((1, tk1, 2, N2, C), lambda i, s: (order, i, 0, 0, 0)),
            pl.BlockSpec((2 * N2, 2 * N2), lambda i, s: (0, 0)),
            pl.BlockSpec((2 * N2, 2 * N2), lambda i, s: (0, 0)),
        ],
        out_specs=pl.BlockSpec(blk, lambda i, s: (s, 0, 0, i, 0)),
        compiler_params=_cparams(("parallel", "parallel")),
        name="hy_dft_b_conv",
    )(sa, kspec, m2, m2i)


def _dft_c_kernel(c_ref, gi_ref, z0_ref, z1_ref, g0_ref, g1_ref, d_ref, o_ref, *, nb):
    d = d_ref[0]
    for j in range(nb):
        cat = _unpack_halves(jnp.concatenate([c_ref[0, 0, j], c_ref[0, 1, j]], axis=0))
        y = _dot(gi_ref[j], cat.astype(BF16))
        for h, (z_ref, g_ref) in enumerate(((z0_ref, g0_ref), (z1_ref, g1_ref))):
            lanes = slice(h * LANES, (h + 1) * LANES)
            _st_time(o_ref, h, j, nb, _ld_time(g_ref, j, nb) * (y[:, lanes] + _ld_time(z_ref, j, nb) * d[:, lanes]))


def _dft_stage_c(sc, g_inv, z, z_specs, gate, gate_specs, d):
    S, _, N2, N1, _ = sc.shape
    C = HY_WIDTH
    K1 = g_inv.shape[1]
    nb = DFT_NB
    return pl.pallas_call(
        functools.partial(_dft_c_kernel, nb=nb),
        out_shape=jax.ShapeDtypeStruct((S, N_HALF, K1, N2, LANES), F32),
        grid=(S, N2 // nb),
        in_specs=[
            pl.BlockSpec((1, 2, nb, N1, LANES), lambda s, j: (s, 0, j, 0, 0)),
            pl.BlockSpec((nb, K1, 2 * N1), lambda s, j: (j, 0, 0)),
        ] + z_specs + gate_specs + [pl.BlockSpec((1, 1, C), lambda s, j: (0, 0, 0))],
        out_specs=pl.BlockSpec((1, N_HALF, K1, nb, LANES), lambda s, j: (s, 0, 0, j, 0)),
        compiler_params=_cparams(("parallel", "parallel")),
        name="hy_dft_c",
    )(sc, g_inv, z, z, gate, gate, d.reshape(1, 1, C))


def _hyena(hc, w1, b1, w2, b2, w3, b3, freq, wout, bias, tables):
    B, L, _ = hc.shape
    C = HY_WIDTH
    g_fwd, g_inv, m2, m2i = tables
    N2 = DFT_N2
    K1 = L // N2
    nb = DFT_NB
    fs, asum = _hyena_filter_time(L, w1, b1, w2, b2, w3, b3, freq, wout)
    fa = _dft_stage_a(fs.reshape(2 * HY_ORDER, K1, N2, C), _lane_half_specs(K1, nb, 0), g_fwd)
    kspec = _dft_stage_b_filter(fa, m2, asum)
    hc4 = hc.reshape(B, K1, N2, (HY_ORDER + 1) * C)
    z, z_specs = hc4, _lane_half_specs(K1, nb, 0)
    for o in range(HY_ORDER):
        sa = _dft_stage_a(z, z_specs, g_fwd)
        sc = _dft_stage_b_conv(sa, kspec, o, m2, m2i)
        z = _dft_stage_c(sc, g_inv, z, z_specs, hc4, _lane_half_specs(K1, nb, 1 + o), bias[o])
        z_specs = _major_half_specs(K1, nb)
    return z.reshape(B, N_HALF, L, LANES)


def _outproj_kernel(x_ref, at_ref, h_ref, gr_ref, zy_ref, w_ref, gt_ref, g_ref, b_ref, o_ref, *, alpha):
    lru = (h_ref[0, 0].astype(F32) + h_ref[1, 0].astype(F32)) * jax.nn.gelu(gr_ref[0])
    na = IN_Q
    at = at_ref[0].reshape(na, at_ref.shape[-1])
    m = lax.dot_general(at, w_ref[0:na], (((0,), (0,)), ((), ())), preferred_element_type=F32)
    rest = jnp.concatenate([lru.astype(BF16)] + [zy_ref[0, h].astype(BF16) for h in range(N_HALF)], axis=1)
    m = m + _dot(rest, w_ref[na:])
    o_ref[0] = _layer_norm(alpha * x_ref[0] + gt_ref[0] * m, g_ref[...], b_ref[...])


def _out_proj(x, attn, h, xg, zy, w_out, gt, ln_g, ln_b, alpha):
    B, L, D = x.shape
    T = _row_tile(L)
    C = LRU_WIDTH
    row = lambda b, i: (b, i, 0)
    vec = lambda b, i: (b, 0, 0)
    const = lambda b, i: (0, 0)
    return pl.pallas_call(
        functools.partial(_outproj_kernel, alpha=alpha),
        out_shape=jax.ShapeDtypeStruct((B, L, D), F32),
        grid=(B, L // T),
        in_specs=[
            pl.BlockSpec((1, T, D), row),
            pl.BlockSpec((1, N_KV_HEADS, KV_GROUP, HEAD_DIM, T), lambda b, i: (b, 0, 0, 0, i)),
            pl.BlockSpec((2, 1, T, C), lambda b, i: (0, b, i, 0)),
            pl.BlockSpec((1, T, C), lambda b, i: (b, i, 1)),
            pl.BlockSpec((1, N_HALF, T, LANES), lambda b, i: (b, 0, i, 0)),
            pl.BlockSpec(w_out.shape, const),
            pl.BlockSpec((1, 1, D), vec),
            pl.BlockSpec((1, D), const),
            pl.BlockSpec((1, D), const),
        ],
        out_specs=pl.BlockSpec((1, T, D), row),
        compiler_params=_cparams(("parallel", "parallel")),
        name="out_proj",
    )(x, attn, h, xg, zy, w_out, gt, ln_g.reshape(1, D), ln_b.reshape(1, D))


FFN_CHUNK = 256


def _ffn_kernel(xp_ref, x_ref, xn_ref, sc_ref, sh_ref, gt_ref, wu_ref, cw_ref, cb_ref, wd_ref, g_ref, b_ref,
                o_ref, u_sc, h_sc, acc_sc, *, T, nt, nf, alpha):
    i = pl.program_id(1)
    H = SUBLANES
    R = T + 2 * H
    F = FFN_CHUNK
    sc = 1.0 + sc_ref[0]
    sh = sh_ref[0]
    u_sc[H:H + T, :] = (x_ref[0] * sc + sh).astype(BF16)
    u_sc[0:H, :] = jnp.where(i > 0, xp_ref[0] * sc + sh, 0.0).astype(BF16)
    u_sc[H + T:R, :] = jnp.where(i < nt - 1, xn_ref[0] * sc + sh, 0.0).astype(BF16)
    acc_sc[...] = jnp.zeros(acc_sc.shape, F32)

    def up(f, slot):
        h_sc[slot] = _dot(u_sc[...], wu_ref[f])

    def down(f, slot):
        h = h_sc[slot]
        cw = cw_ref[f]
        y = cb_ref[f] + pltpu.roll(h, 1, 0) * cw[0:1] + h * cw[1:2] + pltpu.roll(h, R - 1, 0) * cw[2:3]
        y = y[H:H + T]
        act = (jax.nn.gelu(y[:, :F]) * y[:, F:]).astype(BF16)
        acc_sc[...] += _dot(act, wd_ref[f])

    def pair(c2, carry):
        f = 2 * c2
        up(f + 1, 1)
        down(f, 0)
        up(jnp.minimum(f + 2, nf - 1), 0)
        down(f + 1, 1)
        return carry

    up(0, 0)
    lax.fori_loop(0, nf // 2, pair, 0)
    if nf % 2 == 1:
        down(nf - 1, 0)
    o_ref[0] = _layer_norm(alpha * x_ref[0] + gt_ref[0] * acc_sc[...], g_ref[...], b_ref[...])


def _conv_ffn(x, sc, sh, gt, w_up, conv_w, conv_b, w_down, ln_g, ln_b, alpha):
    B, L, D = x.shape
    d_ff = w_down.shape[0]
    T = _row_tile(L)
    F = FFN_CHUNK
    nt = L // T
    nf = d_ff // F
    hb = T // SUBLANES
    n_halo = L // SUBLANES
    nk = conv_w.shape[0]
    chunked = lambda w: jnp.concatenate([w[..., :d_ff].reshape(w.shape[0], nf, F),
                                         w[..., d_ff:].reshape(w.shape[0], nf, F)], axis=-1).transpose(1, 0, 2)
    wu = chunked(w_up)
    cw = chunked(conv_w)
    cb = chunked(conv_b.reshape(1, 2 * d_ff))
    wd = w_down.reshape(nf, F, D)
    row = lambda b, i: (b, i, 0)
    vec = lambda b, i: (b, 0, 0)
    const2 = lambda b, i: (0, 0)
    const3 = lambda b, i: (0, 0, 0)
    resident = lambda shape, imap: pl.BlockSpec(shape, imap, pipeline_mode=pl.Buffered(1))
    return pl.pallas_call(
        functools.partial(_ffn_kernel, T=T, nt=nt, nf=nf, alpha=alpha),
        out_shape=jax.ShapeDtypeStruct((B, L, D), F32),
        grid=(B, nt),
        in_specs=[
            pl.BlockSpec((1, SUBLANES, D), lambda b, i: (b, jnp.maximum(i * hb - 1, 0), 0)),
            pl.BlockSpec((1, T, D), row),
            pl.BlockSpec((1, SUBLANES, D), lambda b, i: (b, jnp.minimum((i + 1) * hb, n_halo - 1), 0)),
            pl.BlockSpec((1, 1, D), vec),
            pl.BlockSpec((1, 1, D), vec),
            pl.BlockSpec((1, 1, D), vec),
            resident((nf, D, 2 * F), const3),
            resident((nf, nk, 2 * F), const3),
            resident((nf, 1, 2 * F), const3),
            resident((nf, F, D), const3),
            pl.BlockSpec((1, D), const2),
            pl.BlockSpec((1, D), const2),
        ],
        out_specs=pl.BlockSpec((1, T, D), row),
        scratch_shapes=[
            pltpu.VMEM((T + 2 * SUBLANES, D), BF16),
            pltpu.VMEM((2, T + 2 * SUBLANES, 2 * F), F32),
            pltpu.VMEM((T, D), F32),
        ],
        compiler_params=_cparams(("parallel", "parallel")),
        name="conv_ffn",
    )(x, x, x, sc, sh, gt, wu, cw, cb, wd, ln_g.reshape(1, D), ln_b.reshape(1, D))


def _trunk(x, mod, p):
    B, L, D = x.shape
    depth = mod.shape[0]
    alpha = (2 * depth) ** 0.25
    rope = _rope_tables(L)
    tables = _dft_tables(L)
    for l in range(depth):
        m6 = mod[l].reshape(B, 6, 1, D)
        sh1, sc1, gt1, sh2, sc2, gt2 = (m6[:, j] for j in range(6))
        q, k, v, xg, hc = _in_proj(x, sc1, sh1, p['w_in'][l], p['q_gain'][l], p['k_gain'][l], rope,
                                   p['hy_conv_w'][l], p['hy_conv_b'][l])
        attn = _attention(q, k, v)
        h = _rglru(xg, p['lru_conv_w'][l], p['lru_conv_b'][l], p['lru_wa'][l], p['lru_ba'][l],
                   p['lru_wx'][l], p['lru_bx'][l], p['lru_lambda'][l])
        zy = _hyena(hc, p['hy_w1'][l], p['hy_b1'][l], p['hy_w2'][l], p['hy_b2'][l], p['hy_w3'][l], p['hy_b3'][l],
                    p['hy_freq'][l], p['hy_wout'][l], p['hy_bias'][l], tables)
        x = _out_proj(x, attn, h, xg, zy, p['w_out'][l], gt1, p['ln1_g'][l], p['ln1_b'][l], alpha)
        x = _conv_ffn(x, sc2, sh2, gt2, p['ffn_w_up'][l], p['ffn_conv_w'][l], p['ffn_conv_b'][l],
                      p['ffn_w_down'][l], p['ln2_g'][l], p['ln2_b'][l], alpha)
    return x


def kernel(x_prompt, x_sample, c_prompt, c_sample, ada_w, ada_b, w_in, q_gain, k_gain, lru_conv_w, lru_conv_b, lru_wa, lru_ba, lru_wx, lru_bx, lru_lambda, hy_conv_w, hy_conv_b, hy_w1, hy_b1, hy_w2, hy_b2, hy_w3, hy_b3, hy_freq, hy_wout, hy_bias, w_out, ln1_g, ln1_b, ffn_w_up, ffn_conv_w, ffn_conv_b, ffn_w_down, ln2_g, ln2_b):
    p = dict(
        w_in=w_in.astype(BF16), q_gain=q_gain, k_gain=k_gain, lru_conv_w=lru_conv_w, lru_conv_b=lru_conv_b,
        lru_wa=lru_wa, lru_ba=lru_ba, lru_wx=lru_wx, lru_bx=lru_bx, lru_lambda=lru_lambda,
        hy_conv_w=hy_conv_w, hy_conv_b=hy_conv_b, hy_w1=hy_w1, hy_b1=hy_b1, hy_w2=hy_w2, hy_b2=hy_b2,
        hy_w3=hy_w3, hy_b3=hy_b3, hy_freq=hy_freq, hy_wout=hy_wout, hy_bias=hy_bias,
        w_out=w_out.astype(BF16), ln1_g=ln1_g, ln1_b=ln1_b, ffn_w_up=ffn_w_up.astype(BF16),
        ffn_conv_w=ffn_conv_w, ffn_conv_b=ffn_conv_b, ffn_w_down=ffn_w_down.astype(BF16), ln2_g=ln2_g, ln2_b=ln2_b,
    )
    nb = x_prompt.shape[0]
    mod = _ada_mod(jnp.concatenate([c_prompt, c_sample], axis=0), ada_w, ada_b)
    y_prompt = _trunk(x_prompt, mod[:, :nb], p)
    y_sample = _trunk(x_sample, mod[:, nb:], p)
    return (y_prompt, y_sample)
```

```python
import functools
import math

import numpy as np
import jax
import jax.numpy as jnp
from jax import lax
from jax.experimental import pallas as pl
from jax.experimental.pallas import tpu as pltpu

F32 = jnp.float32
BF16 = jnp.bfloat16
HIGHEST = lax.Precision.HIGHEST

GRID_W = 64
HEAD_DIM = 64
N_HEADS = 8
N_KV_HEADS = 2
KV_GROUP = N_HEADS // N_KV_HEADS
ROPE_THETA = 10000.0
ROPE_FREQS = HEAD_DIM // 4
QK_EPS = 1e-6
LRU_WIDTH = 256
LRU_HEADS = 4
LRU_C = 8.0
HY_WIDTH = 256
HY_ORDER = 2
HY_BANDS = 16
HY_MIN_DECAY = abs(math.log(1e-2)) / 1.5
HY_MAX_DECAY = abs(math.log(1e-2)) / 0.3
LN_EPS = 1e-5
IN_Q = N_HEADS * HEAD_DIM
IN_KV = N_KV_HEADS * HEAD_DIM

LANES = 128
SUBLANES = 8
V7X_VMEM_BYTES = 64 * 1024 * 1024
VMEM_LIMIT = V7X_VMEM_BYTES * 3 // 4
DFT_N2 = 128

ROW_TILE = 1024
LRU_TILE = 256
ATTN_Q_TILE = 512
ATTN_K_CHUNK = 512


def _row_tile(L):
    return min(ROW_TILE, L)


def _cparams(sem):
    return pltpu.CompilerParams(dimension_semantics=sem, vmem_limit_bytes=VMEM_LIMIT)


def _dot(a, b):
    return jnp.dot(a, b, preferred_element_type=F32)


def _layer_norm(y, g, b):
    mu = jnp.mean(y, axis=-1, keepdims=True)
    yc = y - mu
    var = jnp.mean(yc * yc, axis=-1, keepdims=True)
    return yc * lax.rsqrt(var + LN_EPS) * g + b


def _ada_kernel(c_ref, w_ref, b_ref, o_ref):
    c = c_ref[...]
    s = c * jax.nn.sigmoid(c)
    o_ref[0] = jnp.dot(s, w_ref[0], precision=HIGHEST, preferred_element_type=F32) + b_ref[0]


def _ada_mod(c_all, ada_w, ada_b):
    depth, d, n = ada_w.shape
    rows = c_all.shape[0]
    tn = 768
    return pl.pallas_call(
        _ada_kernel,
        out_shape=jax.ShapeDtypeStruct((depth, rows, n), F32),
        grid=(depth, n // tn),
        in_specs=[
            pl.BlockSpec((rows, d), lambda l, j: (0, 0)),
            pl.BlockSpec((1, d, tn), lambda l, j: (l, 0, j)),
            pl.BlockSpec((1, 1, tn), lambda l, j: (l, 0, j)),
        ],
        out_specs=pl.BlockSpec((1, rows, tn), lambda l, j: (l, 0, j)),
        compiler_params=_cparams(("parallel", "parallel")),
        name="ada_mod",
    )(c_all, ada_w, ada_b.reshape(depth, 1, n))


def _rope_tables(L):
    rows = L // GRID_W
    row = np.repeat(np.arange(rows, dtype=np.float64), GRID_W)
    col = np.tile(np.arange(GRID_W, dtype=np.float64), rows)
    inv = ROPE_THETA ** (-np.arange(ROPE_FREQS, dtype=np.float64) / ROPE_FREQS)
    ar = row[:, None] * inv
    ac = col[:, None] * inv
    zeros = np.zeros_like(ar)
    cos = np.concatenate([np.cos(ar), np.cos(ar), np.cos(ac), np.cos(ac)], axis=1)
    sin_up = np.concatenate([-np.sin(ar), zeros, -np.sin(ac), zeros], axis=1)
    sin_dn = np.concatenate([zeros, np.sin(ar), zeros, np.sin(ac)], axis=1)
    two = lambda t: jnp.asarray(np.concatenate([t, t], axis=1), F32)
    return two(cos), two(sin_up), two(sin_dn)


def _inproj_kernel(xp_ref, x_ref, xn_ref, sc_ref, sh_ref, w_ref, qg_ref, kg_ref, cos_ref, sup_ref, sdn_ref, bd_ref,
                   hcw_ref, hcb_ref, q_ref, k_ref, vt_ref, xg_ref, hc_ref, *, T, nt):
    i = pl.program_id(1)
    H = SUBLANES
    R = T + 2 * H
    sc = 1.0 + sc_ref[0]
    sh = sh_ref[0]
    u_ext = jnp.concatenate([jnp.where(i > 0, xp_ref[0] * sc + sh, 0.0), x_ref[0] * sc + sh,
                             jnp.where(i < nt - 1, xn_ref[0] * sc + sh, 0.0)], axis=0).astype(BF16)
    proj_ext = _dot(u_ext, w_ref[...])
    proj = proj_ext[H:H + T]
    cos = cos_ref[...]
    sup = sup_ref[...]
    sdn = sdn_ref[...]
    bd = bd_ref[...]
    half = ROPE_FREQS

    def norm_rope(t, gain):
        ms = _dot((t * t).astype(BF16), bd)
        tn = t * lax.rsqrt(ms + QK_EPS) * gain
        return (tn * cos + pltpu.roll(tn, LANES - half, 1) * sup + pltpu.roll(tn, half, 1) * sdn)

    qg = qg_ref[...]
    for j in range(IN_Q // LANES):
        sl = slice(j * LANES, (j + 1) * LANES)
        q_ref[0, :, sl] = (norm_rope(proj[:, sl], qg) * (HEAD_DIM ** -0.5 * math.log2(math.e))).astype(BF16)
    kn = norm_rope(proj[:, IN_Q:IN_Q + IN_KV], kg_ref[...])
    v_t = proj[:, IN_Q + IN_KV:IN_Q + 2 * IN_KV].T
    tk = vt_ref.shape[-1]
    tail = (lax.broadcasted_iota(jnp.int32, (V_ROWS - HEAD_DIM, tk), 0) == 0).astype(BF16)
    for g in range(N_KV_HEADS):
        k_ref[0, g] = kn[:, g * HEAD_DIM:(g + 1) * HEAD_DIM].astype(BF16)
        for c in range(T // tk):
            vt_ref[0, g, c, 0:HEAD_DIM, :] = v_t[g * HEAD_DIM:(g + 1) * HEAD_DIM, c * tk:(c + 1) * tk].astype(BF16)
            vt_ref[0, g, c, HEAD_DIM:, :] = tail
    o = IN_Q + 2 * IN_KV
    xg_ref[0] = proj[:, o:o + 2 * LRU_WIDTH]
    hy = proj_ext[:, o + 2 * LRU_WIDTH:]
    cw = hcw_ref[...]
    hc = hcb_ref[...] + pltpu.roll(hy, 1, 0) * cw[0:1] + hy * cw[1:2] + pltpu.roll(hy, R - 1, 0) * cw[2:3]
    hc_ref[0] = hc[H:H + T]


def _in_proj(x, sc, sh, w_in, q_gain, k_gain, rope, hy_conv_w, hy_conv_b):
    B, L, D = x.shape
    n_in = w_in.shape[1]
    T = _row_tile(L)
    nt = L // T
    hb = T // SUBLANES
    n_halo = L // SUBLANES
    _, tk = _attn_tiles(L)
    cos, sup, sdn = rope
    bd = jnp.asarray(np.kron(np.eye(2), np.full((HEAD_DIM, HEAD_DIM), 1.0 / HEAD_DIM)), BF16)
    qg = jnp.tile(q_gain, 2).reshape(1, LANES)
    kg = jnp.tile(k_gain, 2).reshape(1, LANES)
    n_hy = n_in - IN_Q - 2 * IN_KV - 2 * LRU_WIDTH
    row = lambda b, i: (b, i, 0)
    vec = lambda b, i: (b, 0, 0)
    tab = lambda b, i: (i, 0)
    const = lambda b, i: (0, 0)
    return pl.pallas_call(
        functools.partial(_inproj_kernel, T=T, nt=nt),
        out_shape=(
            jax.ShapeDtypeStruct((B, L, IN_Q), BF16),
            jax.ShapeDtypeStruct((B, N_KV_HEADS, L, HEAD_DIM), BF16),
            jax.ShapeDtypeStruct((B, N_KV_HEADS, L // tk, V_ROWS, tk), BF16),
            jax.ShapeDtypeStruct((B, L, 2 * LRU_WIDTH), F32),
            jax.ShapeDtypeStruct((B, L, n_hy), F32),
        ),
        grid=(B, nt),
        in_specs=[
            pl.BlockSpec((1, SUBLANES, D), lambda b, i: (b, jnp.maximum(i * hb - 1, 0), 0)),
            pl.BlockSpec((1, T, D), row),
            pl.BlockSpec((1, SUBLANES, D), lambda b, i: (b, jnp.minimum((i + 1) * hb, n_halo - 1), 0)),
            pl.BlockSpec((1, 1, D), vec),
            pl.BlockSpec((1, 1, D), vec),
            pl.BlockSpec((D, n_in), const),
            pl.BlockSpec((1, LANES), const),
            pl.BlockSpec((1, LANES), const),
            pl.BlockSpec((T, LANES), tab),
            pl.BlockSpec((T, LANES), tab),
            pl.BlockSpec((T, LANES), tab),
            pl.BlockSpec((LANES, LANES), const),
            pl.BlockSpec(hy_conv_w.shape, const),
            pl.BlockSpec((1, n_hy), const),
        ],
        out_specs=(
            pl.BlockSpec((1, T, IN_Q), row),
            pl.BlockSpec((1, N_KV_HEADS, T, HEAD_DIM), lambda b, i: (b, 0, i, 0)),
            pl.BlockSpec((1, N_KV_HEADS, T // tk, V_ROWS, tk), lambda b, i: (b, 0, i, 0, 0)),
            pl.BlockSpec((1, T, 2 * LRU_WIDTH), row),
            pl.BlockSpec((1, T, n_hy), row),
        ),
        compiler_params=_cparams(("parallel", "parallel")),
        name="in_proj",
    )(x, x, x, sc, sh, w_in, qg, kg, cos, sup, sdn, bd, hy_conv_w, hy_conv_b.reshape(1, n_hy))


V_ROWS = HEAD_DIM + 16


def _attn_kernel(qt_ref, k_ref, vt_ref, o_ref, acc_sc, s_sc, *, tk, nk):
    tq = qt_ref.shape[-1]
    acc_sc[...] = jnp.zeros(acc_sc.shape, F32)

    def scores_h(c, slot, h):
        kc = k_ref[0, 0, pl.ds(pl.multiple_of(c * tk, tk), tk), :]
        s = _dot(kc, qt_ref[0, 0, h])
        s_sc[slot, h] = s
        return jnp.max(s, axis=0, keepdims=True)

    def consume_h(c, slot, h, m_prev, m_chunk):
        vc = vt_ref[0, 0, c]
        m_new = jnp.maximum(m_prev, m_chunk)
        p = jnp.exp2(s_sc[slot, h] - m_new).astype(BF16)
        alpha = jnp.exp2(m_prev - m_new)
        acc_sc[h] = alpha * acc_sc[h] + _dot(vc, p)
        return m_new

    def step(c, c_next, slot, carry, lookahead=True):
        ms, mc = carry
        new_m, new_c = [], []
        for h in range(KV_GROUP):
            if lookahead:
                new_c.append(scores_h(c_next, 1 - slot, h))
            new_m.append(consume_h(c, slot, h, ms[h], mc[h]))
        return tuple(new_m), tuple(new_c)

    def pair(c2, carry):
        c = 2 * c2
        carry = step(c, c + 1, 0, carry)
        return step(c + 1, c + 2, 1, carry)

    m0 = tuple(jnp.full((1, tq), -jnp.inf, F32) for _ in range(KV_GROUP))
    c0 = tuple(scores_h(0, 0, h) for h in range(KV_GROUP))
    carry = lax.fori_loop(0, nk // 2 - 1, pair, (m0, c0))
    carry = step(nk - 2, nk - 1, 0, carry)
    step(nk - 1, None, 1, carry, lookahead=False)
    for h in range(KV_GROUP):
        acc = acc_sc[h]
        o_ref[0, 0, h] = (acc[:HEAD_DIM] / acc[HEAD_DIM:HEAD_DIM + 1]).astype(o_ref.dtype)


def _attn_tiles(L):
    return min(ATTN_Q_TILE, L), min(ATTN_K_CHUNK, L // 2)


def _attention(q, kh, vt):
    B, L, _ = q.shape
    tq, tk = _attn_tiles(L)
    nk = L // tk
    assert nk % 2 == 0
    qt = q.reshape(B, L, N_KV_HEADS, KV_GROUP, HEAD_DIM).transpose(0, 2, 3, 4, 1)
    return pl.pallas_call(
        functools.partial(_attn_kernel, tk=tk, nk=nk),
        out_shape=jax.ShapeDtypeStruct((B, N_KV_HEADS, KV_GROUP, HEAD_DIM, L), BF16),
        grid=(B, N_KV_HEADS, L // tq),
        in_specs=[
            pl.BlockSpec((1, 1, KV_GROUP, HEAD_DIM, tq), lambda b, g, i: (b, g, 0, 0, i)),
            pl.BlockSpec((1, 1, L, HEAD_DIM), lambda b, g, i: (b, g, 0, 0)),
            pl.BlockSpec((1, 1, nk, V_ROWS, tk), lambda b, g, i: (b, g, 0, 0, 0)),
        ],
        out_specs=pl.BlockSpec((1, 1, KV_GROUP, HEAD_DIM, tq), lambda b, g, i: (b, g, 0, 0, i)),
        scratch_shapes=[pltpu.VMEM((KV_GROUP, V_ROWS, tq), F32), pltpu.VMEM((2, KV_GROUP, tk, tq), F32)],
        compiler_params=_cparams(("parallel", "parallel", "parallel")),
        name="attention",
    )(qt, kh, vt)


def _lru_kernel(xp_ref, x_ref, xn_ref, cw_ref, cb_ref, wa_ref, ba_ref, wx_ref, bx_ref, lam_ref,
                o_ref, xe_sc, a_sc, b_sc, hs_sc, h_sc, *, T, nt):
    d = pl.program_id(0)
    i = pl.program_id(1)
    tile = jnp.where(d == 0, i, nt - 1 - i)
    nb, _, C = x_ref.shape
    nh = C // LANES
    H = SUBLANES
    keep_prev = (tile > 0).astype(F32)
    keep_next = (tile < nt - 1).astype(F32)
    for b in range(nb):
        for hf in range(nh):
            lanes = slice(hf * LANES, (hf + 1) * LANES)
            xe_sc[hf, pl.ds(b, H, stride=nb), :] = xp_ref[b, :, lanes] * keep_prev
            xe_sc[hf, pl.ds(H * nb + b, T, stride=nb), :] = x_ref[b, :, lanes]
            xe_sc[hf, pl.ds((H + T) * nb + b, H, stride=nb), :] = xn_ref[b, :, lanes] * keep_next
    cw = cw_ref[...]
    halves = []
    for hf in range(nh):
        lanes = slice(hf * LANES, (hf + 1) * LANES)
        acc = cb_ref[:, lanes]
        for k in range(cw.shape[0]):
            acc = acc + xe_sc[hf, (H - 2 + k) * nb:(H - 2 + k + T) * nb, :] * cw[k:k + 1, lanes]
        halves.append(acc)
    xc = jnp.concatenate(halves, axis=1)
    xb = xc.astype(BF16)
    sigmoid = lambda v: 0.5 * jnp.tanh(0.5 * v) + 0.5
    r = sigmoid(_dot(xb, wa_ref[0]) + ba_ref[0])
    ig = sigmoid(_dot(xb, wx_ref[0]) + bx_ref[0])
    lam = lam_ref[0]
    softplus_neg = jnp.maximum(-lam, 0.0) + jnp.log1p(jnp.exp(-jnp.abs(lam)))
    log_a = -LRU_C * r * softplus_neg
    a = jnp.exp(log_a)
    bb = jnp.sqrt(-jnp.tanh(log_a) * (1.0 + a * a)) * (ig * xc)
    for hf in range(nh):
        a_sc[hf] = a[:, hf * LANES:(hf + 1) * LANES]
        b_sc[hf] = bb[:, hf * LANES:(hf + 1) * LANES]

    @pl.when(i == 0)
    def _():
        h_sc[...] = jnp.zeros(h_sc.shape, F32)

    def body(s, hs):
        t = jnp.where(d == 0, s, T - 1 - s)
        rows = pl.ds(pl.multiple_of(t * nb, nb), nb)
        new = []
        for hf in range(nh):
            h = a_sc[hf, rows, :] * hs[hf] + b_sc[hf, rows, :]
            hs_sc[hf, rows, :] = h
            new.append(h)
        return tuple(new)

    hs = lax.fori_loop(0, T, body, tuple(h_sc[hf] for hf in range(nh)), unroll=8)
    for hf in range(nh):
        h_sc[hf] = hs[hf]
        for b in range(nb):
            o_ref[0, b, :, hf * LANES:(hf + 1) * LANES] = hs_sc[hf, pl.ds(b, T, stride=nb), :].astype(o_ref.dtype)


def _rglru(xg, conv_w, conv_b, wa, ba, wx, bx, lam):
    B, L, _ = xg.shape
    C = LRU_WIDTH
    assert B == SUBLANES
    T = min(LRU_TILE, L)
    nt = L // T
    hb = T // SUBLANES
    n_halo = L // SUBLANES

    def tile_of(d, i):
        return jnp.where(d == 0, i, nt - 1 - i)

    def blockdiag(w):
        eye = jnp.eye(LRU_HEADS, dtype=w.dtype)
        return jnp.einsum('dhij,hg->dhigj', w, eye).reshape(2, C, C).astype(BF16)

    kern = functools.partial(_lru_kernel, T=T, nt=nt)
    return pl.pallas_call(
        kern,
        out_shape=jax.ShapeDtypeStruct((2, B, L, C), BF16),
        grid=(2, nt),
        in_specs=[
            pl.BlockSpec((B, SUBLANES, C), lambda d, i: (0, jnp.maximum(tile_of(d, i) * hb - 1, 0), 0)),
            pl.BlockSpec((B, T, C), lambda d, i: (0, tile_of(d, i), 0)),
            pl.BlockSpec((B, SUBLANES, C), lambda d, i: (0, jnp.minimum((tile_of(d, i) + 1) * hb, n_halo - 1), 0)),
            pl.BlockSpec(conv_w.shape, lambda d, i: (0, 0)),
            pl.BlockSpec((1, C), lambda d, i: (0, 0)),
            pl.BlockSpec((1, C, C), lambda d, i: (d, 0, 0)),
            pl.BlockSpec((1, 1, C), lambda d, i: (d, 0, 0)),
            pl.BlockSpec((1, C, C), lambda d, i: (d, 0, 0)),
            pl.BlockSpec((1, 1, C), lambda d, i: (d, 0, 0)),
            pl.BlockSpec((1, 1, C), lambda d, i: (d, 0, 0)),
        ],
        out_specs=pl.BlockSpec((1, B, T, C), lambda d, i: (d, 0, tile_of(d, i), 0)),
        scratch_shapes=[
            pltpu.VMEM((C // LANES, (T + 2 * SUBLANES) * B, LANES), F32),
            pltpu.VMEM((C // LANES, T * B, LANES), F32),
            pltpu.VMEM((C // LANES, T * B, LANES), F32),
            pltpu.VMEM((C // LANES, T * B, LANES), F32),
            pltpu.VMEM((C // LANES, B, LANES), F32),
        ],
        compiler_params=_cparams(("arbitrary", "arbitrary")),
        name="rglru",
    )(xg, xg, xg, conv_w, conv_b.reshape(1, C), blockdiag(wa), ba.reshape(2, 1, C),
      blockdiag(wx), bx.reshape(2, 1, C), lam.reshape(2, 1, C))


def _filter_positions(L):
    t = np.linspace(0.0, 1.0, L)[:, None]
    w = 2.0 * math.pi * np.arange(L, dtype=np.float64)[:, None] / L
    f = np.linspace(1e-4, HY_BANDS - 1, HY_BANDS)[None, :]
    z = np.concatenate([t, np.cos(f * w), -np.sin(f * w)], axis=-1)
    zrev = np.concatenate([z[:1], z[:0:-1]], axis=0)
    return jnp.asarray(np.concatenate([z, zrev], axis=1), F32)


def _filter_kernel(zz_ref, w1_ref, b1_ref, w2_ref, b2_ref, w3_ref, b3_ref, fr_ref, wo_ref, dl_ref,
                   fs_ref, asum_ref, *, T, nemb):
    i = pl.program_id(0)
    fr = fr_ref[...]
    hd = lambda a, b: jnp.dot(a, b, precision=HIGHEST, preferred_element_type=F32)
    zz = zz_ref[...]
    h = jnp.sin(fr * (hd(zz, w1_ref[...]) + b1_ref[...]))
    h = jnp.sin(fr * (hd(h, w2_ref[...]) + b2_ref[...]))
    h = jnp.sin(fr * (hd(h, w3_ref[...]) + b3_ref[...]))
    k = hd(h, wo_ref[...])
    dl = dl_ref[...]
    dec_f = jnp.exp(-zz[:, 0:1] * dl)
    dec_r = jnp.exp(-zz[:, nemb:nemb + 1] * dl)
    C = HY_WIDTH
    first_row = (i * T + lax.broadcasted_iota(jnp.int32, (T, 1), 0)) == 0

    @pl.when(i == 0)
    def _():
        asum_ref[...] = jnp.zeros(asum_ref.shape, F32)

    for o in range(HY_ORDER):
        base = o * 2 * C
        kf = k[:, base:base + C] * dec_f
        kb_here = k[:, base + C:base + 2 * C] * dec_f
        kb_rev = k[:, HY_ORDER * 2 * C + o * C:HY_ORDER * 2 * C + (o + 1) * C] * dec_r
        first = kf + jnp.where(first_row, kb_here, 0.0)
        second = jnp.where(first_row, 0.0, kb_rev)
        fs_ref[o] = first
        fs_ref[HY_ORDER + o] = second
        asum_ref[o:o + 1, :] += jnp.sum(jnp.abs(first) + jnp.abs(second), axis=0, keepdims=True)


def _hyena_filter_time(L, w1, b1, w2, b2, w3, b3, freq, wout):
    zz = _filter_positions(L)
    T = _row_tile(L)
    C = HY_WIDTH
    nemb = zz.shape[1] // 2
    nf = w2.shape[0]
    deltas = jnp.asarray(np.linspace(HY_MIN_DECAY, HY_MAX_DECAY, C)[None, :], F32)
    twice = lambda w: jnp.kron(jnp.eye(2, dtype=w.dtype), w)
    both = lambda v: jnp.tile(v, 2).reshape(1, 2 * nf)
    w_back = wout.reshape(nf, HY_ORDER, 2, C)[:, :, 1, :].reshape(nf, HY_ORDER * C)
    wo = jnp.concatenate([jnp.concatenate([wout, jnp.zeros((nf, HY_ORDER * C), wout.dtype)], axis=1),
                          jnp.concatenate([jnp.zeros_like(wout), w_back], axis=1)], axis=0)
    const = lambda i: (0, 0)
    return pl.pallas_call(
        functools.partial(_filter_kernel, T=T, nemb=nemb),
        out_shape=(jax.ShapeDtypeStruct((2 * HY_ORDER, L, C), F32), jax.ShapeDtypeStruct((HY_ORDER, C), F32)),
        grid=(L // T,),
        in_specs=[
            pl.BlockSpec((T, 2 * nemb), lambda i: (i, 0)),
            pl.BlockSpec((2 * nemb, 2 * nf), const), pl.BlockSpec((1, 2 * nf), const),
            pl.BlockSpec((2 * nf, 2 * nf), const), pl.BlockSpec((1, 2 * nf), const),
            pl.BlockSpec((2 * nf, 2 * nf), const), pl.BlockSpec((1, 2 * nf), const),
            pl.BlockSpec((1, 2 * nf), const),
            pl.BlockSpec(wo.shape, const),
            pl.BlockSpec((1, C), const),
        ],
        out_specs=(pl.BlockSpec((2 * HY_ORDER, T, C), lambda i: (0, i, 0)),
                   pl.BlockSpec((HY_ORDER, C), const)),
        compiler_params=_cparams(("arbitrary",)),
        name="hy_filter_mlp",
    )(zz, twice(w1), both(b1), twice(w2), both(b2), twice(w3), both(b3), both(freq), wo, deltas)


def _dft_tables(L):
    N = 2 * L
    N2 = DFT_N2
    N1 = N // N2
    K1 = N1 // 2
    kept = K1 + 1
    rows = -(-kept // SUBLANES) * SUBLANES
    live = (np.arange(rows) < kept).astype(np.float64)[None, :, None]
    pair = np.where((np.arange(rows) == 0) | (np.arange(rows) == K1), 1.0, 2.0)[None, :, None] * live
    n1 = np.arange(K1)[None, None, :]
    k1 = np.arange(rows)[None, :, None]
    n2 = np.arange(N2)[:, None, None]
    ang = 2.0 * np.pi * (((n1 * N2 + n2) * k1) % N) / N
    g_fwd = np.concatenate([np.cos(ang) * live, -np.sin(ang) * live], axis=1)
    t = lambda a: np.transpose(a, (0, 2, 1))
    g_inv = np.concatenate([t(np.cos(ang) * pair), t(-np.sin(ang) * pair)], axis=2) / N
    a2 = 2.0 * np.pi * ((np.arange(N2)[:, None] * np.arange(N2)[None, :]) % N2) / N2
    cr, ci = np.cos(a2), -np.sin(a2)
    m2 = np.block([[cr, -ci], [ci, cr]])
    m2i = np.block([[cr, ci], [-ci, cr]])
    as_bf = lambda a: jnp.asarray(a, BF16)
    return as_bf(g_fwd), as_bf(g_inv), as_bf(m2), as_bf(m2i)


DFT_NB = 16
N_HALF = HY_WIDTH // LANES


def _lane_half_specs(k1n, nb, part):
    return [pl.BlockSpec((1, k1n, nb, LANES), lambda s, j, h=h: (s, 0, j, part * N_HALF + h)) for h in range(N_HALF)]


def _major_half_specs(k1n, nb):
    return [pl.BlockSpec((1, 1, k1n, nb, LANES), lambda s, j, h=h: (s, h, 0, j, 0)) for h in range(N_HALF)]


def _ld_time(ref, j, nb):
    k1n = ref.shape[-3]
    return ref.reshape(k1n * nb, LANES)[pl.ds(j, k1n, stride=nb), :]


def _st_time(ref, h, j, nb, val):
    k1n = ref.shape[2]
    ref.reshape(N_HALF * k1n * nb, LANES)[pl.ds(h * k1n * nb + j, k1n, stride=nb), :] = val


U32 = jnp.uint32


def _pack_halves(x):
    lo = lax.bitcast_convert_type(x[:, :LANES], U32)
    hi = lax.bitcast_convert_type(x[:, LANES:], U32)
    rnd = jnp.uint32(0x8000)
    return ((lo + rnd) >> 16) | ((hi + rnd) & jnp.uint32(0xFFFF0000))


def _unpack_halves(w):
    lo = lax.bitcast_convert_type(w << 16, F32)
    hi = lax.bitcast_convert_type(w & jnp.uint32(0xFFFF0000), F32)
    return jnp.concatenate([lo, hi], axis=1)


def _dft_a_kernel(x0_ref, x1_ref, g_ref, o_ref, *, nb):
    n1 = o_ref.shape[3]
    for j in range(nb):
        parts = []
        for x_ref in (x0_ref, x1_ref):
            parts.append(_ld_time(x_ref, j, nb))
        xj = jnp.concatenate(parts, axis=1).astype(BF16)
        r = _pack_halves(_dot(g_ref[j], xj))
        o_ref[0, 0, j] = r[:n1]
        o_ref[0, 1, j] = r[n1:]


def _dft_stage_a(x, x_specs, g_fwd):
    S = x.shape[0]
    N2, two_n1, K1 = g_fwd.shape
    N1 = two_n1 // 2
    nb = DFT_NB
    return pl.pallas_call(
        functools.partial(_dft_a_kernel, nb=nb),
        out_shape=jax.ShapeDtypeStruct((S, 2, N2, N1, LANES), U32),
        grid=(S, N2 // nb),
        in_specs=x_specs + [pl.BlockSpec((nb, 2 * N1, K1), lambda s, j: (j, 0, 0))],
        out_specs=pl.BlockSpec((1, 2, nb, N1, LANES), lambda s, j: (s, 0, j, 0, 0)),
        compiler_params=_cparams(("parallel", "parallel")),
        name="hy_dft_a",
    )(x, x, g_fwd)


def _k1_rows(ref, j):
    _, _, n2, tk1, _ = ref.shape
    return ref.reshape(2 * n2 * tk1, LANES), pl.ds(j, 2 * n2, stride=tk1)


def _ld_k1(ref, j):
    r2, rows = _k1_rows(ref, j)
    return r2[rows, :]


def _st_k1(ref, j, val):
    r2, rows = _k1_rows(ref, j)
    r2[rows, :] = val


def _dft_b_filter_kernel(s1_ref, s2_ref, m2_ref, asum_ref, o_ref, *, tk1):
    n2 = s1_ref.shape[2]
    C = o_ref.shape[4]
    scale = 1.0 / asum_ref[0]
    for j in range(tk1):
        sign = 1.0 if j % 2 == 0 else -1.0
        s = _unpack_halves(_ld_k1(s1_ref, j)) + sign * _unpack_halves(_ld_k1(s2_ref, j))
        xk = _dot(m2_ref[...], s.astype(BF16)) * scale
        o_ref[0, j] = xk.reshape(2, n2, C)


def _dft_stage_b_filter(sa, m2, asum):
    _, _, N2, N1, _ = sa.shape
    C = HY_WIDTH
    tk1 = SUBLANES
    blk = (1, 2, N2, tk1, LANES)
    return pl.pallas_call(
        functools.partial(_dft_b_filter_kernel, tk1=tk1),
        out_shape=jax.ShapeDtypeStruct((HY_ORDER, N1, 2, N2, C), F32),
        grid=(HY_ORDER, N1 // tk1),
        in_specs=[
            pl.BlockSpec(blk, lambda o, i: (o, 0, 0, i, 0)),
            pl.BlockSpec(blk, lambda o, i: (HY_ORDER + o, 0, 0, i, 0)),
            pl.BlockSpec((2 * N2, 2 * N2), lambda o, i: (0, 0)),
            pl.BlockSpec((1, 1, C), lambda o, i: (o, 0, 0)),
        ],
        out_specs=pl.BlockSpec((1, tk1, 2, N2, C), lambda o, i: (o, i, 0, 0, 0)),
        compiler_params=_cparams(("parallel", "parallel")),
        name="hy_dft_b_filter",
    )(sa, sa, m2, asum.reshape(HY_ORDER, 1, C))


def _dft_b_conv_kernel(s_ref, kf_ref, m2_ref, m2i_ref, o_ref, *, tk1):
    n2 = s_ref.shape[2]
    for j in range(tk1):
        x = _dot(m2_ref[...], _unpack_halves(_ld_k1(s_ref, j)).astype(BF16))
        xr, xi = x[:n2], x[n2:]
        kr = kf_ref[0, j, 0]
        ki = kf_ref[0, j, 1]
        y = jnp.concatenate([xr * kr - xi * ki, xr * ki + xi * kr], axis=0).astype(BF16)
        _st_k1(o_ref, j, _pack_halves(_dot(m2i_ref[...], y)))


def _dft_stage_b_conv(sa, kspec, order, m2, m2i):
    S, _, N2, N1, _ = sa.shape
    C = HY_WIDTH
    tk1 = SUBLANES
    blk = (1, 2, N2, tk1, LANES)
    return pl.pallas_call(
        functools.partial(_dft_b_conv_kernel, tk1=tk1),
        out_shape=jax.ShapeDtypeStruct(sa.shape, U32),
        grid=(N1 // tk1, S),
        in_specs=[
            pl.BlockSpec(blk, lambda i, s: (s, 0, 0, i, 0)),
            pl.BlockSpec((1, tk1, 2, N2, C), lambda i, s: (order, i, 0, 0, 0)),
            pl.BlockSpec((2 * N2, 2 * N2), lambda i, s: (0, 0)),
            pl.BlockSpec((2 * N2, 2 * N2), lambda i, s: (0, 0)),
        ],
        out_specs=pl.BlockSpec(blk, lambda i, s: (s, 0, 0, i, 0)),
        compiler_params=_cparams(("parallel", "parallel")),
        name="hy_dft_b_conv",
    )(sa, kspec, m2, m2i)


def _dft_c_kernel(c_ref, gi_ref, z0_ref, z1_ref, g0_ref, g1_ref, d_ref, o_ref, *, nb):
    d = d_ref[0]
    for j in range(nb):
        cat = _unpack_halves(jnp.concatenate([c_ref[0, 0, j], c_ref[0, 1, j]], axis=0))
        y = _dot(gi_ref[j], cat.astype(BF16))
        for h, (z_ref, g_ref) in enumerate(((z0_ref, g0_ref), (z1_ref, g1_ref))):
            lanes = slice(h * LANES, (h + 1) * LANES)
            _st_time(o_ref, h, j, nb, _ld_time(g_ref, j, nb) * (y[:, lanes] + _ld_time(z_ref, j, nb) * d[:, lanes]))


def _dft_stage_c(sc, g_inv, z, z_specs, gate, gate_specs, d):
    S, _, N2, N1, _ = sc.shape
    C = HY_WIDTH
    K1 = g_inv.shape[1]
    nb = DFT_NB
    return pl.pallas_call(
        functools.partial(_dft_c_kernel, nb=nb),
        out_shape=jax.ShapeDtypeStruct((S, N_HALF, K1, N2, LANES), F32),
        grid=(S, N2 // nb),
        in_specs=[
            pl.BlockSpec((1, 2, nb, N1, LANES), lambda s, j: (s, 0, j, 0, 0)),
            pl.BlockSpec((nb, K1, 2 * N1), lambda s, j: (j, 0, 0)),
        ] + z_specs + gate_specs + [pl.BlockSpec((1, 1, C), lambda s, j: (0, 0, 0))],
        out_specs=pl.BlockSpec((1, N_HALF, K1, nb, LANES), lambda s, j: (s, 0, 0, j, 0)),
        compiler_params=_cparams(("parallel", "parallel")),
        name="hy_dft_c",
    )(sc, g_inv, z, z, gate, gate, d.reshape(1, 1, C))


def _hyena(hc, w1, b1, w2, b2, w3, b3, freq, wout, bias, tables):
    B, L, _ = hc.shape
    C = HY_WIDTH
    g_fwd, g_inv, m2, m2i = tables
    N2 = DFT_N2
    K1 = L // N2
    nb = DFT_NB
    fs, asum = _hyena_filter_time(L, w1, b1, w2, b2, w3, b3, freq, wout)
    fa = _dft_stage_a(fs.reshape(2 * HY_ORDER, K1, N2, C), _lane_half_specs(K1, nb, 0), g_fwd)
    kspec = _dft_stage_b_filter(fa, m2, asum)
    hc4 = hc.reshape(B, K1, N2, (HY_ORDER + 1) * C)
    z, z_specs = hc4, _lane_half_specs(K1, nb, 0)
    for o in range(HY_ORDER):
        sa = _dft_stage_a(z, z_specs, g_fwd)
        sc = _dft_stage_b_conv(sa, kspec, o, m2, m2i)
        z = _dft_stage_c(sc, g_inv, z, z_specs, hc4, _lane_half_specs(K1, nb, 1 + o), bias[o])
        z_specs = _major_half_specs(K1, nb)
    return z.reshape(B, N_HALF, L, LANES)


def _outproj_kernel(x_ref, at_ref, h_ref, gr_ref, zy_ref, w_ref, gt_ref, g_ref, b_ref, o_ref, *, alpha):
    lru = (h_ref[0, 0].astype(F32) + h_ref[1, 0].astype(F32)) * jax.nn.gelu(gr_ref[0])
    na = IN_Q
    at = at_ref[0].reshape(na, at_ref.shape[-1])
    m = lax.dot_general(at, w_ref[0:na], (((0,), (0,)), ((), ())), preferred_element_type=F32)
    rest = jnp.concatenate([lru.astype(BF16)] + [zy_ref[0, h].astype(BF16) for h in range(N_HALF)], axis=1)
    m = m + _dot(rest, w_ref[na:])
    o_ref[0] = _layer_norm(alpha * x_ref[0] + gt_ref[0] * m, g_ref[...], b_ref[...])


def _out_proj(x, attn, h, xg, zy, w_out, gt, ln_g, ln_b, alpha):
    B, L, D = x.shape
    T = _row_tile(L)
    C = LRU_WIDTH
    row = lambda b, i: (b, i, 0)
    vec = lambda b, i: (b, 0, 0)
    const = lambda b, i: (0, 0)
    return pl.pallas_call(
        functools.partial(_outproj_kernel, alpha=alpha),
        out_shape=jax.ShapeDtypeStruct((B, L, D), F32),
        grid=(B, L // T),
        in_specs=[
            pl.BlockSpec((1, T, D), row),
            pl.BlockSpec((1, N_KV_HEADS, KV_GROUP, HEAD_DIM, T), lambda b, i: (b, 0, 0, 0, i)),
            pl.BlockSpec((2, 1, T, C), lambda b, i: (0, b, i, 0)),
            pl.BlockSpec((1, T, C), lambda b, i: (b, i, 1)),
            pl.BlockSpec((1, N_HALF, T, LANES), lambda b, i: (b, 0, i, 0)),
            pl.BlockSpec(w_out.shape, const),
            pl.BlockSpec((1, 1, D), vec),
            pl.BlockSpec((1, D), const),
            pl.BlockSpec((1, D), const),
        ],
        out_specs=pl.BlockSpec((1, T, D), row),
        compiler_params=_cparams(("parallel", "parallel")),
        name="out_proj",
    )(x, attn, h, xg, zy, w_out, gt, ln_g.reshape(1, D), ln_b.reshape(1, D))


FFN_CHUNK = 256


def _ffn_kernel(xp_ref, x_ref, xn_ref, sc_ref, sh_ref, gt_ref, wu_ref, cw_ref, cb_ref, wd_ref, g_ref, b_ref,
                o_ref, u_sc, h_sc, acc_sc, *, T, nt, nf, alpha):
    i = pl.program_id(1)
    H = SUBLANES
    R = T + 2 * H
    F = FFN_CHUNK
    sc = 1.0 + sc_ref[0]
    sh = sh_ref[0]
    u_sc[H:H + T, :] = (x_ref[0] * sc + sh).astype(BF16)
    u_sc[0:H, :] = jnp.where(i > 0, xp_ref[0] * sc + sh, 0.0).astype(BF16)
    u_sc[H + T:R, :] = jnp.where(i < nt - 1, xn_ref[0] * sc + sh, 0.0).astype(BF16)
    acc_sc[...] = jnp.zeros(acc_sc.shape, F32)

    def up(f, slot):
        h_sc[slot] = _dot(u_sc[...], wu_ref[f])

    def down(f, slot):
        h = h_sc[slot]
        cw = cw_ref[f]
        y = cb_ref[f] + pltpu.roll(h, 1, 0) * cw[0:1] + h * cw[1:2] + pltpu.roll(h, R - 1, 0) * cw[2:3]
        y = y[H:H + T]
        act = (jax.nn.gelu(y[:, :F]) * y[:, F:]).astype(BF16)
        acc_sc[...] += _dot(act, wd_ref[f])

    def pair(c2, carry):
        f = 2 * c2
        up(f + 1, 1)
        down(f, 0)
        up(jnp.minimum(f + 2, nf - 1), 0)
        down(f + 1, 1)
        return carry

    up(0, 0)
    lax.fori_loop(0, nf // 2, pair, 0)
    if nf % 2 == 1:
        down(nf - 1, 0)
    o_ref[0] = _layer_norm(alpha * x_ref[0] + gt_ref[0] * acc_sc[...], g_ref[...], b_ref[...])


def _conv_ffn(x, sc, sh, gt, w_up, conv_w, conv_b, w_down, ln_g, ln_b, alpha):
    B, L, D = x.shape
    d_ff = w_down.shape[0]
    T = _row_tile(L)
    F = FFN_CHUNK
    nt = L // T
    nf = d_ff // F
    hb = T // SUBLANES
    n_halo = L // SUBLANES
    nk = conv_w.shape[0]
    chunked = lambda w: jnp.concatenate([w[..., :d_ff].reshape(w.shape[0], nf, F),
                                         w[..., d_ff:].reshape(w.shape[0], nf, F)], axis=-1).transpose(1, 0, 2)
    wu = chunked(w_up)
    cw = chunked(conv_w)
    cb = chunked(conv_b.reshape(1, 2 * d_ff))
    wd = w_down.reshape(nf, F, D)
    row = lambda b, i: (b, i, 0)
    vec = lambda b, i: (b, 0, 0)
    const2 = lambda b, i: (0, 0)
    const3 = lambda b, i: (0, 0, 0)
    resident = lambda shape, imap: pl.BlockSpec(shape, imap, pipeline_mode=pl.Buffered(1))
    return pl.pallas_call(
        functools.partial(_ffn_kernel, T=T, nt=nt, nf=nf, alpha=alpha),
        out_shape=jax.ShapeDtypeStruct((B, L, D), F32),
        grid=(B, nt),
        in_specs=[
            pl.BlockSpec((1, SUBLANES, D), lambda b, i: (b, jnp.maximum(i * hb - 1, 0), 0)),
            pl.BlockSpec((1, T, D), row),
            pl.BlockSpec((1, SUBLANES, D), lambda b, i: (b, jnp.minimum((i + 1) * hb, n_halo - 1), 0)),
            pl.BlockSpec((1, 1, D), vec),
            pl.BlockSpec((1, 1, D), vec),
            pl.BlockSpec((1, 1, D), vec),
            resident((nf, D, 2 * F), const3),
            resident((nf, nk, 2 * F), const3),
            resident((nf, 1, 2 * F), const3),
            resident((nf, F, D), const3),
            pl.BlockSpec((1, D), const2),
            pl.BlockSpec((1, D), const2),
        ],
        out_specs=pl.BlockSpec((1, T, D), row),
        scratch_shapes=[
            pltpu.VMEM((T + 2 * SUBLANES, D), BF16),
            pltpu.VMEM((2, T + 2 * SUBLANES, 2 * F), F32),
            pltpu.VMEM((T, D), F32),
        ],
        compiler_params=_cparams(("parallel", "parallel")),
        name="conv_ffn",
    )(x, x, x, sc, sh, gt, wu, cw, cb, wd, ln_g.reshape(1, D), ln_b.reshape(1, D))


def _trunk(x, mod, p):
    B, L, D = x.shape
    depth = mod.shape[0]
    alpha = (2 * depth) ** 0.25
    rope = _rope_tables(L)
    tables = _dft_tables(L)
    for l in range(depth):
        m6 = mod[l].reshape(B, 6, 1, D)
        sh1, sc1, gt1, sh2, sc2, gt2 = (m6[:, j] for j in range(6))
        q, k, v, xg, hc = _in_proj(x, sc1, sh1, p['w_in'][l], p['q_gain'][l], p['k_gain'][l], rope,
                                   p['hy_conv_w'][l], p['hy_conv_b'][l])
        attn = _attention(q, k, v)
        h = _rglru(xg, p['lru_conv_w'][l], p['lru_conv_b'][l], p['lru_wa'][l], p['lru_ba'][l],
                   p['lru_wx'][l], p['lru_bx'][l], p['lru_lambda'][l])
        zy = _hyena(hc, p['hy_w1'][l], p['hy_b1'][l], p['hy_w2'][l], p['hy_b2'][l], p['hy_w3'][l], p['hy_b3'][l],
                    p['hy_freq'][l], p['hy_wout'][l], p['hy_bias'][l], tables)
        x = _out_proj(x, attn, h, xg, zy, p['w_out'][l], gt1, p['ln1_g'][l], p['ln1_b'][l], alpha)
        x = _conv_ffn(x, sc2, sh2, gt2, p['ffn_w_up'][l], p['ffn_conv_w'][l], p['ffn_conv_b'][l],
                      p['ffn_w_down'][l], p['ln2_g'][l], p['ln2_b'][l], alpha)
    return x


def kernel(x_prompt, x_sample, c_prompt, c_sample, ada_w, ada_b, w_in, q_gain, k_gain, lru_conv_w, lru_conv_b, lru_wa, lru_ba, lru_wx, lru_bx, lru_lambda, hy_conv_w, hy_conv_b, hy_w1, hy_b1, hy_w2, hy_b2, hy_w3, hy_b3, hy_freq, hy_wout, hy_bias, w_out, ln1_g, ln1_b, ffn_w_up, ffn_conv_w, ffn_conv_b, ffn_w_down, ln2_g, ln2_b):
    p = dict(
        w_in=w_in.astype(BF16), q_gain=q_gain, k_gain=k_gain, lru_conv_w=lru_conv_w, lru_conv_b=lru_conv_b,
        lru_wa=lru_wa, lru_ba=lru_ba, lru_wx=lru_wx, lru_bx=lru_bx, lru_lambda=lru_lambda,
        hy_conv_w=hy_conv_w, hy_conv_b=hy_conv_b, hy_w1=hy_w1, hy_b1=hy_b1, hy_w2=hy_w2, hy_b2=hy_b2,
        hy_w3=hy_w3, hy_b3=hy_b3, hy_freq=hy_freq, hy_wout=hy_wout, hy_bias=hy_bias,
        w_out=w_out.astype(BF16), ln1_g=ln1_g, ln1_b=ln1_b, ffn_w_up=ffn_w_up.astype(BF16),
        ffn_conv_w=ffn_conv_w, ffn_conv_b=ffn_conv_b, ffn_w_down=ffn_w_down.astype(BF16), ln2_g=ln2_g, ln2_b=ln2_b,
    )
    nb = x_prompt.shape[0]
    mod = _ada_mod(jnp.concatenate([c_prompt, c_sample], axis=0), ada_w, ada_b)
    y_prompt = _trunk(x_prompt, mod[:, :nb], p)
    y_sample = _trunk(x_sample, mod[:, nb:], p)
    return (y_prompt, y_sample)
```

```python
import functools
import math

import numpy as np
import jax
import jax.numpy as jnp
from jax import lax
from jax.experimental import pallas as pl
from jax.experimental.pallas import tpu as pltpu

F32 = jnp.float32
BF16 = jnp.bfloat16
HIGHEST = lax.Precision.HIGHEST

GRID_W = 64
HEAD_DIM = 64
N_HEADS = 8
N_KV_HEADS = 2
KV_GROUP = N_HEADS // N_KV_HEADS
ROPE_THETA = 10000.0
ROPE_FREQS = HEAD_DIM // 4
QK_EPS = 1e-6
LRU_WIDTH = 256
LRU_HEADS = 4
LRU_C = 8.0
HY_WIDTH = 256
HY_ORDER = 2
HY_BANDS = 16
HY_MIN_DECAY = abs(math.log(1e-2)) / 1.5
HY_MAX_DECAY = abs(math.log(1e-2)) / 0.3
LN_EPS = 1e-5
IN_Q = N_HEADS * HEAD_DIM
IN_KV = N_KV_HEADS * HEAD_DIM

LANES = 128
SUBLANES = 8
V7X_VMEM_BYTES = 64 * 1024 * 1024
VMEM_LIMIT = V7X_VMEM_BYTES * 3 // 4
DFT_N2 = 128

ROW_TILE = 1024
LRU_TILE = 256
ATTN_Q_TILE = 512
ATTN_K_CHUNK = 512


def _row_tile(L):
    return min(ROW_TILE, L)


def _cparams(sem):
    return pltpu.CompilerParams(dimension_semantics=sem, vmem_limit_bytes=VMEM_LIMIT)


def _dot(a, b):
    return jnp.dot(a, b, preferred_element_type=F32)


def _layer_norm(y, g, b):
    mu = jnp.mean(y, axis=-1, keepdims=True)
    yc = y - mu
    var = jnp.mean(yc * yc, axis=-1, keepdims=True)
    return yc * lax.rsqrt(var + LN_EPS) * g + b


def _ada_kernel(c_ref, w_ref, b_ref, o_ref):
    c = c_ref[...]
    s = c * jax.nn.sigmoid(c)
    o_ref[0] = jnp.dot(s, w_ref[0], precision=HIGHEST, preferred_element_type=F32) + b_ref[0]


def _ada_mod(c_all, ada_w, ada_b):
    depth, d, n = ada_w.shape
    rows = c_all.shape[0]
    tn = 768
    return pl.pallas_call(
        _ada_kernel,
        out_shape=jax.ShapeDtypeStruct((depth, rows, n), F32),
        grid=(depth, n // tn),
        in_specs=[
            pl.BlockSpec((rows, d), lambda l, j: (0, 0)),
            pl.BlockSpec((1, d, tn), lambda l, j: (l, 0, j)),
            pl.BlockSpec((1, 1, tn), lambda l, j: (l, 0, j)),
        ],
        out_specs=pl.BlockSpec((1, rows, tn), lambda l, j: (l, 0, j)),
        compiler_params=_cparams(("parallel", "parallel")),
        name="ada_mod",
    )(c_all, ada_w, ada_b.reshape(depth, 1, n))


def _rope_tables(L):
    rows = L // GRID_W
    row = np.repeat(np.arange(rows, dtype=np.float64), GRID_W)
    col = np.tile(np.arange(GRID_W, dtype=np.float64), rows)
    inv = ROPE_THETA ** (-np.arange(ROPE_FREQS, dtype=np.float64) / ROPE_FREQS)
    ar = row[:, None] * inv
    ac = col[:, None] * inv
    zeros = np.zeros_like(ar)
    cos = np.concatenate([np.cos(ar), np.cos(ar), np.cos(ac), np.cos(ac)], axis=1)
    sin_up = np.concatenate([-np.sin(ar), zeros, -np.sin(ac), zeros], axis=1)
    sin_dn = np.concatenate([zeros, np.sin(ar), zeros, np.sin(ac)], axis=1)
    two = lambda t: jnp.asarray(np.concatenate([t, t], axis=1), F32)
    return two(cos), two(sin_up), two(sin_dn)


def _inproj_kernel(xp_ref, x_ref, xn_ref, sc_ref, sh_ref, w_ref, qg_ref, kg_ref, cos_ref, sup_ref, sdn_ref, bd_ref,
                   hcw_ref, hcb_ref, q_ref, k_ref, vt_ref, xg_ref, hc_ref, *, T, nt):
    i = pl.program_id(1)
    H = SUBLANES
    R = T + 2 * H
    sc = 1.0 + sc_ref[0]
    sh = sh_ref[0]
    u_ext = jnp.concatenate([jnp.where(i > 0, xp_ref[0] * sc + sh, 0.0), x_ref[0] * sc + sh,
                             jnp.where(i < nt - 1, xn_ref[0] * sc + sh, 0.0)], axis=0).astype(BF16)
    proj_ext = _dot(u_ext, w_ref[...])
    proj = proj_ext[H:H + T]
    cos = cos_ref[...]
    sup = sup_ref[...]
    sdn = sdn_ref[...]
    bd = bd_ref[...]
    half = ROPE_FREQS

    def norm_rope(t, gain):
        ms = _dot((t * t).astype(BF16), bd)
        tn = t * lax.rsqrt(ms + QK_EPS) * gain
        return (tn * cos + pltpu.roll(tn, LANES - half, 1) * sup + pltpu.roll(tn, half, 1) * sdn)

    qg = qg_ref[...]
    for j in range(IN_Q // LANES):
        sl = slice(j * LANES, (j + 1) * LANES)
        q_ref[0, :, sl] = (norm_rope(proj[:, sl], qg) * (HEAD_DIM ** -0.5 * math.log2(math.e))).astype(BF16)
    kn = norm_rope(proj[:, IN_Q:IN_Q + IN_KV], kg_ref[...])
    v_t = proj[:, IN_Q + IN_KV:IN_Q + 2 * IN_KV].T
    tk = vt_ref.shape[-1]
    for g in range(N_KV_HEADS):
        k_ref[0, g] = kn[:, g * HEAD_DIM:(g + 1) * HEAD_DIM].astype(BF16)
        for c in range(T // tk):
            vt_ref[0, g, c] = v_t[g * HEAD_DIM:(g + 1) * HEAD_DIM, c * tk:(c + 1) * tk].astype(BF16)
    o = IN_Q + 2 * IN_KV
    xg_ref[0] = proj[:, o:o + 2 * LRU_WIDTH]
    hy = proj_ext[:, o + 2 * LRU_WIDTH:]
    cw = hcw_ref[...]
    hc = hcb_ref[...] + pltpu.roll(hy, 1, 0) * cw[0:1] + hy * cw[1:2] + pltpu.roll(hy, R - 1, 0) * cw[2:3]
    hc_ref[0] = hc[H:H + T]


def _in_proj(x, sc, sh, w_in, q_gain, k_gain, rope, hy_conv_w, hy_conv_b):
    B, L, D = x.shape
    n_in = w_in.shape[1]
    T = _row_tile(L)
    nt = L // T
    hb = T // SUBLANES
    n_halo = L // SUBLANES
    _, tk = _attn_tiles(L)
    cos, sup, sdn = rope
    bd = jnp.asarray(np.kron(np.eye(2), np.full((HEAD_DIM, HEAD_DIM), 1.0 / HEAD_DIM)), BF16)
    qg = jnp.tile(q_gain, 2).reshape(1, LANES)
    kg = jnp.tile(k_gain, 2).reshape(1, LANES)
    n_hy = n_in - IN_Q - 2 * IN_KV - 2 * LRU_WIDTH
    row = lambda b, i: (b, i, 0)
    vec = lambda b, i: (b, 0, 0)
    tab = lambda b, i: (i, 0)
    const = lambda b, i: (0, 0)
    return pl.pallas_call(
        functools.partial(_inproj_kernel, T=T, nt=nt),
        out_shape=(
            jax.ShapeDtypeStruct((B, L, IN_Q), BF16),
            jax.ShapeDtypeStruct((B, N_KV_HEADS, L, HEAD_DIM), BF16),
            jax.ShapeDtypeStruct((B, N_KV_HEADS, L // tk, V_ROWS, tk), BF16),
            jax.ShapeDtypeStruct((B, L, 2 * LRU_WIDTH), F32),
            jax.ShapeDtypeStruct((B, L, n_hy), F32),
        ),
        grid=(B, nt),
        in_specs=[
            pl.BlockSpec((1, SUBLANES, D), lambda b, i: (b, jnp.maximum(i * hb - 1, 0), 0)),
            pl.BlockSpec((1, T, D), row),
            pl.BlockSpec((1, SUBLANES, D), lambda b, i: (b, jnp.minimum((i + 1) * hb, n_halo - 1), 0)),
            pl.BlockSpec((1, 1, D), vec),
            pl.BlockSpec((1, 1, D), vec),
            pl.BlockSpec((D, n_in), const),
            pl.BlockSpec((1, LANES), const),
            pl.BlockSpec((1, LANES), const),
            pl.BlockSpec((T, LANES), tab),
            pl.BlockSpec((T, LANES), tab),
            pl.BlockSpec((T, LANES), tab),
            pl.BlockSpec((LANES, LANES), const),
            pl.BlockSpec(hy_conv_w.shape, const),
            pl.BlockSpec((1, n_hy), const),
        ],
        out_specs=(
            pl.BlockSpec((1, T, IN_Q), row),
            pl.BlockSpec((1, N_KV_HEADS, T, HEAD_DIM), lambda b, i: (b, 0, i, 0)),
            pl.BlockSpec((1, N_KV_HEADS, T // tk, V_ROWS, tk), lambda b, i: (b, 0, i, 0, 0)),
            pl.BlockSpec((1, T, 2 * LRU_WIDTH), row),
            pl.BlockSpec((1, T, n_hy), row),
        ),
        compiler_params=_cparams(("parallel", "parallel")),
        name="in_proj",
    )(x, x, x, sc, sh, w_in, qg, kg, cos, sup, sdn, bd, hy_conv_w, hy_conv_b.reshape(1, n_hy))


V_ROWS = HEAD_DIM


def _attn_kernel(qt_ref, k_ref, vt_ref, o_ref, acc_sc, s_sc, *, tk, nk):
    tq = qt_ref.shape[-1]
    acc_sc[...] = jnp.zeros(acc_sc.shape, F32)

    def scores_h(c, slot, h):
        kc = k_ref[0, 0, pl.ds(pl.multiple_of(c * tk, tk), tk), :]
        s = _dot(kc, qt_ref[0, 0, h])
        s_sc[slot, h] = s
        return jnp.max(s, axis=0, keepdims=True)

    def consume_h(c, slot, h, m_prev, l_prev, m_chunk):
        vc = vt_ref[0, 0, c]
        m_new = jnp.maximum(m_prev, m_chunk)
        p = jnp.exp2(s_sc[slot, h] - m_new)
        alpha = jnp.exp2(m_prev - m_new)
        l_new = alpha * l_prev + jnp.sum(p, axis=0, keepdims=True)
        acc_sc[h] = alpha * acc_sc[h] + _dot(vc, p.astype(BF16))
        return m_new, l_new

    def step(c, c_next, slot, carry, lookahead=True):
        ms, ls, mc = carry
        new_m, new_l, new_c = [], [], []
        for h in range(KV_GROUP):
            if lookahead:
                new_c.append(scores_h(c_next, 1 - slot, h))
            m_new, l_new = consume_h(c, slot, h, ms[h], ls[h], mc[h])
            new_m.append(m_new)
            new_l.append(l_new)
        return tuple(new_m), tuple(new_l), tuple(new_c)

    def pair(c2, carry):
        c = 2 * c2
        carry = step(c, c + 1, 0, carry)
        return step(c + 1, c + 2, 1, carry)

    m0 = tuple(jnp.full((1, tq), -jnp.inf, F32) for _ in range(KV_GROUP))
    c0 = tuple(scores_h(0, 0, h) for h in range(KV_GROUP))
    l0 = tuple(jnp.zeros((1, tq), F32) for _ in range(KV_GROUP))
    carry = lax.fori_loop(0, nk // 2 - 1, pair, (m0, l0, c0))
    carry = step(nk - 2, nk - 1, 0, carry)
    _, ls, _ = step(nk - 1, None, 1, carry, lookahead=False)
    for h in range(KV_GROUP):
        o_ref[0, 0, h] = (acc_sc[h] / ls[h]).astype(o_ref.dtype)


def _attn_tiles(L):
    return min(ATTN_Q_TILE, L), min(ATTN_K_CHUNK, L // 2)


def _attention(q, kh, vt):
    B, L, _ = q.shape
    tq, tk = _attn_tiles(L)
    nk = L // tk
    assert nk % 2 == 0
    qt = q.reshape(B, L, N_KV_HEADS, KV_GROUP, HEAD_DIM).transpose(0, 2, 3, 4, 1)
    return pl.pallas_call(
        functools.partial(_attn_kernel, tk=tk, nk=nk),
        out_shape=jax.ShapeDtypeStruct((B, N_KV_HEADS, KV_GROUP, HEAD_DIM, L), BF16),
        grid=(B, N_KV_HEADS, L // tq),
        in_specs=[
            pl.BlockSpec((1, 1, KV_GROUP, HEAD_DIM, tq), lambda b, g, i: (b, g, 0, 0, i)),
            pl.BlockSpec((1, 1, L, HEAD_DIM), lambda b, g, i: (b, g, 0, 0)),
            pl.BlockSpec((1, 1, nk, V_ROWS, tk), lambda b, g, i: (b, g, 0, 0, 0)),
        ],
        out_specs=pl.BlockSpec((1, 1, KV_GROUP, HEAD_DIM, tq), lambda b, g, i: (b, g, 0, 0, i)),
        scratch_shapes=[pltpu.VMEM((KV_GROUP, V_ROWS, tq), F32), pltpu.VMEM((2, KV_GROUP, tk, tq), F32)],
        compiler_params=_cparams(("parallel", "parallel", "parallel")),
        name="attention",
    )(qt, kh, vt)


def _lru_kernel(xp_ref, x_ref, xn_ref, cw_ref, cb_ref, wa_ref, ba_ref, wx_ref, bx_ref, lam_ref,
                o_ref, xe_sc, a_sc, b_sc, hs_sc, h_sc, *, T, nt):
    d = pl.program_id(0)
    i = pl.program_id(1)
    tile = jnp.where(d == 0, i, nt - 1 - i)
    nb, _, C = x_ref.shape
    nh = C // LANES
    H = SUBLANES
    keep_prev = (tile > 0).astype(F32)
    keep_next = (tile < nt - 1).astype(F32)
    for b in range(nb):
        for hf in range(nh):
            lanes = slice(hf * LANES, (hf + 1) * LANES)
            xe_sc[hf, pl.ds(b, H, stride=nb), :] = xp_ref[b, :, lanes] * keep_prev
            xe_sc[hf, pl.ds(H * nb + b, T, stride=nb), :] = x_ref[b, :, lanes]
            xe_sc[hf, pl.ds((H + T) * nb + b, H, stride=nb), :] = xn_ref[b, :, lanes] * keep_next
    cw = cw_ref[...]
    halves = []
    for hf in range(nh):
        lanes = slice(hf * LANES, (hf + 1) * LANES)
        acc = cb_ref[:, lanes]
        for k in range(cw.shape[0]):
            acc = acc + xe_sc[hf, (H - 2 + k) * nb:(H - 2 + k + T) * nb, :] * cw[k:k + 1, lanes]
        halves.append(acc)
    xc = jnp.concatenate(halves, axis=1)
    xb = xc.astype(BF16)
    sigmoid = lambda v: 0.5 * jnp.tanh(0.5 * v) + 0.5
    r = sigmoid(_dot(xb, wa_ref[0]) + ba_ref[0])
    ig = sigmoid(_dot(xb, wx_ref[0]) + bx_ref[0])
    lam = lam_ref[0]
    softplus_neg = jnp.maximum(-lam, 0.0) + jnp.log1p(jnp.exp(-jnp.abs(lam)))
    log_a = -LRU_C * r * softplus_neg
    a = jnp.exp(log_a)
    bb = jnp.sqrt(-jnp.tanh(log_a) * (1.0 + a * a)) * (ig * xc)
    for hf in range(nh):
        a_sc[hf] = a[:, hf * LANES:(hf + 1) * LANES]
        b_sc[hf] = bb[:, hf * LANES:(hf + 1) * LANES]

    @pl.when(i == 0)
    def _():
        h_sc[...] = jnp.zeros(h_sc.shape, F32)

    def body(s, hs):
        t = jnp.where(d == 0, s, T - 1 - s)
        rows = pl.ds(pl.multiple_of(t * nb, nb), nb)
        new = []
        for hf in range(nh):
            h = a_sc[hf, rows, :] * hs[hf] + b_sc[hf, rows, :]
            hs_sc[hf, rows, :] = h
            new.append(h)
        return tuple(new)

    hs = lax.fori_loop(0, T, body, tuple(h_sc[hf] for hf in range(nh)), unroll=8)
    for hf in range(nh):
        h_sc[hf] = hs[hf]
        for b in range(nb):
            o_ref[0, b, :, hf * LANES:(hf + 1) * LANES] = hs_sc[hf, pl.ds(b, T, stride=nb), :].astype(o_ref.dtype)


def _rglru(xg, conv_w, conv_b, wa, ba, wx, bx, lam):
    B, L, _ = xg.shape
    C = LRU_WIDTH
    assert B == SUBLANES
    T = min(LRU_TILE, L)
    nt = L // T
    hb = T // SUBLANES
    n_halo = L // SUBLANES

    def tile_of(d, i):
        return jnp.where(d == 0, i, nt - 1 - i)

    def blockdiag(w):
        eye = jnp.eye(LRU_HEADS, dtype=w.dtype)
        return jnp.einsum('dhij,hg->dhigj', w, eye).reshape(2, C, C).astype(BF16)

    kern = functools.partial(_lru_kernel, T=T, nt=nt)
    return pl.pallas_call(
        kern,
        out_shape=jax.ShapeDtypeStruct((2, B, L, C), BF16),
        grid=(2, nt),
        in_specs=[
            pl.BlockSpec((B, SUBLANES, C), lambda d, i: (0, jnp.maximum(tile_of(d, i) * hb - 1, 0), 0)),
            pl.BlockSpec((B, T, C), lambda d, i: (0, tile_of(d, i), 0)),
            pl.BlockSpec((B, SUBLANES, C), lambda d, i: (0, jnp.minimum((tile_of(d, i) + 1) * hb, n_halo - 1), 0)),
            pl.BlockSpec(conv_w.shape, lambda d, i: (0, 0)),
            pl.BlockSpec((1, C), lambda d, i: (0, 0)),
            pl.BlockSpec((1, C, C), lambda d, i: (d, 0, 0)),
            pl.BlockSpec((1, 1, C), lambda d, i: (d, 0, 0)),
            pl.BlockSpec((1, C, C), lambda d, i: (d, 0, 0)),
            pl.BlockSpec((1, 1, C), lambda d, i: (d, 0, 0)),
            pl.BlockSpec((1, 1, C), lambda d, i: (d, 0, 0)),
        ],
        out_specs=pl.BlockSpec((1, B, T, C), lambda d, i: (d, 0, tile_of(d, i), 0)),
        scratch_shapes=[
            pltpu.VMEM((C // LANES, (T + 2 * SUBLANES) * B, LANES), F32),
            pltpu.VMEM((C // LANES, T * B, LANES), F32),
            pltpu.VMEM((C // LANES, T * B, LANES), F32),
            pltpu.VMEM((C // LANES, T * B, LANES), F32),
            pltpu.VMEM((C // LANES, B, LANES), F32),
        ],
        compiler_params=_cparams(("arbitrary", "arbitrary")),
        name="rglru",
    )(xg, xg, xg, conv_w, conv_b.reshape(1, C), blockdiag(wa), ba.reshape(2, 1, C),
      blockdiag(wx), bx.reshape(2, 1, C), lam.reshape(2, 1, C))


def _filter_positions(L):
    t = np.linspace(0.0, 1.0, L)[:, None]
    w = 2.0 * math.pi * np.arange(L, dtype=np.float64)[:, None] / L
    f = np.linspace(1e-4, HY_BANDS - 1, HY_BANDS)[None, :]
    z = np.concatenate([t, np.cos(f * w), -np.sin(f * w)], axis=-1)
    zrev = np.concatenate([z[:1], z[:0:-1]], axis=0)
    return jnp.asarray(np.concatenate([z, zrev], axis=1), F32)


def _filter_kernel(zz_ref, w1_ref, b1_ref, w2_ref, b2_ref, w3_ref, b3_ref, fr_ref, wo_ref, dl_ref,
                   fs_ref, asum_ref, *, T, nemb):
    i = pl.program_id(0)
    fr = fr_ref[...]
    hd = lambda a, b: jnp.dot(a, b, precision=HIGHEST, preferred_element_type=F32)
    zz = zz_ref[...]
    h = jnp.sin(fr * (hd(zz, w1_ref[...]) + b1_ref[...]))
    h = jnp.sin(fr * (hd(h, w2_ref[...]) + b2_ref[...]))
    h = jnp.sin(fr * (hd(h, w3_ref[...]) + b3_ref[...]))
    k = hd(h, wo_ref[...])
    dl = dl_ref[...]
    dec_f = jnp.exp(-zz[:, 0:1] * dl)
    dec_r = jnp.exp(-zz[:, nemb:nemb + 1] * dl)
    C = HY_WIDTH
    first_row = (i * T + lax.broadcasted_iota(jnp.int32, (T, 1), 0)) == 0

    @pl.when(i == 0)
    def _():
        asum_ref[...] = jnp.zeros(asum_ref.shape, F32)

    for o in range(HY_ORDER):
        base = o * 2 * C
        kf = k[:, base:base + C] * dec_f
        kb_here = k[:, base + C:base + 2 * C] * dec_f
        kb_rev = k[:, HY_ORDER * 2 * C + o * C:HY_ORDER * 2 * C + (o + 1) * C] * dec_r
        first = kf + jnp.where(first_row, kb_here, 0.0)
        second = jnp.where(first_row, 0.0, kb_rev)
        fs_ref[o] = first
        fs_ref[HY_ORDER + o] = second
        asum_ref[o:o + 1, :] += jnp.sum(jnp.abs(first) + jnp.abs(second), axis=0, keepdims=True)


def _hyena_filter_time(L, w1, b1, w2, b2, w3, b3, freq, wout):
    zz = _filter_positions(L)
    T = _row_tile(L)
    C = HY_WIDTH
    nemb = zz.shape[1] // 2
    nf = w2.shape[0]
    deltas = jnp.asarray(np.linspace(HY_MIN_DECAY, HY_MAX_DECAY, C)[None, :], F32)
    twice = lambda w: jnp.kron(jnp.eye(2, dtype=w.dtype), w)
    both = lambda v: jnp.tile(v, 2).reshape(1, 2 * nf)
    w_back = wout.reshape(nf, HY_ORDER, 2, C)[:, :, 1, :].reshape(nf, HY_ORDER * C)
    wo = jnp.concatenate([jnp.concatenate([wout, jnp.zeros((nf, HY_ORDER * C), wout.dtype)], axis=1),
                          jnp.concatenate([jnp.zeros_like(wout), w_back], axis=1)], axis=0)
    const = lambda i: (0, 0)
    return pl.pallas_call(
        functools.partial(_filter_kernel, T=T, nemb=nemb),
        out_shape=(jax.ShapeDtypeStruct((2 * HY_ORDER, L, C), F32), jax.ShapeDtypeStruct((HY_ORDER, C), F32)),
        grid=(L // T,),
        in_specs=[
            pl.BlockSpec((T, 2 * nemb), lambda i: (i, 0)),
            pl.BlockSpec((2 * nemb, 2 * nf), const), pl.BlockSpec((1, 2 * nf), const),
            pl.BlockSpec((2 * nf, 2 * nf), const), pl.BlockSpec((1, 2 * nf), const),
            pl.BlockSpec((2 * nf, 2 * nf), const), pl.BlockSpec((1, 2 * nf), const),
            pl.BlockSpec((1, 2 * nf), const),
            pl.BlockSpec(wo.shape, const),
            pl.BlockSpec((1, C), const),
        ],
        out_specs=(pl.BlockSpec((2 * HY_ORDER, T, C), lambda i: (0, i, 0)),
                   pl.BlockSpec((HY_ORDER, C), const)),
        compiler_params=_cparams(("arbitrary",)),
        name="hy_filter_mlp",
    )(zz, twice(w1), both(b1), twice(w2), both(b2), twice(w3), both(b3), both(freq), wo, deltas)


def _dft_tables(L):
    N = 2 * L
    N2 = DFT_N2
    N1 = N // N2
    K1 = N1 // 2
    kept = K1 + 1
    rows = -(-kept // SUBLANES) * SUBLANES
    live = (np.arange(rows) < kept).astype(np.float64)[None, :, None]
    pair = np.where((np.arange(rows) == 0) | (np.arange(rows) == K1), 1.0, 2.0)[None, :, None] * live
    n1 = np.arange(K1)[None, None, :]
    k1 = np.arange(rows)[None, :, None]
    n2 = np.arange(N2)[:, None, None]
    ang = 2.0 * np.pi * (((n1 * N2 + n2) * k1) % N) / N
    g_fwd = np.concatenate([np.cos(ang) * live, -np.sin(ang) * live], axis=1)
    t = lambda a: np.transpose(a, (0, 2, 1))
    g_inv = np.concatenate([t(np.cos(ang) * pair), t(-np.sin(ang) * pair)], axis=2) / N
    a2 = 2.0 * np.pi * ((np.arange(N2)[:, None] * np.arange(N2)[None, :]) % N2) / N2
    cr, ci = np.cos(a2), -np.sin(a2)
    m2 = np.block([[cr, -ci], [ci, cr]])
    m2i = np.block([[cr, ci], [-ci, cr]])
    as_bf = lambda a: jnp.asarray(a, BF16)
    return as_bf(g_fwd), as_bf(g_inv), as_bf(m2), as_bf(m2i)


DFT_NB = 16
N_HALF = HY_WIDTH // LANES


def _lane_half_specs(k1n, nb, part):
    return [pl.BlockSpec((1, k1n, nb, LANES), lambda s, j, h=h: (s, 0, j, part * N_HALF + h)) for h in range(N_HALF)]


def _major_half_specs(k1n, nb):
    return [pl.BlockSpec((1, 1, k1n, nb, LANES), lambda s, j, h=h: (s, h, 0, j, 0)) for h in range(N_HALF)]


def _ld_time(ref, j, nb):
    k1n = ref.shape[-3]
    return ref.reshape(k1n * nb, LANES)[pl.ds(j, k1n, stride=nb), :]


def _st_time(ref, h, j, nb, val):
    k1n = ref.shape[2]
    ref.reshape(N_HALF * k1n * nb, LANES)[pl.ds(h * k1n * nb + j, k1n, stride=nb), :] = val


U32 = jnp.uint32


def _pack_halves(x):
    lo = lax.bitcast_convert_type(x[:, :LANES], U32)
    hi = lax.bitcast_convert_type(x[:, LANES:], U32)
    rnd = jnp.uint32(0x8000)
    return ((lo + rnd) >> 16) | ((hi + rnd) & jnp.uint32(0xFFFF0000))


def _unpack_halves(w):
    lo = lax.bitcast_convert_type(w << 16, F32)
    hi = lax.bitcast_convert_type(w & jnp.uint32(0xFFFF0000), F32)
    return jnp.concatenate([lo, hi], axis=1)


def _dft_a_kernel(x0_ref, x1_ref, g_ref, o_ref, *, nb):
    n1 = o_ref.shape[3]
    for j in range(nb):
        parts = []
        for x_ref in (x0_ref, x1_ref):
            parts.append(_ld_time(x_ref, j, nb))
        xj = jnp.concatenate(parts, axis=1).astype(BF16)
        r = _pack_halves(_dot(g_ref[j], xj))
        o_ref[0, 0, j] = r[:n1]
        o_ref[0, 1, j] = r[n1:]


def _dft_stage_a(x, x_specs, g_fwd):
    S = x.shape[0]
    N2, two_n1, K1 = g_fwd.shape
    N1 = two_n1 // 2
    nb = DFT_NB
    return pl.pallas_call(
        functools.partial(_dft_a_kernel, nb=nb),
        out_shape=jax.ShapeDtypeStruct((S, 2, N2, N1, LANES), U32),
        grid=(S, N2 // nb),
        in_specs=x_specs + [pl.BlockSpec((nb, 2 * N1, K1), lambda s, j: (j, 0, 0))],
        out_specs=pl.BlockSpec((1, 2, nb, N1, LANES), lambda s, j: (s, 0, j, 0, 0)),
        compiler_params=_cparams(("parallel", "parallel")),
        name="hy_dft_a",
    )(x, x, g_fwd)


def _k1_rows(ref, j):
    _, _, n2, tk1, _ = ref.shape
    return ref.reshape(2 * n2 * tk1, LANES), pl.ds(j, 2 * n2, stride=tk1)


def _ld_k1(ref, j):
    r2, rows = _k1_rows(ref, j)
    return r2[rows, :]


def _st_k1(ref, j, val):
    r2, rows = _k1_rows(ref, j)
    r2[rows, :] = val


def _dft_b_filter_kernel(s1_ref, s2_ref, m2_ref, asum_ref, o_ref, *, tk1):
    n2 = s1_ref.shape[2]
    C = o_ref.shape[4]
    scale = 1.0 / asum_ref[0]
    for j in range(tk1):
        sign = 1.0 if j % 2 == 0 else -1.0
        s = _unpack_halves(_ld_k1(s1_ref, j)) + sign * _unpack_halves(_ld_k1(s2_ref, j))
        xk = _dot(m2_ref[...], s.astype(BF16)) * scale
        o_ref[0, j] = xk.reshape(2, n2, C)


def _dft_stage_b_filter(sa, m2, asum):
    _, _, N2, N1, _ = sa.shape
    C = HY_WIDTH
    tk1 = SUBLANES
    blk = (1, 2, N2, tk1, LANES)
    return pl.pallas_call(
        functools.partial(_dft_b_filter_kernel, tk1=tk1),
        out_shape=jax.ShapeDtypeStruct((HY_ORDER, N1, 2, N2, C), F32),
        grid=(HY_ORDER, N1 // tk1),
        in_specs=[
            pl.BlockSpec(blk, lambda o, i: (o, 0, 0, i, 0)),
            pl.BlockSpec(blk, lambda o, i: (HY_ORDER + o, 0, 0, i, 0)),
            pl.BlockSpec((2 * N2, 2 * N2), lambda o, i: (0, 0)),
            pl.BlockSpec((1, 1, C), lambda o, i: (o, 0, 0)),
        ],
        out_specs=pl.BlockSpec((1, tk1, 2, N2, C), lambda o, i: (o, i, 0, 0, 0)),
        compiler_params=_cparams(("parallel", "parallel")),
        name="hy_dft_b_filter",
    )(sa, sa, m2, asum.reshape(HY_ORDER, 1, C))


def _dft_b_conv_kernel(s_ref, kf_ref, m2_ref, m2i_ref, o_ref, *, tk1):
    n2 = s_ref.shape[2]
    for j in range(tk1):
        x = _dot(m2_ref[...], _unpack_halves(_ld_k1(s_ref, j)).astype(BF16))
        xr, xi = x[:n2], x[n2:]
        kr = kf_ref[0, j, 0]
        ki = kf_ref[0, j, 1]
        y = jnp.concatenate([xr * kr - xi * ki, xr * ki + xi * kr], axis=0).astype(BF16)
        _st_k1(o_ref, j, _pack_halves(_dot(m2i_ref[...], y)))


def _dft_stage_b_conv(sa, kspec, order, m2, m2i):
    S, _, N2, N1, _ = sa.shape
    C = HY_WIDTH
    tk1 = SUBLANES
    blk = (1, 2, N2, tk1, LANES)
    return pl.pallas_call(
        functools.partial(_dft_b_conv_kernel, tk1=tk1),
        out_shape=jax.ShapeDtypeStruct(sa.shape, U32),
        grid=(N1 // tk1, S),
        in_specs=[
            pl.BlockSpec(blk, lambda i, s: (s, 0, 0, i, 0)),
            pl.BlockSpec((1, tk1, 2, N2, C), lambda i, s: (order, i, 0, 0, 0)),
            pl.BlockSpec((2 * N2, 2 * N2), lambda i, s: (0, 0)),
            pl.BlockSpec((2 * N2, 2 * N2), lambda i, s: (0, 0)),
        ],
        out_specs=pl.BlockSpec(blk, lambda i, s: (s, 0, 0, i, 0)),
        compiler_params=_cparams(("parallel", "parallel")),
        name="hy_dft_b_conv",
    )(sa, kspec, m2, m2i)


def _dft_c_kernel(c_ref, gi_ref, z0_ref, z1_ref, g0_ref, g1_ref, d_ref, o_ref, *, nb):
    d = d_ref[0]
    for j in range(nb):
        cat = _unpack_halves(jnp.concatenate([c_ref[0, 0, j], c_ref[0, 1, j]], axis=0))
        y = _dot(gi_ref[j], cat.astype(BF16))
        for h, (z_ref, g_ref) in enumerate(((z0_ref, g0_ref), (z1_ref, g1_ref))):
            lanes = slice(h * LANES, (h + 1) * LANES)
            _st_time(o_ref, h, j, nb, _ld_time(g_ref, j, nb) * (y[:, lanes] + _ld_time(z_ref, j, nb) * d[:, lanes]))


def _dft_stage_c(sc, g_inv, z, z_specs, gate, gate_specs, d):
    S, _, N2, N1, _ = sc.shape
    C = HY_WIDTH
    K1 = g_inv.shape[1]
    nb = DFT_NB
    return pl.pallas_call(
        functools.partial(_dft_c_kernel, nb=nb),
        out_shape=jax.ShapeDtypeStruct((S, N_HALF, K1, N2, LANES), F32),
        grid=(S, N2 // nb),
        in_specs=[
            pl.BlockSpec((1, 2, nb, N1, LANES), lambda s, j: (s, 0, j, 0, 0)),
            pl.BlockSpec((nb, K1, 2 * N1), lambda s, j: (j, 0, 0)),
        ] + z_specs + gate_specs + [pl.BlockSpec((1, 1, C), lambda s, j: (0, 0, 0))],
        out_specs=pl.BlockSpec((1, N_HALF, K1, nb, LANES), lambda s, j: (s, 0, 0, j, 0)),
        compiler_params=_cparams(("parallel", "parallel")),
        name="hy_dft_c",
    )(sc, g_inv, z, z, gate, gate, d.reshape(1, 1, C))


def _hyena(hc, w1, b1, w2, b2, w3, b3, freq, wout, bias, tables):
    B, L, _ = hc.shape
    C = HY_WIDTH
    g_fwd, g_inv, m2, m2i = tables
    N2 = DFT_N2
    K1 = L // N2
    nb = DFT_NB
    fs, asum = _hyena_filter_time(L, w1, b1, w2, b2, w3, b3, freq, wout)
    fa = _dft_stage_a(fs.reshape(2 * HY_ORDER, K1, N2, C), _lane_half_specs(K1, nb, 0), g_fwd)
    kspec = _dft_stage_b_filter(fa, m2, asum)
    hc4 = hc.reshape(B, K1, N2, (HY_ORDER + 1) * C)
    z, z_specs = hc4, _lane_half_specs(K1, nb, 0)
    for o in range(HY_ORDER):
        sa = _dft_stage_a(z, z_specs, g_fwd)
        sc = _dft_stage_b_conv(sa, kspec, o, m2, m2i)
        z = _dft_stage_c(sc, g_inv, z, z_specs, hc4, _lane_half_specs(K1, nb, 1 + o), bias[o])
        z_specs = _major_half_specs(K1, nb)
    return z.reshape(B, N_HALF, L, LANES)


def _outproj_kernel(x_ref, at_ref, h_ref, gr_ref, zy_ref, w_ref, gt_ref, g_ref, b_ref, o_ref, *, alpha):
    lru = (h_ref[0, 0].astype(F32) + h_ref[1, 0].astype(F32)) * jax.nn.gelu(gr_ref[0])
    na = IN_Q
    at = at_ref[0].reshape(na, at_ref.shape[-1])
    m = lax.dot_general(at, w_ref[0:na], (((0,), (0,)), ((), ())), preferred_element_type=F32)
    rest = jnp.concatenate([lru.astype(BF16)] + [zy_ref[0, h].astype(BF16) for h in range(N_HALF)], axis=1)
    m = m + _dot(rest, w_ref[na:])
    o_ref[0] = _layer_norm(alpha * x_ref[0] + gt_ref[0] * m, g_ref[...], b_ref[...])


def _out_proj(x, attn, h, xg, zy, w_out, gt, ln_g, ln_b, alpha):
    B, L, D = x.shape
    T = _row_tile(L)
    C = LRU_WIDTH
    row = lambda b, i: (b, i, 0)
    vec = lambda b, i: (b, 0, 0)
    const = lambda b, i: (0, 0)
    return pl.pallas_call(
        functools.partial(_outproj_kernel, alpha=alpha),
        out_shape=jax.ShapeDtypeStruct((B, L, D), F32),
        grid=(B, L // T),
        in_specs=[
            pl.BlockSpec((1, T, D), row),
            pl.BlockSpec((1, N_KV_HEADS, KV_GROUP, HEAD_DIM, T), lambda b, i: (b, 0, 0, 0, i)),
            pl.BlockSpec((2, 1, T, C), lambda b, i: (0, b, i, 0)),
            pl.BlockSpec((1, T, C), lambda b, i: (b, i, 1)),
            pl.BlockSpec((1, N_HALF, T, LANES), lambda b, i: (b, 0, i, 0)),
            pl.BlockSpec(w_out.shape, const),
            pl.BlockSpec((1, 1, D), vec),
            pl.BlockSpec((1, D), const),
            pl.BlockSpec((1, D), const),
        ],
        out_specs=pl.BlockSpec((1, T, D), row),
        compiler_params=_cparams(("parallel", "parallel")),
        name="out_proj",
    )(x, attn, h, xg, zy, w_out, gt, ln_g.reshape(1, D), ln_b.reshape(1, D))


FFN_CHUNK = 256


def _ffn_kernel(xp_ref, x_ref, xn_ref, sc_ref, sh_ref, gt_ref, wu_ref, cw_ref, cb_ref, wd_ref, g_ref, b_ref,
                o_ref, u_sc, h_sc, acc_sc, *, T, nt, nf, alpha):
    i = pl.program_id(1)
    H = SUBLANES
    R = T + 2 * H
    F = FFN_CHUNK
    sc = 1.0 + sc_ref[0]
    sh = sh_ref[0]
    u_sc[H:H + T, :] = (x_ref[0] * sc + sh).astype(BF16)
    u_sc[0:H, :] = jnp.where(i > 0, xp_ref[0] * sc + sh, 0.0).astype(BF16)
    u_sc[H + T:R, :] = jnp.where(i < nt - 1, xn_ref[0] * sc + sh, 0.0).astype(BF16)
    acc_sc[...] = jnp.zeros(acc_sc.shape, F32)

    def up(f, slot):
        h_sc[slot] = _dot(u_sc[...], wu_ref[f])

    def down(f, slot):
        h = h_sc[slot]
        cw = cw_ref[f]
        y = cb_ref[f] + pltpu.roll(h, 1, 0) * cw[0:1] + h * cw[1:2] + pltpu.roll(h, R - 1, 0) * cw[2:3]
        y = y[H:H + T]
        act = (jax.nn.gelu(y[:, :F]) * y[:, F:]).astype(BF16)
        acc_sc[...] += _dot(act, wd_ref[f])

    def pair(c2, carry):
        f = 2 * c2
        up(f + 1, 1)
        down(f, 0)
        up(jnp.minimum(f + 2, nf - 1), 0)
        down(f + 1, 1)
        return carry

    up(0, 0)
    lax.fori_loop(0, nf // 2, pair, 0)
    if nf % 2 == 1:
        down(nf - 1, 0)
    o_ref[0] = _layer_norm(alpha * x_ref[0] + gt_ref[0] * acc_sc[...], g_ref[...], b_ref[...])


def _conv_ffn(x, sc, sh, gt, w_up, conv_w, conv_b, w_down, ln_g, ln_b, alpha):
    B, L, D = x.shape
    d_ff = w_down.shape[0]
    T = _row_tile(L)
    F = FFN_CHUNK
    nt = L // T
    nf = d_ff // F
    hb = T // SUBLANES
    n_halo = L // SUBLANES
    nk = conv_w.shape[0]
    chunked = lambda w: jnp.concatenate([w[..., :d_ff].reshape(w.shape[0], nf, F),
                                         w[..., d_ff:].reshape(w.shape[0], nf, F)], axis=-1).transpose(1, 0, 2)
    wu = chunked(w_up)
    cw = chunked(conv_w)
    cb = chunked(conv_b.reshape(1, 2 * d_ff))
    wd = w_down.reshape(nf, F, D)
    row = lambda b, i: (b, i, 0)
    vec = lambda b, i: (b, 0, 0)
    const2 = lambda b, i: (0, 0)
    const3 = lambda b, i: (0, 0, 0)
    resident = lambda shape, imap: pl.BlockSpec(shape, imap, pipeline_mode=pl.Buffered(1))
    return pl.pallas_call(
        functools.partial(_ffn_kernel, T=T, nt=nt, nf=nf, alpha=alpha),
        out_shape=jax.ShapeDtypeStruct((B, L, D), F32),
        grid=(B, nt),
        in_specs=[
            pl.BlockSpec((1, SUBLANES, D), lambda b, i: (b, jnp.maximum(i * hb - 1, 0), 0)),
            pl.BlockSpec((1, T, D), row),
            pl.BlockSpec((1, SUBLANES, D), lambda b, i: (b, jnp.minimum((i + 1) * hb, n_halo - 1), 0)),
            pl.BlockSpec((1, 1, D), vec),
            pl.BlockSpec((1, 1, D), vec),
            pl.BlockSpec((1, 1, D), vec),
            resident((nf, D, 2 * F), const3),
            resident((nf, nk, 2 * F), const3),
            resident((nf, 1, 2 * F), const3),
            resident((nf, F, D), const3),
            pl.BlockSpec((1, D), const2),
            pl.BlockSpec((1, D), const2),
        ],
        out_specs=pl.BlockSpec((1, T, D), row),
        scratch_shapes=[
            pltpu.VMEM((T + 2 * SUBLANES, D), BF16),
            pltpu.VMEM((2, T + 2 * SUBLANES, 2 * F), F32),
            pltpu.VMEM((T, D), F32),
        ],
        compiler_params=_cparams(("parallel", "parallel")),
        name="conv_ffn",
    )(x, x, x, sc, sh, gt, wu, cw, cb, wd, ln_g.reshape(1, D), ln_b.reshape(1, D))


def _trunk(x, mod, p):
    B, L, D = x.shape
    depth = mod.shape[0]
    alpha = (2 * depth) ** 0.25
    rope = _rope_tables(L)
    tables = _dft_tables(L)
    for l in range(depth):
        m6 = mod[l].reshape(B, 6, 1, D)
        sh1, sc1, gt1, sh2, sc2, gt2 = (m6[:, j] for j in range(6))
        q, k, v, xg, hc = _in_proj(x, sc1, sh1, p['w_in'][l], p['q_gain'][l], p['k_gain'][l], rope,
                                   p['hy_conv_w'][l], p['hy_conv_b'][l])
        attn = _attention(q, k, v)
        h = _rglru(xg, p['lru_conv_w'][l], p['lru_conv_b'][l], p['lru_wa'][l], p['lru_ba'][l],
                   p['lru_wx'][l], p['lru_bx'][l], p['lru_lambda'][l])
        zy = _hyena(hc, p['hy_w1'][l], p['hy_b1'][l], p['hy_w2'][l], p['hy_b2'][l], p['hy_w3'][l], p['hy_b3'][l],
                    p['hy_freq'][l], p['hy_wout'][l], p['hy_bias'][l], tables)
        x = _out_proj(x, attn, h, xg, zy, p['w_out'][l], gt1, p['ln1_g'][l], p['ln1_b'][l], alpha)
        x = _conv_ffn(x, sc2, sh2, gt2, p['ffn_w_up'][l], p['ffn_conv_w'][l], p['ffn_conv_b'][l],
                      p['ffn_w_down'][l], p['ln2_g'][l], p['ln2_b'][l], alpha)
    return x


def kernel(x_prompt, x_sample, c_prompt, c_sample, ada_w, ada_b, w_in, q_gain, k_gain, lru_conv_w, lru_conv_b, lru_wa, lru_ba, lru_wx, lru_bx, lru_lambda, hy_conv_w, hy_conv_b, hy_w1, hy_b1, hy_w2, hy_b2, hy_w3, hy_b3, hy_freq, hy_wout, hy_bias, w_out, ln1_g, ln1_b, ffn_w_up, ffn_conv_w, ffn_conv_b, ffn_w_down, ln2_g, ln2_b):
    p = dict(
        w_in=w_in.astype(BF16), q_gain=q_gain, k_gain=k_gain, lru_conv_w=lru_conv_w, lru_conv_b=lru_conv_b,
        lru_wa=lru_wa, lru_ba=lru_ba, lru_wx=lru_wx, lru_bx=lru_bx, lru_lambda=lru_lambda,
        hy_conv_w=hy_conv_w, hy_conv_b=hy_conv_b, hy_w1=hy_w1, hy_b1=hy_b1, hy_w2=hy_w2, hy_b2=hy_b2,
        hy_w3=hy_w3, hy_b3=hy_b3, hy_freq=hy_freq, hy_wout=hy_wout, hy_bias=hy_bias,
        w_out=w_out.astype(BF16), ln1_g=ln1_g, ln1_b=ln1_b, ffn_w_up=ffn_w_up.astype(BF16),
        ffn_conv_w=ffn_conv_w, ffn_conv_b=ffn_conv_b, ffn_w_down=ffn_w_down.astype(BF16), ln2_g=ln2_g, ln2_b=ln2_b,
    )
    nb = x_prompt.shape[0]
    mod = _ada_mod(jnp.concatenate([c_prompt, c_sample], axis=0), ada_w, ada_b)
    y_prompt = _trunk(x_prompt, mod[:, :nb], p)
    y_sample = _trunk(x_sample, mod[:, nb:], p)
    return (y_prompt, y_sample)
```
